```python
import jax, jax.numpy as jnp
from jax import lax
import numpy as np

D_MODEL = 2048
BATCH = 2
SEQ = 4096
DEPTH = 4
DEC_BATCH = 8
DEC_SEQ = 1
PAST_LEN = 16384
PAGE_SIZE = 128

HEAD_DIM = 128
ROPE_THETA = 500000.0
ROPE_DIM = HEAD_DIM // 4
MEM_LEN = 256
MEM_HEADS = 4
MEM_W = MEM_HEADS * HEAD_DIM
TOK_W = D_MODEL - MEM_W
POOL_WINDOWS = (2, 4, 8, 16)
POOL_GROUP = TOK_W // len(POOL_WINDOWS)
POOL_STATE = max(POOL_WINDOWS) - 1
NSA_HEADS = TOK_W // HEAD_DIM
NSA_KV_HEADS = 2
NSA_GROUP = NSA_HEADS // NSA_KV_HEADS
KV_W = NSA_KV_HEADS * HEAD_DIM
CMP_BLOCK = 32
CMP_STRIDE = 16
CMP_HIDDEN = HEAD_DIM
SEL_BLOCK = 64
SEL_TOP = 16
WINDOW = 512
Q_BLOCK = 128
NSA_TOK_IN = NSA_HEADS * HEAD_DIM + 6 * KV_W + 3 * NSA_HEADS
NSA_IN = NSA_TOK_IN + MEM_W
N_EXPERTS = 16
N_GROUPS = 4
EXP_PER_GROUP = N_EXPERTS // N_GROUPS
TOP_K = 2
D_FF = 1024
N_POOL_LAYERS = (DEPTH + 1) // 2
N_NSA_LAYERS = DEPTH // 2
EPS = 1e-6
NEG = -1e30
SCALE = HEAD_DIM ** -0.5

kernel_name = "hybrid_pool_nsa_memx_groupmoe_step"


def rms_norm(x, g):
    xf = x.astype(jnp.float32)
    y = xf * lax.rsqrt(jnp.mean(xf * xf, axis=-1, keepdims=True) + EPS)
    return (y * g.astype(jnp.float32)).astype(x.dtype)


def rope_partial(x, pos):
    half = ROPE_DIM // 2
    inv = 1.0 / (ROPE_THETA ** (jnp.arange(half, dtype=jnp.float32) * 2.0 / ROPE_DIM))
    ang = pos.astype(jnp.float32)[:, None] * inv[None, :]
    cos = jnp.cos(ang)[None, :, None, :]
    sin = jnp.sin(ang)[None, :, None, :]
    xf = x.astype(jnp.float32)
    x1 = xf[..., :half]
    x2 = xf[..., half:ROPE_DIM]
    out = jnp.concatenate([x1 * cos - x2 * sin, x2 * cos + x1 * sin, xf[..., ROPE_DIM:]], axis=-1)
    return out.astype(x.dtype)


def masked_softmax(s, valid):
    s = jnp.where(valid, s, NEG)
    m = jnp.max(s, axis=-1, keepdims=True)
    e = jnp.where(valid, jnp.exp(s - m), 0.0)
    return e / jnp.maximum(jnp.sum(e, axis=-1, keepdims=True), 1e-30)


def mem_kv(mem, g, w_kv, k_g):
    b, m = mem.shape[:2]
    kv = rms_norm(mem, g) @ w_kv
    k = rms_norm(kv[..., :MEM_W].reshape(b, m, MEM_HEADS, HEAD_DIM), k_g)
    v = kv[..., MEM_W:].reshape(b, m, MEM_HEADS, HEAD_DIM)
    return k, v


def mem_attend(mq, k, v, q_g):
    b, t = mq.shape[:2]
    q = rms_norm(mq.reshape(b, t, MEM_HEADS, HEAD_DIM), q_g)
    s = jnp.einsum('bthd,bmhd->bhtm', q, k).astype(jnp.float32) * SCALE
    p = jax.nn.softmax(s, axis=-1).astype(v.dtype)
    return jnp.einsum('bhtm,bmhd->bthd', p, v).reshape(b, t, MEM_W)


def pool_mix(u, prev, pos0, w_grp, scale):
    b, t, w_all = u.shape
    p_len = prev.shape[1]
    ext = jnp.concatenate([prev.astype(u.dtype), u], axis=1)
    ef = ext.astype(jnp.float32)
    c0 = jnp.concatenate([jnp.zeros((b, 1, w_all), jnp.float32), jnp.cumsum(ef, axis=1)], axis=1)
    tt = jnp.arange(t, dtype=jnp.int32)
    hi = p_len + tt + 1
    pos = pos0 + tt
    means = []
    for g, w in enumerate(POOL_WINDOWS):
        cg = c0[:, :, g * POOL_GROUP:(g + 1) * POOL_GROUP]
        lo = jnp.maximum(hi - w, 0)
        cnt = jnp.minimum(w, pos + 1).astype(jnp.float32)
        means.append((cg[:, hi] - cg[:, lo]) / cnt[None, :, None])
    d = (jnp.concatenate(means, axis=-1) - ef[:, p_len:]).astype(u.dtype)
    d = d.reshape(b, t, len(POOL_WINDOWS), POOL_GROUP)
    y = jnp.einsum('btgc,gce->btge', d, w_grp).reshape(b, t, w_all) * scale
    return y, ext[:, -POOL_STATE:]


def nsa_split(z, pos, q_g, k_g, b_gate):
    b, t = z.shape[:2]
    bounds = np.cumsum([NSA_HEADS * HEAD_DIM] + [KV_W] * 6).tolist()
    q, kc, vc, ks, vs, kw, vw, gl = jnp.split(z, bounds, axis=-1)
    q = rope_partial(rms_norm(q.reshape(b, t, NSA_HEADS, HEAD_DIM), q_g), pos)
    q = q.reshape(b, t, NSA_KV_HEADS, NSA_GROUP, HEAD_DIM)
    kvr = lambda a: a.reshape(b, t, NSA_KV_HEADS, HEAD_DIM)
    kc = rope_partial(rms_norm(kvr(kc), k_g[0]), pos)
    ks = rope_partial(rms_norm(kvr(ks), k_g[1]), pos)
    kw = rope_partial(rms_norm(kvr(kw), k_g[2]), pos)
    gate = jax.nn.sigmoid((gl + b_gate).astype(jnp.float32))
    gate = gate.reshape(b, t, NSA_KV_HEADS, NSA_GROUP, 3).astype(z.dtype)
    return q, kc, kvr(vc), ks, kvr(vs), kw, kvr(vw), gate


def compress(k, pe, w1, w2):
    b, l = k.shape[:2]
    r = CMP_BLOCK // CMP_STRIDE
    n_chunks = l // CMP_STRIDE
    nc = n_chunks - r + 1
    c = k[:, :n_chunks * CMP_STRIDE].reshape(b, n_chunks, CMP_STRIDE, NSA_KV_HEADS, HEAD_DIM)
    blk = jnp.concatenate([c[:, i:i + nc] for i in range(r)], axis=2)
    blk = blk + pe[None, None, :, None, :]
    flat = blk.transpose(0, 1, 3, 2, 4).reshape(b, nc, NSA_KV_HEADS, CMP_BLOCK * HEAD_DIM)
    hid = jax.nn.gelu(jnp.einsum('bnhf,fe->bnhe', flat, w1))
    return jnp.einsum('bnhe,ed->bnhd', hid, w2)


def cmp_attend(q, q_pos, kc, vc):
    nc = kc.shape[1]
    c_last = jnp.arange(nc, dtype=jnp.int32) * CMP_STRIDE + CMP_BLOCK - 1
    s = jnp.einsum('bqhgd,bnhd->bqhgn', q, kc).astype(jnp.float32) * SCALE
    valid = (c_last[None, :] <= q_pos[:, None])[None, :, None, None, :]
    p = masked_softmax(s, valid)
    o = jnp.einsum('bqhgn,bnhd->bqhgd', p.astype(vc.dtype), vc)
    return o, p


def select_blocks(p, q_pos, n_sel):
    nc = p.shape[-1]
    c_start = jnp.arange(nc, dtype=jnp.int32) * CMP_STRIDE
    c_last = c_start + CMP_BLOCK - 1
    s_start = jnp.arange(n_sel, dtype=jnp.int32) * SEL_BLOCK
    cover = ((c_start[:, None] < s_start[None, :] + SEL_BLOCK) & (c_last[:, None] >= s_start[None, :])).astype(jnp.float32)
    imp = jnp.einsum('bqhgn,ns->bqhs', p, cover)
    blk = jnp.arange(n_sel, dtype=jnp.int32)[None, :]
    cur = (q_pos // SEL_BLOCK)[:, None]
    forced = (blk == 0) | (blk == cur) | (blk == cur - 1)
    future = blk * SEL_BLOCK > q_pos[:, None]
    imp = jnp.where(future[None, :, None, :], -1e9, jnp.where(forced[None, :, None, :], 1e9, imp))
    _, idx = lax.top_k(imp, min(SEL_TOP, n_sel))
    return idx


def to_blocks(k):
    b, l = k.shape[:2]
    ns = -(-l // SEL_BLOCK)
    k = jnp.pad(k, ((0, 0), (0, ns * SEL_BLOCK - l), (0, 0), (0, 0)))
    return k.reshape(b, ns, SEL_BLOCK, NSA_KV_HEADS, HEAD_DIM).transpose(0, 3, 1, 2, 4)


def sel_attend(q, q_pos, idx, kb, vb):
    idx_t = idx.transpose(0, 2, 1, 3)
    take = jax.vmap(jax.vmap(lambda blocks, i: blocks[i]))
    kg = take(kb, idx_t)
    vg = take(vb, idx_t)
    kpos = idx_t[..., None] * SEL_BLOCK + jnp.arange(SEL_BLOCK, dtype=jnp.int32)
    valid = (kpos <= q_pos[None, None, :, None, None]).transpose(0, 2, 1, 3, 4)[:, :, :, None]
    s = jnp.einsum('bqhgd,bhqkld->bqhgkl', q, kg).astype(jnp.float32) * SCALE
    b, tq, h, g, kk, lb = s.shape
    p = masked_softmax(s.reshape(b, tq, h, g, kk * lb), valid.reshape(b, tq, h, 1, kk * lb)).reshape(s.shape)
    return jnp.einsum('bqhgkl,bhqkld->bqhgd', p.astype(vg.dtype), vg)


def window_attend(q, q_pos, k, v, k_pos):
    s = jnp.einsum('bqhgd,bkhd->bqhgk', q, k).astype(jnp.float32) * SCALE
    diff = q_pos[:, None] - k_pos[None, :]
    valid = ((diff >= 0) & (diff < WINDOW) & (k_pos[None, :] >= 0))[None, :, None, None, :]
    p = masked_softmax(s, valid)
    return jnp.einsum('bqhgk,bkhd->bqhgd', p.astype(v.dtype), v)


def combine(gate, o_cmp, o_sel, o_win):
    return gate[..., 0, None] * o_cmp + gate[..., 1, None] * o_sel + gate[..., 2, None] * o_win


def nsa_prompt(z, pos, q_g, k_g, b_gate, pe, w1, w2):
    b, t = z.shape[:2]
    q, kc, vc, ks, vs, kw, vw, gate = nsa_split(z, pos, q_g, k_g, b_gate)
    kcmp = compress(kc, pe[0], w1[0], w2[0])
    vcmp = compress(vc, pe[1], w1[1], w2[1])
    o_cmp, p_cmp = cmp_attend(q, pos, kcmp, vcmp)
    ksb, vsb = to_blocks(ks), to_blocks(vs)
    idx = select_blocks(p_cmp, pos, ksb.shape[2])
    nb = t // Q_BLOCK
    qb = q.reshape(b, nb, Q_BLOCK, NSA_KV_HEADS, NSA_GROUP, HEAD_DIM)
    ib = idx.reshape(b, nb, Q_BLOCK, NSA_KV_HEADS, idx.shape[-1])
    pb = pos.reshape(nb, Q_BLOCK)
    o_sel = lax.map(lambda a: sel_attend(a[0], a[1], a[2], ksb, vsb), (qb.swapaxes(0, 1), pb, ib.swapaxes(0, 1)))
    o_sel = o_sel.swapaxes(0, 1).reshape(q.shape)
    nwb = WINDOW // Q_BLOCK
    pad = ((0, 0), (nwb, 0), (0, 0), (0, 0), (0, 0))
    kwp = jnp.pad(kw.reshape(b, nb, Q_BLOCK, NSA_KV_HEADS, HEAD_DIM), pad)
    vwp = jnp.pad(vw.reshape(b, nb, Q_BLOCK, NSA_KV_HEADS, HEAD_DIM), pad)
    kband = jnp.concatenate([kwp[:, j:j + nb] for j in range(nwb + 1)], axis=2)
    vband = jnp.concatenate([vwp[:, j:j + nb] for j in range(nwb + 1)], axis=2)
    kpos = (jnp.arange(nb, dtype=jnp.int32)[:, None] - nwb) * Q_BLOCK + jnp.arange((nwb + 1) * Q_BLOCK, dtype=jnp.int32)[None, :]
    o_win = jax.vmap(window_attend, in_axes=(1, 0, 1, 1, 0), out_axes=1)(qb, pb, kband, vband, kpos).reshape(q.shape)
    o = combine(gate, o_cmp, o_sel, o_win).reshape(b, t, TOK_W)
    wb = min(WINDOW, t)
    return o, (kc, vc, ks, vs), (kw[:, t - wb:], vw[:, t - wb:])


def gather_pages(cache, page_table):
    c = cache[page_table]
    return c.reshape(page_table.shape[0], page_table.shape[1] * cache.shape[1], *cache.shape[2:])


def nsa_sample(z, pos, past, win_k, win_v, page_table, past_len, q_g, k_g, b_gate, pe, w1, w2):
    b, t = z.shape[:2]
    q, kc, vc, ks, vs, kw, vw, gate = nsa_split(z, pos, q_g, k_g, b_gate)
    full = [jnp.concatenate([gather_pages(c, page_table).astype(new.dtype), new], axis=1)
            for c, new in zip(past, (kc, vc, ks, vs))]
    kcmp = compress(full[0], pe[0], w1[0], w2[0])
    vcmp = compress(full[1], pe[1], w1[1], w2[1])
    o_cmp, p_cmp = cmp_attend(q, pos, kcmp, vcmp)
    ksb, vsb = to_blocks(full[2]), to_blocks(full[3])
    idx = select_blocks(p_cmp, pos, ksb.shape[2])
    o_sel = sel_attend(q, pos, idx, ksb, vsb)
    wb = win_k.shape[1]
    kwin = jnp.concatenate([win_k.astype(kw.dtype), kw], axis=1)
    vwin = jnp.concatenate([win_v.astype(vw.dtype), vw], axis=1)
    kpos = past_len - wb + jnp.arange(wb + t, dtype=jnp.int32)
    o_win = window_attend(q, pos, kwin, vwin, kpos)
    o = combine(gate, o_cmp, o_sel, o_win).reshape(b, t, TOK_W)
    return o, (kc, vc, ks, vs), (kwin[:, -wb:], vwin[:, -wb:])


def moe_ffn(h, w_router, b_router, wg, wu, wd):
    b, t, d = h.shape
    xt = h.reshape(-1, d)
    n = xt.shape[0]
    logits = xt.astype(jnp.float32) @ w_router.astype(jnp.float32)
    aff = jax.nn.softmax(logits, axis=-1)
    sel = (aff + b_router.astype(jnp.float32)).reshape(n, N_GROUPS, EXP_PER_GROUP)
    grp_score = lax.top_k(sel, TOP_K)[0].sum(-1)
    g_idx = jnp.argmax(grp_score, axis=-1)
    in_grp = sel[jnp.arange(n), g_idx]
    _, loc = lax.top_k(in_grp, TOP_K)
    experts = g_idx[:, None] * EXP_PER_GROUP + loc
    gates = jnp.take_along_axis(aff, experts, axis=1)
    gates = gates / jnp.sum(gates, axis=-1, keepdims=True)
    flat_e = experts.reshape(-1)
    flat_t = jnp.repeat(jnp.arange(n, dtype=jnp.int32), TOP_K)
    order = jnp.argsort(flat_e)
    tok = flat_t[order]
    xs = xt[tok]
    sizes = jnp.bincount(flat_e, length=N_EXPERTS).astype(jnp.int32)
    a = lax.ragged_dot(xs, wg, sizes)
    u = lax.ragged_dot(xs, wu, sizes)
    y = lax.ragged_dot(jax.nn.silu(a) * u, wd, sizes)
    y = y * gates.reshape(-1)[order][:, None].astype(y.dtype)
    out = jnp.zeros_like(xt).at[tok].add(y)
    return out.reshape(b, t, d)


def setup_inputs(seed: int = 0) -> dict:
    key = jax.random.key(seed)
    keys = jax.random.split(key, 48)
    counter = [0]

    def nxt():
        counter[0] += 1
        return keys[counter[0] - 1]

    def nrm(shape, scale=1.0):
        return jax.random.normal(nxt(), shape, jnp.float32) * scale

    def gain(shape):
        return 1.0 + nrm(shape, 0.01)

    n_pages = PAST_LEN // PAGE_SIZE
    n_used = DEC_BATCH * n_pages
    n_phys = n_used + max(n_used // 4, 1)
    page_table = jax.random.permutation(nxt(), n_phys)[:n_used].reshape(DEC_BATCH, n_pages).astype(jnp.int32)
    wb = min(WINDOW, PAST_LEN)
    d = D_MODEL
    out_scale = (d * 2 * DEPTH) ** -0.5
    paged = (N_NSA_LAYERS, n_phys, PAGE_SIZE, NSA_KV_HEADS, HEAD_DIM)
    win = (N_NSA_LAYERS, DEC_BATCH, wb, NSA_KV_HEADS, HEAD_DIM)
    memc = (DEPTH, DEC_BATCH, MEM_LEN, MEM_HEADS, HEAD_DIM)
    return {
        'x_prompt': nrm((BATCH, SEQ, d)),
        'x_sample': nrm((DEC_BATCH, DEC_SEQ, d)),
        'state_pool': nrm((N_POOL_LAYERS, DEC_BATCH, POOL_STATE, TOK_W)),
        'cache_cmp_k': nrm(paged),
        'cache_cmp_v': nrm(paged),
        'cache_sel_k': nrm(paged),
        'cache_sel_v': nrm(paged),
        'state_win_k': nrm(win),
        'state_win_v': nrm(win),
        'cache_mem_k': nrm(memc),
        'cache_mem_v': nrm(memc),
        'page_table': page_table,
        'mem_prompt': nrm((BATCH, MEM_LEN, d)),
        'norm_mix_g': gain((DEPTH, d)),
        'norm_ffn_g': gain((DEPTH, d)),
        'norm_mem_g': gain((DEPTH, d)),
        'w_mem_kv': nrm((DEPTH, d, 2 * MEM_W), d ** -0.5),
        'mem_q_norm_g': gain((DEPTH, HEAD_DIM)),
        'mem_k_norm_g': gain((DEPTH, HEAD_DIM)),
        'w_in_pool': nrm((N_POOL_LAYERS, d, TOK_W + MEM_W), d ** -0.5),
        'w_pool_grp': nrm((N_POOL_LAYERS, len(POOL_WINDOWS), POOL_GROUP, POOL_GROUP), POOL_GROUP ** -0.5),
        'pool_scale': 1.0 + nrm((N_POOL_LAYERS, TOK_W), 0.1),
        'w_out_pool': nrm((N_POOL_LAYERS, TOK_W + MEM_W, d), out_scale),
        'w_in_nsa': nrm((N_NSA_LAYERS, d, NSA_IN), d ** -0.5),
        'b_gate': nrm((N_NSA_LAYERS, 3 * NSA_HEADS), 0.01),
        'nsa_q_norm_g': gain((N_NSA_LAYERS, HEAD_DIM)),
        'nsa_k_norm_g': gain((N_NSA_LAYERS, 3, HEAD_DIM)),
        'cmp_pe': nrm((N_NSA_LAYERS, 2, CMP_BLOCK, HEAD_DIM), 0.02),
        'cmp_w1': nrm((N_NSA_LAYERS, 2, CMP_BLOCK * HEAD_DIM, CMP_HIDDEN), (CMP_BLOCK * HEAD_DIM) ** -0.5),
        'cmp_w2': nrm((N_NSA_LAYERS, 2, CMP_HIDDEN, HEAD_DIM), CMP_HIDDEN ** -0.5),
        'w_out_nsa': nrm((N_NSA_LAYERS, TOK_W + MEM_W, d), out_scale),
        'w_router': nrm((d, N_EXPERTS), d ** -0.5),
        'b_router': nrm((N_EXPERTS,), 0.01),
        'w_gate': nrm((DEPTH, N_EXPERTS, d, D_FF), d ** -0.5),
        'w_up': nrm((DEPTH, N_EXPERTS, d, D_FF), d ** -0.5),
        'w_down': nrm((DEPTH, N_EXPERTS, D_FF, d), (D_FF * 2 * DEPTH) ** -0.5),
    }


def reference(x_prompt, x_sample, state_pool, cache_cmp_k, cache_cmp_v, cache_sel_k, cache_sel_v,
              state_win_k, state_win_v, cache_mem_k, cache_mem_v, page_table, mem_prompt,
              norm_mix_g, norm_ffn_g, norm_mem_g, w_mem_kv, mem_q_norm_g, mem_k_norm_g,
              w_in_pool, w_pool_grp, pool_scale, w_out_pool, w_in_nsa, b_gate, nsa_q_norm_g,
              nsa_k_norm_g, cmp_pe, cmp_w1, cmp_w2, w_out_nsa, w_router, b_router,
              w_gate, w_up, w_down):
    t_p = x_prompt.shape[1]
    t_s = x_sample.shape[1]
    past_len = page_table.shape[1] * PAGE_SIZE
    pos_p = jnp.arange(t_p, dtype=jnp.int32)
    pos_s = past_len + jnp.arange(t_s, dtype=jnp.int32)
    xp, xs = x_prompt, x_sample
    pool_p, pool_s = [], []
    rows_p, rows_s = [], []
    win_p, win_s = [], []
    mem_k_p, mem_v_p = [], []
    for i in range(DEPTH):
        li = i // 2
        mk_p, mv_p = mem_kv(mem_prompt, norm_mem_g[i], w_mem_kv[i], mem_k_norm_g[i])
        mem_k_p.append(mk_p)
        mem_v_p.append(mv_p)
        hp = rms_norm(xp, norm_mix_g[i])
        hs = rms_norm(xs, norm_mix_g[i])
        if i % 2 == 0:
            zp = hp @ w_in_pool[li]
            zs = hs @ w_in_pool[li]
            op, st_p = pool_mix(zp[..., :TOK_W], jnp.zeros((xp.shape[0], 0, TOK_W), zp.dtype), 0, w_pool_grp[li], pool_scale[li])
            os_, st_s = pool_mix(zs[..., :TOK_W], state_pool[li], past_len, w_pool_grp[li], pool_scale[li])
            pool_p.append(st_p)
            pool_s.append(st_s)
            mq_p, mq_s = zp[..., TOK_W:], zs[..., TOK_W:]
            w_out = w_out_pool[li]
        else:
            zp = hp @ w_in_nsa[li]
            zs = hs @ w_in_nsa[li]
            op, r_p, wn_p = nsa_prompt(zp[..., :NSA_TOK_IN], pos_p, nsa_q_norm_g[li], nsa_k_norm_g[li], b_gate[li],
                                       cmp_pe[li], cmp_w1[li], cmp_w2[li])
            past = (cache_cmp_k[li], cache_cmp_v[li], cache_sel_k[li], cache_sel_v[li])
            os_, r_s, wn_s = nsa_sample(zs[..., :NSA_TOK_IN], pos_s, past, state_win_k[li], state_win_v[li],
                                        page_table, past_len, nsa_q_norm_g[li], nsa_k_norm_g[li], b_gate[li],
                                        cmp_pe[li], cmp_w1[li], cmp_w2[li])
            rows_p.append(r_p)
            rows_s.append(r_s)
            win_p.append(wn_p)
            win_s.append(wn_s)
            mq_p, mq_s = zp[..., NSA_TOK_IN:], zs[..., NSA_TOK_IN:]
            w_out = w_out_nsa[li]
        ap = mem_attend(mq_p, mk_p, mv_p, mem_q_norm_g[i])
        as_ = mem_attend(mq_s, cache_mem_k[i].astype(xs.dtype), cache_mem_v[i].astype(xs.dtype), mem_q_norm_g[i])
        xp = xp + jnp.concatenate([op, ap], axis=-1) @ w_out
        xs = xs + jnp.concatenate([os_, as_], axis=-1) @ w_out
        xp = xp + moe_ffn(rms_norm(xp, norm_ffn_g[i]), w_router, b_router, w_gate[i], w_up[i], w_down[i])
        xs = xs + moe_ffn(rms_norm(xs, norm_ffn_g[i]), w_router, b_router, w_gate[i], w_up[i], w_down[i])
    stk = lambda lst, j: jnp.stack([r[j] for r in lst])
    return (xp, xs,
            jnp.stack(pool_p), jnp.stack(pool_s),
            stk(rows_p, 0), stk(rows_p, 1), stk(rows_p, 2), stk(rows_p, 3),
            stk(rows_s, 0), stk(rows_s, 1), stk(rows_s, 2), stk(rows_s, 3),
            stk(win_p, 0), stk(win_p, 1), stk(win_s, 0), stk(win_s, 1),
            jnp.stack(mem_k_p), jnp.stack(mem_v_p))
```

```python
import functools

import jax
import jax.numpy as jnp
import numpy as np
from jax import lax
from jax.experimental import pallas as pl
from jax.experimental.pallas import tpu as pltpu

D_MODEL = 2048
DEPTH = 4
PAGE_SIZE = 128
HEAD_DIM = 128
ROPE_THETA = 500000.0
ROPE_DIM = HEAD_DIM // 4
MEM_HEADS = 4
MEM_W = MEM_HEADS * HEAD_DIM
TOK_W = D_MODEL - MEM_W
POOL_WINDOWS = (2, 4, 8, 16)
POOL_GROUP = TOK_W // len(POOL_WINDOWS)
POOL_STATE = max(POOL_WINDOWS) - 1
NSA_HEADS = TOK_W // HEAD_DIM
NSA_KV_HEADS = 2
NSA_GROUP = NSA_HEADS // NSA_KV_HEADS
KV_W = NSA_KV_HEADS * HEAD_DIM
CMP_BLOCK = 32
CMP_STRIDE = 16
SEL_BLOCK = 64
SEL_TOP = 16
WINDOW = 512
Q_BLOCK = 128
NSA_TOK_IN = NSA_HEADS * HEAD_DIM + 6 * KV_W + 3 * NSA_HEADS
N_EXPERTS = 16
N_GROUPS = 4
EXP_PER_GROUP = N_EXPERTS // N_GROUPS
TOP_K = 2
D_FF = 1024
EPS = 1e-6
NEG = -1e30
SCALE = HEAD_DIM ** -0.5

V7X_VMEM_BYTES = 64 * 1024 * 1024
VMEM_LIMIT = V7X_VMEM_BYTES * 3 // 4

BF16 = jnp.bfloat16
F32 = jnp.float32


def _cparams(sem):
    return pltpu.CompilerParams(dimension_semantics=sem, vmem_limit_bytes=VMEM_LIMIT)


def _proj_kernel(*refs, norm, residual):
    x_ref, refs = refs[0], refs[1:]
    if norm:
        g_ref, refs = refs[0], refs[1:]
    w_ref, refs = refs[0], refs[1:]
    if residual:
        r_ref, refs = refs[0], refs[1:]
    o_ref, xn_ref = refs

    @pl.when(pl.program_id(1) == 0)
    def _():
        x = x_ref[...].astype(F32)
        if norm:
            x = x * lax.rsqrt(jnp.mean(x * x, axis=-1, keepdims=True) + EPS) * g_ref[...]
        xn_ref[...] = x.astype(BF16)

    y = jnp.dot(xn_ref[...], w_ref[...], preferred_element_type=F32)
    if residual:
        y = y + r_ref[...]
    o_ref[...] = y


def _pick_tile(n, cap, unit):
    if n <= cap:
        return n
    best = None
    for t in range(unit, cap + 1, unit):
        if n % t == 0:
            best = t
    assert best is not None, (n, cap, unit)
    return best


def proj(x, w_bf16, gain=None, residual=None):
    m, k = x.shape
    n = w_bf16.shape[1]
    tm = _pick_tile(m, 512, 8)
    tn = _pick_tile(n, 1024, 128)
    in_specs = [pl.BlockSpec((tm, k), lambda i, j: (i, 0))]
    args = [x]
    if gain is not None:
        in_specs.append(pl.BlockSpec((1, k), lambda i, j: (0, 0)))
        args.append(gain.reshape(1, k).astype(F32))
    in_specs.append(pl.BlockSpec((k, tn), lambda i, j: (0, j)))
    args.append(w_bf16)
    if residual is not None:
        in_specs.append(pl.BlockSpec((tm, tn), lambda i, j: (i, j)))
        args.append(residual)
    return pl.pallas_call(
        functools.partial(_proj_kernel, norm=gain is not None, residual=residual is not None),
        grid=(m // tm, n // tn),
        in_specs=in_specs,
        out_specs=pl.BlockSpec((tm, tn), lambda i, j: (i, j)),
        out_shape=jax.ShapeDtypeStruct((m, n), F32),
        scratch_shapes=[pltpu.VMEM((tm, k), BF16)],
        compiler_params=_cparams(("parallel", "arbitrary")),
        name="proj",
    )(*args)


MOE_TM = 512
MOE_TF = 256


def _moe_kernel(te_ref, nu_ref, x_ref, wg_ref, wu_ref, wd_ref, o_ref):
    i = pl.program_id(0)
    f = pl.program_id(1)

    @pl.when(i < nu_ref[0])
    def _():
        x = x_ref[...]
        a = jnp.dot(x, wg_ref[...].astype(BF16), preferred_element_type=F32)
        u = jnp.dot(x, wu_ref[...].astype(BF16), preferred_element_type=F32)
        h = (a * jax.nn.sigmoid(a) * u).astype(BF16)
        y = jnp.dot(h, wd_ref[...].astype(BF16), preferred_element_type=F32)

        @pl.when(f == 0)
        def _():
            o_ref[...] = y

        @pl.when(f > 0)
        def _():
            o_ref[...] += y


def moe_experts(xs, tile_expert, n_used, w_gate, w_up, w_down, layer):
    s, d = xs.shape
    n_tiles = s // MOE_TM
    nf = D_FF // MOE_TF

    def tile(i, nu):
        return jnp.minimum(i, nu[0] - 1)

    def fcol(i, f, nu):
        return jnp.where(i < nu[0], f, nf - 1)

    grid_spec = pltpu.PrefetchScalarGridSpec(
        num_scalar_prefetch=2,
        grid=(n_tiles, nf),
        in_specs=[
            pl.BlockSpec((MOE_TM, d), lambda i, f, te, nu: (tile(i, nu), 0)),
            pl.BlockSpec((None, None, d, MOE_TF), lambda i, f, te, nu: (layer, te[tile(i, nu)], 0, fcol(i, f, nu))),
            pl.BlockSpec((None, None, d, MOE_TF), lambda i, f, te, nu: (layer, te[tile(i, nu)], 0, fcol(i, f, nu))),
            pl.BlockSpec((None, None, MOE_TF, d), lambda i, f, te, nu: (layer, te[tile(i, nu)], fcol(i, f, nu), 0)),
        ],
        out_specs=pl.BlockSpec((MOE_TM, d), lambda i, f, te, nu: (tile(i, nu), 0)),
    )
    return pl.pallas_call(
        _moe_kernel,
        grid_spec=grid_spec,
        out_shape=jax.ShapeDtypeStruct((s, d), F32),
        compiler_params=_cparams(("arbitrary", "arbitrary")),
        name="moe_experts",
    )(tile_expert, n_used, xs, w_gate, w_up, w_down)


SEL_TQ = 128
SEL_TK = 512


def _sel_kernel(q_ref, k_ref, v_ref, sel_ref, exp_ref, o_ref, mask_ref, m_ref, l_ref, acc_ref, *, t):
    qi = pl.program_id(2)
    q = q_ref[...]
    q6 = jnp.concatenate([q[:, g * HEAD_DIM:(g + 1) * HEAD_DIM] for g in range(NSA_GROUP)], axis=0).astype(BF16)
    mask_ref[...] = jnp.dot(sel_ref[...], exp_ref[...], preferred_element_type=F32)
    m_ref[...] = jnp.full(m_ref.shape, NEG, F32)
    l_ref[...] = jnp.zeros(l_ref.shape, F32)
    acc_ref[...] = jnp.zeros(acc_ref.shape, F32)
    q_pos = qi * SEL_TQ + lax.broadcasted_iota(jnp.int32, (SEL_TQ, SEL_TK), 0)

    for j in range(t // SEL_TK):
        @pl.when(j * SEL_TK <= qi * SEL_TQ + SEL_TQ - 1)
        def _():
            k = k_ref[j * SEL_TK:(j + 1) * SEL_TK, :].astype(BF16)
            v = v_ref[j * SEL_TK:(j + 1) * SEL_TK, :].astype(BF16)
            s = lax.dot_general(q6, k, (((1,), (1,)), ((), ())), preferred_element_type=F32) * SCALE
            k_pos = j * SEL_TK + lax.broadcasted_iota(jnp.int32, (SEL_TQ, SEL_TK), 1)
            valid = jnp.where(k_pos <= q_pos, mask_ref[:, j * SEL_TK:(j + 1) * SEL_TK], 0.0) > 0.5
            s = s.reshape(NSA_GROUP, SEL_TQ, SEL_TK)
            s = jnp.where(valid[None], s, NEG)
            m_old = m_ref[...]
            m_new = jnp.maximum(m_old, jnp.max(s, axis=-1, keepdims=True))
            e = jnp.where(valid[None], jnp.exp(s - m_new), 0.0)
            alpha = jnp.exp(m_old - m_new)
            l_ref[...] = alpha * l_ref[...] + jnp.sum(e, axis=-1, keepdims=True)
            pv = jnp.dot(e.reshape(NSA_GROUP * SEL_TQ, SEL_TK).astype(BF16), v, preferred_element_type=F32)
            acc_ref[...] = alpha * acc_ref[...] + pv.reshape(NSA_GROUP, SEL_TQ, HEAD_DIM)
            m_ref[...] = m_new

    o = acc_ref[...] / jnp.maximum(l_ref[...], 1e-30)
    o_ref[...] = jnp.concatenate([o[g] for g in range(NSA_GROUP)], axis=-1)


def sel_attention_prompt(q, ks, vs, sel):
    b, t, _ = q.shape
    n_blk = sel.shape[-1]
    expand = (np.arange(t)[None, :] // SEL_BLOCK == np.arange(n_blk)[:, None]).astype(np.float32)
    expand = jnp.asarray(expand, BF16)
    gw = NSA_GROUP * HEAD_DIM
    return pl.pallas_call(
        functools.partial(_sel_kernel, t=t),
        grid=(b, NSA_KV_HEADS, t // SEL_TQ),
        in_specs=[
            pl.BlockSpec((None, SEL_TQ, gw), lambda bi, h, qi: (bi, qi, h)),
            pl.BlockSpec((None, t, HEAD_DIM), lambda bi, h, qi: (bi, 0, h)),
            pl.BlockSpec((None, t, HEAD_DIM), lambda bi, h, qi: (bi, 0, h)),
            pl.BlockSpec((None, None, SEL_TQ, n_blk), lambda bi, h, qi: (bi, h, qi, 0)),
            pl.BlockSpec((n_blk, t), lambda bi, h, qi: (0, 0)),
        ],
        out_specs=pl.BlockSpec((None, SEL_TQ, gw), lambda bi, h, qi: (bi, qi, h)),
        out_shape=jax.ShapeDtypeStruct((b, t, NSA_HEADS * HEAD_DIM), F32),
        scratch_shapes=[
            pltpu.VMEM((SEL_TQ, t), F32),
            pltpu.VMEM((NSA_GROUP, SEL_TQ, 1), F32),
            pltpu.VMEM((NSA_GROUP, SEL_TQ, 1), F32),
            pltpu.VMEM((NSA_GROUP, SEL_TQ, HEAD_DIM), F32),
        ],
        compiler_params=_cparams(("parallel", "parallel", "arbitrary")),
        name="sel_attention_prompt",
    )(q, ks, vs, sel, expand)


def rms_norm(x, g):
    xf = x.astype(F32)
    y = xf * lax.rsqrt(jnp.mean(xf * xf, axis=-1, keepdims=True) + EPS)
    return (y * g.astype(F32)).astype(x.dtype)


def rope_partial(x, pos):
    half = ROPE_DIM // 2
    inv = 1.0 / (ROPE_THETA ** (jnp.arange(half, dtype=F32) * 2.0 / ROPE_DIM))
    ang = pos.astype(F32)[:, None] * inv[None, :]
    cos = jnp.cos(ang)[None, :, None, :]
    sin = jnp.sin(ang)[None, :, None, :]
    xf = x.astype(F32)
    x1 = xf[..., :half]
    x2 = xf[..., half:ROPE_DIM]
    out = jnp.concatenate([x1 * cos - x2 * sin, x2 * cos + x1 * sin, xf[..., ROPE_DIM:]], axis=-1)
    return out.astype(x.dtype)


def masked_softmax(s, valid):
    s = jnp.where(valid, s, NEG)
    m = jnp.max(s, axis=-1, keepdims=True)
    e = jnp.where(valid, jnp.exp(s - m), 0.0)
    return e / jnp.maximum(jnp.sum(e, axis=-1, keepdims=True), 1e-30)


def mem_attend(mq, k, v, q_g):
    b, t = mq.shape[:2]
    q = rms_norm(mq.reshape(b, t, MEM_HEADS, HEAD_DIM), q_g)
    s = jnp.einsum('bthd,bmhd->bhtm', q, k).astype(F32) * SCALE
    p = jax.nn.softmax(s, axis=-1).astype(v.dtype)
    return jnp.einsum('bhtm,bmhd->bthd', p, v).reshape(b, t, MEM_W)


def pool_mix(u, prev, pos0, w_grp, scale):
    b, t, w_all = u.shape
    p_len = prev.shape[1]
    ext = jnp.concatenate([prev.astype(u.dtype), u], axis=1)
    ef = ext.astype(F32)
    c0 = jnp.concatenate([jnp.zeros((b, 1, w_all), F32), jnp.cumsum(ef, axis=1)], axis=1)
    tt = jnp.arange(t, dtype=jnp.int32)
    hi = p_len + tt + 1
    pos = pos0 + tt
    means = []
    for g, w in enumerate(POOL_WINDOWS):
        cg = c0[:, :, g * POOL_GROUP:(g + 1) * POOL_GROUP]
        lo = jnp.maximum(hi - w, 0)
        cnt = jnp.minimum(w, pos + 1).astype(F32)
        means.append((cg[:, hi] - cg[:, lo]) / cnt[None, :, None])
    d = (jnp.concatenate(means, axis=-1) - ef[:, p_len:]).astype(u.dtype)
    d = d.reshape(b, t, len(POOL_WINDOWS), POOL_GROUP)
    y = jnp.einsum('btgc,gce->btge', d, w_grp).reshape(b, t, w_all) * scale
    return y, ext[:, -POOL_STATE:]


def nsa_split(z, pos, q_g, k_g, b_gate):
    b, t = z.shape[:2]
    bounds = np.cumsum([NSA_HEADS * HEAD_DIM] + [KV_W] * 6).tolist()
    q, kc, vc, ks, vs, kw, vw, gl = jnp.split(z, bounds, axis=-1)
    q = rope_partial(rms_norm(q.reshape(b, t, NSA_HEADS, HEAD_DIM), q_g), pos)
    q = q.reshape(b, t, NSA_KV_HEADS, NSA_GROUP, HEAD_DIM)
    kvr = lambda a: a.reshape(b, t, NSA_KV_HEADS, HEAD_DIM)
    kc = rope_partial(rms_norm(kvr(kc), k_g[0]), pos)
    ks = rope_partial(rms_norm(kvr(ks), k_g[1]), pos)
    kw = rope_partial(rms_norm(kvr(kw), k_g[2]), pos)
    gate = jax.nn.sigmoid((gl + b_gate).astype(F32))
    gate = gate.reshape(b, t, NSA_KV_HEADS, NSA_GROUP, 3).astype(z.dtype)
    return q, kc, kvr(vc), ks, kvr(vs), kw, kvr(vw), gate


def compress(k, pe, w1, w2):
    b, l = k.shape[:2]
    r = CMP_BLOCK // CMP_STRIDE
    n_chunks = l // CMP_STRIDE
    nc = n_chunks - r + 1
    c = k[:, :n_chunks * CMP_STRIDE].reshape(b, n_chunks, CMP_STRIDE, NSA_KV_HEADS, HEAD_DIM)
    blk = jnp.concatenate([c[:, i:i + nc] for i in range(r)], axis=2)
    blk = blk + pe[None, None, :, None, :]
    flat = blk.transpose(0, 1, 3, 2, 4).reshape(b, nc, NSA_KV_HEADS, CMP_BLOCK * HEAD_DIM)
    hid = jax.nn.gelu(jnp.einsum('bnhf,fe->bnhe', flat, w1))
    return jnp.einsum('bnhe,ed->bnhd', hid, w2)


def cmp_attend(q, q_pos, kc, vc):
    nc = kc.shape[1]
    c_last = jnp.arange(nc, dtype=jnp.int32) * CMP_STRIDE + CMP_BLOCK - 1
    s = jnp.einsum('bqhgd,bnhd->bqhgn', q, kc).astype(F32) * SCALE
    valid = (c_last[None, :] <= q_pos[:, None])[None, :, None, None, :]
    p = masked_softmax(s, valid)
    o = jnp.einsum('bqhgn,bnhd->bqhgd', p.astype(vc.dtype), vc)
    return o, p


def select_blocks(p, q_pos, n_sel):
    nc = p.shape[-1]
    c_start = jnp.arange(nc, dtype=jnp.int32) * CMP_STRIDE
    c_last = c_start + CMP_BLOCK - 1
    s_start = jnp.arange(n_sel, dtype=jnp.int32) * SEL_BLOCK
    cover = ((c_start[:, None] < s_start[None, :] + SEL_BLOCK) & (c_last[:, None] >= s_start[None, :])).astype(F32)
    imp = jnp.einsum('bqhgn,ns->bqhs', p, cover)
    blk = jnp.arange(n_sel, dtype=jnp.int32)[None, :]
    cur = (q_pos // SEL_BLOCK)[:, None]
    forced = (blk == 0) | (blk == cur) | (blk == cur - 1)
    future = blk * SEL_BLOCK > q_pos[:, None]
    imp = jnp.where(future[None, :, None, :], -1e9, jnp.where(forced[None, :, None, :], 1e9, imp))
    _, idx = lax.top_k(imp, min(SEL_TOP, n_sel))
    return idx


def to_blocks(k):
    b, l = k.shape[:2]
    ns = -(-l // SEL_BLOCK)
    k = jnp.pad(k, ((0, 0), (0, ns * SEL_BLOCK - l), (0, 0), (0, 0)))
    return k.reshape(b, ns, SEL_BLOCK, NSA_KV_HEADS, HEAD_DIM).transpose(0, 3, 1, 2, 4)


def sel_attend(q, q_pos, idx, kb, vb):
    idx_t = idx.transpose(0, 2, 1, 3)
    take = jax.vmap(jax.vmap(lambda blocks, i: blocks[i]))
    kg = take(kb, idx_t)
    vg = take(vb, idx_t)
    kpos = idx_t[..., None] * SEL_BLOCK + jnp.arange(SEL_BLOCK, dtype=jnp.int32)
    valid = (kpos <= q_pos[None, None, :, None, None]).transpose(0, 2, 1, 3, 4)[:, :, :, None]
    s = jnp.einsum('bqhgd,bhqkld->bqhgkl', q, kg).astype(F32) * SCALE
    b, tq, h, g, kk, lb = s.shape
    p = masked_softmax(s.reshape(b, tq, h, g, kk * lb), valid.reshape(b, tq, h, 1, kk * lb)).reshape(s.shape)
    return jnp.einsum('bqhgkl,bhqkld->bqhgd', p.astype(vg.dtype), vg)


def window_attend(q, q_pos, k, v, k_pos):
    s = jnp.einsum('bqhgd,bkhd->bqhgk', q, k).astype(F32) * SCALE
    diff = q_pos[:, None] - k_pos[None, :]
    valid = ((diff >= 0) & (diff < WINDOW) & (k_pos[None, :] >= 0))[None, :, None, None, :]
    p = masked_softmax(s, valid)
    return jnp.einsum('bqhgk,bkhd->bqhgd', p.astype(v.dtype), v)


def combine(gate, o_cmp, o_sel, o_win):
    return gate[..., 0, None] * o_cmp + gate[..., 1, None] * o_sel + gate[..., 2, None] * o_win


def nsa_prompt(z, pos, q_g, k_g, b_gate, pe, w1, w2):
    b, t = z.shape[:2]
    q, kc, vc, ks, vs, kw, vw, gate = nsa_split(z, pos, q_g, k_g, b_gate)
    kcmp = compress(kc, pe[0], w1[0], w2[0])
    vcmp = compress(vc, pe[1], w1[1], w2[1])
    o_cmp, p_cmp = cmp_attend(q, pos, kcmp, vcmp)
    n_sel = -(-t // SEL_BLOCK)
    idx = select_blocks(p_cmp, pos, n_sel)
    sel = jnp.sum(jax.nn.one_hot(idx, n_sel, dtype=F32), axis=-2)
    sel = sel.transpose(0, 2, 1, 3).astype(BF16)
    o_sel = sel_attention_prompt(q.reshape(b, t, TOK_W), ks.reshape(b, t, KV_W), vs.reshape(b, t, KV_W), sel)
    o_sel = o_sel.reshape(q.shape)
    nb = t // Q_BLOCK
    qb = q.reshape(b, nb, Q_BLOCK, NSA_KV_HEADS, NSA_GROUP, HEAD_DIM)
    pb = pos.reshape(nb, Q_BLOCK)
    nwb = WINDOW // Q_BLOCK
    pad = ((0, 0), (nwb, 0), (0, 0), (0, 0), (0, 0))
    kwp = jnp.pad(kw.reshape(b, nb, Q_BLOCK, NSA_KV_HEADS, HEAD_DIM), pad)
    vwp = jnp.pad(vw.reshape(b, nb, Q_BLOCK, NSA_KV_HEADS, HEAD_DIM), pad)
    kband = jnp.concatenate([kwp[:, j:j + nb] for j in range(nwb + 1)], axis=2)
    vband = jnp.concatenate([vwp[:, j:j + nb] for j in range(nwb + 1)], axis=2)
    kpos = (jnp.arange(nb, dtype=jnp.int32)[:, None] - nwb) * Q_BLOCK + jnp.arange((nwb + 1) * Q_BLOCK, dtype=jnp.int32)[None, :]
    o_win = jax.vmap(window_attend, in_axes=(1, 0, 1, 1, 0), out_axes=1)(qb, pb, kband, vband, kpos).reshape(q.shape)
    o = combine(gate, o_cmp, o_sel, o_win).reshape(b, t, TOK_W)
    wb = min(WINDOW, t)
    return o, (kc, vc, ks, vs), (kw[:, t - wb:], vw[:, t - wb:])


def gather_pages(cache, page_table):
    c = cache[page_table]
    return c.reshape(page_table.shape[0], page_table.shape[1] * cache.shape[1], *cache.shape[2:])


def nsa_sample(z, pos, past, win_k, win_v, page_table, past_len, q_g, k_g, b_gate, pe, w1, w2):
    b, t = z.shape[:2]
    q, kc, vc, ks, vs, kw, vw, gate = nsa_split(z, pos, q_g, k_g, b_gate)
    full = [jnp.concatenate([gather_pages(c, page_table).astype(new.dtype), new], axis=1)
            for c, new in zip(past, (kc, vc, ks, vs))]
    kcmp = compress(full[0], pe[0], w1[0], w2[0])
    vcmp = compress(full[1], pe[1], w1[1], w2[1])
    o_cmp, p_cmp = cmp_attend(q, pos, kcmp, vcmp)
    ksb, vsb = to_blocks(full[2]), to_blocks(full[3])
    idx = select_blocks(p_cmp, pos, ksb.shape[2])
    o_sel = sel_attend(q, pos, idx, ksb, vsb)
    wb = win_k.shape[1]
    kwin = jnp.concatenate([win_k.astype(kw.dtype), kw], axis=1)
    vwin = jnp.concatenate([win_v.astype(vw.dtype), vw], axis=1)
    kpos = past_len - wb + jnp.arange(wb + t, dtype=jnp.int32)
    o_win = window_attend(q, pos, kwin, vwin, kpos)
    o = combine(gate, o_cmp, o_sel, o_win).reshape(b, t, TOK_W)
    return o, (kc, vc, ks, vs), (kwin[:, -wb:], vwin[:, -wb:])


def moe_ffn(h, w_router, b_router, w_gate, w_up, w_down, layer):
    n, d = h.shape
    logits = h.astype(F32) @ w_router.astype(F32)
    aff = jax.nn.softmax(logits, axis=-1)
    sel = (aff + b_router.astype(F32)).reshape(n, N_GROUPS, EXP_PER_GROUP)
    grp_score = lax.top_k(sel, TOP_K)[0].sum(-1)
    g_idx = jnp.argmax(grp_score, axis=-1)
    in_grp = sel[jnp.arange(n), g_idx]
    _, loc = lax.top_k(in_grp, TOP_K)
    experts = g_idx[:, None] * EXP_PER_GROUP + loc
    gates = jnp.take_along_axis(aff, experts, axis=1)
    gates = gates / jnp.sum(gates, axis=-1, keepdims=True)

    flat_e = experts.reshape(-1).astype(jnp.int32)
    n_items = n * TOP_K
    n_tiles = -(-n_items // MOE_TM) + N_EXPERTS
    sizes = jnp.bincount(flat_e, length=N_EXPERTS).astype(jnp.int32)
    padded = -(-sizes // MOE_TM) * MOE_TM
    ends_p = jnp.cumsum(padded)
    starts_p = ends_p - padded
    starts = jnp.cumsum(sizes) - sizes
    order = jnp.argsort(flat_e)
    sorted_e = flat_e[order]
    slot_sorted = starts_p[sorted_e] + jnp.arange(n_items, dtype=jnp.int32) - starts[sorted_e]
    slot_of_item = jnp.zeros((n_items,), jnp.int32).at[order].set(slot_sorted)
    tok_of_slot = jnp.full((n_tiles * MOE_TM,), n, jnp.int32).at[slot_sorted].set((order // TOP_K).astype(jnp.int32))
    tile_expert = jnp.searchsorted(ends_p, jnp.arange(n_tiles, dtype=jnp.int32) * MOE_TM, side='right')
    tile_expert = jnp.minimum(tile_expert, N_EXPERTS - 1).astype(jnp.int32)
    n_used = (ends_p[-1:] // MOE_TM).astype(jnp.int32)

    h_ext = jnp.concatenate([h.astype(BF16), jnp.zeros((1, d), BF16)], axis=0)
    xs = h_ext[tok_of_slot]
    ys = moe_experts(xs, tile_expert, n_used, w_gate, w_up, w_down, layer)
    pos = slot_of_item.reshape(n, TOP_K)
    return ys[pos[:, 0]] * gates[:, 0:1] + ys[pos[:, 1]] * gates[:, 1:2]


def kernel(x_prompt, x_sample, state_pool, cache_cmp_k, cache_cmp_v, cache_sel_k, cache_sel_v, state_win_k, state_win_v, cache_mem_k, cache_mem_v, page_table, mem_prompt, norm_mix_g, norm_ffn_g, norm_mem_g, w_mem_kv, mem_q_norm_g, mem_k_norm_g, w_in_pool, w_pool_grp, pool_scale, w_out_pool, w_in_nsa, b_gate, nsa_q_norm_g, nsa_k_norm_g, cmp_pe, cmp_w1, cmp_w2, w_out_nsa, w_router, b_router, w_gate, w_up, w_down):
    bp, t_p, d = x_prompt.shape
    bs, t_s, _ = x_sample.shape
    n_p, n_s = bp * t_p, bs * t_s
    past_len = page_table.shape[1] * PAGE_SIZE
    pos_p = jnp.arange(t_p, dtype=jnp.int32)
    pos_s = past_len + jnp.arange(t_s, dtype=jnp.int32)
    xp = x_prompt.reshape(n_p, d)
    xs = x_sample.reshape(n_s, d)
    mem_flat = mem_prompt.reshape(-1, d)
    pool_p, pool_s, rows_p, rows_s, win_p, win_s, mem_k_p, mem_v_p = [], [], [], [], [], [], [], []
    for i in range(DEPTH):
        li = i // 2
        kv = proj(mem_flat, w_mem_kv[i].astype(BF16), gain=norm_mem_g[i]).reshape(bp, -1, 2 * MEM_W)
        mk_p = rms_norm(kv[..., :MEM_W].reshape(bp, -1, MEM_HEADS, HEAD_DIM), mem_k_norm_g[i])
        mv_p = kv[..., MEM_W:].reshape(bp, -1, MEM_HEADS, HEAD_DIM)
        mem_k_p.append(mk_p)
        mem_v_p.append(mv_p)
        if i % 2 == 0:
            w_in = w_in_pool[li].astype(BF16)
            zp = proj(xp, w_in, gain=norm_mix_g[i]).reshape(bp, t_p, -1)
            zs = proj(xs, w_in, gain=norm_mix_g[i]).reshape(bs, t_s, -1)
            op, st_p = pool_mix(zp[..., :TOK_W], jnp.zeros((bp, 0, TOK_W), zp.dtype), 0, w_pool_grp[li], pool_scale[li])
            os_, st_s = pool_mix(zs[..., :TOK_W], state_pool[li], past_len, w_pool_grp[li], pool_scale[li])
            pool_p.append(st_p)
            pool_s.append(st_s)
            mq_p, mq_s = zp[..., TOK_W:], zs[..., TOK_W:]
            w_out = w_out_pool[li]
        else:
            n_in = w_in_nsa.shape[-1]
            n_pad = -(-n_in // 128) * 128
            w_in = jnp.pad(w_in_nsa[li], ((0, 0), (0, n_pad - n_in))).astype(BF16)
            zp = proj(xp, w_in, gain=norm_mix_g[i]).reshape(bp, t_p, -1)[..., :n_in]
            zs = proj(xs, w_in, gain=norm_mix_g[i]).reshape(bs, t_s, -1)[..., :n_in]
            op, r_p, wn_p = nsa_prompt(zp[..., :NSA_TOK_IN], pos_p, nsa_q_norm_g[li], nsa_k_norm_g[li], b_gate[li],
                                       cmp_pe[li], cmp_w1[li], cmp_w2[li])
            past = (cache_cmp_k[li], cache_cmp_v[li], cache_sel_k[li], cache_sel_v[li])
            os_, r_s, wn_s = nsa_sample(zs[..., :NSA_TOK_IN], pos_s, past, state_win_k[li], state_win_v[li],
                                        page_table, past_len, nsa_q_norm_g[li], nsa_k_norm_g[li], b_gate[li],
                                        cmp_pe[li], cmp_w1[li], cmp_w2[li])
            rows_p.append(r_p)
            rows_s.append(r_s)
            win_p.append(wn_p)
            win_s.append(wn_s)
            mq_p, mq_s = zp[..., NSA_TOK_IN:], zs[..., NSA_TOK_IN:]
            w_out = w_out_nsa[li]
        ap = mem_attend(mq_p, mk_p, mv_p, mem_q_norm_g[i])
        as_ = mem_attend(mq_s, cache_mem_k[i].astype(xs.dtype), cache_mem_v[i].astype(xs.dtype), mem_q_norm_g[i])
        w_out = w_out.astype(BF16)
        xp = proj(jnp.concatenate([op, ap], axis=-1).reshape(n_p, d), w_out, residual=xp)
        xs = proj(jnp.concatenate([os_, as_], axis=-1).reshape(n_s, d), w_out, residual=xs)
        xa = jnp.concatenate([xp, xs], axis=0)
        ya = moe_ffn(rms_norm(xa, norm_ffn_g[i]), w_router, b_router, w_gate, w_up, w_down, i)
        xp = xp + ya[:n_p]
        xs = xs + ya[n_p:]
    stk = lambda lst, j: jnp.stack([r[j] for r in lst])
    return (xp.reshape(bp, t_p, d), xs.reshape(bs, t_s, d),
            jnp.stack(pool_p), jnp.stack(pool_s),
            stk(rows_p, 0), stk(rows_p, 1), stk(rows_p, 2), stk(rows_p, 3),
            stk(rows_s, 0), stk(rows_s, 1), stk(rows_s, 2), stk(rows_s, 3),
            stk(win_p, 0), stk(win_p, 1), stk(win_s, 0), stk(win_s, 1),
            jnp.stack(mem_k_p), jnp.stack(mem_v_p))
```

```python
import functools

import jax
import jax.numpy as jnp
import numpy as np
from jax import lax
from jax.experimental import pallas as pl
from jax.experimental.pallas import tpu as pltpu

D_MODEL = 2048
DEPTH = 4
PAGE_SIZE = 128
HEAD_DIM = 128
ROPE_THETA = 500000.0
ROPE_DIM = HEAD_DIM // 4
MEM_HEADS = 4
MEM_W = MEM_HEADS * HEAD_DIM
TOK_W = D_MODEL - MEM_W
POOL_WINDOWS = (2, 4, 8, 16)
POOL_GROUP = TOK_W // len(POOL_WINDOWS)
POOL_STATE = max(POOL_WINDOWS) - 1
POOL_HALO = POOL_STATE + 1
NSA_HEADS = TOK_W // HEAD_DIM
NSA_KV_HEADS = 2
NSA_GROUP = NSA_HEADS // NSA_KV_HEADS
KV_W = NSA_KV_HEADS * HEAD_DIM
GROUP_W = NSA_GROUP * HEAD_DIM
CMP_BLOCK = 32
CMP_STRIDE = 16
SEL_BLOCK = 64
SEL_TOP = 16
WINDOW = 512
NSA_TOK_IN = NSA_HEADS * HEAD_DIM + 6 * KV_W + 3 * NSA_HEADS
N_EXPERTS = 16
N_GROUPS = 4
EXP_PER_GROUP = N_EXPERTS // N_GROUPS
TOP_K = 2
D_FF = 1024
EPS = 1e-6
NEG = -1e30
SCALE = HEAD_DIM ** -0.5
LANES = 128

V7X_VMEM_BYTES = 64 * 1024 * 1024
VMEM_LIMIT = V7X_VMEM_BYTES * 3 // 4

BF16 = jnp.bfloat16
F32 = jnp.float32
I32 = jnp.int32

NSA_Q0 = 0
NSA_KV0 = NSA_HEADS * HEAD_DIM
NSA_MQ0 = NSA_KV0 + 6 * KV_W
NSA_GL0 = NSA_MQ0 + MEM_W
NSA_ZW = 3840


def _cparams(sem):
    return pltpu.CompilerParams(dimension_semantics=sem, vmem_limit_bytes=VMEM_LIMIT)


def _nt(a, b):
    return lax.dot_general(a, b, (((1,), (1,)), ((), ())), preferred_element_type=F32)


def _tn(a, b):
    return lax.dot_general(a, b, (((0,), (0,)), ((), ())), preferred_element_type=F32)


def _rms(x, g):
    return x * lax.rsqrt(jnp.mean(x * x, axis=-1, keepdims=True) + EPS) * g


def _pick_tile(n, cap, unit):
    if n <= cap:
        return n
    best = None
    for t in range(unit, cap + 1, unit):
        if n % t == 0:
            best = t
    assert best is not None, (n, cap, unit)
    return best


def _proj_kernel(*refs, n_x, norm, residual):
    x_refs, refs = refs[:n_x], refs[n_x:]
    if norm:
        g_ref, refs = refs[0], refs[1:]
    w_ref, refs = refs[0], refs[1:]
    if residual:
        r_ref, refs = refs[0], refs[1:]
    o_ref, xn_ref = refs

    @pl.when(pl.program_id(1) == 0)
    def _():
        off = 0
        for x_ref in x_refs:
            x = x_ref[...].astype(F32)
            if norm:
                x = _rms(x, g_ref[...])
            xn_ref[:, off:off + x.shape[1]] = x.astype(BF16)
            off += x.shape[1]

    y = jnp.dot(xn_ref[...], w_ref[...], preferred_element_type=F32)
    if residual:
        y = y + r_ref[...]
    o_ref[...] = y


def proj(xs, w_bf16, gain=None, residual=None):
    if not isinstance(xs, (list, tuple)):
        xs = [xs]
    assert gain is None or len(xs) == 1
    m = xs[0].shape[0]
    k = sum(x.shape[1] for x in xs)
    n = w_bf16.shape[1]
    tm = _pick_tile(m, 512, 8)
    tn = _pick_tile(n, 1024, LANES)
    in_specs = [pl.BlockSpec((tm, x.shape[1]), lambda i, j: (i, 0)) for x in xs]
    args = list(xs)
    if gain is not None:
        in_specs.append(pl.BlockSpec((1, k), lambda i, j: (0, 0)))
        args.append(gain.reshape(1, k).astype(F32))
    in_specs.append(pl.BlockSpec((k, tn), lambda i, j: (0, j)))
    args.append(w_bf16)
    if residual is not None:
        in_specs.append(pl.BlockSpec((tm, tn), lambda i, j: (i, j)))
        args.append(residual)
    return pl.pallas_call(
        functools.partial(_proj_kernel, n_x=len(xs), norm=gain is not None, residual=residual is not None),
        grid=(m // tm, n // tn),
        in_specs=in_specs,
        out_specs=pl.BlockSpec((tm, tn), lambda i, j: (i, j)),
        out_shape=jax.ShapeDtypeStruct((m, n), F32),
        scratch_shapes=[pltpu.VMEM((tm, k), BF16)],
        compiler_params=_cparams(("parallel", "arbitrary")),
        name="proj",
    )(*args)


def _head_norm_kernel(x_ref, g_ref, o_ref):
    x = x_ref[...]
    g = g_ref[...]
    n_heads = x.shape[1] // HEAD_DIM
    o_ref[...] = jnp.concatenate(
        [_rms(x[:, h * HEAD_DIM:(h + 1) * HEAD_DIM], g) for h in range(n_heads)], axis=1)


def head_norm(x, col0, width, gain):
    m = x.shape[0]
    assert col0 % width == 0
    return pl.pallas_call(
        _head_norm_kernel,
        grid=(1,),
        in_specs=[pl.BlockSpec((m, width), lambda i: (0, col0 // width)),
                  pl.BlockSpec((1, HEAD_DIM), lambda i: (0, 0))],
        out_specs=pl.BlockSpec((m, width), lambda i: (0, 0)),
        out_shape=jax.ShapeDtypeStruct((m, width), F32),
        compiler_params=_cparams(("arbitrary",)),
        name="head_norm",
    )(x, gain.reshape(1, HEAD_DIM))


def _mem_attn_kernel(q_ref, k_ref, v_ref, g_ref, o_ref):
    q = q_ref[...]
    rows = q.shape[0]
    if rows < 8:
        q = jnp.broadcast_to(q[0:1], (8, q.shape[1]))
    g = g_ref[...]
    outs = []
    for h in range(MEM_HEADS):
        sl = slice(h * HEAD_DIM, (h + 1) * HEAD_DIM)
        qh = _rms(q[:, sl], g).astype(BF16)
        s = _nt(qh, k_ref[:, sl].astype(BF16)) * SCALE
        m = jnp.max(s, axis=-1, keepdims=True)
        e = jnp.exp(s - m)
        p = e / jnp.sum(e, axis=-1, keepdims=True)
        outs.append(jnp.dot(p.astype(BF16), v_ref[:, sl].astype(BF16), preferred_element_type=F32))
    o = jnp.concatenate(outs, axis=1)
    o_ref[...] = o[:rows]


def mem_attention(z, col0, k, v, gain):
    b, t, _ = z.shape
    m = k.shape[1]
    tq = _pick_tile(t, 512, 8)
    assert col0 % MEM_W == 0
    return pl.pallas_call(
        _mem_attn_kernel,
        grid=(b, t // tq),
        in_specs=[pl.BlockSpec((None, tq, MEM_W), lambda bi, i: (bi, i, col0 // MEM_W)),
                  pl.BlockSpec((None, m, MEM_W), lambda bi, i: (bi, 0, 0)),
                  pl.BlockSpec((None, m, MEM_W), lambda bi, i: (bi, 0, 0)),
                  pl.BlockSpec((1, HEAD_DIM), lambda bi, i: (0, 0))],
        out_specs=pl.BlockSpec((None, tq, MEM_W), lambda bi, i: (bi, i, 0)),
        out_shape=jax.ShapeDtypeStruct((b, t, MEM_W), F32),
        compiler_params=_cparams(("parallel", "arbitrary")),
        name="mem_attention",
    )(z, k, v, gain.reshape(1, HEAD_DIM))


def _pool_kernel(u_ref, halo_ref, w_ref, sc_ref, o_ref, *, pos0, zero_first_halo):
    tq = u_ref.shape[0]
    qi = pl.program_id(1)
    u = u_ref[...]
    halo = halo_ref[...]
    if zero_first_halo:
        halo = jnp.where(qi > 0, halo, 0.0)
    pos = pos0 + qi * tq + lax.broadcasted_iota(I32, (tq, 1), 0)
    outs = []
    for g, w in enumerate(POOL_WINDOWS):
        cs = slice(g * POOL_GROUP, (g + 1) * POOL_GROUP)
        ug = u[:, cs]
        acc = jnp.concatenate([halo[:, cs], ug], axis=0)
        span = 1
        while span < w:
            acc = acc[span:] + acc[:-span]
            span *= 2
        ssum = acc[POOL_HALO - (w - 1):POOL_HALO - (w - 1) + tq]
        cnt = jnp.minimum(w, pos + 1).astype(F32)
        d = (ssum / cnt - ug).astype(BF16)
        y = jnp.dot(d, w_ref[g].astype(BF16), preferred_element_type=F32)
        outs.append(y)
    o_ref[...] = jnp.concatenate(outs, axis=1) * sc_ref[...]


def pool_mix(z, halo, w_grp, scale, pos0):
    b, t, _ = z.shape
    tq = _pick_tile(t, 256, 16)
    if halo is None:
        halo_arr = z
        halo_spec = pl.BlockSpec((None, POOL_HALO, TOK_W),
                                 lambda bi, i: (bi, jnp.maximum(i * (tq // POOL_HALO) - 1, 0), 0))
    else:
        assert t == tq
        halo_arr = halo
        halo_spec = pl.BlockSpec((None, POOL_HALO, TOK_W), lambda bi, i: (bi, 0, 0))
    return pl.pallas_call(
        functools.partial(_pool_kernel, pos0=pos0, zero_first_halo=halo is None),
        grid=(b, t // tq),
        in_specs=[pl.BlockSpec((None, tq, TOK_W), lambda bi, i: (bi, i, 0)),
                  halo_spec,
                  pl.BlockSpec(w_grp.shape, lambda bi, i: (0, 0, 0)),
                  pl.BlockSpec((1, TOK_W), lambda bi, i: (0, 0))],
        out_specs=pl.BlockSpec((None, tq, TOK_W), lambda bi, i: (bi, i, 0)),
        out_shape=jax.ShapeDtypeStruct((b, t, TOK_W), F32),
        compiler_params=_cparams(("parallel", "arbitrary")),
        name="pool_mix",
    )(z, halo_arr, w_grp, scale.reshape(1, TOK_W))


def _rope_tables(pos):
    half = ROPE_DIM // 2
    inv = 1.0 / (ROPE_THETA ** (jnp.arange(half, dtype=F32) * 2.0 / ROPE_DIM))
    ang = pos.astype(F32)[:, None] * inv[None, :]
    cos, sin = jnp.cos(ang), jnp.sin(ang)
    t = pos.shape[0]
    rest = HEAD_DIM - ROPE_DIM
    c = jnp.concatenate([cos, cos, jnp.ones((t, rest), F32)], axis=1)
    s_lo = jnp.concatenate([-sin, jnp.zeros((t, HEAD_DIM - half), F32)], axis=1)
    s_hi = jnp.concatenate([jnp.zeros((t, half), F32), sin, jnp.zeros((t, rest), F32)], axis=1)
    return c, s_lo, s_hi


def _nsa_prep_kernel(z_ref, c_ref, slo_ref, shi_ref, qg_ref, kg_ref, bg_ref,
                     q_ref, kc_ref, vc_ref, ks_ref, vs_ref, kw_ref, vw_ref, gate_ref):
    rows = z_ref.shape[0]
    c, slo, shi = c_ref[...], slo_ref[...], shi_ref[...]
    half = ROPE_DIM // 2

    def rope_norm(x, g):
        if rows < 8:
            x = jnp.broadcast_to(x[0:1], (8, HEAD_DIM))
        x = _rms(x, g)
        y = x * c + pltpu.roll(x, HEAD_DIM - half, 1) * slo + pltpu.roll(x, half, 1) * shi
        return y[:rows]

    qg = qg_ref[...]
    q_ref[...] = jnp.concatenate(
        [rope_norm(z_ref[:, NSA_Q0 + h * HEAD_DIM:NSA_Q0 + (h + 1) * HEAD_DIM], qg) for h in range(NSA_HEADS)], axis=1)
    for j, (o_ref, which) in enumerate(((kc_ref, 0), (vc_ref, None), (ks_ref, 1), (vs_ref, None), (kw_ref, 2), (vw_ref, None))):
        c0 = NSA_KV0 + j * KV_W
        if which is None:
            o_ref[...] = z_ref[:, c0:c0 + KV_W]
        else:
            g = kg_ref[which:which + 1, :]
            o_ref[...] = jnp.concatenate(
                [rope_norm(z_ref[:, c0 + h * HEAD_DIM:c0 + (h + 1) * HEAD_DIM], g) for h in range(NSA_KV_HEADS)], axis=1)
    gate_ref[...] = jax.nn.sigmoid(z_ref[:, NSA_GL0:NSA_GL0 + LANES] + bg_ref[...])


def nsa_prep(z, pos, q_g, k_g, b_gate):
    b, t, _ = z.shape
    tq = _pick_tile(t, 256, 8)
    c, slo, shi = _rope_tables(pos)
    if t < 8:
        c, slo, shi = (jnp.broadcast_to(a, (8, HEAD_DIM)) for a in (c, slo, shi))
    tt = max(tq, 8)
    bg = jnp.pad(b_gate.reshape(1, -1), ((0, 0), (0, LANES - b_gate.shape[-1])))
    tab_spec = pl.BlockSpec((tt, HEAD_DIM), lambda bi, i: (i, 0))
    kv_spec = pl.BlockSpec((None, tq, KV_W), lambda bi, i: (bi, i, 0))
    kv_shape = jax.ShapeDtypeStruct((b, t, KV_W), F32)
    return pl.pallas_call(
        _nsa_prep_kernel,
        grid=(b, t // tq),
        in_specs=[pl.BlockSpec((None, tq, NSA_ZW), lambda bi, i: (bi, i, 0)),
                  tab_spec, tab_spec, tab_spec,
                  pl.BlockSpec((1, HEAD_DIM), lambda bi, i: (0, 0)),
                  pl.BlockSpec((3, HEAD_DIM), lambda bi, i: (0, 0)),
                  pl.BlockSpec((1, LANES), lambda bi, i: (0, 0))],
        out_specs=[pl.BlockSpec((None, tq, TOK_W), lambda bi, i: (bi, i, 0))] + [kv_spec] * 6
                  + [pl.BlockSpec((None, tq, LANES), lambda bi, i: (bi, i, 0))],
        out_shape=[jax.ShapeDtypeStruct((b, t, TOK_W), F32)] + [kv_shape] * 6
                  + [jax.ShapeDtypeStruct((b, t, LANES), F32)],
        compiler_params=_cparams(("parallel", "arbitrary")),
        name="nsa_prep",
    )(z, c, slo, shi, q_g.reshape(1, HEAD_DIM), k_g, bg)


def _compress_body(x, pe_ref, w1_ref, w2_ref, o_ref):
    n_chunks = x.shape[0]
    pe = pe_ref[...]
    pe_lo = jnp.concatenate([pe[r:r + 1] for r in range(CMP_STRIDE)], axis=1)
    pe_hi = jnp.concatenate([pe[CMP_STRIDE + r:CMP_STRIDE + r + 1] for r in range(CMP_STRIDE)], axis=1)
    half_k = CMP_STRIDE * HEAD_DIM
    w_lo = w1_ref[0:half_k, :].astype(BF16)
    w_hi = w1_ref[half_k:2 * half_k, :].astype(BF16)
    w2 = w2_ref[...].astype(BF16)
    for h in range(NSA_KV_HEADS):
        xh = jnp.concatenate(
            [x[:, (r * NSA_KV_HEADS + h) * HEAD_DIM:(r * NSA_KV_HEADS + h + 1) * HEAD_DIM] for r in range(CMP_STRIDE)], axis=1)
        a = jnp.dot((xh + pe_lo).astype(BF16), w_lo, preferred_element_type=F32)
        bb = jnp.dot((xh + pe_hi).astype(BF16), w_hi, preferred_element_type=F32)
        hid = jax.nn.gelu(a + pltpu.roll(bb, n_chunks - 1, 0))
        o_ref[h] = jnp.dot(hid.astype(BF16), w2, preferred_element_type=F32)


def _compress_kernel(x_ref, pe_ref, w1_ref, w2_ref, o_ref):
    _compress_body(x_ref[...], pe_ref, w1_ref, w2_ref, o_ref)


def compress_rows(x, pe, w1, w2):
    b, t, _ = x.shape
    n_chunks = t // CMP_STRIDE
    cw = CMP_STRIDE * KV_W
    xc = x.reshape(b, n_chunks, cw)
    return pl.pallas_call(
        _compress_kernel,
        grid=(b,),
        in_specs=[pl.BlockSpec((None, n_chunks, cw), lambda bi: (bi, 0, 0)),
                  pl.BlockSpec(pe.shape, lambda bi: (0, 0)),
                  pl.BlockSpec(w1.shape, lambda bi: (0, 0)),
                  pl.BlockSpec(w2.shape, lambda bi: (0, 0))],
        out_specs=pl.BlockSpec((None, NSA_KV_HEADS, n_chunks, HEAD_DIM), lambda bi: (bi, 0, 0, 0)),
        out_shape=jax.ShapeDtypeStruct((b, NSA_KV_HEADS, n_chunks, HEAD_DIM), F32),
        compiler_params=_cparams(("parallel",)),
        name="compress_rows",
    )(xc, pe, w1, w2)


def _stack_heads(q):
    return jnp.concatenate([q[:, g * HEAD_DIM:(g + 1) * HEAD_DIM] for g in range(NSA_GROUP)], axis=0)


def _unstack_heads(o, tq):
    return jnp.concatenate([o[g * tq:(g + 1) * tq] for g in range(NSA_GROUP)], axis=1)


def _rank_select(imp_t, n_real):
    n_blk = imp_t.shape[0]
    blk = lax.broadcasted_iota(I32, imp_t.shape, 0)
    cnt = jnp.zeros(imp_t.shape, F32)
    for j in range(n_real):
        row = imp_t[j:j + 1, :]
        beats = jnp.where(row > imp_t, 1.0, jnp.where(row == imp_t, jnp.where(blk > j, 1.0, 0.0), 0.0))
        cnt = cnt + beats
    return jnp.where(cnt < SEL_TOP, 1.0, 0.0)


def _cmp_select_kernel(q_ref, kc_ref, vc_ref, cov_ref, o_ref, sel_ref):
    tq = q_ref.shape[0]
    nc = kc_ref.shape[0]
    n_blk = sel_ref.shape[0]
    qi = pl.program_id(2)
    q6 = _stack_heads(q_ref[...]).astype(BF16)
    s = _nt(q6, kc_ref[...].astype(BF16)) * SCALE
    s = s.reshape(NSA_GROUP, tq, nc)
    q_pos = qi * tq + lax.broadcasted_iota(I32, (tq, nc), 0)
    c_last = lax.broadcasted_iota(I32, (tq, nc), 1) * CMP_STRIDE + (CMP_BLOCK - 1)
    valid = (c_last <= q_pos)[None]
    s = jnp.where(valid, s, NEG)
    m = jnp.max(s, axis=-1, keepdims=True)
    e = jnp.where(valid, jnp.exp(s - m), 0.0)
    p = (e / jnp.maximum(jnp.sum(e, axis=-1, keepdims=True), 1e-30)).astype(BF16)
    o = jnp.dot(p.reshape(NSA_GROUP * tq, nc), vc_ref[...].astype(BF16), preferred_element_type=F32)
    o_ref[...] = _unstack_heads(o, tq)
    p_cat = jnp.concatenate([p[g] for g in range(NSA_GROUP)], axis=1)
    imp_t = _nt(cov_ref[...], p_cat)
    blk = lax.broadcasted_iota(I32, (n_blk, tq), 0)
    pos_t = qi * tq + lax.broadcasted_iota(I32, (n_blk, tq), 1)
    cur = pos_t // SEL_BLOCK
    forced = (blk == 0) | (blk == cur) | (blk == cur - 1)
    future = blk * SEL_BLOCK > pos_t
    imp_t = jnp.where(future, -1e9, jnp.where(forced, 1e9, imp_t))
    sel_ref[...] = _rank_select(imp_t, n_blk).astype(BF16)


def _cover_t(nc, n_blk, nc_valid):
    c_start = np.arange(nc) * CMP_STRIDE
    c_last = c_start + CMP_BLOCK - 1
    s_start = np.arange(n_blk) * SEL_BLOCK
    cov = (c_start[None, :] < s_start[:, None] + SEL_BLOCK) & (c_last[None, :] >= s_start[:, None])
    cov = cov & (np.arange(nc)[None, :] < nc_valid)
    return np.tile(cov.astype(np.float32), (1, NSA_GROUP))


def cmp_select_prompt(q, kcmp, vcmp):
    b, t, _ = q.shape
    nc = kcmp.shape[2]
    n_blk = -(-t // SEL_BLOCK)
    tq = _pick_tile(t, 256, LANES)
    cov = jnp.asarray(_cover_t(nc, n_blk, nc - 1), BF16)
    return pl.pallas_call(
        _cmp_select_kernel,
        grid=(b, NSA_KV_HEADS, t // tq),
        in_specs=[pl.BlockSpec((None, tq, GROUP_W), lambda bi, h, i: (bi, i, h)),
                  pl.BlockSpec((None, None, nc, HEAD_DIM), lambda bi, h, i: (bi, h, 0, 0)),
                  pl.BlockSpec((None, None, nc, HEAD_DIM), lambda bi, h, i: (bi, h, 0, 0)),
                  pl.BlockSpec(cov.shape, lambda bi, h, i: (0, 0))],
        out_specs=[pl.BlockSpec((None, tq, GROUP_W), lambda bi, h, i: (bi, i, h)),
                   pl.BlockSpec((None, None, n_blk, tq), lambda bi, h, i: (bi, h, 0, i))],
        out_shape=[jax.ShapeDtypeStruct((b, t, TOK_W), F32),
                   jax.ShapeDtypeStruct((b, NSA_KV_HEADS, n_blk, t), BF16)],
        compiler_params=_cparams(("parallel", "parallel", "arbitrary")),
        name="cmp_select_prompt",
    )(q, kcmp, vcmp, cov)


SEL_TQ = 128
SEL_TK = 512


def _sel_kernel(q_ref, k_ref, v_ref, sel_ref, exp_ref, o_ref, mask_ref, m_ref, l_ref, acc_ref):
    t = k_ref.shape[0]
    qi = pl.program_id(2)
    q6 = _stack_heads(q_ref[...]).astype(BF16)
    mask_ref[...] = _tn(sel_ref[...], exp_ref[...])
    m_ref[...] = jnp.full(m_ref.shape, NEG, F32)
    l_ref[...] = jnp.zeros(l_ref.shape, F32)
    acc_ref[...] = jnp.zeros(acc_ref.shape, F32)
    q_pos = qi * SEL_TQ + lax.broadcasted_iota(I32, (SEL_TQ, SEL_TK), 0)

    for j in range(t // SEL_TK):
        @pl.when(j * SEL_TK <= qi * SEL_TQ + SEL_TQ - 1)
        def _():
            k = k_ref[j * SEL_TK:(j + 1) * SEL_TK, :].astype(BF16)
            v = v_ref[j * SEL_TK:(j + 1) * SEL_TK, :].astype(BF16)
            s = _nt(q6, k) * SCALE
            k_pos = j * SEL_TK + lax.broadcasted_iota(I32, (SEL_TQ, SEL_TK), 1)
            valid = jnp.where(k_pos <= q_pos, mask_ref[:, j * SEL_TK:(j + 1) * SEL_TK], 0.0) > 0.5
            s = s.reshape(NSA_GROUP, SEL_TQ, SEL_TK)
            s = jnp.where(valid[None], s, NEG)
            m_old = m_ref[...]
            m_new = jnp.maximum(m_old, jnp.max(s, axis=-1, keepdims=True))
            e = jnp.where(valid[None], jnp.exp(s - m_new), 0.0)
            alpha = jnp.exp(m_old - m_new)
            l_ref[...] = alpha * l_ref[...] + jnp.sum(e, axis=-1, keepdims=True)
            pv = jnp.dot(e.reshape(NSA_GROUP * SEL_TQ, SEL_TK).astype(BF16), v, preferred_element_type=F32)
            acc_ref[...] = alpha * acc_ref[...] + pv.reshape(NSA_GROUP, SEL_TQ, HEAD_DIM)
            m_ref[...] = m_new

    o = acc_ref[...] / jnp.maximum(l_ref[...], 1e-30)
    o_ref[...] = jnp.concatenate([o[g] for g in range(NSA_GROUP)], axis=-1)


def sel_attention_prompt(q, ks, vs, sel_t):
    b, t, _ = q.shape
    n_blk = sel_t.shape[2]
    expand = (np.arange(t)[None, :] // SEL_BLOCK == np.arange(n_blk)[:, None]).astype(np.float32)
    expand = jnp.asarray(expand, BF16)
    return pl.pallas_call(
        _sel_kernel,
        grid=(b, NSA_KV_HEADS, t // SEL_TQ),
        in_specs=[pl.BlockSpec((None, SEL_TQ, GROUP_W), lambda bi, h, i: (bi, i, h)),
                  pl.BlockSpec((None, t, HEAD_DIM), lambda bi, h, i: (bi, 0, h)),
                  pl.BlockSpec((None, t, HEAD_DIM), lambda bi, h, i: (bi, 0, h)),
                  pl.BlockSpec((None, None, n_blk, SEL_TQ), lambda bi, h, i: (bi, h, 0, i)),
                  pl.BlockSpec((n_blk, t), lambda bi, h, i: (0, 0))],
        out_specs=pl.BlockSpec((None, SEL_TQ, GROUP_W), lambda bi, h, i: (bi, i, h)),
        out_shape=jax.ShapeDtypeStruct((b, t, TOK_W), F32),
        scratch_shapes=[pltpu.VMEM((SEL_TQ, t), F32),
                        pltpu.VMEM((NSA_GROUP, SEL_TQ, 1), F32),
                        pltpu.VMEM((NSA_GROUP, SEL_TQ, 1), F32),
                        pltpu.VMEM((NSA_GROUP, SEL_TQ, HEAD_DIM), F32)],
        compiler_params=_cparams(("parallel", "parallel", "arbitrary")),
        name="sel_attention_prompt",
    )(q, ks, vs, sel_t, expand)


WIN_TQ = 128


def _gate_cols(gate, h, j, tq):
    cols = []
    for g in range(NSA_GROUP):
        c = (h * NSA_GROUP + g) * 3 + j
        cols.append(jnp.broadcast_to(gate[:, c:c + 1], (tq, HEAD_DIM)))
    return jnp.concatenate(cols, axis=1)


def _win_combine_kernel(q_ref, k_ref, v_ref, oc_ref, os_ref, gate_ref, o_ref):
    tq = q_ref.shape[0]
    span = WINDOW + tq
    h = pl.program_id(1)
    qi = pl.program_id(2)
    start = pl.multiple_of(jnp.maximum(qi * tq - WINDOW, 0), tq)
    q6 = _stack_heads(q_ref[...]).astype(BF16)
    k = k_ref[pl.ds(start, span), :].astype(BF16)
    v = v_ref[pl.ds(start, span), :].astype(BF16)
    s = _nt(q6, k).reshape(NSA_GROUP, tq, span) * SCALE
    diff = (qi * tq + lax.broadcasted_iota(I32, (tq, span), 0)) - (start + lax.broadcasted_iota(I32, (tq, span), 1))
    valid = ((diff >= 0) & (diff < WINDOW))[None]
    s = jnp.where(valid, s, NEG)
    m = jnp.max(s, axis=-1, keepdims=True)
    e = jnp.where(valid, jnp.exp(s - m), 0.0)
    p = e / jnp.maximum(jnp.sum(e, axis=-1, keepdims=True), 1e-30)
    o = jnp.dot(p.reshape(NSA_GROUP * tq, span).astype(BF16), v, preferred_element_type=F32)
    o_win = _unstack_heads(o, tq)
    gate = gate_ref[...]
    for hh in range(NSA_KV_HEADS):
        @pl.when(h == hh)
        def _():
            o_ref[...] = (_gate_cols(gate, hh, 0, tq) * oc_ref[...] + _gate_cols(gate, hh, 1, tq) * os_ref[...]
                          + _gate_cols(gate, hh, 2, tq) * o_win)


def win_combine_prompt(q, kw, vw, o_cmp, o_sel, gate):
    b, t, _ = q.shape
    assert t >= WINDOW + WIN_TQ
    qspec = pl.BlockSpec((None, WIN_TQ, GROUP_W), lambda bi, h, i: (bi, i, h))
    kspec = pl.BlockSpec((None, t, HEAD_DIM), lambda bi, h, i: (bi, 0, h))
    return pl.pallas_call(
        _win_combine_kernel,
        grid=(b, NSA_KV_HEADS, t // WIN_TQ),
        in_specs=[qspec, kspec, kspec, qspec, qspec,
                  pl.BlockSpec((None, WIN_TQ, LANES), lambda bi, h, i: (bi, i, 0))],
        out_specs=qspec,
        out_shape=jax.ShapeDtypeStruct((b, t, TOK_W), F32),
        compiler_params=_cparams(("parallel", "parallel", "arbitrary")),
        name="win_combine_prompt",
    )(q, kw, vw, o_cmp, o_sel, gate)


def _router_kernel(x_ref, g_ref, wt_ref, b_ref, xn_ref, e_ref, gt_ref):
    xn = _rms(x_ref[...], g_ref[...])
    xn_ref[...] = xn
    lt = _nt(wt_ref[...], xn.astype(BF16))
    ex = jnp.exp(lt - jnp.max(lt, axis=0, keepdims=True))
    aff = ex / jnp.sum(ex, axis=0, keepdims=True)
    sel = aff + b_ref[...]
    row = lambda a, r: a[r:r + 1, :]

    best, g_idx = None, None
    for g in range(N_GROUPS):
        a, b, c, d = (row(sel, g * EXP_PER_GROUP + j) for j in range(EXP_PER_GROUP))
        hi1, lo1, hi2, lo2 = jnp.maximum(a, b), jnp.minimum(a, b), jnp.maximum(c, d), jnp.minimum(c, d)
        score = jnp.maximum(hi1, hi2) + jnp.maximum(jnp.minimum(hi1, hi2), jnp.maximum(lo1, lo2))
        if g == 0:
            best, g_idx = score, jnp.zeros(score.shape, I32)
        else:
            g_idx = jnp.where(score > best, g, g_idx)
            best = jnp.maximum(best, score)

    def in_group(a, j):
        out = row(a, j)
        for g in range(1, N_GROUPS):
            out = jnp.where(g_idx == g, row(a, g * EXP_PER_GROUP + j), out)
        return out

    v = [in_group(sel, j) for j in range(EXP_PER_GROUP)]
    af = [in_group(aff, j) for j in range(EXP_PER_GROUP)]

    def first_max(vals):
        m = functools.reduce(jnp.maximum, vals)
        loc = jnp.full(m.shape, EXP_PER_GROUP - 1, I32)
        for j in range(EXP_PER_GROUP - 2, -1, -1):
            loc = jnp.where(vals[j] == m, j, loc)
        return loc

    l1 = first_max(v)
    l2 = first_max([jnp.where(l1 == j, -jnp.inf, v[j]) for j in range(EXP_PER_GROUP)])
    pick = lambda loc: functools.reduce(lambda acc, j: jnp.where(loc == j, af[j], acc), range(1, EXP_PER_GROUP), af[0])
    a1, a2 = pick(l1), pick(l2)
    tot = a1 + a2
    e_ref[...] = jnp.concatenate([g_idx * EXP_PER_GROUP + l1, g_idx * EXP_PER_GROUP + l2], axis=0)
    gt_ref[...] = jnp.concatenate([a1 / tot, a2 / tot], axis=0)


def router(x, gain, w_router, b_router):
    n, d = x.shape
    tm = _pick_tile(n, 1024, LANES)
    return pl.pallas_call(
        _router_kernel,
        grid=(n // tm,),
        in_specs=[pl.BlockSpec((tm, d), lambda i: (i, 0)),
                  pl.BlockSpec((1, d), lambda i: (0, 0)),
                  pl.BlockSpec((N_EXPERTS, d), lambda i: (0, 0)),
                  pl.BlockSpec((N_EXPERTS, 1), lambda i: (0, 0))],
        out_specs=[pl.BlockSpec((tm, d), lambda i: (i, 0)),
                   pl.BlockSpec((TOP_K, tm), lambda i: (0, i)),
                   pl.BlockSpec((TOP_K, tm), lambda i: (0, i))],
        out_shape=[jax.ShapeDtypeStruct((n, d), F32),
                   jax.ShapeDtypeStruct((TOP_K, n), I32),
                   jax.ShapeDtypeStruct((TOP_K, n), F32)],
        compiler_params=_cparams(("parallel",)),
        name="router",
    )(x, gain.reshape(1, d), w_router.T.astype(BF16), b_router.reshape(N_EXPERTS, 1).astype(F32))


MOE_TM = 512
MOE_NF = 4
MOE_TF = D_FF // MOE_NF
MOE_ROWS = MOE_TM // MOE_NF


def _moe_kernel(te_ref, nu_ref, src_ref, dst_ref, x_hbm, wg_ref, wu_ref, wd_ref, y_hbm,
                xbuf, xb, ybuf, gsem, ssem):
    i = pl.program_id(0)
    f = pl.program_id(1)
    n_used = nu_ref[0]
    d = xbuf.shape[-1]

    def gather_copy(tile, buf, r):
        row = src_ref[tile * MOE_TM + r]
        return pltpu.make_async_copy(x_hbm.at[pl.ds(row, 1), :], xbuf.at[buf, pl.ds(r, 1), :], gsem.at[buf])

    def scatter_copy(tile, buf, r):
        row = dst_ref[tile * MOE_TM + r]
        return pltpu.make_async_copy(ybuf.at[buf, pl.ds(r, 1), :], y_hbm.at[pl.ds(row, 1), :], ssem.at[buf])

    def wait_all(hbm, vmem_buf, sem):
        pltpu.make_async_copy(hbm.at[pl.ds(0, MOE_TM), :], vmem_buf, sem).wait()

    @pl.when((i == 0) & (f == 0))
    def _():
        def body(r, carry):
            gather_copy(0, 0, r).start()
            return carry
        lax.fori_loop(0, MOE_TM, body, 0)

    nxt = i + 1
    prv = i - 1

    @pl.when(nxt < n_used)
    def _():
        for r in range(MOE_ROWS):
            gather_copy(nxt, nxt % 2, f * MOE_ROWS + r).start()

    @pl.when((prv >= 0) & (prv < n_used))
    def _():
        for r in range(MOE_ROWS):
            scatter_copy(prv, prv % 2, f * MOE_ROWS + r).start()

    @pl.when(i < n_used)
    def _():
        cur = i % 2

        @pl.when(f == 0)
        def _():
            wait_all(x_hbm, xbuf.at[cur], gsem.at[cur])
            xb[...] = xbuf[cur].astype(BF16)

        x = xb[...]
        a = jnp.dot(x, wg_ref[...].astype(BF16), preferred_element_type=F32)
        u = jnp.dot(x, wu_ref[...].astype(BF16), preferred_element_type=F32)
        h = (a * jax.nn.sigmoid(a) * u).astype(BF16)
        y = jnp.dot(h, wd_ref[...].astype(BF16), preferred_element_type=F32)

        @pl.when(f == 0)
        def _():
            ybuf[cur] = y

        @pl.when(f > 0)
        def _():
            ybuf[cur] += y

    @pl.when((f == MOE_NF - 1) & (prv >= 0) & (prv < n_used))
    def _():
        wait_all(y_hbm, ybuf.at[prv % 2], ssem.at[prv % 2])


def moe_experts(xn, tile_expert, n_used, src_row, dst_row, n_out_rows, w_gate, w_up, w_down, layer):
    n, d = xn.shape
    n_tiles = tile_expert.shape[0]

    def tile(i, nu):
        return jnp.minimum(i, nu[0] - 1)

    def fcol(i, f, nu):
        return jnp.where(i < nu[0], f, MOE_NF - 1)

    wspec = lambda blk, im: pl.BlockSpec(blk, im)
    grid_spec = pltpu.PrefetchScalarGridSpec(
        num_scalar_prefetch=4,
        grid=(n_tiles, MOE_NF),
        in_specs=[
            pl.BlockSpec(memory_space=pl.ANY),
            wspec((None, None, d, MOE_TF), lambda i, f, te, nu, s, t: (layer, te[tile(i, nu)], 0, fcol(i, f, nu))),
            wspec((None, None, d, MOE_TF), lambda i, f, te, nu, s, t: (layer, te[tile(i, nu)], 0, fcol(i, f, nu))),
            wspec((None, None, MOE_TF, d), lambda i, f, te, nu, s, t: (layer, te[tile(i, nu)], fcol(i, f, nu), 0)),
        ],
        out_specs=pl.BlockSpec(memory_space=pl.ANY),
        scratch_shapes=[pltpu.VMEM((2, MOE_TM, d), F32),
                        pltpu.VMEM((MOE_TM, d), BF16),
                        pltpu.VMEM((2, MOE_TM, d), F32),
                        pltpu.SemaphoreType.DMA((2,)),
                        pltpu.SemaphoreType.DMA((2,))],
    )
    return pl.pallas_call(
        _moe_kernel,
        grid_spec=grid_spec,
        out_shape=jax.ShapeDtypeStruct((n_out_rows, d), F32),
        compiler_params=_cparams(("arbitrary", "arbitrary")),
        name="moe_experts",
    )(tile_expert, n_used, src_row, dst_row, xn, w_gate, w_up, w_down)


def moe_ffn(x, xn, experts, gates, w_gate, w_up, w_down, layer):
    n, d = x.shape
    n_items = n * TOP_K
    n_tiles = -(-n_items // MOE_TM) + N_EXPERTS
    n_slots = n_tiles * MOE_TM
    flat_e = experts.reshape(-1)
    sizes = jnp.sum((flat_e[:, None] == jnp.arange(N_EXPERTS, dtype=I32)[None, :]).astype(I32), axis=0)
    padded = -(-sizes // MOE_TM) * MOE_TM
    ends_p = jnp.cumsum(padded)
    starts_p = ends_p - padded
    starts = jnp.cumsum(sizes) - sizes
    order = jnp.argsort(flat_e).astype(I32)
    tile_expert = jnp.searchsorted(ends_p, jnp.arange(n_tiles, dtype=I32) * MOE_TM, side='right')
    tile_expert = jnp.minimum(tile_expert, N_EXPERTS - 1).astype(I32)
    n_used = (ends_p[-1:] // MOE_TM).astype(I32)
    slot = jnp.arange(n_slots, dtype=I32)
    slot_e = jnp.repeat(tile_expert, MOE_TM)
    within = slot - starts_p[slot_e]
    is_item = within < sizes[slot_e]
    item = order[jnp.clip(starts[slot_e] + within, 0, n_items - 1)]
    src_row = jnp.where(is_item, item % n, 0).astype(I32)
    dst_row = jnp.where(is_item, item, n_items + slot).astype(I32)
    y = moe_experts(xn, tile_expert, n_used, src_row, dst_row, n_items + n_slots, w_gate, w_up, w_down, layer)
    g = gates.astype(F32)
    return x + y[:n] * g[0][:, None] + y[n:n_items] * g[1][:, None]


def rms_norm(x, g):
    xf = x.astype(F32)
    y = xf * lax.rsqrt(jnp.mean(xf * xf, axis=-1, keepdims=True) + EPS)
    return (y * g.astype(F32)).astype(x.dtype)


def masked_softmax(s, valid):
    s = jnp.where(valid, s, NEG)
    m = jnp.max(s, axis=-1, keepdims=True)
    e = jnp.where(valid, jnp.exp(s - m), 0.0)
    return e / jnp.maximum(jnp.sum(e, axis=-1, keepdims=True), 1e-30)


def compress(k, pe, w1, w2):
    b, l = k.shape[:2]
    r = CMP_BLOCK // CMP_STRIDE
    n_chunks = l // CMP_STRIDE
    nc = n_chunks - r + 1
    c = k[:, :n_chunks * CMP_STRIDE].reshape(b, n_chunks, CMP_STRIDE, NSA_KV_HEADS, HEAD_DIM)
    blk = jnp.concatenate([c[:, i:i + nc] for i in range(r)], axis=2)
    blk = blk + pe[None, None, :, None, :]
    flat = blk.transpose(0, 1, 3, 2, 4).reshape(b, nc, NSA_KV_HEADS, CMP_BLOCK * HEAD_DIM)
    hid = jax.nn.gelu(jnp.einsum('bnhf,fe->bnhe', flat, w1))
    return jnp.einsum('bnhe,ed->bnhd', hid, w2)


def cmp_attend(q, q_pos, kc, vc):
    nc = kc.shape[1]
    c_last = jnp.arange(nc, dtype=I32) * CMP_STRIDE + CMP_BLOCK - 1
    s = jnp.einsum('bqhgd,bnhd->bqhgn', q, kc).astype(F32) * SCALE
    valid = (c_last[None, :] <= q_pos[:, None])[None, :, None, None, :]
    p = masked_softmax(s, valid)
    o = jnp.einsum('bqhgn,bnhd->bqhgd', p.astype(vc.dtype), vc)
    return o, p


def select_blocks(p, q_pos, n_sel):
    nc = p.shape[-1]
    c_start = jnp.arange(nc, dtype=I32) * CMP_STRIDE
    c_last = c_start + CMP_BLOCK - 1
    s_start = jnp.arange(n_sel, dtype=I32) * SEL_BLOCK
    cover = ((c_start[:, None] < s_start[None, :] + SEL_BLOCK) & (c_last[:, None] >= s_start[None, :])).astype(F32)
    imp = jnp.einsum('bqhgn,ns->bqhs', p, cover)
    blk = jnp.arange(n_sel, dtype=I32)[None, :]
    cur = (q_pos // SEL_BLOCK)[:, None]
    forced = (blk == 0) | (blk == cur) | (blk == cur - 1)
    future = blk * SEL_BLOCK > q_pos[:, None]
    imp = jnp.where(future[None, :, None, :], -1e9, jnp.where(forced[None, :, None, :], 1e9, imp))
    _, idx = lax.top_k(imp, min(SEL_TOP, n_sel))
    return idx


def to_blocks(k):
    b, l = k.shape[:2]
    ns = -(-l // SEL_BLOCK)
    k = jnp.pad(k, ((0, 0), (0, ns * SEL_BLOCK - l), (0, 0), (0, 0)))
    return k.reshape(b, ns, SEL_BLOCK, NSA_KV_HEADS, HEAD_DIM).transpose(0, 3, 1, 2, 4)


def sel_attend(q, q_pos, idx, kb, vb):
    idx_t = idx.transpose(0, 2, 1, 3)
    take = jax.vmap(jax.vmap(lambda blocks, i: blocks[i]))
    kg = take(kb, idx_t)
    vg = take(vb, idx_t)
    kpos = idx_t[..., None] * SEL_BLOCK + jnp.arange(SEL_BLOCK, dtype=I32)
    valid = (kpos <= q_pos[None, None, :, None, None]).transpose(0, 2, 1, 3, 4)[:, :, :, None]
    s = jnp.einsum('bqhgd,bhqkld->bqhgkl', q, kg).astype(F32) * SCALE
    b, tq, h, g, kk, lb = s.shape
    p = masked_softmax(s.reshape(b, tq, h, g, kk * lb), valid.reshape(b, tq, h, 1, kk * lb)).reshape(s.shape)
    return jnp.einsum('bqhgkl,bhqkld->bqhgd', p.astype(vg.dtype), vg)


def window_attend(q, q_pos, k, v, k_pos):
    s = jnp.einsum('bqhgd,bkhd->bqhgk', q, k).astype(F32) * SCALE
    diff = q_pos[:, None] - k_pos[None, :]
    valid = ((diff >= 0) & (diff < WINDOW) & (k_pos[None, :] >= 0))[None, :, None, None, :]
    p = masked_softmax(s, valid)
    return jnp.einsum('bqhgk,bkhd->bqhgd', p.astype(v.dtype), v)


def gather_pages(cache, page_table):
    c = cache[page_table]
    return c.reshape(page_table.shape[0], page_table.shape[1] * cache.shape[1], *cache.shape[2:])


def nsa_sample(prep, pos, past, win_k, win_v, page_table, past_len, pe, w1, w2):
    q, kc, vc, ks, vs, kw, vw, gate = prep
    b, t = q.shape[:2]
    kvr = lambda a: a.reshape(b, t, NSA_KV_HEADS, HEAD_DIM)
    q = q.reshape(b, t, NSA_KV_HEADS, NSA_GROUP, HEAD_DIM)
    kc, vc, ks, vs, kw, vw = (kvr(a) for a in (kc, vc, ks, vs, kw, vw))
    gate = gate[..., :3 * NSA_HEADS].reshape(b, t, NSA_KV_HEADS, NSA_GROUP, 3)
    full = [jnp.concatenate([gather_pages(c, page_table).astype(new.dtype), new], axis=1)
            for c, new in zip(past, (kc, vc, ks, vs))]
    kcmp = compress(full[0], pe[0], w1[0], w2[0])
    vcmp = compress(full[1], pe[1], w1[1], w2[1])
    o_cmp, p_cmp = cmp_attend(q, pos, kcmp, vcmp)
    ksb, vsb = to_blocks(full[2]), to_blocks(full[3])
    idx = select_blocks(p_cmp, pos, ksb.shape[2])
    o_sel = sel_attend(q, pos, idx, ksb, vsb)
    wb = win_k.shape[1]
    kwin = jnp.concatenate([win_k.astype(kw.dtype), kw], axis=1)
    vwin = jnp.concatenate([win_v.astype(vw.dtype), vw], axis=1)
    kpos = past_len - wb + jnp.arange(wb + t, dtype=I32)
    o_win = window_attend(q, pos, kwin, vwin, kpos)
    o = (gate[..., 0, None] * o_cmp + gate[..., 1, None] * o_sel + gate[..., 2, None] * o_win).reshape(b, t, TOK_W)
    return o, (kc, vc, ks, vs), (kwin[:, -wb:], vwin[:, -wb:])


def _reorder_nsa_weight(w):
    n_gl = 3 * NSA_HEADS
    parts = [w[:, :NSA_MQ0], w[:, NSA_MQ0 + n_gl:NSA_MQ0 + n_gl + MEM_W], w[:, NSA_MQ0:NSA_MQ0 + n_gl]]
    wr = jnp.concatenate(parts, axis=1)
    return jnp.pad(wr, ((0, 0), (0, NSA_ZW - wr.shape[1]))).astype(BF16)


def kernel(x_prompt, x_sample, state_pool, cache_cmp_k, cache_cmp_v, cache_sel_k, cache_sel_v, state_win_k, state_win_v, cache_mem_k, cache_mem_v, page_table, mem_prompt, norm_mix_g, norm_ffn_g, norm_mem_g, w_mem_kv, mem_q_norm_g, mem_k_norm_g, w_in_pool, w_pool_grp, pool_scale, w_out_pool, w_in_nsa, b_gate, nsa_q_norm_g, nsa_k_norm_g, cmp_pe, cmp_w1, cmp_w2, w_out_nsa, w_router, b_router, w_gate, w_up, w_down):
    bp, t_p, d = x_prompt.shape
    bs, t_s, _ = x_sample.shape
    assert t_s == 1
    n_p, n_s = bp * t_p, bs * t_s
    m_len = mem_prompt.shape[1]
    past_len = page_table.shape[1] * PAGE_SIZE
    pos_p = jnp.arange(t_p, dtype=I32)
    pos_s = past_len + jnp.arange(t_s, dtype=I32)
    xp = x_prompt.reshape(n_p, d)
    xs = x_sample.reshape(n_s, d)
    mem_flat = mem_prompt.reshape(-1, d)
    pool_p, pool_s, rows_p, rows_s, win_p, win_s, mem_k_p, mem_v_p = [], [], [], [], [], [], [], []
    for i in range(DEPTH):
        li = i // 2
        kv = proj(mem_flat, w_mem_kv[i].astype(BF16), gain=norm_mem_g[i])
        mk = head_norm(kv, 0, MEM_W, mem_k_norm_g[i]).reshape(bp, m_len, MEM_W)
        mv = kv[:, MEM_W:].reshape(bp, m_len, MEM_W)
        mem_k_p.append(mk.reshape(bp, m_len, MEM_HEADS, HEAD_DIM))
        mem_v_p.append(mv.reshape(bp, m_len, MEM_HEADS, HEAD_DIM))
        mk_s = cache_mem_k[i].reshape(bs, m_len, MEM_W)
        mv_s = cache_mem_v[i].reshape(bs, m_len, MEM_W)
        if i % 2 == 0:
            w_in = w_in_pool[li].astype(BF16)
            zp = proj(xp, w_in, gain=norm_mix_g[i]).reshape(bp, t_p, -1)
            zs = proj(xs, w_in, gain=norm_mix_g[i]).reshape(bs, t_s, -1)
            op = pool_mix(zp, None, w_pool_grp[li], pool_scale[li], 0)
            zs16 = jnp.pad(zs, ((0, 0), (0, POOL_HALO - t_s), (0, 0)))
            halo = jnp.pad(state_pool[li], ((0, 0), (1, 0), (0, 0)))
            os_ = pool_mix(zs16, halo, w_pool_grp[li], pool_scale[li], past_len)[:, :t_s]
            pool_p.append(zp[:, t_p - POOL_STATE:, :TOK_W])
            pool_s.append(jnp.concatenate([state_pool[li], zs[..., :TOK_W]], axis=1)[:, -POOL_STATE:])
            mq0 = TOK_W
            w_out = w_out_pool[li]
        else:
            w_in = _reorder_nsa_weight(w_in_nsa[li])
            zp = proj(xp, w_in, gain=norm_mix_g[i]).reshape(bp, t_p, -1)
            zs = proj(xs, w_in, gain=norm_mix_g[i]).reshape(bs, t_s, -1)
            q, kc, vc, ks, vs, kw, vw, gate = nsa_prep(zp, pos_p, nsa_q_norm_g[li], nsa_k_norm_g[li], b_gate[li])
            kcmp = compress_rows(kc, cmp_pe[li, 0], cmp_w1[li, 0], cmp_w2[li, 0])
            vcmp = compress_rows(vc, cmp_pe[li, 1], cmp_w1[li, 1], cmp_w2[li, 1])
            o_cmp, sel_t = cmp_select_prompt(q, kcmp, vcmp)
            o_sel = sel_attention_prompt(q, ks, vs, sel_t)
            op = win_combine_prompt(q, kw, vw, o_cmp, o_sel, gate)
            kvr = lambda a: a.reshape(a.shape[0], a.shape[1], NSA_KV_HEADS, HEAD_DIM)
            rows_p.append(tuple(kvr(a) for a in (kc, vc, ks, vs)))
            wb = min(WINDOW, t_p)
            win_p.append((kvr(kw[:, t_p - wb:]), kvr(vw[:, t_p - wb:])))
            prep_s = nsa_prep(zs, pos_s, nsa_q_norm_g[li], nsa_k_norm_g[li], b_gate[li])
            past = (cache_cmp_k[li], cache_cmp_v[li], cache_sel_k[li], cache_sel_v[li])
            os_, r_s, wn_s = nsa_sample(prep_s, pos_s, past, state_win_k[li], state_win_v[li], page_table, past_len,
                                        cmp_pe[li], cmp_w1[li], cmp_w2[li])
            rows_s.append(r_s)
            win_s.append(wn_s)
            mq0 = NSA_MQ0
            w_out = w_out_nsa[li]
        ap = mem_attention(zp, mq0, mk, mv, mem_q_norm_g[i])
        as_ = mem_attention(zs, mq0, mk_s, mv_s, mem_q_norm_g[i])
        w_out = w_out.astype(BF16)
        xp = proj([op.reshape(n_p, TOK_W), ap.reshape(n_p, MEM_W)], w_out, residual=xp)
        xs = proj([os_.reshape(n_s, TOK_W), as_.reshape(n_s, MEM_W)], w_out, residual=xs)
        xn_p, e_p, g_p = router(xp, norm_ffn_g[i], w_router, b_router)
        xs_pad = jnp.pad(xs, ((0, LANES - n_s), (0, 0)))
        xn_s, e_s, g_s = router(xs_pad, norm_ffn_g[i], w_router, b_router)
        xa = moe_ffn(jnp.concatenate([xp, xs], axis=0),
                     jnp.concatenate([xn_p, xn_s[:n_s]], axis=0),
                     jnp.concatenate([e_p, e_s[:, :n_s]], axis=1),
                     jnp.concatenate([g_p, g_s[:, :n_s]], axis=1),
                     w_gate, w_up, w_down, i)
        xp, xs = xa[:n_p], xa[n_p:]
    stk = lambda lst, j: jnp.stack([r[j] for r in lst])
    return (xp.reshape(bp, t_p, d), xs.reshape(bs, t_s, d),
            jnp.stack(pool_p), jnp.stack(pool_s),
            stk(rows_p, 0), stk(rows_p, 1), stk(rows_p, 2), stk(rows_p, 3),
            stk(rows_s, 0), stk(rows_s, 1), stk(rows_s, 2), stk(rows_s, 3),
            stk(win_p, 0), stk(win_p, 1), stk(win_s, 0), stk(win_s, 1),
            jnp.stack(mem_k_p), jnp.stack(mem_v_p))
```

```python
import functools

import jax
import jax.numpy as jnp
import numpy as np
from jax import lax
from jax.experimental import pallas as pl
from jax.experimental.pallas import tpu as pltpu

D_MODEL = 2048
DEPTH = 4
PAGE_SIZE = 128
HEAD_DIM = 128
ROPE_THETA = 500000.0
ROPE_DIM = HEAD_DIM // 4
MEM_HEADS = 4
MEM_W = MEM_HEADS * HEAD_DIM
TOK_W = D_MODEL - MEM_W
POOL_WINDOWS = (2, 4, 8, 16)
POOL_GROUP = TOK_W // len(POOL_WINDOWS)
POOL_STATE = max(POOL_WINDOWS) - 1
POOL_HALO = POOL_STATE + 1
NSA_HEADS = TOK_W // HEAD_DIM
NSA_KV_HEADS = 2
NSA_GROUP = NSA_HEADS // NSA_KV_HEADS
KV_W = NSA_KV_HEADS * HEAD_DIM
GROUP_W = NSA_GROUP * HEAD_DIM
CMP_BLOCK = 32
CMP_STRIDE = 16
SEL_BLOCK = 64
SEL_TOP = 16
WINDOW = 512
NSA_TOK_IN = NSA_HEADS * HEAD_DIM + 6 * KV_W + 3 * NSA_HEADS
N_EXPERTS = 16
N_GROUPS = 4
EXP_PER_GROUP = N_EXPERTS // N_GROUPS
TOP_K = 2
D_FF = 1024
EPS = 1e-6
NEG = -1e30
SCALE = HEAD_DIM ** -0.5
LANES = 128

V7X_VMEM_BYTES = 64 * 1024 * 1024
VMEM_LIMIT = V7X_VMEM_BYTES * 3 // 4

BF16 = jnp.bfloat16
F32 = jnp.float32
I32 = jnp.int32

NSA_Q0 = 0
NSA_KV0 = NSA_HEADS * HEAD_DIM
NSA_MQ0 = NSA_KV0 + 6 * KV_W
NSA_GL0 = NSA_MQ0 + MEM_W
NSA_ZW = 3840


def _cparams(sem):
    return pltpu.CompilerParams(dimension_semantics=sem, vmem_limit_bytes=VMEM_LIMIT)


def _nt(a, b):
    return lax.dot_general(a, b, (((1,), (1,)), ((), ())), preferred_element_type=F32)


def _tn(a, b):
    return lax.dot_general(a, b, (((0,), (0,)), ((), ())), preferred_element_type=F32)


def _rms(x, g):
    return x * lax.rsqrt(jnp.mean(x * x, axis=-1, keepdims=True) + EPS) * g


def _pick_tile(n, cap, unit):
    if n <= cap:
        return n
    best = None
    for t in range(unit, cap + 1, unit):
        if n % t == 0:
            best = t
    assert best is not None, (n, cap, unit)
    return best


def _proj_kernel(*refs, n_x, norm, residual):
    x_refs, refs = refs[:n_x], refs[n_x:]
    if norm:
        g_ref, refs = refs[0], refs[1:]
    w_ref, refs = refs[0], refs[1:]
    if residual:
        r_ref, refs = refs[0], refs[1:]
    o_ref, xn_ref = refs

    @pl.when(pl.program_id(1) == 0)
    def _():
        off = 0
        for x_ref in x_refs:
            x = x_ref[...].astype(F32)
            if norm:
                x = _rms(x, g_ref[...])
            xn_ref[:, off:off + x.shape[1]] = x.astype(BF16)
            off += x.shape[1]

    y = jnp.dot(xn_ref[...], w_ref[...], preferred_element_type=F32)
    if residual:
        y = y + r_ref[...]
    o_ref[...] = y


def proj(xs, w_bf16, gain=None, residual=None):
    if not isinstance(xs, (list, tuple)):
        xs = [xs]
    assert gain is None or len(xs) == 1
    m = xs[0].shape[0]
    k = sum(x.shape[1] for x in xs)
    n = w_bf16.shape[1]
    tm = _pick_tile(m, 512, 8)
    tn = _pick_tile(n, 1024, LANES)
    in_specs = [pl.BlockSpec((tm, x.shape[1]), lambda i, j: (i, 0)) for x in xs]
    args = list(xs)
    if gain is not None:
        in_specs.append(pl.BlockSpec((1, k), lambda i, j: (0, 0)))
        args.append(gain.reshape(1, k).astype(F32))
    in_specs.append(pl.BlockSpec((k, tn), lambda i, j: (0, j)))
    args.append(w_bf16)
    if residual is not None:
        in_specs.append(pl.BlockSpec((tm, tn), lambda i, j: (i, j)))
        args.append(residual)
    return pl.pallas_call(
        functools.partial(_proj_kernel, n_x=len(xs), norm=gain is not None, residual=residual is not None),
        grid=(m // tm, n // tn),
        in_specs=in_specs,
        out_specs=pl.BlockSpec((tm, tn), lambda i, j: (i, j)),
        out_shape=jax.ShapeDtypeStruct((m, n), F32),
        scratch_shapes=[pltpu.VMEM((tm, k), BF16)],
        compiler_params=_cparams(("parallel", "arbitrary")),
        name="proj",
    )(*args)


def _head_norm_kernel(x_ref, g_ref, o_ref):
    x = x_ref[...]
    g = g_ref[...]
    n_heads = x.shape[1] // HEAD_DIM
    o_ref[...] = jnp.concatenate(
        [_rms(x[:, h * HEAD_DIM:(h + 1) * HEAD_DIM], g) for h in range(n_heads)], axis=1)


def head_norm(x, col0, width, gain):
    m = x.shape[0]
    assert col0 % width == 0
    return pl.pallas_call(
        _head_norm_kernel,
        grid=(1,),
        in_specs=[pl.BlockSpec((m, width), lambda i: (0, col0 // width)),
                  pl.BlockSpec((1, HEAD_DIM), lambda i: (0, 0))],
        out_specs=pl.BlockSpec((m, width), lambda i: (0, 0)),
        out_shape=jax.ShapeDtypeStruct((m, width), F32),
        compiler_params=_cparams(("arbitrary",)),
        name="head_norm",
    )(x, gain.reshape(1, HEAD_DIM))


def _mem_attn_kernel(q_ref, k_ref, v_ref, g_ref, o_ref):
    q = q_ref[...]
    rows = q.shape[0]
    if rows < 8:
        q = jnp.broadcast_to(q[0:1], (8, q.shape[1]))
    g = g_ref[...]
    outs = []
    for h in range(MEM_HEADS):
        sl = slice(h * HEAD_DIM, (h + 1) * HEAD_DIM)
        qh = _rms(q[:, sl], g).astype(BF16)
        s = _nt(qh, k_ref[:, sl].astype(BF16)) * SCALE
        m = jnp.max(s, axis=-1, keepdims=True)
        e = jnp.exp(s - m)
        p = e / jnp.sum(e, axis=-1, keepdims=True)
        outs.append(jnp.dot(p.astype(BF16), v_ref[:, sl].astype(BF16), preferred_element_type=F32))
    o = jnp.concatenate(outs, axis=1)
    o_ref[...] = o[:rows]


def mem_attention(z, col0, k, v, gain):
    b, t, _ = z.shape
    m = k.shape[1]
    tq = _pick_tile(t, 512, 8)
    assert col0 % MEM_W == 0
    return pl.pallas_call(
        _mem_attn_kernel,
        grid=(b, t // tq),
        in_specs=[pl.BlockSpec((None, tq, MEM_W), lambda bi, i: (bi, i, col0 // MEM_W)),
                  pl.BlockSpec((None, m, MEM_W), lambda bi, i: (bi, 0, 0)),
                  pl.BlockSpec((None, m, MEM_W), lambda bi, i: (bi, 0, 0)),
                  pl.BlockSpec((1, HEAD_DIM), lambda bi, i: (0, 0))],
        out_specs=pl.BlockSpec((None, tq, MEM_W), lambda bi, i: (bi, i, 0)),
        out_shape=jax.ShapeDtypeStruct((b, t, MEM_W), F32),
        compiler_params=_cparams(("parallel", "arbitrary")),
        name="mem_attention",
    )(z, k, v, gain.reshape(1, HEAD_DIM))


def _pool_kernel(u_ref, halo_ref, w_ref, sc_ref, o_ref, *, pos0, zero_first_halo):
    tq = u_ref.shape[0]
    qi = pl.program_id(1)
    u = u_ref[...]
    halo = halo_ref[...]
    if zero_first_halo:
        halo = jnp.where(qi > 0, halo, 0.0)
    pos = pos0 + qi * tq + lax.broadcasted_iota(I32, (tq, 1), 0)
    outs = []
    for g, w in enumerate(POOL_WINDOWS):
        cs = slice(g * POOL_GROUP, (g + 1) * POOL_GROUP)
        ug = u[:, cs]
        acc = jnp.concatenate([halo[:, cs], ug], axis=0)
        span = 1
        while span < w:
            acc = acc[span:] + acc[:-span]
            span *= 2
        ssum = acc[POOL_HALO - (w - 1):POOL_HALO - (w - 1) + tq]
        cnt = jnp.minimum(w, pos + 1).astype(F32)
        d = (ssum / cnt - ug).astype(BF16)
        y = jnp.dot(d, w_ref[g].astype(BF16), preferred_element_type=F32)
        outs.append(y)
    o_ref[...] = jnp.concatenate(outs, axis=1) * sc_ref[...]


def pool_mix(z, halo, w_grp, scale, pos0):
    b, t, _ = z.shape
    tq = _pick_tile(t, 256, 16)
    if halo is None:
        halo_arr = z
        halo_spec = pl.BlockSpec((None, POOL_HALO, TOK_W),
                                 lambda bi, i: (bi, jnp.maximum(i * (tq // POOL_HALO) - 1, 0), 0))
    else:
        assert t == tq
        halo_arr = halo
        halo_spec = pl.BlockSpec((None, POOL_HALO, TOK_W), lambda bi, i: (bi, 0, 0))
    return pl.pallas_call(
        functools.partial(_pool_kernel, pos0=pos0, zero_first_halo=halo is None),
        grid=(b, t // tq),
        in_specs=[pl.BlockSpec((None, tq, TOK_W), lambda bi, i: (bi, i, 0)),
                  halo_spec,
                  pl.BlockSpec(w_grp.shape, lambda bi, i: (0, 0, 0)),
                  pl.BlockSpec((1, TOK_W), lambda bi, i: (0, 0))],
        out_specs=pl.BlockSpec((None, tq, TOK_W), lambda bi, i: (bi, i, 0)),
        out_shape=jax.ShapeDtypeStruct((b, t, TOK_W), F32),
        compiler_params=_cparams(("parallel", "arbitrary")),
        name="pool_mix",
    )(z, halo_arr, w_grp, scale.reshape(1, TOK_W))


def _rope_tables(pos):
    half = ROPE_DIM // 2
    inv = 1.0 / (ROPE_THETA ** (jnp.arange(half, dtype=F32) * 2.0 / ROPE_DIM))
    ang = pos.astype(F32)[:, None] * inv[None, :]
    cos, sin = jnp.cos(ang), jnp.sin(ang)
    t = pos.shape[0]
    rest = HEAD_DIM - ROPE_DIM
    c = jnp.concatenate([cos, cos, jnp.ones((t, rest), F32)], axis=1)
    s_lo = jnp.concatenate([-sin, jnp.zeros((t, HEAD_DIM - half), F32)], axis=1)
    s_hi = jnp.concatenate([jnp.zeros((t, half), F32), sin, jnp.zeros((t, rest), F32)], axis=1)
    return c, s_lo, s_hi


def _nsa_prep_kernel(z_ref, c_ref, slo_ref, shi_ref, qg_ref, kg_ref, bg_ref,
                     q_ref, kc_ref, vc_ref, ks_ref, vs_ref, kw_ref, vw_ref, gate_ref):
    rows = z_ref.shape[0]
    c, slo, shi = c_ref[...], slo_ref[...], shi_ref[...]
    half = ROPE_DIM // 2

    def rope_norm(x, g):
        if rows < 8:
            x = jnp.broadcast_to(x[0:1], (8, HEAD_DIM))
        x = _rms(x, g)
        y = x * c + pltpu.roll(x, HEAD_DIM - half, 1) * slo + pltpu.roll(x, half, 1) * shi
        return y[:rows]

    qg = qg_ref[...]
    q_ref[...] = jnp.concatenate(
        [rope_norm(z_ref[:, NSA_Q0 + h * HEAD_DIM:NSA_Q0 + (h + 1) * HEAD_DIM], qg) for h in range(NSA_HEADS)], axis=1)
    for j, (o_ref, which) in enumerate(((kc_ref, 0), (vc_ref, None), (ks_ref, 1), (vs_ref, None), (kw_ref, 2), (vw_ref, None))):
        c0 = NSA_KV0 + j * KV_W
        if which is None:
            o_ref[...] = z_ref[:, c0:c0 + KV_W]
        else:
            g = kg_ref[which:which + 1, :]
            o_ref[...] = jnp.concatenate(
                [rope_norm(z_ref[:, c0 + h * HEAD_DIM:c0 + (h + 1) * HEAD_DIM], g) for h in range(NSA_KV_HEADS)], axis=1)
    gate_ref[...] = jax.nn.sigmoid(z_ref[:, NSA_GL0:NSA_GL0 + LANES] + bg_ref[...])


def nsa_prep(z, pos, q_g, k_g, b_gate):
    b, t, _ = z.shape
    tq = _pick_tile(t, 256, 8)
    c, slo, shi = _rope_tables(pos)
    if t < 8:
        c, slo, shi = (jnp.broadcast_to(a, (8, HEAD_DIM)) for a in (c, slo, shi))
    tt = max(tq, 8)
    bg = jnp.pad(b_gate.reshape(1, -1), ((0, 0), (0, LANES - b_gate.shape[-1])))
    tab_spec = pl.BlockSpec((tt, HEAD_DIM), lambda bi, i: (i, 0))
    kv_spec = pl.BlockSpec((None, tq, KV_W), lambda bi, i: (bi, i, 0))
    kv_shape = jax.ShapeDtypeStruct((b, t, KV_W), F32)
    return pl.pallas_call(
        _nsa_prep_kernel,
        grid=(b, t // tq),
        in_specs=[pl.BlockSpec((None, tq, NSA_ZW), lambda bi, i: (bi, i, 0)),
                  tab_spec, tab_spec, tab_spec,
                  pl.BlockSpec((1, HEAD_DIM), lambda bi, i: (0, 0)),
                  pl.BlockSpec((3, HEAD_DIM), lambda bi, i: (0, 0)),
                  pl.BlockSpec((1, LANES), lambda bi, i: (0, 0))],
        out_specs=[pl.BlockSpec((None, tq, TOK_W), lambda bi, i: (bi, i, 0))] + [kv_spec] * 6
                  + [pl.BlockSpec((None, tq, LANES), lambda bi, i: (bi, i, 0))],
        out_shape=[jax.ShapeDtypeStruct((b, t, TOK_W), F32)] + [kv_shape] * 6
                  + [jax.ShapeDtypeStruct((b, t, LANES), F32)],
        compiler_params=_cparams(("parallel", "arbitrary")),
        name="nsa_prep",
    )(z, c, slo, shi, q_g.reshape(1, HEAD_DIM), k_g, bg)


def _compress_body(x, pe_ref, w1_ref, w2_ref, o_ref):
    n_chunks = x.shape[0]
    pe = pe_ref[...]
    pe_lo = jnp.concatenate([pe[r:r + 1] for r in range(CMP_STRIDE)], axis=1)
    pe_hi = jnp.concatenate([pe[CMP_STRIDE + r:CMP_STRIDE + r + 1] for r in range(CMP_STRIDE)], axis=1)
    half_k = CMP_STRIDE * HEAD_DIM
    w_lo = w1_ref[0:half_k, :].astype(BF16)
    w_hi = w1_ref[half_k:2 * half_k, :].astype(BF16)
    w2 = w2_ref[...].astype(BF16)
    for h in range(NSA_KV_HEADS):
        xh = jnp.concatenate(
            [x[:, (r * NSA_KV_HEADS + h) * HEAD_DIM:(r * NSA_KV_HEADS + h + 1) * HEAD_DIM] for r in range(CMP_STRIDE)], axis=1)
        a = jnp.dot((xh + pe_lo).astype(BF16), w_lo, preferred_element_type=F32)
        bb = jnp.dot((xh + pe_hi).astype(BF16), w_hi, preferred_element_type=F32)
        hid = jax.nn.gelu(a + pltpu.roll(bb, n_chunks - 1, 0))
        o_ref[h] = jnp.dot(hid.astype(BF16), w2, preferred_element_type=F32)


def _compress_kernel(x_ref, pe_ref, w1_ref, w2_ref, o_ref):
    _compress_body(x_ref[...], pe_ref, w1_ref, w2_ref, o_ref)


def compress_rows(x, pe, w1, w2):
    b, t, _ = x.shape
    n_chunks = t // CMP_STRIDE
    cw = CMP_STRIDE * KV_W
    xc = x.reshape(b, n_chunks, cw)
    return pl.pallas_call(
        _compress_kernel,
        grid=(b,),
        in_specs=[pl.BlockSpec((None, n_chunks, cw), lambda bi: (bi, 0, 0)),
                  pl.BlockSpec(pe.shape, lambda bi: (0, 0)),
                  pl.BlockSpec(w1.shape, lambda bi: (0, 0)),
                  pl.BlockSpec(w2.shape, lambda bi: (0, 0))],
        out_specs=pl.BlockSpec((None, NSA_KV_HEADS, n_chunks, HEAD_DIM), lambda bi: (bi, 0, 0, 0)),
        out_shape=jax.ShapeDtypeStruct((b, NSA_KV_HEADS, n_chunks, HEAD_DIM), F32),
        compiler_params=_cparams(("parallel",)),
        name="compress_rows",
    )(xc, pe, w1, w2)


def _stack_heads(q):
    return jnp.concatenate([q[:, g * HEAD_DIM:(g + 1) * HEAD_DIM] for g in range(NSA_GROUP)], axis=0)


def _unstack_heads(o, tq):
    return jnp.concatenate([o[g * tq:(g + 1) * tq] for g in range(NSA_GROUP)], axis=1)


def _rank_select(imp_t, n_real):
    n_blk = imp_t.shape[0]
    blk = lax.broadcasted_iota(I32, imp_t.shape, 0)
    cnt = jnp.zeros(imp_t.shape, F32)
    for j in range(n_real):
        row = imp_t[j:j + 1, :]
        beats = jnp.where(row > imp_t, 1.0, jnp.where(row == imp_t, jnp.where(blk > j, 1.0, 0.0), 0.0))
        cnt = cnt + beats
    return jnp.where(cnt < SEL_TOP, 1.0, 0.0)


def _cmp_select_kernel(q_ref, kc_ref, vc_ref, cov_ref, o_ref, sel_ref):
    tq = q_ref.shape[0]
    nc = kc_ref.shape[0]
    n_blk = sel_ref.shape[0]
    qi = pl.program_id(2)
    q6 = _stack_heads(q_ref[...]).astype(BF16)
    s = _nt(q6, kc_ref[...].astype(BF16)) * SCALE
    s = s.reshape(NSA_GROUP, tq, nc)
    q_pos = qi * tq + lax.broadcasted_iota(I32, (tq, nc), 0)
    c_last = lax.broadcasted_iota(I32, (tq, nc), 1) * CMP_STRIDE + (CMP_BLOCK - 1)
    valid = (c_last <= q_pos)[None]
    s = jnp.where(valid, s, NEG)
    m = jnp.max(s, axis=-1, keepdims=True)
    e = jnp.where(valid, jnp.exp(s - m), 0.0)
    p = (e / jnp.maximum(jnp.sum(e, axis=-1, keepdims=True), 1e-30)).astype(BF16)
    o = jnp.dot(p.reshape(NSA_GROUP * tq, nc), vc_ref[...].astype(BF16), preferred_element_type=F32)
    o_ref[...] = _unstack_heads(o, tq)
    p_cat = jnp.concatenate([p[g] for g in range(NSA_GROUP)], axis=1)
    imp_t = _nt(cov_ref[...], p_cat)
    blk = lax.broadcasted_iota(I32, (n_blk, tq), 0)
    pos_t = qi * tq + lax.broadcasted_iota(I32, (n_blk, tq), 1)
    cur = pos_t // SEL_BLOCK
    forced = (blk == 0) | (blk == cur) | (blk == cur - 1)
    future = blk * SEL_BLOCK > pos_t
    imp_t = jnp.where(future, -1e9, jnp.where(forced, 1e9, imp_t))
    sel_ref[...] = _rank_select(imp_t, n_blk).astype(BF16)


def _cover_t(nc, n_blk, nc_valid):
    c_start = np.arange(nc) * CMP_STRIDE
    c_last = c_start + CMP_BLOCK - 1
    s_start = np.arange(n_blk) * SEL_BLOCK
    cov = (c_start[None, :] < s_start[:, None] + SEL_BLOCK) & (c_last[None, :] >= s_start[:, None])
    cov = cov & (np.arange(nc)[None, :] < nc_valid)
    return np.tile(cov.astype(np.float32), (1, NSA_GROUP))


def cmp_select_prompt(q, kcmp, vcmp):
    b, t, _ = q.shape
    nc = kcmp.shape[2]
    n_blk = -(-t // SEL_BLOCK)
    tq = _pick_tile(t, 256, LANES)
    cov = jnp.asarray(_cover_t(nc, n_blk, nc - 1), BF16)
    return pl.pallas_call(
        _cmp_select_kernel,
        grid=(b, NSA_KV_HEADS, t // tq),
        in_specs=[pl.BlockSpec((None, tq, GROUP_W), lambda bi, h, i: (bi, i, h)),
                  pl.BlockSpec((None, None, nc, HEAD_DIM), lambda bi, h, i: (bi, h, 0, 0)),
                  pl.BlockSpec((None, None, nc, HEAD_DIM), lambda bi, h, i: (bi, h, 0, 0)),
                  pl.BlockSpec(cov.shape, lambda bi, h, i: (0, 0))],
        out_specs=[pl.BlockSpec((None, tq, GROUP_W), lambda bi, h, i: (bi, i, h)),
                   pl.BlockSpec((None, None, n_blk, tq), lambda bi, h, i: (bi, h, 0, i))],
        out_shape=[jax.ShapeDtypeStruct((b, t, TOK_W), F32),
                   jax.ShapeDtypeStruct((b, NSA_KV_HEADS, n_blk, t), BF16)],
        compiler_params=_cparams(("parallel", "parallel", "arbitrary")),
        name="cmp_select_prompt",
    )(q, kcmp, vcmp, cov)


SEL_TQ = 128
SEL_TK = 512


def _sel_kernel(q_ref, k_ref, v_ref, sel_ref, exp_ref, o_ref, mask_ref, m_ref, l_ref, acc_ref):
    t = k_ref.shape[0]
    qi = pl.program_id(2)
    q6 = _stack_heads(q_ref[...]).astype(BF16)
    mask_ref[...] = (1.0 - _tn(sel_ref[...], exp_ref[...])) * NEG
    m_ref[...] = jnp.full(m_ref.shape, NEG, F32)
    l_ref[...] = jnp.zeros(l_ref.shape, F32)
    acc_ref[...] = jnp.zeros(acc_ref.shape, F32)
    q_lo = qi * SEL_TQ

    def key_tile(j, causal):
        lo, hi = j * SEL_TK, (j + 1) * SEL_TK
        k = k_ref[lo:hi, :].astype(BF16)
        v = v_ref[lo:hi, :].astype(BF16)
        bias = mask_ref[:, lo:hi]
        if causal:
            q_pos = q_lo + lax.broadcasted_iota(I32, (SEL_TQ, SEL_TK), 0)
            k_pos = lo + lax.broadcasted_iota(I32, (SEL_TQ, SEL_TK), 1)
            bias = jnp.where(k_pos <= q_pos, bias, NEG)
        s = _nt(q6, k).reshape(NSA_GROUP, SEL_TQ, SEL_TK) * SCALE + bias[None]
        m_old = m_ref[...]
        m_new = jnp.maximum(m_old, jnp.max(s, axis=-1, keepdims=True))
        e = jnp.exp(s - m_new)
        alpha = jnp.exp(m_old - m_new)
        l_ref[...] = alpha * l_ref[...] + jnp.sum(e, axis=-1, keepdims=True)
        pv = jnp.dot(e.reshape(NSA_GROUP * SEL_TQ, SEL_TK).astype(BF16), v, preferred_element_type=F32)
        acc_ref[...] = alpha * acc_ref[...] + pv.reshape(NSA_GROUP, SEL_TQ, HEAD_DIM)
        m_ref[...] = m_new

    for j in range(t // SEL_TK):
        lo, hi = j * SEL_TK, (j + 1) * SEL_TK

        @pl.when(hi - 1 <= q_lo)
        def _():
            key_tile(j, False)

        @pl.when((lo <= q_lo + SEL_TQ - 1) & (hi - 1 > q_lo))
        def _():
            key_tile(j, True)

    o = acc_ref[...] / jnp.maximum(l_ref[...], 1e-30)
    o_ref[...] = jnp.concatenate([o[g] for g in range(NSA_GROUP)], axis=-1)


def sel_attention_prompt(q, ks, vs, sel_t):
    b, t, _ = q.shape
    n_blk = sel_t.shape[2]
    expand = (np.arange(t)[None, :] // SEL_BLOCK == np.arange(n_blk)[:, None]).astype(np.float32)
    expand = jnp.asarray(expand, BF16)
    return pl.pallas_call(
        _sel_kernel,
        grid=(b, NSA_KV_HEADS, t // SEL_TQ),
        in_specs=[pl.BlockSpec((None, SEL_TQ, GROUP_W), lambda bi, h, i: (bi, i, h)),
                  pl.BlockSpec((None, t, HEAD_DIM), lambda bi, h, i: (bi, 0, h)),
                  pl.BlockSpec((None, t, HEAD_DIM), lambda bi, h, i: (bi, 0, h)),
                  pl.BlockSpec((None, None, n_blk, SEL_TQ), lambda bi, h, i: (bi, h, 0, i)),
                  pl.BlockSpec((n_blk, t), lambda bi, h, i: (0, 0))],
        out_specs=pl.BlockSpec((None, SEL_TQ, GROUP_W), lambda bi, h, i: (bi, i, h)),
        out_shape=jax.ShapeDtypeStruct((b, t, TOK_W), F32),
        scratch_shapes=[pltpu.VMEM((SEL_TQ, t), F32),
                        pltpu.VMEM((NSA_GROUP, SEL_TQ, 1), F32),
                        pltpu.VMEM((NSA_GROUP, SEL_TQ, 1), F32),
                        pltpu.VMEM((NSA_GROUP, SEL_TQ, HEAD_DIM), F32)],
        compiler_params=_cparams(("parallel", "parallel", "arbitrary")),
        name="sel_attention_prompt",
    )(q, ks, vs, sel_t, expand)


WIN_TQ = 128


def _gate_cols(gate, h, j, tq):
    cols = []
    for g in range(NSA_GROUP):
        c = (h * NSA_GROUP + g) * 3 + j
        cols.append(jnp.broadcast_to(gate[:, c:c + 1], (tq, HEAD_DIM)))
    return jnp.concatenate(cols, axis=1)


def _win_combine_kernel(q_ref, k_ref, v_ref, oc_ref, os_ref, gate_ref, o_ref):
    tq = q_ref.shape[0]
    span = WINDOW + tq
    h = pl.program_id(1)
    qi = pl.program_id(2)
    start = pl.multiple_of(jnp.maximum(qi * tq - WINDOW, 0), tq)
    q6 = _stack_heads(q_ref[...]).astype(BF16)
    k = k_ref[pl.ds(start, span), :].astype(BF16)
    v = v_ref[pl.ds(start, span), :].astype(BF16)
    s = _nt(q6, k).reshape(NSA_GROUP, tq, span) * SCALE
    diff = (qi * tq + lax.broadcasted_iota(I32, (tq, span), 0)) - (start + lax.broadcasted_iota(I32, (tq, span), 1))
    valid = ((diff >= 0) & (diff < WINDOW))[None]
    s = jnp.where(valid, s, NEG)
    m = jnp.max(s, axis=-1, keepdims=True)
    e = jnp.where(valid, jnp.exp(s - m), 0.0)
    p = e / jnp.maximum(jnp.sum(e, axis=-1, keepdims=True), 1e-30)
    o = jnp.dot(p.reshape(NSA_GROUP * tq, span).astype(BF16), v, preferred_element_type=F32)
    o_win = _unstack_heads(o, tq)
    gate = gate_ref[...]
    for hh in range(NSA_KV_HEADS):
        @pl.when(h == hh)
        def _():
            o_ref[...] = (_gate_cols(gate, hh, 0, tq) * oc_ref[...] + _gate_cols(gate, hh, 1, tq) * os_ref[...]
                          + _gate_cols(gate, hh, 2, tq) * o_win)


def win_combine_prompt(q, kw, vw, o_cmp, o_sel, gate):
    b, t, _ = q.shape
    assert t >= WINDOW + WIN_TQ
    qspec = pl.BlockSpec((None, WIN_TQ, GROUP_W), lambda bi, h, i: (bi, i, h))
    kspec = pl.BlockSpec((None, t, HEAD_DIM), lambda bi, h, i: (bi, 0, h))
    return pl.pallas_call(
        _win_combine_kernel,
        grid=(b, NSA_KV_HEADS, t // WIN_TQ),
        in_specs=[qspec, kspec, kspec, qspec, qspec,
                  pl.BlockSpec((None, WIN_TQ, LANES), lambda bi, h, i: (bi, i, 0))],
        out_specs=qspec,
        out_shape=jax.ShapeDtypeStruct((b, t, TOK_W), F32),
        compiler_params=_cparams(("parallel", "parallel", "arbitrary")),
        name="win_combine_prompt",
    )(q, kw, vw, o_cmp, o_sel, gate)


def _router_kernel(x_ref, g_ref, wt_ref, b_ref, xn_ref, e_ref, gt_ref):
    xn = _rms(x_ref[...], g_ref[...])
    xn_ref[...] = xn
    lt = _nt(wt_ref[...], xn.astype(BF16))
    ex = jnp.exp(lt - jnp.max(lt, axis=0, keepdims=True))
    aff = ex / jnp.sum(ex, axis=0, keepdims=True)
    sel = aff + b_ref[...]
    row = lambda a, r: a[r:r + 1, :]

    best, g_idx = None, None
    for g in range(N_GROUPS):
        a, b, c, d = (row(sel, g * EXP_PER_GROUP + j) for j in range(EXP_PER_GROUP))
        hi1, lo1, hi2, lo2 = jnp.maximum(a, b), jnp.minimum(a, b), jnp.maximum(c, d), jnp.minimum(c, d)
        score = jnp.maximum(hi1, hi2) + jnp.maximum(jnp.minimum(hi1, hi2), jnp.maximum(lo1, lo2))
        if g == 0:
            best, g_idx = score, jnp.zeros(score.shape, I32)
        else:
            g_idx = jnp.where(score > best, g, g_idx)
            best = jnp.maximum(best, score)

    def in_group(a, j):
        out = row(a, j)
        for g in range(1, N_GROUPS):
            out = jnp.where(g_idx == g, row(a, g * EXP_PER_GROUP + j), out)
        return out

    v = [in_group(sel, j) for j in range(EXP_PER_GROUP)]
    af = [in_group(aff, j) for j in range(EXP_PER_GROUP)]

    def first_max(vals):
        m = functools.reduce(jnp.maximum, vals)
        loc = jnp.full(m.shape, EXP_PER_GROUP - 1, I32)
        for j in range(EXP_PER_GROUP - 2, -1, -1):
            loc = jnp.where(vals[j] == m, j, loc)
        return loc

    l1 = first_max(v)
    l2 = first_max([jnp.where(l1 == j, -jnp.inf, v[j]) for j in range(EXP_PER_GROUP)])
    pick = lambda loc: functools.reduce(lambda acc, j: jnp.where(loc == j, af[j], acc), range(1, EXP_PER_GROUP), af[0])
    a1, a2 = pick(l1), pick(l2)
    tot = a1 + a2
    e_ref[...] = jnp.concatenate([g_idx * EXP_PER_GROUP + l1, g_idx * EXP_PER_GROUP + l2], axis=0)
    gt_ref[...] = jnp.concatenate([a1 / tot, a2 / tot], axis=0)


def router(x, gain, w_router, b_router):
    n, d = x.shape
    tm = _pick_tile(n, 1024, LANES)
    return pl.pallas_call(
        _router_kernel,
        grid=(n // tm,),
        in_specs=[pl.BlockSpec((tm, d), lambda i: (i, 0)),
                  pl.BlockSpec((1, d), lambda i: (0, 0)),
                  pl.BlockSpec((N_EXPERTS, d), lambda i: (0, 0)),
                  pl.BlockSpec((N_EXPERTS, 1), lambda i: (0, 0))],
        out_specs=[pl.BlockSpec((tm, d), lambda i: (i, 0)),
                   pl.BlockSpec((TOP_K, tm), lambda i: (0, i)),
                   pl.BlockSpec((TOP_K, tm), lambda i: (0, i))],
        out_shape=[jax.ShapeDtypeStruct((n, d), F32),
                   jax.ShapeDtypeStruct((TOP_K, n), I32),
                   jax.ShapeDtypeStruct((TOP_K, n), F32)],
        compiler_params=_cparams(("parallel",)),
        name="router",
    )(x, gain.reshape(1, d), w_router.T.astype(BF16), b_router.reshape(N_EXPERTS, 1).astype(F32))


MOE_TM = 512
MOE_NF = 4
MOE_TF = D_FF // MOE_NF
MOE_ROWS = MOE_TM // MOE_NF
MOE_SPLIT = 128


def _moe_kernel(te_ref, nu_ref, src_ref, dst_ref, x_hbm, wg_ref, wu_ref, wd_ref, y_hbm,
                xbuf, xb, ybuf, gsem, ssem):
    i = pl.program_id(0)
    f = pl.program_id(1)
    n_used = nu_ref[0]
    d = xbuf.shape[-1]

    def gather_copy(tile, buf, r):
        row = src_ref[tile * MOE_TM + r]
        return pltpu.make_async_copy(x_hbm.at[pl.ds(row, 1), :], xbuf.at[buf, pl.ds(r, 1), :], gsem.at[buf])

    def scatter_copy(tile, buf, r):
        row = dst_ref[tile * MOE_TM + r]
        return pltpu.make_async_copy(ybuf.at[buf, pl.ds(r, 1), :], y_hbm.at[pl.ds(row, 1), :], ssem.at[buf])

    def wait_all(hbm, vmem_buf, sem):
        pltpu.make_async_copy(hbm.at[pl.ds(0, MOE_TM), :], vmem_buf, sem).wait()

    dump0 = y_hbm.shape[0] - MOE_TM
    cur = i % 2
    oth = 1 - cur

    @pl.when((i == 0) & (f == 0))
    def _():
        def body(r, carry):
            gather_copy(0, 0, r).start()
            return carry
        lax.fori_loop(0, MOE_TM, body, 0)
        ybuf[1] = jnp.zeros(ybuf.shape[1:], F32)

    @pl.when(i < n_used)
    def _():
        @pl.when(f == 0)
        def _():
            wait_all(x_hbm, xbuf.at[cur], gsem.at[cur])
            xb[...] = xbuf[cur].astype(BF16)
            ybuf[cur] = jnp.zeros(ybuf.shape[1:], F32)

        nxt = jnp.minimum(i + 1, n_used - 1)
        prv = jnp.maximum(i - 1, 0)
        wg, wu, wd = wg_ref[...].astype(BF16), wu_ref[...].astype(BF16), wd_ref[...].astype(BF16)
        for c, (lo, hi) in enumerate(((0, MOE_SPLIT), (MOE_SPLIT, MOE_TM))):
            x = xb[lo:hi, :]
            a = jnp.dot(x, wg, preferred_element_type=F32)
            u = jnp.dot(x, wu, preferred_element_type=F32)
            h = (a * jax.nn.sigmoid(a) * u).astype(BF16)
            ybuf[cur, lo:hi, :] += jnp.dot(h, wd, preferred_element_type=F32)
            if c == 0:
                r0 = pl.multiple_of(f * MOE_ROWS, MOE_ROWS)
                for r in range(MOE_ROWS):
                    rr = r0 + r
                    gather_copy(nxt, oth, rr).start()
                    row = jnp.where(i > 0, dst_ref[prv * MOE_TM + rr], dump0 + rr)
                    pltpu.make_async_copy(ybuf.at[oth, pl.ds(rr, 1), :], y_hbm.at[pl.ds(row, 1), :], ssem.at[oth]).start()

        @pl.when(f == MOE_NF - 1)
        def _():
            wait_all(y_hbm, ybuf.at[oth], ssem.at[oth])

    @pl.when((i == n_used) & (f == 0))
    def _():
        wait_all(x_hbm, xbuf.at[cur], gsem.at[cur])

        def body(r, carry):
            scatter_copy(i - 1, oth, r).start()
            return carry
        lax.fori_loop(0, MOE_TM, body, 0)
        wait_all(y_hbm, ybuf.at[oth], ssem.at[oth])


def moe_experts(xn, tile_expert, n_used, src_row, dst_row, n_out_rows, w_gate, w_up, w_down, layer):
    n, d = xn.shape
    n_tiles = tile_expert.shape[0]

    def tile(i, nu):
        return jnp.minimum(i, nu[0] - 1)

    def fcol(i, f, nu):
        return jnp.where(i < nu[0], f, MOE_NF - 1)

    wspec = lambda blk, im: pl.BlockSpec(blk, im)
    grid_spec = pltpu.PrefetchScalarGridSpec(
        num_scalar_prefetch=4,
        grid=(n_tiles, MOE_NF),
        in_specs=[
            pl.BlockSpec(memory_space=pl.ANY),
            wspec((None, None, d, MOE_TF), lambda i, f, te, nu, s, t: (layer, te[tile(i, nu)], 0, fcol(i, f, nu))),
            wspec((None, None, d, MOE_TF), lambda i, f, te, nu, s, t: (layer, te[tile(i, nu)], 0, fcol(i, f, nu))),
            wspec((None, None, MOE_TF, d), lambda i, f, te, nu, s, t: (layer, te[tile(i, nu)], fcol(i, f, nu), 0)),
        ],
        out_specs=pl.BlockSpec(memory_space=pl.ANY),
        scratch_shapes=[pltpu.VMEM((2, MOE_TM, d), F32),
                        pltpu.VMEM((MOE_TM, d), BF16),
                        pltpu.VMEM((2, MOE_TM, d), F32),
                        pltpu.SemaphoreType.DMA((2,)),
                        pltpu.SemaphoreType.DMA((2,))],
    )
    return pl.pallas_call(
        _moe_kernel,
        grid_spec=grid_spec,
        out_shape=jax.ShapeDtypeStruct((n_out_rows, d), F32),
        compiler_params=_cparams(("arbitrary", "arbitrary")),
        name="moe_experts",
    )(tile_expert, n_used, src_row, dst_row, xn, w_gate, w_up, w_down)


def moe_ffn(x, xn, experts, gates, w_gate, w_up, w_down, layer):
    n, d = x.shape
    n_items = n * TOP_K
    n_tiles = -(-n_items // MOE_TM) + N_EXPERTS
    n_slots = n_tiles * MOE_TM
    flat_e = experts.reshape(-1)
    sizes = jnp.sum((flat_e[:, None] == jnp.arange(N_EXPERTS, dtype=I32)[None, :]).astype(I32), axis=0)
    padded = -(-sizes // MOE_TM) * MOE_TM
    ends_p = jnp.cumsum(padded)
    starts_p = ends_p - padded
    starts = jnp.cumsum(sizes) - sizes
    order = jnp.argsort(flat_e).astype(I32)
    tile_expert = jnp.searchsorted(ends_p, jnp.arange(n_tiles, dtype=I32) * MOE_TM, side='right')
    tile_expert = jnp.minimum(tile_expert, N_EXPERTS - 1).astype(I32)
    n_used = (ends_p[-1:] // MOE_TM).astype(I32)
    slot = jnp.arange(n_slots, dtype=I32)
    slot_e = jnp.repeat(tile_expert, MOE_TM)
    within = slot - starts_p[slot_e]
    is_item = within < sizes[slot_e]
    item = order[jnp.clip(starts[slot_e] + within, 0, n_items - 1)]
    src_row = jnp.where(is_item, item % n, 0).astype(I32)
    dst_row = jnp.where(is_item, item, n_items + slot).astype(I32)
    y = moe_experts(xn, tile_expert, n_used, src_row, dst_row, n_items + n_slots + MOE_TM, w_gate, w_up, w_down, layer)
    g = gates.astype(F32)
    return x + y[:n] * g[0][:, None] + y[n:n_items] * g[1][:, None]


GROUP_ROWS = 8


def _group_rows(q_ref, h):
    rows = [q_ref[:, (h * NSA_GROUP + g) * HEAD_DIM:(h * NSA_GROUP + g + 1) * HEAD_DIM] for g in range(NSA_GROUP)]
    rows.append(jnp.zeros((GROUP_ROWS - NSA_GROUP, HEAD_DIM), F32))
    return jnp.concatenate(rows, axis=0)


def _compress_paged_kernel(pt_ref, cache_hbm, pe_ref, w1_ref, w2_ref, o_ref, xbuf, sem, *, layer, n_pages):
    b = pl.program_id(0)
    chunks = PAGE_SIZE // CMP_STRIDE

    def page_copy(p):
        page = pt_ref[b * n_pages + p]
        return pltpu.make_async_copy(cache_hbm.at[layer, page], xbuf.at[pl.ds(p * chunks, chunks), :], sem)

    def start(p, carry):
        page_copy(p).start()
        return carry

    def wait(p, carry):
        page_copy(p).wait()
        return carry

    lax.fori_loop(0, n_pages, start, 0)
    lax.fori_loop(0, n_pages, wait, 0)
    _compress_body(xbuf[...], pe_ref, w1_ref, w2_ref, o_ref)


def compress_paged(cache, layer, page_table, pe, w1, w2):
    b, n_pages = page_table.shape
    chunks = PAGE_SIZE // CMP_STRIDE
    cw = CMP_STRIDE * KV_W
    cache_c = cache.reshape(cache.shape[0], cache.shape[1], chunks, cw)
    n_chunks = n_pages * chunks
    grid_spec = pltpu.PrefetchScalarGridSpec(
        num_scalar_prefetch=1,
        grid=(b,),
        in_specs=[pl.BlockSpec(memory_space=pl.ANY),
                  pl.BlockSpec(pe.shape, lambda bi, pt: (0, 0)),
                  pl.BlockSpec(w1.shape, lambda bi, pt: (0, 0)),
                  pl.BlockSpec(w2.shape, lambda bi, pt: (0, 0))],
        out_specs=pl.BlockSpec((None, NSA_KV_HEADS, n_chunks, HEAD_DIM), lambda bi, pt: (bi, 0, 0, 0)),
        scratch_shapes=[pltpu.VMEM((n_chunks, cw), F32), pltpu.SemaphoreType.DMA(())],
    )
    return pl.pallas_call(
        functools.partial(_compress_paged_kernel, layer=layer, n_pages=n_pages),
        grid_spec=grid_spec,
        out_shape=jax.ShapeDtypeStruct((b, NSA_KV_HEADS, n_chunks, HEAD_DIM), F32),
        compiler_params=_cparams(("arbitrary",)),
        name="compress_paged",
    )(page_table.reshape(-1), cache_c, pe, w1, w2)


def _cmp_select_sample_kernel(q_ref, kc_ref, vc_ref, cov_ref, o_ref, idx_ref, q8_ref, *, q_pos, n_blk):
    nc = kc_ref.shape[1]
    nb_pad = cov_ref.shape[1]
    blk = lax.broadcasted_iota(I32, (1, nb_pad), 1)
    cur = q_pos // SEL_BLOCK
    forced = (blk == 0) | (blk == cur) | (blk == cur - 1)
    future = (blk * SEL_BLOCK > q_pos) | (blk >= n_blk)
    c_last = lax.broadcasted_iota(I32, (GROUP_ROWS, nc), 1) * CMP_STRIDE + (CMP_BLOCK - 1)
    valid = (c_last <= q_pos) & (lax.broadcasted_iota(I32, (GROUP_ROWS, nc), 0) < NSA_GROUP)
    for h in range(NSA_KV_HEADS):
        q8f = _group_rows(q_ref, h)
        q8_ref[h] = q8f
        q8 = q8f.astype(BF16)
        s = jnp.where(valid, _nt(q8, kc_ref[h].astype(BF16)) * SCALE, NEG)
        m = jnp.max(s, axis=-1, keepdims=True)
        e = jnp.where(valid, jnp.exp(s - m), 0.0)
        p = (e / jnp.maximum(jnp.sum(e, axis=-1, keepdims=True), 1e-30)).astype(BF16)
        o_ref[h] = jnp.dot(p, vc_ref[h].astype(BF16), preferred_element_type=F32)
        imp = jnp.sum(jnp.dot(p, cov_ref[...], preferred_element_type=F32), axis=0, keepdims=True)
        imp = jnp.where(future, -1e9, jnp.where(forced, 1e9, imp))
        jj = lax.broadcasted_iota(I32, (nb_pad, nb_pad), 0)
        ss = lax.broadcasted_iota(I32, (nb_pad, nb_pad), 1)
        row = jnp.broadcast_to(imp, (nb_pad, nb_pad))
        col = jnp.sum(jnp.where(jj == ss, row, 0.0), axis=1, keepdims=True)
        beats = jnp.where(col > row, 1.0, jnp.where(col == row, jnp.where(jj < ss, 1.0, 0.0), 0.0))
        rank = jnp.sum(beats, axis=0, keepdims=True)
        want = lax.broadcasted_iota(I32, (SEL_TOP, nb_pad), 0).astype(F32)
        lane = lax.broadcasted_iota(I32, (SEL_TOP, nb_pad), 1).astype(F32)
        idx = jnp.sum(jnp.where(jnp.broadcast_to(rank, (SEL_TOP, nb_pad)) == want, lane, 0.0), axis=1, keepdims=True)
        idx_ref[h] = idx.astype(I32)


def cmp_select_sample(q, kcmp, vcmp, q_pos):
    b = q.shape[0]
    nc = kcmp.shape[2]
    n_blk = q_pos // SEL_BLOCK + 1
    nb_pad = -(-n_blk // LANES) * LANES
    cov = _cover_t(nc, nb_pad, nc - 1)[:, :nc].T
    cov = jnp.asarray(cov, BF16)
    return pl.pallas_call(
        functools.partial(_cmp_select_sample_kernel, q_pos=q_pos, n_blk=n_blk),
        grid=(b,),
        in_specs=[pl.BlockSpec((None, 1, TOK_W), lambda bi: (bi, 0, 0)),
                  pl.BlockSpec((None, NSA_KV_HEADS, nc, HEAD_DIM), lambda bi: (bi, 0, 0, 0)),
                  pl.BlockSpec((None, NSA_KV_HEADS, nc, HEAD_DIM), lambda bi: (bi, 0, 0, 0)),
                  pl.BlockSpec(cov.shape, lambda bi: (0, 0))],
        out_specs=[pl.BlockSpec((None, NSA_KV_HEADS, GROUP_ROWS, HEAD_DIM), lambda bi: (bi, 0, 0, 0)),
                   pl.BlockSpec((None, NSA_KV_HEADS, SEL_TOP, 1), lambda bi: (bi, 0, 0, 0)),
                   pl.BlockSpec((None, NSA_KV_HEADS, GROUP_ROWS, HEAD_DIM), lambda bi: (bi, 0, 0, 0))],
        out_shape=[jax.ShapeDtypeStruct((b, NSA_KV_HEADS, GROUP_ROWS, HEAD_DIM), F32),
                   jax.ShapeDtypeStruct((b, NSA_KV_HEADS, SEL_TOP, 1), I32),
                   jax.ShapeDtypeStruct((b, NSA_KV_HEADS, GROUP_ROWS, HEAD_DIM), F32)],
        compiler_params=_cparams(("parallel",)),
        name="cmp_select_sample",
    )(q, kcmp, vcmp, cov)


def _dot_new_row(q8, k_row):
    a = q8.astype(BF16).astype(F32)
    b = k_row.astype(BF16).astype(F32)
    return jnp.sum(a * b, axis=-1, keepdims=True)


def _sel_sample_kernel(pt_ref, ix_ref, q_ref, k_ref, v_ref, kn_ref, vn_ref, o_ref, m_ref, l_ref, acc_ref, *, n_cache_blk):
    b, h, r = pl.program_id(0), pl.program_id(1), pl.program_id(2)

    @pl.when(r == 0)
    def _():
        m_ref[...] = jnp.full(m_ref.shape, NEG, F32)
        l_ref[...] = jnp.zeros(l_ref.shape, F32)
        acc_ref[...] = jnp.zeros(acc_ref.shape, F32)

    q8 = q_ref[...]
    in_cache = ix_ref[(b * NSA_KV_HEADS + h) * SEL_TOP + r] < n_cache_blk
    rows_ok = lax.broadcasted_iota(I32, (GROUP_ROWS, SEL_BLOCK), 0) < NSA_GROUP
    valid = rows_ok & in_cache
    s = jnp.where(valid, _nt(q8.astype(BF16), k_ref[...].astype(BF16)) * SCALE, NEG)
    m_old = m_ref[...]
    m_new = jnp.maximum(m_old, jnp.max(s, axis=-1, keepdims=True))
    e = jnp.where(valid, jnp.exp(s - m_new), 0.0)
    alpha = jnp.exp(m_old - m_new)
    l_ref[...] = alpha * l_ref[...] + jnp.sum(e, axis=-1, keepdims=True)
    acc_ref[...] = alpha * acc_ref[...] + jnp.dot(e.astype(BF16), v_ref[...].astype(BF16), preferred_element_type=F32)
    m_ref[...] = m_new

    @pl.when(r == SEL_TOP - 1)
    def _():
        s_new = _dot_new_row(q8, kn_ref[...]) * SCALE
        m_old = m_ref[...]
        m_fin = jnp.maximum(m_old, s_new)
        alpha = jnp.exp(m_old - m_fin)
        e_new = jnp.exp(s_new - m_fin)
        l_fin = alpha * l_ref[...] + e_new
        acc = alpha * acc_ref[...] + e_new.astype(BF16).astype(F32) * vn_ref[...].astype(BF16).astype(F32)
        o_ref[...] = acc / jnp.maximum(l_fin, 1e-30)


def sel_attention_sample(q8, idx, cache_k, cache_v, layer, page_table, k_new, v_new):
    b, n_pages = page_table.shape
    per_page = PAGE_SIZE // SEL_BLOCK
    n_cache_blk = n_pages * per_page
    ck = cache_k.reshape(cache_k.shape[0], cache_k.shape[1] * per_page, SEL_BLOCK, KV_W)
    cv = cache_v.reshape(cache_v.shape[0], cache_v.shape[1] * per_page, SEL_BLOCK, KV_W)

    def blk_index(bi, h, r, pt, ix):
        s = jnp.minimum(ix[(bi * NSA_KV_HEADS + h) * SEL_TOP + r], n_cache_blk - 1)
        return (layer, pt[bi * n_pages + s // per_page] * per_page + s % per_page, 0, h)

    grid_spec = pltpu.PrefetchScalarGridSpec(
        num_scalar_prefetch=2,
        grid=(b, NSA_KV_HEADS, SEL_TOP),
        in_specs=[pl.BlockSpec((None, None, GROUP_ROWS, HEAD_DIM), lambda bi, h, r, pt, ix: (bi, h, 0, 0)),
                  pl.BlockSpec((None, None, SEL_BLOCK, HEAD_DIM), blk_index),
                  pl.BlockSpec((None, None, SEL_BLOCK, HEAD_DIM), blk_index),
                  pl.BlockSpec((None, 1, HEAD_DIM), lambda bi, h, r, pt, ix: (bi, 0, h)),
                  pl.BlockSpec((None, 1, HEAD_DIM), lambda bi, h, r, pt, ix: (bi, 0, h))],
        out_specs=pl.BlockSpec((None, None, GROUP_ROWS, HEAD_DIM), lambda bi, h, r, pt, ix: (bi, h, 0, 0)),
        scratch_shapes=[pltpu.VMEM((GROUP_ROWS, 1), F32), pltpu.VMEM((GROUP_ROWS, 1), F32),
                        pltpu.VMEM((GROUP_ROWS, HEAD_DIM), F32)],
    )
    return pl.pallas_call(
        functools.partial(_sel_sample_kernel, n_cache_blk=n_cache_blk),
        grid_spec=grid_spec,
        out_shape=jax.ShapeDtypeStruct((b, NSA_KV_HEADS, GROUP_ROWS, HEAD_DIM), F32),
        compiler_params=_cparams(("arbitrary", "arbitrary", "arbitrary")),
        name="sel_attention_sample",
    )(page_table.reshape(-1), idx.reshape(-1), q8, ck, cv, k_new, v_new)


def _win_sample_kernel(q_ref, k_ref, v_ref, kn_ref, vn_ref, oc_ref, os_ref, gate_ref, o_ref):
    h = pl.program_id(1)
    wb = k_ref.shape[0]
    q8 = q_ref[...]
    col = lax.broadcasted_iota(I32, (GROUP_ROWS, wb), 1)
    valid = col >= 1
    s = jnp.where(valid, _nt(q8.astype(BF16), k_ref[...].astype(BF16)) * SCALE, NEG)
    s_new = _dot_new_row(q8, kn_ref[...]) * SCALE
    m = jnp.maximum(jnp.max(s, axis=-1, keepdims=True), s_new)
    e = jnp.where(valid, jnp.exp(s - m), 0.0)
    e_new = jnp.exp(s_new - m)
    den = jnp.maximum(jnp.sum(e, axis=-1, keepdims=True) + e_new, 1e-30)
    p = (e / den).astype(BF16)
    p_new = (e_new / den).astype(BF16).astype(F32)
    o_win = jnp.dot(p, v_ref[...].astype(BF16), preferred_element_type=F32) + p_new * vn_ref[...].astype(BF16).astype(F32)
    gate = jnp.broadcast_to(gate_ref[...], (GROUP_ROWS, LANES))
    lane = lax.broadcasted_iota(I32, (GROUP_ROWS, LANES), 1)
    head = h * NSA_GROUP + lax.broadcasted_iota(I32, (GROUP_ROWS, LANES), 0)
    g = [jnp.sum(jnp.where(lane == head * 3 + j, gate, 0.0), axis=-1, keepdims=True) for j in range(3)]
    o_ref[...] = g[0] * oc_ref[...] + g[1] * os_ref[...] + g[2] * o_win


def win_combine_sample(q8, win_k, win_v, layer, k_new, v_new, o_cmp, o_sel, gate):
    b = q8.shape[0]
    wb = win_k.shape[2]
    assert wb == WINDOW
    wk = win_k.reshape(win_k.shape[0], b, wb, KV_W)
    wv = win_v.reshape(win_v.shape[0], b, wb, KV_W)
    gspec = pl.BlockSpec((None, None, GROUP_ROWS, HEAD_DIM), lambda bi, h: (bi, h, 0, 0))
    wspec = pl.BlockSpec((None, None, wb, HEAD_DIM), lambda bi, h: (layer, bi, 0, h))
    nspec = pl.BlockSpec((None, 1, HEAD_DIM), lambda bi, h: (bi, 0, h))
    return pl.pallas_call(
        _win_sample_kernel,
        grid=(b, NSA_KV_HEADS),
        in_specs=[gspec, wspec, wspec, nspec, nspec, gspec, gspec,
                  pl.BlockSpec((None, 1, LANES), lambda bi, h: (bi, 0, 0))],
        out_specs=gspec,
        out_shape=jax.ShapeDtypeStruct((b, NSA_KV_HEADS, GROUP_ROWS, HEAD_DIM), F32),
        compiler_params=_cparams(("parallel", "arbitrary")),
        name="win_combine_sample",
    )(q8, wk, wv, k_new, v_new, o_cmp, o_sel, gate)


def nsa_sample_step(prep, layer, caches, win_k, win_v, page_table, pe, w1, w2):
    q, kc, vc, ks, vs, kw, vw, gate = prep
    b = q.shape[0]
    past_len = page_table.shape[1] * PAGE_SIZE
    cache_ck, cache_cv, cache_sk, cache_sv = caches
    kcmp = compress_paged(cache_ck, layer, page_table, pe[0], w1[0], w2[0])
    vcmp = compress_paged(cache_cv, layer, page_table, pe[1], w1[1], w2[1])
    o_cmp, idx, q8 = cmp_select_sample(q, kcmp, vcmp, past_len)
    o_sel = sel_attention_sample(q8, idx, cache_sk, cache_sv, layer, page_table, ks, vs)
    o = win_combine_sample(q8, win_k, win_v, layer, kw, vw, o_cmp, o_sel, gate)
    return o[:, :, :NSA_GROUP].reshape(b, 1, TOK_W)


def rms_norm(x, g):
    xf = x.astype(F32)
    y = xf * lax.rsqrt(jnp.mean(xf * xf, axis=-1, keepdims=True) + EPS)
    return (y * g.astype(F32)).astype(x.dtype)


def masked_softmax(s, valid):
    s = jnp.where(valid, s, NEG)
    m = jnp.max(s, axis=-1, keepdims=True)
    e = jnp.where(valid, jnp.exp(s - m), 0.0)
    return e / jnp.maximum(jnp.sum(e, axis=-1, keepdims=True), 1e-30)


def compress(k, pe, w1, w2):
    b, l = k.shape[:2]
    r = CMP_BLOCK // CMP_STRIDE
    n_chunks = l // CMP_STRIDE
    nc = n_chunks - r + 1
    c = k[:, :n_chunks * CMP_STRIDE].reshape(b, n_chunks, CMP_STRIDE, NSA_KV_HEADS, HEAD_DIM)
    blk = jnp.concatenate([c[:, i:i + nc] for i in range(r)], axis=2)
    blk = blk + pe[None, None, :, None, :]
    flat = blk.transpose(0, 1, 3, 2, 4).reshape(b, nc, NSA_KV_HEADS, CMP_BLOCK * HEAD_DIM)
    hid = jax.nn.gelu(jnp.einsum('bnhf,fe->bnhe', flat, w1))
    return jnp.einsum('bnhe,ed->bnhd', hid, w2)


def cmp_attend(q, q_pos, kc, vc):
    nc = kc.shape[1]
    c_last = jnp.arange(nc, dtype=I32) * CMP_STRIDE + CMP_BLOCK - 1
    s = jnp.einsum('bqhgd,bnhd->bqhgn', q, kc).astype(F32) * SCALE
    valid = (c_last[None, :] <= q_pos[:, None])[None, :, None, None, :]
    p = masked_softmax(s, valid)
    o = jnp.einsum('bqhgn,bnhd->bqhgd', p.astype(vc.dtype), vc)
    return o, p


def select_blocks(p, q_pos, n_sel):
    nc = p.shape[-1]
    c_start = jnp.arange(nc, dtype=I32) * CMP_STRIDE
    c_last = c_start + CMP_BLOCK - 1
    s_start = jnp.arange(n_sel, dtype=I32) * SEL_BLOCK
    cover = ((c_start[:, None] < s_start[None, :] + SEL_BLOCK) & (c_last[:, None] >= s_start[None, :])).astype(F32)
    imp = jnp.einsum('bqhgn,ns->bqhs', p, cover)
    blk = jnp.arange(n_sel, dtype=I32)[None, :]
    cur = (q_pos // SEL_BLOCK)[:, None]
    forced = (blk == 0) | (blk == cur) | (blk == cur - 1)
    future = blk * SEL_BLOCK > q_pos[:, None]
    imp = jnp.where(future[None, :, None, :], -1e9, jnp.where(forced[None, :, None, :], 1e9, imp))
    _, idx = lax.top_k(imp, min(SEL_TOP, n_sel))
    return idx


def to_blocks(k):
    b, l = k.shape[:2]
    ns = -(-l // SEL_BLOCK)
    k = jnp.pad(k, ((0, 0), (0, ns * SEL_BLOCK - l), (0, 0), (0, 0)))
    return k.reshape(b, ns, SEL_BLOCK, NSA_KV_HEADS, HEAD_DIM).transpose(0, 3, 1, 2, 4)


def sel_attend(q, q_pos, idx, kb, vb):
    idx_t = idx.transpose(0, 2, 1, 3)
    take = jax.vmap(jax.vmap(lambda blocks, i: blocks[i]))
    kg = take(kb, idx_t)
    vg = take(vb, idx_t)
    kpos = idx_t[..., None] * SEL_BLOCK + jnp.arange(SEL_BLOCK, dtype=I32)
    valid = (kpos <= q_pos[None, None, :, None, None]).transpose(0, 2, 1, 3, 4)[:, :, :, None]
    s = jnp.einsum('bqhgd,bhqkld->bqhgkl', q, kg).astype(F32) * SCALE
    b, tq, h, g, kk, lb = s.shape
    p = masked_softmax(s.reshape(b, tq, h, g, kk * lb), valid.reshape(b, tq, h, 1, kk * lb)).reshape(s.shape)
    return jnp.einsum('bqhgkl,bhqkld->bqhgd', p.astype(vg.dtype), vg)


def window_attend(q, q_pos, k, v, k_pos):
    s = jnp.einsum('bqhgd,bkhd->bqhgk', q, k).astype(F32) * SCALE
    diff = q_pos[:, None] - k_pos[None, :]
    valid = ((diff >= 0) & (diff < WINDOW) & (k_pos[None, :] >= 0))[None, :, None, None, :]
    p = masked_softmax(s, valid)
    return jnp.einsum('bqhgk,bkhd->bqhgd', p.astype(v.dtype), v)


def gather_pages(cache, page_table):
    c = cache[page_table]
    return c.reshape(page_table.shape[0], page_table.shape[1] * cache.shape[1], *cache.shape[2:])


def nsa_sample(prep, pos, past, win_k, win_v, page_table, past_len, pe, w1, w2):
    q, kc, vc, ks, vs, kw, vw, gate = prep
    b, t = q.shape[:2]
    kvr = lambda a: a.reshape(b, t, NSA_KV_HEADS, HEAD_DIM)
    q = q.reshape(b, t, NSA_KV_HEADS, NSA_GROUP, HEAD_DIM)
    kc, vc, ks, vs, kw, vw = (kvr(a) for a in (kc, vc, ks, vs, kw, vw))
    gate = gate[..., :3 * NSA_HEADS].reshape(b, t, NSA_KV_HEADS, NSA_GROUP, 3)
    full = [jnp.concatenate([gather_pages(c, page_table).astype(new.dtype), new], axis=1)
            for c, new in zip(past, (kc, vc, ks, vs))]
    kcmp = compress(full[0], pe[0], w1[0], w2[0])
    vcmp = compress(full[1], pe[1], w1[1], w2[1])
    o_cmp, p_cmp = cmp_attend(q, pos, kcmp, vcmp)
    ksb, vsb = to_blocks(full[2]), to_blocks(full[3])
    idx = select_blocks(p_cmp, pos, ksb.shape[2])
    o_sel = sel_attend(q, pos, idx, ksb, vsb)
    wb = win_k.shape[1]
    kwin = jnp.concatenate([win_k.astype(kw.dtype), kw], axis=1)
    vwin = jnp.concatenate([win_v.astype(vw.dtype), vw], axis=1)
    kpos = past_len - wb + jnp.arange(wb + t, dtype=I32)
    o_win = window_attend(q, pos, kwin, vwin, kpos)
    o = (gate[..., 0, None] * o_cmp + gate[..., 1, None] * o_sel + gate[..., 2, None] * o_win).reshape(b, t, TOK_W)
    return o, (kc, vc, ks, vs), (kwin[:, -wb:], vwin[:, -wb:])


def _reorder_nsa_weight(w):
    n_gl = 3 * NSA_HEADS
    parts = [w[:, :NSA_MQ0], w[:, NSA_MQ0 + n_gl:NSA_MQ0 + n_gl + MEM_W], w[:, NSA_MQ0:NSA_MQ0 + n_gl]]
    wr = jnp.concatenate(parts, axis=1)
    return jnp.pad(wr, ((0, 0), (0, NSA_ZW - wr.shape[1]))).astype(BF16)


def kernel(x_prompt, x_sample, state_pool, cache_cmp_k, cache_cmp_v, cache_sel_k, cache_sel_v, state_win_k, state_win_v, cache_mem_k, cache_mem_v, page_table, mem_prompt, norm_mix_g, norm_ffn_g, norm_mem_g, w_mem_kv, mem_q_norm_g, mem_k_norm_g, w_in_pool, w_pool_grp, pool_scale, w_out_pool, w_in_nsa, b_gate, nsa_q_norm_g, nsa_k_norm_g, cmp_pe, cmp_w1, cmp_w2, w_out_nsa, w_router, b_router, w_gate, w_up, w_down):
    bp, t_p, d = x_prompt.shape
    bs, t_s, _ = x_sample.shape
    assert t_s == 1
    n_p, n_s = bp * t_p, bs * t_s
    m_len = mem_prompt.shape[1]
    past_len = page_table.shape[1] * PAGE_SIZE
    pos_p = jnp.arange(t_p, dtype=I32)
    pos_s = past_len + jnp.arange(t_s, dtype=I32)
    xp = x_prompt.reshape(n_p, d)
    xs = x_sample.reshape(n_s, d)
    mem_flat = mem_prompt.reshape(-1, d)
    pool_p, pool_s, rows_p, rows_s, win_p, win_s, mem_k_p, mem_v_p = [], [], [], [], [], [], [], []
    for i in range(DEPTH):
        li = i // 2
        kv = proj(mem_flat, w_mem_kv[i].astype(BF16), gain=norm_mem_g[i])
        mk = head_norm(kv, 0, MEM_W, mem_k_norm_g[i]).reshape(bp, m_len, MEM_W)
        mv = kv[:, MEM_W:].reshape(bp, m_len, MEM_W)
        mem_k_p.append(mk.reshape(bp, m_len, MEM_HEADS, HEAD_DIM))
        mem_v_p.append(mv.reshape(bp, m_len, MEM_HEADS, HEAD_DIM))
        mk_s = cache_mem_k[i].reshape(bs, m_len, MEM_W)
        mv_s = cache_mem_v[i].reshape(bs, m_len, MEM_W)
        if i % 2 == 0:
            w_in = w_in_pool[li].astype(BF16)
            zp = proj(xp, w_in, gain=norm_mix_g[i]).reshape(bp, t_p, -1)
            zs = proj(xs, w_in, gain=norm_mix_g[i]).reshape(bs, t_s, -1)
            op = pool_mix(zp, None, w_pool_grp[li], pool_scale[li], 0)
            zs16 = jnp.pad(zs, ((0, 0), (0, POOL_HALO - t_s), (0, 0)))
            halo = jnp.pad(state_pool[li], ((0, 0), (1, 0), (0, 0)))
            os_ = pool_mix(zs16, halo, w_pool_grp[li], pool_scale[li], past_len)[:, :t_s]
            pool_p.append(zp[:, t_p - POOL_STATE:, :TOK_W])
            pool_s.append(jnp.concatenate([state_pool[li], zs[..., :TOK_W]], axis=1)[:, -POOL_STATE:])
            mq0 = TOK_W
            w_out = w_out_pool[li]
        else:
            w_in = _reorder_nsa_weight(w_in_nsa[li])
            zp = proj(xp, w_in, gain=norm_mix_g[i]).reshape(bp, t_p, -1)
            zs = proj(xs, w_in, gain=norm_mix_g[i]).reshape(bs, t_s, -1)
            q, kc, vc, ks, vs, kw, vw, gate = nsa_prep(zp, pos_p, nsa_q_norm_g[li], nsa_k_norm_g[li], b_gate[li])
            kcmp = compress_rows(kc, cmp_pe[li, 0], cmp_w1[li, 0], cmp_w2[li, 0])
            vcmp = compress_rows(vc, cmp_pe[li, 1], cmp_w1[li, 1], cmp_w2[li, 1])
            o_cmp, sel_t = cmp_select_prompt(q, kcmp, vcmp)
            o_sel = sel_attention_prompt(q, ks, vs, sel_t)
            op = win_combine_prompt(q, kw, vw, o_cmp, o_sel, gate)
            kvr = lambda a: a.reshape(a.shape[0], a.shape[1], NSA_KV_HEADS, HEAD_DIM)
            rows_p.append(tuple(kvr(a) for a in (kc, vc, ks, vs)))
            wb = min(WINDOW, t_p)
            win_p.append((kvr(kw[:, t_p - wb:]), kvr(vw[:, t_p - wb:])))
            prep_s = nsa_prep(zs, pos_s, nsa_q_norm_g[li], nsa_k_norm_g[li], b_gate[li])
            os_ = nsa_sample_step(prep_s, li, (cache_cmp_k, cache_cmp_v, cache_sel_k, cache_sel_v),
                                  state_win_k, state_win_v, page_table, cmp_pe[li], cmp_w1[li], cmp_w2[li])
            rows_s.append(tuple(kvr(a) for a in prep_s[1:5]))
            win_s.append(tuple(jnp.concatenate([st[li][:, t_s:], kvr(new)], axis=1)
                               for st, new in ((state_win_k, prep_s[5]), (state_win_v, prep_s[6]))))
            mq0 = NSA_MQ0
            w_out = w_out_nsa[li]
        ap = mem_attention(zp, mq0, mk, mv, mem_q_norm_g[i])
        as_ = mem_attention(zs, mq0, mk_s, mv_s, mem_q_norm_g[i])
        w_out = w_out.astype(BF16)
        xp = proj([op.reshape(n_p, TOK_W), ap.reshape(n_p, MEM_W)], w_out, residual=xp)
        xs = proj([os_.reshape(n_s, TOK_W), as_.reshape(n_s, MEM_W)], w_out, residual=xs)
        xn_p, e_p, g_p = router(xp, norm_ffn_g[i], w_router, b_router)
        xs_pad = jnp.pad(xs, ((0, LANES - n_s), (0, 0)))
        xn_s, e_s, g_s = router(xs_pad, norm_ffn_g[i], w_router, b_router)
        xa = moe_ffn(jnp.concatenate([xp, xs], axis=0),
                     jnp.concatenate([xn_p, xn_s[:n_s]], axis=0),
                     jnp.concatenate([e_p, e_s[:, :n_s]], axis=1),
                     jnp.concatenate([g_p, g_s[:, :n_s]], axis=1),
                     w_gate, w_up, w_down, i)
        xp, xs = xa[:n_p], xa[n_p:]
    stk = lambda lst, j: jnp.stack([r[j] for r in lst])
    return (xp.reshape(bp, t_p, d), xs.reshape(bs, t_s, d),
            jnp.stack(pool_p), jnp.stack(pool_s),
            stk(rows_p, 0), stk(rows_p, 1), stk(rows_p, 2), stk(rows_p, 3),
            stk(rows_s, 0), stk(rows_s, 1), stk(rows_s, 2), stk(rows_s, 3),
            stk(win_p, 0), stk(win_p, 1), stk(win_s, 0), stk(win_s, 1),
            jnp.stack(mem_k_p), jnp.stack(mem_v_p))
```

```python
import functools

import jax
import jax.numpy as jnp
import numpy as np
from jax import lax
from jax.experimental import pallas as pl
from jax.experimental.pallas import tpu as pltpu

D_MODEL = 2048
DEPTH = 4
PAGE_SIZE = 128
HEAD_DIM = 128
ROPE_THETA = 500000.0
ROPE_DIM = HEAD_DIM // 4
MEM_HEADS = 4
MEM_W = MEM_HEADS * HEAD_DIM
TOK_W = D_MODEL - MEM_W
POOL_WINDOWS = (2, 4, 8, 16)
POOL_GROUP = TOK_W // len(POOL_WINDOWS)
POOL_STATE = max(POOL_WINDOWS) - 1
POOL_HALO = POOL_STATE + 1
NSA_HEADS = TOK_W // HEAD_DIM
NSA_KV_HEADS = 2
NSA_GROUP = NSA_HEADS // NSA_KV_HEADS
KV_W = NSA_KV_HEADS * HEAD_DIM
GROUP_W = NSA_GROUP * HEAD_DIM
CMP_BLOCK = 32
CMP_STRIDE = 16
SEL_BLOCK = 64
SEL_TOP = 16
WINDOW = 512
NSA_TOK_IN = NSA_HEADS * HEAD_DIM + 6 * KV_W + 3 * NSA_HEADS
N_EXPERTS = 16
N_GROUPS = 4
EXP_PER_GROUP = N_EXPERTS // N_GROUPS
TOP_K = 2
D_FF = 1024
EPS = 1e-6
NEG = -1e30
SCALE = HEAD_DIM ** -0.5
LANES = 128

V7X_VMEM_BYTES = 64 * 1024 * 1024
VMEM_LIMIT = V7X_VMEM_BYTES * 3 // 4

BF16 = jnp.bfloat16
F32 = jnp.float32
I32 = jnp.int32

NSA_Q0 = 0
NSA_KV0 = NSA_HEADS * HEAD_DIM
NSA_MQ0 = NSA_KV0 + 6 * KV_W
NSA_GL0 = NSA_MQ0 + MEM_W
NSA_ZW = 3840


def _cparams(sem):
    return pltpu.CompilerParams(dimension_semantics=sem, vmem_limit_bytes=VMEM_LIMIT)


def _nt(a, b):
    return lax.dot_general(a, b, (((1,), (1,)), ((), ())), preferred_element_type=F32)


def _tn(a, b):
    return lax.dot_general(a, b, (((0,), (0,)), ((), ())), preferred_element_type=F32)


def _rms(x, g):
    return x * lax.rsqrt(jnp.mean(x * x, axis=-1, keepdims=True) + EPS) * g


def _pick_tile(n, cap, unit):
    if n <= cap:
        return n
    best = None
    for t in range(unit, cap + 1, unit):
        if n % t == 0:
            best = t
    assert best is not None, (n, cap, unit)
    return best


def _proj_kernel(*refs, n_x, norm, residual):
    x_refs, refs = refs[:n_x], refs[n_x:]
    if norm:
        g_ref, refs = refs[0], refs[1:]
    w_ref, refs = refs[0], refs[1:]
    if residual:
        r_ref, refs = refs[0], refs[1:]
    o_ref, xn_ref = refs

    @pl.when(pl.program_id(1) == 0)
    def _():
        off = 0
        for x_ref in x_refs:
            x = x_ref[...].astype(F32)
            if norm:
                x = _rms(x, g_ref[...])
            xn_ref[:, off:off + x.shape[1]] = x.astype(BF16)
            off += x.shape[1]

    y = jnp.dot(xn_ref[...], w_ref[...], preferred_element_type=F32)
    if residual:
        y = y + r_ref[...]
    o_ref[...] = y


def proj(xs, w_bf16, gain=None, residual=None):
    if not isinstance(xs, (list, tuple)):
        xs = [xs]
    assert gain is None or len(xs) == 1
    m = xs[0].shape[0]
    k = sum(x.shape[1] for x in xs)
    n = w_bf16.shape[1]
    tm = _pick_tile(m, 512, 8)
    tn = _pick_tile(n, 1024, LANES)
    in_specs = [pl.BlockSpec((tm, x.shape[1]), lambda i, j: (i, 0)) for x in xs]
    args = list(xs)
    if gain is not None:
        in_specs.append(pl.BlockSpec((1, k), lambda i, j: (0, 0)))
        args.append(gain.reshape(1, k).astype(F32))
    in_specs.append(pl.BlockSpec((k, tn), lambda i, j: (0, j)))
    args.append(w_bf16)
    if residual is not None:
        in_specs.append(pl.BlockSpec((tm, tn), lambda i, j: (i, j)))
        args.append(residual)
    return pl.pallas_call(
        functools.partial(_proj_kernel, n_x=len(xs), norm=gain is not None, residual=residual is not None),
        grid=(m // tm, n // tn),
        in_specs=in_specs,
        out_specs=pl.BlockSpec((tm, tn), lambda i, j: (i, j)),
        out_shape=jax.ShapeDtypeStruct((m, n), F32),
        scratch_shapes=[pltpu.VMEM((tm, k), BF16)],
        compiler_params=_cparams(("parallel", "arbitrary")),
        name="proj",
    )(*args)


def _head_norm_kernel(x_ref, g_ref, o_ref):
    x = x_ref[...]
    g = g_ref[...]
    n_heads = x.shape[1] // HEAD_DIM
    o_ref[...] = jnp.concatenate(
        [_rms(x[:, h * HEAD_DIM:(h + 1) * HEAD_DIM], g) for h in range(n_heads)], axis=1)


def head_norm(x, col0, width, gain):
    m = x.shape[0]
    assert col0 % width == 0
    return pl.pallas_call(
        _head_norm_kernel,
        grid=(1,),
        in_specs=[pl.BlockSpec((m, width), lambda i: (0, col0 // width)),
                  pl.BlockSpec((1, HEAD_DIM), lambda i: (0, 0))],
        out_specs=pl.BlockSpec((m, width), lambda i: (0, 0)),
        out_shape=jax.ShapeDtypeStruct((m, width), F32),
        compiler_params=_cparams(("arbitrary",)),
        name="head_norm",
    )(x, gain.reshape(1, HEAD_DIM))


def _mem_attn_kernel(q_ref, k_ref, v_ref, g_ref, o_ref):
    q = q_ref[...]
    rows = q.shape[0]
    if rows < 8:
        q = jnp.broadcast_to(q[0:1], (8, q.shape[1]))
    g = g_ref[...]
    outs = []
    for h in range(MEM_HEADS):
        sl = slice(h * HEAD_DIM, (h + 1) * HEAD_DIM)
        qh = _rms(q[:, sl], g).astype(BF16)
        s = _nt(qh, k_ref[:, sl].astype(BF16)) * SCALE
        m = jnp.max(s, axis=-1, keepdims=True)
        e = jnp.exp(s - m)
        p = e / jnp.sum(e, axis=-1, keepdims=True)
        outs.append(jnp.dot(p.astype(BF16), v_ref[:, sl].astype(BF16), preferred_element_type=F32))
    o = jnp.concatenate(outs, axis=1)
    o_ref[...] = o[:rows]


def mem_attention(z, col0, k, v, gain):
    b, t, _ = z.shape
    m = k.shape[1]
    tq = _pick_tile(t, 512, 8)
    assert col0 % MEM_W == 0
    return pl.pallas_call(
        _mem_attn_kernel,
        grid=(b, t // tq),
        in_specs=[pl.BlockSpec((None, tq, MEM_W), lambda bi, i: (bi, i, col0 // MEM_W)),
                  pl.BlockSpec((None, m, MEM_W), lambda bi, i: (bi, 0, 0)),
                  pl.BlockSpec((None, m, MEM_W), lambda bi, i: (bi, 0, 0)),
                  pl.BlockSpec((1, HEAD_DIM), lambda bi, i: (0, 0))],
        out_specs=pl.BlockSpec((None, tq, MEM_W), lambda bi, i: (bi, i, 0)),
        out_shape=jax.ShapeDtypeStruct((b, t, MEM_W), F32),
        compiler_params=_cparams(("parallel", "arbitrary")),
        name="mem_attention",
    )(z, k, v, gain.reshape(1, HEAD_DIM))


def _pool_kernel(u_ref, halo_ref, w_ref, sc_ref, o_ref, *, pos0, zero_first_halo):
    tq = u_ref.shape[0]
    qi = pl.program_id(1)
    u = u_ref[...]
    halo = halo_ref[...]
    if zero_first_halo:
        halo = jnp.where(qi > 0, halo, 0.0)
    pos = pos0 + qi * tq + lax.broadcasted_iota(I32, (tq, 1), 0)
    outs = []
    for g, w in enumerate(POOL_WINDOWS):
        cs = slice(g * POOL_GROUP, (g + 1) * POOL_GROUP)
        ug = u[:, cs]
        acc = jnp.concatenate([halo[:, cs], ug], axis=0)
        span = 1
        while span < w:
            acc = acc[span:] + acc[:-span]
            span *= 2
        ssum = acc[POOL_HALO - (w - 1):POOL_HALO - (w - 1) + tq]
        cnt = jnp.minimum(w, pos + 1).astype(F32)
        d = (ssum / cnt - ug).astype(BF16)
        y = jnp.dot(d, w_ref[g].astype(BF16), preferred_element_type=F32)
        outs.append(y)
    o_ref[...] = jnp.concatenate(outs, axis=1) * sc_ref[...]


def pool_mix(z, halo, w_grp, scale, pos0):
    b, t, _ = z.shape
    tq = _pick_tile(t, 256, 16)
    if halo is None:
        halo_arr = z
        halo_spec = pl.BlockSpec((None, POOL_HALO, TOK_W),
                                 lambda bi, i: (bi, jnp.maximum(i * (tq // POOL_HALO) - 1, 0), 0))
    else:
        assert t == tq
        halo_arr = halo
        halo_spec = pl.BlockSpec((None, POOL_HALO, TOK_W), lambda bi, i: (bi, 0, 0))
    return pl.pallas_call(
        functools.partial(_pool_kernel, pos0=pos0, zero_first_halo=halo is None),
        grid=(b, t // tq),
        in_specs=[pl.BlockSpec((None, tq, TOK_W), lambda bi, i: (bi, i, 0)),
                  halo_spec,
                  pl.BlockSpec(w_grp.shape, lambda bi, i: (0, 0, 0)),
                  pl.BlockSpec((1, TOK_W), lambda bi, i: (0, 0))],
        out_specs=pl.BlockSpec((None, tq, TOK_W), lambda bi, i: (bi, i, 0)),
        out_shape=jax.ShapeDtypeStruct((b, t, TOK_W), F32),
        compiler_params=_cparams(("parallel", "arbitrary")),
        name="pool_mix",
    )(z, halo_arr, w_grp, scale.reshape(1, TOK_W))


def _rope_tables(pos):
    half = ROPE_DIM // 2
    inv = 1.0 / (ROPE_THETA ** (jnp.arange(half, dtype=F32) * 2.0 / ROPE_DIM))
    ang = pos.astype(F32)[:, None] * inv[None, :]
    cos, sin = jnp.cos(ang), jnp.sin(ang)
    t = pos.shape[0]
    rest = HEAD_DIM - ROPE_DIM
    c = jnp.concatenate([cos, cos, jnp.ones((t, rest), F32)], axis=1)
    s_lo = jnp.concatenate([-sin, jnp.zeros((t, HEAD_DIM - half), F32)], axis=1)
    s_hi = jnp.concatenate([jnp.zeros((t, half), F32), sin, jnp.zeros((t, rest), F32)], axis=1)
    return c, s_lo, s_hi


def _nsa_prep_kernel(z_ref, c_ref, slo_ref, shi_ref, qg_ref, kg_ref, bg_ref,
                     q_ref, kc_ref, vc_ref, ks_ref, vs_ref, kw_ref, vw_ref, gate_ref):
    rows = z_ref.shape[0]
    c, slo, shi = c_ref[...], slo_ref[...], shi_ref[...]
    half = ROPE_DIM // 2

    def rope_norm(x, g):
        if rows < 8:
            x = jnp.broadcast_to(x[0:1], (8, HEAD_DIM))
        x = _rms(x, g)
        y = x * c + pltpu.roll(x, HEAD_DIM - half, 1) * slo + pltpu.roll(x, half, 1) * shi
        return y[:rows]

    qg = qg_ref[...]
    q_ref[...] = jnp.concatenate(
        [rope_norm(z_ref[:, NSA_Q0 + h * HEAD_DIM:NSA_Q0 + (h + 1) * HEAD_DIM], qg) for h in range(NSA_HEADS)], axis=1)
    for j, (o_ref, which) in enumerate(((kc_ref, 0), (vc_ref, None), (ks_ref, 1), (vs_ref, None), (kw_ref, 2), (vw_ref, None))):
        c0 = NSA_KV0 + j * KV_W
        if which is None:
            o_ref[...] = z_ref[:, c0:c0 + KV_W]
        else:
            g = kg_ref[which:which + 1, :]
            o_ref[...] = jnp.concatenate(
                [rope_norm(z_ref[:, c0 + h * HEAD_DIM:c0 + (h + 1) * HEAD_DIM], g) for h in range(NSA_KV_HEADS)], axis=1)
    gate_ref[...] = jax.nn.sigmoid(z_ref[:, NSA_GL0:NSA_GL0 + LANES] + bg_ref[...])


def nsa_prep(z, pos, q_g, k_g, b_gate):
    b, t, _ = z.shape
    tq = _pick_tile(t, 256, 8)
    c, slo, shi = _rope_tables(pos)
    if t < 8:
        c, slo, shi = (jnp.broadcast_to(a, (8, HEAD_DIM)) for a in (c, slo, shi))
    tt = max(tq, 8)
    bg = jnp.pad(b_gate.reshape(1, -1), ((0, 0), (0, LANES - b_gate.shape[-1])))
    tab_spec = pl.BlockSpec((tt, HEAD_DIM), lambda bi, i: (i, 0))
    kv_spec = pl.BlockSpec((None, tq, KV_W), lambda bi, i: (bi, i, 0))
    kv_shape = jax.ShapeDtypeStruct((b, t, KV_W), F32)
    return pl.pallas_call(
        _nsa_prep_kernel,
        grid=(b, t // tq),
        in_specs=[pl.BlockSpec((None, tq, NSA_ZW), lambda bi, i: (bi, i, 0)),
                  tab_spec, tab_spec, tab_spec,
                  pl.BlockSpec((1, HEAD_DIM), lambda bi, i: (0, 0)),
                  pl.BlockSpec((3, HEAD_DIM), lambda bi, i: (0, 0)),
                  pl.BlockSpec((1, LANES), lambda bi, i: (0, 0))],
        out_specs=[pl.BlockSpec((None, tq, TOK_W), lambda bi, i: (bi, i, 0))] + [kv_spec] * 6
                  + [pl.BlockSpec((None, tq, LANES), lambda bi, i: (bi, i, 0))],
        out_shape=[jax.ShapeDtypeStruct((b, t, TOK_W), F32)] + [kv_shape] * 6
                  + [jax.ShapeDtypeStruct((b, t, LANES), F32)],
        compiler_params=_cparams(("parallel", "arbitrary")),
        name="nsa_prep",
    )(z, c, slo, shi, q_g.reshape(1, HEAD_DIM), k_g, bg)


def _compress_body(slab, n_chunks, pe_ref, w1_ref, w2_ref, o_ref):
    pe = pe_ref[...]
    pe_lo = jnp.concatenate([pe[r:r + 1] for r in range(CMP_STRIDE)], axis=1)
    pe_hi = jnp.concatenate([pe[CMP_STRIDE + r:CMP_STRIDE + r + 1] for r in range(CMP_STRIDE)], axis=1)
    half_k = CMP_STRIDE * HEAD_DIM
    w_lo = w1_ref[0:half_k, :].astype(BF16)
    w_hi = w1_ref[half_k:2 * half_k, :].astype(BF16)
    w2 = w2_ref[...].astype(BF16)
    for h in range(NSA_KV_HEADS):
        xh = jnp.concatenate([slab(r, h) for r in range(CMP_STRIDE)], axis=1)
        a = jnp.dot((xh + pe_lo).astype(BF16), w_lo, preferred_element_type=F32)
        bb = jnp.dot((xh + pe_hi).astype(BF16), w_hi, preferred_element_type=F32)
        hid = jax.nn.gelu(a + pltpu.roll(bb, n_chunks - 1, 0))
        o_ref[h] = jnp.dot(hid.astype(BF16), w2, preferred_element_type=F32)


def _compress_kernel(x_ref, pe_ref, w1_ref, w2_ref, o_ref):
    def slab(r, h):
        c0 = (r * NSA_KV_HEADS + h) * HEAD_DIM
        return x_ref[:, c0:c0 + HEAD_DIM]
    _compress_body(slab, x_ref.shape[0], pe_ref, w1_ref, w2_ref, o_ref)


def compress_rows(x, pe, w1, w2):
    b, t, _ = x.shape
    n_chunks = t // CMP_STRIDE
    cw = CMP_STRIDE * KV_W
    xc = x.reshape(b, n_chunks, cw)
    return pl.pallas_call(
        _compress_kernel,
        grid=(b,),
        in_specs=[pl.BlockSpec((None, n_chunks, cw), lambda bi: (bi, 0, 0)),
                  pl.BlockSpec(pe.shape, lambda bi: (0, 0)),
                  pl.BlockSpec(w1.shape, lambda bi: (0, 0)),
                  pl.BlockSpec(w2.shape, lambda bi: (0, 0))],
        out_specs=pl.BlockSpec((None, NSA_KV_HEADS, n_chunks, HEAD_DIM), lambda bi: (bi, 0, 0, 0)),
        out_shape=jax.ShapeDtypeStruct((b, NSA_KV_HEADS, n_chunks, HEAD_DIM), F32),
        compiler_params=_cparams(("parallel",)),
        name="compress_rows",
    )(xc, pe, w1, w2)


def _stack_heads(q):
    return jnp.concatenate([q[:, g * HEAD_DIM:(g + 1) * HEAD_DIM] for g in range(NSA_GROUP)], axis=0)


def _unstack_heads(o, tq):
    return jnp.concatenate([o[g * tq:(g + 1) * tq] for g in range(NSA_GROUP)], axis=1)


def _rank_select(imp_t, n_real):
    n_blk = imp_t.shape[0]
    blk = lax.broadcasted_iota(I32, imp_t.shape, 0)
    cnt = jnp.zeros(imp_t.shape, F32)
    for j in range(n_real):
        row = imp_t[j:j + 1, :]
        beats = jnp.where(row > imp_t, 1.0, jnp.where(row == imp_t, jnp.where(blk > j, 1.0, 0.0), 0.0))
        cnt = cnt + beats
    return jnp.where(cnt < SEL_TOP, 1.0, 0.0)


def _cmp_select_kernel(q_ref, kc_ref, vc_ref, cov_ref, o_ref, sel_ref):
    tq = q_ref.shape[0]
    nc = kc_ref.shape[0]
    n_blk = sel_ref.shape[0]
    qi = pl.program_id(2)
    q6 = _stack_heads(q_ref[...]).astype(BF16)
    s = _nt(q6, kc_ref[...].astype(BF16)) * SCALE
    s = s.reshape(NSA_GROUP, tq, nc)
    q_pos = qi * tq + lax.broadcasted_iota(I32, (tq, nc), 0)
    c_last = lax.broadcasted_iota(I32, (tq, nc), 1) * CMP_STRIDE + (CMP_BLOCK - 1)
    valid = (c_last <= q_pos)[None]
    s = jnp.where(valid, s, NEG)
    m = jnp.max(s, axis=-1, keepdims=True)
    e = jnp.where(valid, jnp.exp(s - m), 0.0)
    p = (e / jnp.maximum(jnp.sum(e, axis=-1, keepdims=True), 1e-30)).astype(BF16)
    o = jnp.dot(p.reshape(NSA_GROUP * tq, nc), vc_ref[...].astype(BF16), preferred_element_type=F32)
    o_ref[...] = _unstack_heads(o, tq)
    p_cat = jnp.concatenate([p[g] for g in range(NSA_GROUP)], axis=1)
    imp_t = _nt(cov_ref[...], p_cat)
    blk = lax.broadcasted_iota(I32, (n_blk, tq), 0)
    pos_t = qi * tq + lax.broadcasted_iota(I32, (n_blk, tq), 1)
    cur = pos_t // SEL_BLOCK
    forced = (blk == 0) | (blk == cur) | (blk == cur - 1)
    future = blk * SEL_BLOCK > pos_t
    imp_t = jnp.where(future, -1e9, jnp.where(forced, 1e9, imp_t))
    sel_ref[...] = _rank_select(imp_t, n_blk).astype(BF16)


def _cover_t(nc, n_blk, nc_valid):
    c_start = np.arange(nc) * CMP_STRIDE
    c_last = c_start + CMP_BLOCK - 1
    s_start = np.arange(n_blk) * SEL_BLOCK
    cov = (c_start[None, :] < s_start[:, None] + SEL_BLOCK) & (c_last[None, :] >= s_start[:, None])
    cov = cov & (np.arange(nc)[None, :] < nc_valid)
    return np.tile(cov.astype(np.float32), (1, NSA_GROUP))


def cmp_select_prompt(q, kcmp, vcmp):
    b, t, _ = q.shape
    nc = kcmp.shape[2]
    n_blk = -(-t // SEL_BLOCK)
    tq = _pick_tile(t, 256, LANES)
    cov = jnp.asarray(_cover_t(nc, n_blk, nc - 1), BF16)
    return pl.pallas_call(
        _cmp_select_kernel,
        grid=(b, NSA_KV_HEADS, t // tq),
        in_specs=[pl.BlockSpec((None, tq, GROUP_W), lambda bi, h, i: (bi, i, h)),
                  pl.BlockSpec((None, None, nc, HEAD_DIM), lambda bi, h, i: (bi, h, 0, 0)),
                  pl.BlockSpec((None, None, nc, HEAD_DIM), lambda bi, h, i: (bi, h, 0, 0)),
                  pl.BlockSpec(cov.shape, lambda bi, h, i: (0, 0))],
        out_specs=[pl.BlockSpec((None, tq, GROUP_W), lambda bi, h, i: (bi, i, h)),
                   pl.BlockSpec((None, None, n_blk, tq), lambda bi, h, i: (bi, h, 0, i))],
        out_shape=[jax.ShapeDtypeStruct((b, t, TOK_W), F32),
                   jax.ShapeDtypeStruct((b, NSA_KV_HEADS, n_blk, t), BF16)],
        compiler_params=_cparams(("parallel", "parallel", "arbitrary")),
        name="cmp_select_prompt",
    )(q, kcmp, vcmp, cov)


SEL_TQ = 128
SEL_TK = 512


def _sel_kernel(q_ref, k_ref, v_ref, sel_ref, exp_ref, o_ref, mask_ref, m_ref, l_ref, acc_ref):
    t = k_ref.shape[0]
    qi = pl.program_id(2)
    q6 = _stack_heads(q_ref[...]).astype(BF16)
    mask_ref[...] = (1.0 - _tn(sel_ref[...], exp_ref[...])) * NEG
    m_ref[...] = jnp.full(m_ref.shape, NEG, F32)
    l_ref[...] = jnp.zeros(l_ref.shape, F32)
    acc_ref[...] = jnp.zeros(acc_ref.shape, F32)
    q_lo = qi * SEL_TQ

    def key_tile(j, causal):
        lo, hi = j * SEL_TK, (j + 1) * SEL_TK
        k = k_ref[lo:hi, :].astype(BF16)
        v = v_ref[lo:hi, :].astype(BF16)
        bias = mask_ref[:, lo:hi]
        if causal:
            q_pos = q_lo + lax.broadcasted_iota(I32, (SEL_TQ, SEL_TK), 0)
            k_pos = lo + lax.broadcasted_iota(I32, (SEL_TQ, SEL_TK), 1)
            bias = jnp.where(k_pos <= q_pos, bias, NEG)
        s = _nt(q6, k).reshape(NSA_GROUP, SEL_TQ, SEL_TK) * SCALE + bias[None]
        m_old = m_ref[...]
        m_new = jnp.maximum(m_old, jnp.max(s, axis=-1, keepdims=True))
        e = jnp.exp(s - m_new)
        alpha = jnp.exp(m_old - m_new)
        l_ref[...] = alpha * l_ref[...] + jnp.sum(e, axis=-1, keepdims=True)
        pv = jnp.dot(e.reshape(NSA_GROUP * SEL_TQ, SEL_TK).astype(BF16), v, preferred_element_type=F32)
        acc_ref[...] = alpha * acc_ref[...] + pv.reshape(NSA_GROUP, SEL_TQ, HEAD_DIM)
        m_ref[...] = m_new

    for j in range(t // SEL_TK):
        lo, hi = j * SEL_TK, (j + 1) * SEL_TK

        @pl.when(hi - 1 <= q_lo)
        def _():
            key_tile(j, False)

        @pl.when((lo <= q_lo + SEL_TQ - 1) & (hi - 1 > q_lo))
        def _():
            key_tile(j, True)

    o = acc_ref[...] / jnp.maximum(l_ref[...], 1e-30)
    o_ref[...] = jnp.concatenate([o[g] for g in range(NSA_GROUP)], axis=-1)


def sel_attention_prompt(q, ks, vs, sel_t):
    b, t, _ = q.shape
    n_blk = sel_t.shape[2]
    expand = (np.arange(t)[None, :] // SEL_BLOCK == np.arange(n_blk)[:, None]).astype(np.float32)
    expand = jnp.asarray(expand, BF16)
    return pl.pallas_call(
        _sel_kernel,
        grid=(b, NSA_KV_HEADS, t // SEL_TQ),
        in_specs=[pl.BlockSpec((None, SEL_TQ, GROUP_W), lambda bi, h, i: (bi, i, h)),
                  pl.BlockSpec((None, t, HEAD_DIM), lambda bi, h, i: (bi, 0, h)),
                  pl.BlockSpec((None, t, HEAD_DIM), lambda bi, h, i: (bi, 0, h)),
                  pl.BlockSpec((None, None, n_blk, SEL_TQ), lambda bi, h, i: (bi, h, 0, i)),
                  pl.BlockSpec((n_blk, t), lambda bi, h, i: (0, 0))],
        out_specs=pl.BlockSpec((None, SEL_TQ, GROUP_W), lambda bi, h, i: (bi, i, h)),
        out_shape=jax.ShapeDtypeStruct((b, t, TOK_W), F32),
        scratch_shapes=[pltpu.VMEM((SEL_TQ, t), F32),
                        pltpu.VMEM((NSA_GROUP, SEL_TQ, 1), F32),
                        pltpu.VMEM((NSA_GROUP, SEL_TQ, 1), F32),
                        pltpu.VMEM((NSA_GROUP, SEL_TQ, HEAD_DIM), F32)],
        compiler_params=_cparams(("parallel", "parallel", "arbitrary")),
        name="sel_attention_prompt",
    )(q, ks, vs, sel_t, expand)


WIN_TQ = 128


def _gate_cols(gate, h, j, tq):
    cols = []
    for g in range(NSA_GROUP):
        c = (h * NSA_GROUP + g) * 3 + j
        cols.append(jnp.broadcast_to(gate[:, c:c + 1], (tq, HEAD_DIM)))
    return jnp.concatenate(cols, axis=1)


def _win_combine_kernel(q_ref, k_ref, v_ref, oc_ref, os_ref, gate_ref, o_ref):
    tq = q_ref.shape[0]
    span = WINDOW + tq
    h = pl.program_id(1)
    qi = pl.program_id(2)
    start = pl.multiple_of(jnp.maximum(qi * tq - WINDOW, 0), tq)
    q6 = _stack_heads(q_ref[...]).astype(BF16)
    k = k_ref[pl.ds(start, span), :].astype(BF16)
    v = v_ref[pl.ds(start, span), :].astype(BF16)
    s = _nt(q6, k).reshape(NSA_GROUP, tq, span) * SCALE
    diff = (qi * tq + lax.broadcasted_iota(I32, (tq, span), 0)) - (start + lax.broadcasted_iota(I32, (tq, span), 1))
    valid = ((diff >= 0) & (diff < WINDOW))[None]
    s = jnp.where(valid, s, NEG)
    m = jnp.max(s, axis=-1, keepdims=True)
    e = jnp.where(valid, jnp.exp(s - m), 0.0)
    p = e / jnp.maximum(jnp.sum(e, axis=-1, keepdims=True), 1e-30)
    o = jnp.dot(p.reshape(NSA_GROUP * tq, span).astype(BF16), v, preferred_element_type=F32)
    o_win = _unstack_heads(o, tq)
    gate = gate_ref[...]
    for hh in range(NSA_KV_HEADS):
        @pl.when(h == hh)
        def _():
            o_ref[...] = (_gate_cols(gate, hh, 0, tq) * oc_ref[...] + _gate_cols(gate, hh, 1, tq) * os_ref[...]
                          + _gate_cols(gate, hh, 2, tq) * o_win)


def win_combine_prompt(q, kw, vw, o_cmp, o_sel, gate):
    b, t, _ = q.shape
    assert t >= WINDOW + WIN_TQ
    qspec = pl.BlockSpec((None, WIN_TQ, GROUP_W), lambda bi, h, i: (bi, i, h))
    kspec = pl.BlockSpec((None, t, HEAD_DIM), lambda bi, h, i: (bi, 0, h))
    return pl.pallas_call(
        _win_combine_kernel,
        grid=(b, NSA_KV_HEADS, t // WIN_TQ),
        in_specs=[qspec, kspec, kspec, qspec, qspec,
                  pl.BlockSpec((None, WIN_TQ, LANES), lambda bi, h, i: (bi, i, 0))],
        out_specs=qspec,
        out_shape=jax.ShapeDtypeStruct((b, t, TOK_W), F32),
        compiler_params=_cparams(("parallel", "parallel", "arbitrary")),
        name="win_combine_prompt",
    )(q, kw, vw, o_cmp, o_sel, gate)


def _router_kernel(x_ref, g_ref, wt_ref, b_ref, xn_ref, e_ref, gt_ref):
    xn = _rms(x_ref[...], g_ref[...])
    xn_ref[...] = xn
    lt = _nt(wt_ref[...], xn.astype(BF16))
    ex = jnp.exp(lt - jnp.max(lt, axis=0, keepdims=True))
    aff = ex / jnp.sum(ex, axis=0, keepdims=True)
    sel = aff + b_ref[...]
    row = lambda a, r: a[r:r + 1, :]

    best, g_idx = None, None
    for g in range(N_GROUPS):
        a, b, c, d = (row(sel, g * EXP_PER_GROUP + j) for j in range(EXP_PER_GROUP))
        hi1, lo1, hi2, lo2 = jnp.maximum(a, b), jnp.minimum(a, b), jnp.maximum(c, d), jnp.minimum(c, d)
        score = jnp.maximum(hi1, hi2) + jnp.maximum(jnp.minimum(hi1, hi2), jnp.maximum(lo1, lo2))
        if g == 0:
            best, g_idx = score, jnp.zeros(score.shape, I32)
        else:
            g_idx = jnp.where(score > best, g, g_idx)
            best = jnp.maximum(best, score)

    def in_group(a, j):
        out = row(a, j)
        for g in range(1, N_GROUPS):
            out = jnp.where(g_idx == g, row(a, g * EXP_PER_GROUP + j), out)
        return out

    v = [in_group(sel, j) for j in range(EXP_PER_GROUP)]
    af = [in_group(aff, j) for j in range(EXP_PER_GROUP)]

    def first_max(vals):
        m = functools.reduce(jnp.maximum, vals)
        loc = jnp.full(m.shape, EXP_PER_GROUP - 1, I32)
        for j in range(EXP_PER_GROUP - 2, -1, -1):
            loc = jnp.where(vals[j] == m, j, loc)
        return loc

    l1 = first_max(v)
    l2 = first_max([jnp.where(l1 == j, -jnp.inf, v[j]) for j in range(EXP_PER_GROUP)])
    pick = lambda loc: functools.reduce(lambda acc, j: jnp.where(loc == j, af[j], acc), range(1, EXP_PER_GROUP), af[0])
    a1, a2 = pick(l1), pick(l2)
    tot = a1 + a2
    e_ref[...] = jnp.concatenate([g_idx * EXP_PER_GROUP + l1, g_idx * EXP_PER_GROUP + l2], axis=0)
    gt_ref[...] = jnp.concatenate([a1 / tot, a2 / tot], axis=0)


def router(x, gain, w_router, b_router):
    n, d = x.shape
    tm = _pick_tile(n, 1024, LANES)
    return pl.pallas_call(
        _router_kernel,
        grid=(n // tm,),
        in_specs=[pl.BlockSpec((tm, d), lambda i: (i, 0)),
                  pl.BlockSpec((1, d), lambda i: (0, 0)),
                  pl.BlockSpec((N_EXPERTS, d), lambda i: (0, 0)),
                  pl.BlockSpec((N_EXPERTS, 1), lambda i: (0, 0))],
        out_specs=[pl.BlockSpec((tm, d), lambda i: (i, 0)),
                   pl.BlockSpec((TOP_K, tm), lambda i: (0, i)),
                   pl.BlockSpec((TOP_K, tm), lambda i: (0, i))],
        out_shape=[jax.ShapeDtypeStruct((n, d), F32),
                   jax.ShapeDtypeStruct((TOP_K, n), I32),
                   jax.ShapeDtypeStruct((TOP_K, n), F32)],
        compiler_params=_cparams(("parallel",)),
        name="router",
    )(x, gain.reshape(1, d), w_router.T.astype(BF16), b_router.reshape(N_EXPERTS, 1).astype(F32))


MOE_TM = 512
MOE_NF = 4
MOE_TF = D_FF // MOE_NF
MOE_ROWS = MOE_TM // MOE_NF
MOE_SPLIT = 128


def _moe_kernel(te_ref, nu_ref, src_ref, dst_ref, x_hbm, wg_ref, wu_ref, wd_ref, y_hbm,
                xbuf, xb, ybuf, gsem, ssem):
    i = pl.program_id(0)
    f = pl.program_id(1)
    n_used = nu_ref[0]
    d = xbuf.shape[-1]

    def gather_copy(tile, buf, r):
        row = src_ref[tile * MOE_TM + r]
        return pltpu.make_async_copy(x_hbm.at[pl.ds(row, 1), :], xbuf.at[buf, pl.ds(r, 1), :], gsem.at[buf])

    def scatter_copy(tile, buf, r):
        row = dst_ref[tile * MOE_TM + r]
        return pltpu.make_async_copy(ybuf.at[buf, pl.ds(r, 1), :], y_hbm.at[pl.ds(row, 1), :], ssem.at[buf])

    def wait_all(hbm, vmem_buf, sem):
        pltpu.make_async_copy(hbm.at[pl.ds(0, MOE_TM), :], vmem_buf, sem).wait()

    dump0 = y_hbm.shape[0] - MOE_TM
    cur = i % 2
    oth = 1 - cur

    @pl.when((i == 0) & (f == 0))
    def _():
        def body(r, carry):
            gather_copy(0, 0, r).start()
            return carry
        lax.fori_loop(0, MOE_TM, body, 0)
        ybuf[1] = jnp.zeros(ybuf.shape[1:], F32)

    @pl.when(i < n_used)
    def _():
        @pl.when(f == 0)
        def _():
            wait_all(x_hbm, xbuf.at[cur], gsem.at[cur])
            xb[...] = xbuf[cur].astype(BF16)
            ybuf[cur] = jnp.zeros(ybuf.shape[1:], F32)

        nxt = jnp.minimum(i + 1, n_used - 1)
        prv = jnp.maximum(i - 1, 0)
        wg, wu, wd = wg_ref[...].astype(BF16), wu_ref[...].astype(BF16), wd_ref[...].astype(BF16)
        for c, (lo, hi) in enumerate(((0, MOE_SPLIT), (MOE_SPLIT, MOE_TM))):
            x = xb[lo:hi, :]
            a = jnp.dot(x, wg, preferred_element_type=F32)
            u = jnp.dot(x, wu, preferred_element_type=F32)
            h = (a * jax.nn.sigmoid(a) * u).astype(BF16)
            ybuf[cur, lo:hi, :] += jnp.dot(h, wd, preferred_element_type=F32)
            if c == 0:
                r0 = pl.multiple_of(f * MOE_ROWS, MOE_ROWS)
                for r in range(MOE_ROWS):
                    rr = r0 + r
                    gather_copy(nxt, oth, rr).start()
                    row = jnp.where(i > 0, dst_ref[prv * MOE_TM + rr], dump0 + rr)
                    pltpu.make_async_copy(ybuf.at[oth, pl.ds(rr, 1), :], y_hbm.at[pl.ds(row, 1), :], ssem.at[oth]).start()

        @pl.when(f == MOE_NF - 1)
        def _():
            wait_all(y_hbm, ybuf.at[oth], ssem.at[oth])

    @pl.when((i == n_used) & (f == 0))
    def _():
        wait_all(x_hbm, xbuf.at[cur], gsem.at[cur])

        def body(r, carry):
            scatter_copy(i - 1, oth, r).start()
            return carry
        lax.fori_loop(0, MOE_TM, body, 0)
        wait_all(y_hbm, ybuf.at[oth], ssem.at[oth])


def moe_experts(xn, tile_expert, n_used, src_row, dst_row, n_out_rows, w_gate, w_up, w_down, layer):
    n, d = xn.shape
    n_tiles = tile_expert.shape[0]

    def tile(i, nu):
        return jnp.minimum(i, nu[0] - 1)

    def fcol(i, f, nu):
        return jnp.where(i < nu[0], f, MOE_NF - 1)

    wspec = lambda blk, im: pl.BlockSpec(blk, im)
    grid_spec = pltpu.PrefetchScalarGridSpec(
        num_scalar_prefetch=4,
        grid=(n_tiles, MOE_NF),
        in_specs=[
            pl.BlockSpec(memory_space=pl.ANY),
            wspec((None, None, d, MOE_TF), lambda i, f, te, nu, s, t: (layer, te[tile(i, nu)], 0, fcol(i, f, nu))),
            wspec((None, None, d, MOE_TF), lambda i, f, te, nu, s, t: (layer, te[tile(i, nu)], 0, fcol(i, f, nu))),
            wspec((None, None, MOE_TF, d), lambda i, f, te, nu, s, t: (layer, te[tile(i, nu)], fcol(i, f, nu), 0)),
        ],
        out_specs=pl.BlockSpec(memory_space=pl.ANY),
        scratch_shapes=[pltpu.VMEM((2, MOE_TM, d), F32),
                        pltpu.VMEM((MOE_TM, d), BF16),
                        pltpu.VMEM((2, MOE_TM, d), F32),
                        pltpu.SemaphoreType.DMA((2,)),
                        pltpu.SemaphoreType.DMA((2,))],
    )
    return pl.pallas_call(
        _moe_kernel,
        grid_spec=grid_spec,
        out_shape=jax.ShapeDtypeStruct((n_out_rows, d), F32),
        compiler_params=_cparams(("arbitrary", "arbitrary")),
        name="moe_experts",
    )(tile_expert, n_used, src_row, dst_row, xn, w_gate, w_up, w_down)


def moe_ffn(x, xn, experts, gates, w_gate, w_up, w_down, layer):
    n, d = x.shape
    n_items = n * TOP_K
    n_tiles = -(-n_items // MOE_TM) + N_EXPERTS
    n_slots = n_tiles * MOE_TM
    flat_e = experts.reshape(-1)
    sizes = jnp.sum((flat_e[:, None] == jnp.arange(N_EXPERTS, dtype=I32)[None, :]).astype(I32), axis=0)
    padded = -(-sizes // MOE_TM) * MOE_TM
    ends_p = jnp.cumsum(padded)
    starts_p = ends_p - padded
    starts = jnp.cumsum(sizes) - sizes
    order = jnp.argsort(flat_e).astype(I32)
    tile_expert = jnp.searchsorted(ends_p, jnp.arange(n_tiles, dtype=I32) * MOE_TM, side='right')
    tile_expert = jnp.minimum(tile_expert, N_EXPERTS - 1).astype(I32)
    n_used = (ends_p[-1:] // MOE_TM).astype(I32)
    slot = jnp.arange(n_slots, dtype=I32)
    slot_e = jnp.repeat(tile_expert, MOE_TM)
    within = slot - starts_p[slot_e]
    is_item = within < sizes[slot_e]
    item = order[jnp.clip(starts[slot_e] + within, 0, n_items - 1)]
    src_row = jnp.where(is_item, item % n, 0).astype(I32)
    dst_row = jnp.where(is_item, item, n_items + slot).astype(I32)
    y = moe_experts(xn, tile_expert, n_used, src_row, dst_row, n_items + n_slots + MOE_TM, w_gate, w_up, w_down, layer)
    g = gates.astype(F32)
    return x + y[:n] * g[0][:, None] + y[n:n_items] * g[1][:, None]


MOE2_TM = 1024
MOE2_GROUP = 64
MOE2_CHUNK = 256
MOE2_GPS = MOE2_TM // MOE_NF // MOE2_GROUP
MOE2_VMEM_LIMIT = V7X_VMEM_BYTES * 7 // 8


def _moe2_kernel(te_ref, nu_ref, nv_ref, src_ref, dst_ref, x_hbm, wg_ref, wu_ref, wd_ref, y_hbm,
                 xbuf, xb, ybuf, gsem, ssem):
    i = pl.program_id(0)
    f = pl.program_id(1)
    n_used = nu_ref[0]
    cur = i % 2
    oth = 1 - cur
    n_groups = MOE2_TM // MOE2_GROUP

    def gather_group(tile, g):
        for r in range(MOE2_GROUP):
            rr = g * MOE2_GROUP + r
            row = src_ref[tile * MOE2_TM + rr]
            pltpu.make_async_copy(x_hbm.at[pl.ds(row, 1), :], xbuf.at[pl.ds(rr, 1), :], gsem).start()

    def scatter_group(tile, buf, g):
        for r in range(MOE2_GROUP):
            rr = g * MOE2_GROUP + r
            row = dst_ref[tile * MOE2_TM + rr]
            pltpu.make_async_copy(ybuf.at[buf, pl.ds(rr, 1), :], y_hbm.at[pl.ds(row, 1), :], ssem.at[buf]).start()

    def wait_gather_group():
        pltpu.make_async_copy(x_hbm.at[pl.ds(0, MOE2_GROUP), :], xbuf.at[pl.ds(0, MOE2_GROUP), :], gsem).wait()

    def wait_scatter_group(buf):
        pltpu.make_async_copy(ybuf.at[buf, pl.ds(0, MOE2_GROUP), :], y_hbm.at[pl.ds(0, MOE2_GROUP), :], ssem.at[buf]).wait()

    @pl.when((i == 0) & (f == 0))
    def _():
        xbuf[...] = jnp.zeros(xbuf.shape, F32)

        def body(g, carry):
            @pl.when(g * MOE2_GROUP < nv_ref[0])
            def _():
                gather_group(0, g)
            return carry
        lax.fori_loop(0, n_groups, body, 0)

    @pl.when(i < n_used)
    def _():
        n_valid = nv_ref[i]

        @pl.when(f == 0)
        def _():
            for g in range(n_groups):
                @pl.when(g * MOE2_GROUP < n_valid)
                def _():
                    wait_gather_group()
            xb[...] = xbuf[...].astype(BF16)
            ybuf[cur] = jnp.zeros(ybuf.shape[1:], F32)

        nv_next = nv_ref[i + 1]
        nv_prev = nv_ref[jnp.maximum(i - 1, 0)]
        for gi in range(MOE2_GPS):
            g = f * MOE2_GPS + gi

            @pl.when((i + 1 < n_used) & (g * MOE2_GROUP < nv_next))
            def _():
                gather_group(i + 1, g)

            @pl.when((i > 0) & (g * MOE2_GROUP < nv_prev))
            def _():
                scatter_group(i - 1, oth, g)

        for c in range(MOE2_TM // MOE2_CHUNK):
            lo, hi = c * MOE2_CHUNK, (c + 1) * MOE2_CHUNK

            @pl.when(lo < n_valid)
            def _():
                x = xb[lo:hi, :]
                a = jnp.dot(x, wg_ref[...].astype(BF16), preferred_element_type=F32)
                u = jnp.dot(x, wu_ref[...].astype(BF16), preferred_element_type=F32)
                h = (a * jax.nn.sigmoid(a) * u).astype(BF16)
                ybuf[cur, lo:hi, :] += jnp.dot(h, wd_ref[...].astype(BF16), preferred_element_type=F32)

        @pl.when(f == MOE_NF - 1)
        def _():
            for g in range(n_groups):
                @pl.when((i > 0) & (g * MOE2_GROUP < nv_prev))
                def _():
                    wait_scatter_group(oth)

    @pl.when((i == n_used) & (f == 0))
    def _():
        nv_last = nv_ref[i - 1]

        def start(g, carry):
            @pl.when(g * MOE2_GROUP < nv_last)
            def _():
                scatter_group(i - 1, oth, g)
            return carry
        lax.fori_loop(0, n_groups, start, 0)

        def wait(g, carry):
            @pl.when(g * MOE2_GROUP < nv_last)
            def _():
                wait_scatter_group(oth)
            return carry
        lax.fori_loop(0, n_groups, wait, 0)


def moe_experts2(xn, tile_expert, n_used, n_valid, src_row, dst_row, n_out_rows, w_gate, w_up, w_down, layer):
    n, d = xn.shape
    n_tiles = tile_expert.shape[0]

    def tile(i, nu):
        return jnp.minimum(i, nu[0] - 1)

    def fcol(i, f, nu):
        return jnp.where(i < nu[0], f, MOE_NF - 1)

    grid_spec = pltpu.PrefetchScalarGridSpec(
        num_scalar_prefetch=5,
        grid=(n_tiles, MOE_NF),
        in_specs=[
            pl.BlockSpec(memory_space=pl.ANY),
            pl.BlockSpec((None, None, d, MOE_TF), lambda i, f, te, nu, nv, s, t: (layer, te[tile(i, nu)], 0, fcol(i, f, nu))),
            pl.BlockSpec((None, None, d, MOE_TF), lambda i, f, te, nu, nv, s, t: (layer, te[tile(i, nu)], 0, fcol(i, f, nu))),
            pl.BlockSpec((None, None, MOE_TF, d), lambda i, f, te, nu, nv, s, t: (layer, te[tile(i, nu)], fcol(i, f, nu), 0)),
        ],
        out_specs=pl.BlockSpec(memory_space=pl.ANY),
        scratch_shapes=[pltpu.VMEM((MOE2_TM, d), F32),
                        pltpu.VMEM((MOE2_TM, d), BF16),
                        pltpu.VMEM((2, MOE2_TM, d), F32),
                        pltpu.SemaphoreType.DMA(()),
                        pltpu.SemaphoreType.DMA((2,))],
    )
    return pl.pallas_call(
        _moe2_kernel,
        grid_spec=grid_spec,
        out_shape=jax.ShapeDtypeStruct((n_out_rows, d), F32),
        compiler_params=pltpu.CompilerParams(dimension_semantics=("arbitrary", "arbitrary"),
                                             vmem_limit_bytes=MOE2_VMEM_LIMIT),
        name="moe_experts",
    )(tile_expert, n_used, n_valid, src_row, dst_row, xn, w_gate, w_up, w_down)


def moe_ffn2(x, xn, experts, gates, w_gate, w_up, w_down, layer):
    n, d = x.shape
    tm = MOE2_TM
    n_items = n * TOP_K
    n_tiles = -(-n_items // tm) + N_EXPERTS
    n_slots = n_tiles * tm
    flat_e = experts.reshape(-1)
    sizes = jnp.sum((flat_e[:, None] == jnp.arange(N_EXPERTS, dtype=I32)[None, :]).astype(I32), axis=0)
    padded = -(-sizes // tm) * tm
    ends_p = jnp.cumsum(padded)
    starts_p = ends_p - padded
    starts = jnp.cumsum(sizes) - sizes
    order = jnp.argsort(flat_e).astype(I32)
    tile_start = jnp.arange(n_tiles, dtype=I32) * tm
    tile_expert = jnp.minimum(jnp.searchsorted(ends_p, tile_start, side='right'), N_EXPERTS - 1).astype(I32)
    n_used = (ends_p[-1:] // tm).astype(I32)
    n_valid = jnp.clip(sizes[tile_expert] - (tile_start - starts_p[tile_expert]), 0, tm).astype(I32)
    slot = jnp.arange(n_slots, dtype=I32)
    slot_e = jnp.repeat(tile_expert, tm)
    within = slot - starts_p[slot_e]
    is_item = within < sizes[slot_e]
    item = order[jnp.clip(starts[slot_e] + within, 0, n_items - 1)]
    src_row = jnp.where(is_item, item % n, 0).astype(I32)
    dst_row = jnp.where(is_item, item, n_items + slot).astype(I32)
    y = moe_experts2(xn, tile_expert, n_used, n_valid, src_row, dst_row, n_items + n_slots,
                     w_gate, w_up, w_down, layer)
    g = gates.astype(F32)
    return x + y[:n] * g[0][:, None] + y[n:n_items] * g[1][:, None]


GROUP_ROWS = 8


def _pick_head(ref, h):
    out = ref[:, 0, :]
    for hh in range(1, NSA_KV_HEADS):
        out = jnp.where(h == hh, ref[:, hh, :], out)
    return out


def _group_rows(q_ref, h):
    rows = [q_ref[:, (h * NSA_GROUP + g) * HEAD_DIM:(h * NSA_GROUP + g + 1) * HEAD_DIM] for g in range(NSA_GROUP)]
    rows.append(jnp.zeros((GROUP_ROWS - NSA_GROUP, HEAD_DIM), F32))
    return jnp.concatenate(rows, axis=0)


def _compress_paged_kernel(pt_ref, cache_hbm, pe_ref, w1_ref, w2_ref, o_ref, xbuf, sem, *, layer, n_pages):
    b = pl.program_id(0)
    n_chunks = n_pages * (PAGE_SIZE // CMP_STRIDE)

    def page_copy(p):
        page = pt_ref[b * n_pages + p]
        return pltpu.make_async_copy(cache_hbm.at[layer, page], xbuf.at[pl.ds(p * PAGE_SIZE, PAGE_SIZE)], sem)

    def start(p, carry):
        page_copy(p).start()
        return carry

    def wait(p, carry):
        page_copy(p).wait()
        return carry

    lax.fori_loop(0, n_pages, start, 0)
    lax.fori_loop(0, n_pages, wait, 0)

    def slab(r, h):
        return xbuf[pl.ds(r, n_chunks, stride=CMP_STRIDE), h, :]
    _compress_body(slab, n_chunks, pe_ref, w1_ref, w2_ref, o_ref)


def compress_paged(cache, layer, page_table, pe, w1, w2):
    b, n_pages = page_table.shape
    n_chunks = n_pages * (PAGE_SIZE // CMP_STRIDE)
    grid_spec = pltpu.PrefetchScalarGridSpec(
        num_scalar_prefetch=1,
        grid=(b,),
        in_specs=[pl.BlockSpec(memory_space=pl.ANY),
                  pl.BlockSpec(pe.shape, lambda bi, pt: (0, 0)),
                  pl.BlockSpec(w1.shape, lambda bi, pt: (0, 0)),
                  pl.BlockSpec(w2.shape, lambda bi, pt: (0, 0))],
        out_specs=pl.BlockSpec((None, NSA_KV_HEADS, n_chunks, HEAD_DIM), lambda bi, pt: (bi, 0, 0, 0)),
        scratch_shapes=[pltpu.VMEM((n_pages * PAGE_SIZE, NSA_KV_HEADS, HEAD_DIM), F32), pltpu.SemaphoreType.DMA(())],
    )
    return pl.pallas_call(
        functools.partial(_compress_paged_kernel, layer=layer, n_pages=n_pages),
        grid_spec=grid_spec,
        out_shape=jax.ShapeDtypeStruct((b, NSA_KV_HEADS, n_chunks, HEAD_DIM), F32),
        compiler_params=_cparams(("arbitrary",)),
        name="compress_paged",
    )(page_table.reshape(-1), cache, pe, w1, w2)


def _cmp_select_sample_kernel(q_ref, kc_ref, vc_ref, cov_ref, o_ref, idx_ref, q8_ref, *, q_pos, n_blk):
    nc = kc_ref.shape[1]
    nb_pad = cov_ref.shape[1]
    blk = lax.broadcasted_iota(I32, (1, nb_pad), 1)
    cur = q_pos // SEL_BLOCK
    forced = (blk == 0) | (blk == cur) | (blk == cur - 1)
    future = (blk * SEL_BLOCK > q_pos) | (blk >= n_blk)
    c_last = lax.broadcasted_iota(I32, (GROUP_ROWS, nc), 1) * CMP_STRIDE + (CMP_BLOCK - 1)
    valid = (c_last <= q_pos) & (lax.broadcasted_iota(I32, (GROUP_ROWS, nc), 0) < NSA_GROUP)
    for h in range(NSA_KV_HEADS):
        q8f = _group_rows(q_ref, h)
        q8_ref[h] = q8f
        q8 = q8f.astype(BF16)
        s = jnp.where(valid, _nt(q8, kc_ref[h].astype(BF16)) * SCALE, NEG)
        m = jnp.max(s, axis=-1, keepdims=True)
        e = jnp.where(valid, jnp.exp(s - m), 0.0)
        p = (e / jnp.maximum(jnp.sum(e, axis=-1, keepdims=True), 1e-30)).astype(BF16)
        o_ref[h] = jnp.dot(p, vc_ref[h].astype(BF16), preferred_element_type=F32)
        imp = jnp.sum(jnp.dot(p, cov_ref[...], preferred_element_type=F32), axis=0, keepdims=True)
        imp = jnp.where(future, -1e9, jnp.where(forced, 1e9, imp))
        jj = lax.broadcasted_iota(I32, (nb_pad, nb_pad), 0)
        ss = lax.broadcasted_iota(I32, (nb_pad, nb_pad), 1)
        row = jnp.broadcast_to(imp, (nb_pad, nb_pad))
        col = jnp.sum(jnp.where(jj == ss, row, 0.0), axis=1, keepdims=True)
        beats = jnp.where(col > row, 1.0, jnp.where(col == row, jnp.where(jj < ss, 1.0, 0.0), 0.0))
        rank = jnp.sum(beats, axis=0, keepdims=True)
        want = lax.broadcasted_iota(I32, (SEL_TOP, nb_pad), 0).astype(F32)
        lane = lax.broadcasted_iota(I32, (SEL_TOP, nb_pad), 1).astype(F32)
        idx = jnp.sum(jnp.where(jnp.broadcast_to(rank, (SEL_TOP, nb_pad)) == want, lane, 0.0), axis=1, keepdims=True)
        idx_ref[h] = idx.astype(I32)


def cmp_select_sample(q, kcmp, vcmp, q_pos):
    b = q.shape[0]
    nc = kcmp.shape[2]
    n_blk = q_pos // SEL_BLOCK + 1
    nb_pad = -(-n_blk // LANES) * LANES
    cov = _cover_t(nc, nb_pad, nc - 1)[:, :nc].T
    cov = jnp.asarray(cov, BF16)
    return pl.pallas_call(
        functools.partial(_cmp_select_sample_kernel, q_pos=q_pos, n_blk=n_blk),
        grid=(b,),
        in_specs=[pl.BlockSpec((None, 1, TOK_W), lambda bi: (bi, 0, 0)),
                  pl.BlockSpec((None, NSA_KV_HEADS, nc, HEAD_DIM), lambda bi: (bi, 0, 0, 0)),
                  pl.BlockSpec((None, NSA_KV_HEADS, nc, HEAD_DIM), lambda bi: (bi, 0, 0, 0)),
                  pl.BlockSpec(cov.shape, lambda bi: (0, 0))],
        out_specs=[pl.BlockSpec((None, NSA_KV_HEADS, GROUP_ROWS, HEAD_DIM), lambda bi: (bi, 0, 0, 0)),
                   pl.BlockSpec((None, NSA_KV_HEADS, SEL_TOP, 1), lambda bi: (bi, 0, 0, 0)),
                   pl.BlockSpec((None, NSA_KV_HEADS, GROUP_ROWS, HEAD_DIM), lambda bi: (bi, 0, 0, 0))],
        out_shape=[jax.ShapeDtypeStruct((b, NSA_KV_HEADS, GROUP_ROWS, HEAD_DIM), F32),
                   jax.ShapeDtypeStruct((b, NSA_KV_HEADS, SEL_TOP, 1), I32),
                   jax.ShapeDtypeStruct((b, NSA_KV_HEADS, GROUP_ROWS, HEAD_DIM), F32)],
        compiler_params=_cparams(("parallel",)),
        name="cmp_select_sample",
    )(q, kcmp, vcmp, cov)


def _dot_new_row(q8, k_row):
    a = q8.astype(BF16).astype(F32)
    b = k_row.astype(BF16).astype(F32)
    return jnp.sum(a * b, axis=-1, keepdims=True)


def _sel_sample_kernel(pt_ref, ix_ref, q_ref, k_ref, v_ref, kn_ref, vn_ref, o_ref, m_ref, l_ref, acc_ref, *, n_cache_blk):
    b, h, r = pl.program_id(0), pl.program_id(1), pl.program_id(2)

    @pl.when(r == 0)
    def _():
        m_ref[...] = jnp.full(m_ref.shape, NEG, F32)
        l_ref[...] = jnp.zeros(l_ref.shape, F32)
        acc_ref[...] = jnp.zeros(acc_ref.shape, F32)

    q8 = q_ref[...]
    in_cache = ix_ref[(b * NSA_KV_HEADS + h) * SEL_TOP + r] < n_cache_blk
    rows_ok = lax.broadcasted_iota(I32, (GROUP_ROWS, SEL_BLOCK), 0) < NSA_GROUP
    valid = rows_ok & in_cache
    k_blk = _pick_head(k_ref, h).astype(BF16)
    v_blk = _pick_head(v_ref, h).astype(BF16)
    s = jnp.where(valid, _nt(q8.astype(BF16), k_blk) * SCALE, NEG)
    m_old = m_ref[...]
    m_new = jnp.maximum(m_old, jnp.max(s, axis=-1, keepdims=True))
    e = jnp.where(valid, jnp.exp(s - m_new), 0.0)
    alpha = jnp.exp(m_old - m_new)
    l_ref[...] = alpha * l_ref[...] + jnp.sum(e, axis=-1, keepdims=True)
    acc_ref[...] = alpha * acc_ref[...] + jnp.dot(e.astype(BF16), v_blk, preferred_element_type=F32)
    m_ref[...] = m_new

    @pl.when(r == SEL_TOP - 1)
    def _():
        s_new = _dot_new_row(q8, kn_ref[...]) * SCALE
        m_old = m_ref[...]
        m_fin = jnp.maximum(m_old, s_new)
        alpha = jnp.exp(m_old - m_fin)
        e_new = jnp.exp(s_new - m_fin)
        l_fin = alpha * l_ref[...] + e_new
        acc = alpha * acc_ref[...] + e_new.astype(BF16).astype(F32) * vn_ref[...].astype(BF16).astype(F32)
        o_ref[...] = acc / jnp.maximum(l_fin, 1e-30)


def sel_attention_sample(q8, idx, cache_k, cache_v, layer, page_table, k_new, v_new):
    b, n_pages = page_table.shape
    per_page = PAGE_SIZE // SEL_BLOCK
    n_cache_blk = n_pages * per_page

    def blk_index(bi, h, r, pt, ix):
        s = jnp.minimum(ix[(bi * NSA_KV_HEADS + h) * SEL_TOP + r], n_cache_blk - 1)
        return (layer, pt[bi * n_pages + s // per_page], s % per_page, 0, 0)

    grid_spec = pltpu.PrefetchScalarGridSpec(
        num_scalar_prefetch=2,
        grid=(b, NSA_KV_HEADS, SEL_TOP),
        in_specs=[pl.BlockSpec((None, None, GROUP_ROWS, HEAD_DIM), lambda bi, h, r, pt, ix: (bi, h, 0, 0)),
                  pl.BlockSpec((None, None, SEL_BLOCK, NSA_KV_HEADS, HEAD_DIM), blk_index),
                  pl.BlockSpec((None, None, SEL_BLOCK, NSA_KV_HEADS, HEAD_DIM), blk_index),
                  pl.BlockSpec((None, 1, HEAD_DIM), lambda bi, h, r, pt, ix: (bi, 0, h)),
                  pl.BlockSpec((None, 1, HEAD_DIM), lambda bi, h, r, pt, ix: (bi, 0, h))],
        out_specs=pl.BlockSpec((None, None, GROUP_ROWS, HEAD_DIM), lambda bi, h, r, pt, ix: (bi, h, 0, 0)),
        scratch_shapes=[pltpu.VMEM((GROUP_ROWS, 1), F32), pltpu.VMEM((GROUP_ROWS, 1), F32),
                        pltpu.VMEM((GROUP_ROWS, HEAD_DIM), F32)],
    )
    return pl.pallas_call(
        functools.partial(_sel_sample_kernel, n_cache_blk=n_cache_blk),
        grid_spec=grid_spec,
        out_shape=jax.ShapeDtypeStruct((b, NSA_KV_HEADS, GROUP_ROWS, HEAD_DIM), F32),
        compiler_params=_cparams(("arbitrary", "arbitrary", "arbitrary")),
        name="sel_attention_sample",
    )(page_table.reshape(-1), idx.reshape(-1), q8, cache_k, cache_v, k_new, v_new)


def _win_sample_kernel(q_ref, k_ref, v_ref, kn_ref, vn_ref, oc_ref, os_ref, gate_ref, o_ref):
    h = pl.program_id(1)
    wb = k_ref.shape[0]
    k_win = _pick_head(k_ref, h).astype(BF16)
    v_win = _pick_head(v_ref, h).astype(BF16)
    q8 = q_ref[...]
    col = lax.broadcasted_iota(I32, (GROUP_ROWS, wb), 1)
    valid = col >= 1
    s = jnp.where(valid, _nt(q8.astype(BF16), k_win) * SCALE, NEG)
    s_new = _dot_new_row(q8, kn_ref[...]) * SCALE
    m = jnp.maximum(jnp.max(s, axis=-1, keepdims=True), s_new)
    e = jnp.where(valid, jnp.exp(s - m), 0.0)
    e_new = jnp.exp(s_new - m)
    den = jnp.maximum(jnp.sum(e, axis=-1, keepdims=True) + e_new, 1e-30)
    p = (e / den).astype(BF16)
    p_new = (e_new / den).astype(BF16).astype(F32)
    o_win = jnp.dot(p, v_win, preferred_element_type=F32) + p_new * vn_ref[...].astype(BF16).astype(F32)
    gate = jnp.broadcast_to(gate_ref[...], (GROUP_ROWS, LANES))
    lane = lax.broadcasted_iota(I32, (GROUP_ROWS, LANES), 1)
    head = h * NSA_GROUP + lax.broadcasted_iota(I32, (GROUP_ROWS, LANES), 0)
    g = [jnp.sum(jnp.where(lane == head * 3 + j, gate, 0.0), axis=-1, keepdims=True) for j in range(3)]
    o_ref[...] = g[0] * oc_ref[...] + g[1] * os_ref[...] + g[2] * o_win


def win_combine_sample(q8, win_k, win_v, layer, k_new, v_new, o_cmp, o_sel, gate):
    b = q8.shape[0]
    wb = win_k.shape[2]
    assert wb == WINDOW
    gspec = pl.BlockSpec((None, None, GROUP_ROWS, HEAD_DIM), lambda bi, h: (bi, h, 0, 0))
    wspec = pl.BlockSpec((None, None, wb, NSA_KV_HEADS, HEAD_DIM), lambda bi, h: (layer, bi, 0, 0, 0))
    nspec = pl.BlockSpec((None, 1, HEAD_DIM), lambda bi, h: (bi, 0, h))
    return pl.pallas_call(
        _win_sample_kernel,
        grid=(b, NSA_KV_HEADS),
        in_specs=[gspec, wspec, wspec, nspec, nspec, gspec, gspec,
                  pl.BlockSpec((None, 1, LANES), lambda bi, h: (bi, 0, 0))],
        out_specs=gspec,
        out_shape=jax.ShapeDtypeStruct((b, NSA_KV_HEADS, GROUP_ROWS, HEAD_DIM), F32),
        compiler_params=_cparams(("parallel", "arbitrary")),
        name="win_combine_sample",
    )(q8, win_k, win_v, k_new, v_new, o_cmp, o_sel, gate)


def nsa_sample_step(prep, layer, caches, win_k, win_v, page_table, pe, w1, w2):
    q, kc, vc, ks, vs, kw, vw, gate = prep
    b = q.shape[0]
    past_len = page_table.shape[1] * PAGE_SIZE
    cache_ck, cache_cv, cache_sk, cache_sv = caches
    kcmp = compress_paged(cache_ck, layer, page_table, pe[0], w1[0], w2[0])
    vcmp = compress_paged(cache_cv, layer, page_table, pe[1], w1[1], w2[1])
    o_cmp, idx, q8 = cmp_select_sample(q, kcmp, vcmp, past_len)
    o_sel = sel_attention_sample(q8, idx, cache_sk, cache_sv, layer, page_table, ks, vs)
    o = win_combine_sample(q8, win_k, win_v, layer, kw, vw, o_cmp, o_sel, gate)
    return o[:, :, :NSA_GROUP].reshape(b, 1, TOK_W)


def rms_norm(x, g):
    xf = x.astype(F32)
    y = xf * lax.rsqrt(jnp.mean(xf * xf, axis=-1, keepdims=True) + EPS)
    return (y * g.astype(F32)).astype(x.dtype)


def masked_softmax(s, valid):
    s = jnp.where(valid, s, NEG)
    m = jnp.max(s, axis=-1, keepdims=True)
    e = jnp.where(valid, jnp.exp(s - m), 0.0)
    return e / jnp.maximum(jnp.sum(e, axis=-1, keepdims=True), 1e-30)


def compress(k, pe, w1, w2):
    b, l = k.shape[:2]
    r = CMP_BLOCK // CMP_STRIDE
    n_chunks = l // CMP_STRIDE
    nc = n_chunks - r + 1
    c = k[:, :n_chunks * CMP_STRIDE].reshape(b, n_chunks, CMP_STRIDE, NSA_KV_HEADS, HEAD_DIM)
    blk = jnp.concatenate([c[:, i:i + nc] for i in range(r)], axis=2)
    blk = blk + pe[None, None, :, None, :]
    flat = blk.transpose(0, 1, 3, 2, 4).reshape(b, nc, NSA_KV_HEADS, CMP_BLOCK * HEAD_DIM)
    hid = jax.nn.gelu(jnp.einsum('bnhf,fe->bnhe', flat, w1))
    return jnp.einsum('bnhe,ed->bnhd', hid, w2)


def cmp_attend(q, q_pos, kc, vc):
    nc = kc.shape[1]
    c_last = jnp.arange(nc, dtype=I32) * CMP_STRIDE + CMP_BLOCK - 1
    s = jnp.einsum('bqhgd,bnhd->bqhgn', q, kc).astype(F32) * SCALE
    valid = (c_last[None, :] <= q_pos[:, None])[None, :, None, None, :]
    p = masked_softmax(s, valid)
    o = jnp.einsum('bqhgn,bnhd->bqhgd', p.astype(vc.dtype), vc)
    return o, p


def select_blocks(p, q_pos, n_sel):
    nc = p.shape[-1]
    c_start = jnp.arange(nc, dtype=I32) * CMP_STRIDE
    c_last = c_start + CMP_BLOCK - 1
    s_start = jnp.arange(n_sel, dtype=I32) * SEL_BLOCK
    cover = ((c_start[:, None] < s_start[None, :] + SEL_BLOCK) & (c_last[:, None] >= s_start[None, :])).astype(F32)
    imp = jnp.einsum('bqhgn,ns->bqhs', p, cover)
    blk = jnp.arange(n_sel, dtype=I32)[None, :]
    cur = (q_pos // SEL_BLOCK)[:, None]
    forced = (blk == 0) | (blk == cur) | (blk == cur - 1)
    future = blk * SEL_BLOCK > q_pos[:, None]
    imp = jnp.where(future[None, :, None, :], -1e9, jnp.where(forced[None, :, None, :], 1e9, imp))
    _, idx = lax.top_k(imp, min(SEL_TOP, n_sel))
    return idx


def to_blocks(k):
    b, l = k.shape[:2]
    ns = -(-l // SEL_BLOCK)
    k = jnp.pad(k, ((0, 0), (0, ns * SEL_BLOCK - l), (0, 0), (0, 0)))
    return k.reshape(b, ns, SEL_BLOCK, NSA_KV_HEADS, HEAD_DIM).transpose(0, 3, 1, 2, 4)


def sel_attend(q, q_pos, idx, kb, vb):
    idx_t = idx.transpose(0, 2, 1, 3)
    take = jax.vmap(jax.vmap(lambda blocks, i: blocks[i]))
    kg = take(kb, idx_t)
    vg = take(vb, idx_t)
    kpos = idx_t[..., None] * SEL_BLOCK + jnp.arange(SEL_BLOCK, dtype=I32)
    valid = (kpos <= q_pos[None, None, :, None, None]).transpose(0, 2, 1, 3, 4)[:, :, :, None]
    s = jnp.einsum('bqhgd,bhqkld->bqhgkl', q, kg).astype(F32) * SCALE
    b, tq, h, g, kk, lb = s.shape
    p = masked_softmax(s.reshape(b, tq, h, g, kk * lb), valid.reshape(b, tq, h, 1, kk * lb)).reshape(s.shape)
    return jnp.einsum('bqhgkl,bhqkld->bqhgd', p.astype(vg.dtype), vg)


def window_attend(q, q_pos, k, v, k_pos):
    s = jnp.einsum('bqhgd,bkhd->bqhgk', q, k).astype(F32) * SCALE
    diff = q_pos[:, None] - k_pos[None, :]
    valid = ((diff >= 0) & (diff < WINDOW) & (k_pos[None, :] >= 0))[None, :, None, None, :]
    p = masked_softmax(s, valid)
    return jnp.einsum('bqhgk,bkhd->bqhgd', p.astype(v.dtype), v)


def gather_pages(cache, page_table):
    c = cache[page_table]
    return c.reshape(page_table.shape[0], page_table.shape[1] * cache.shape[1], *cache.shape[2:])


def nsa_sample(prep, pos, past, win_k, win_v, page_table, past_len, pe, w1, w2):
    q, kc, vc, ks, vs, kw, vw, gate = prep
    b, t = q.shape[:2]
    kvr = lambda a: a.reshape(b, t, NSA_KV_HEADS, HEAD_DIM)
    q = q.reshape(b, t, NSA_KV_HEADS, NSA_GROUP, HEAD_DIM)
    kc, vc, ks, vs, kw, vw = (kvr(a) for a in (kc, vc, ks, vs, kw, vw))
    gate = gate[..., :3 * NSA_HEADS].reshape(b, t, NSA_KV_HEADS, NSA_GROUP, 3)
    full = [jnp.concatenate([gather_pages(c, page_table).astype(new.dtype), new], axis=1)
            for c, new in zip(past, (kc, vc, ks, vs))]
    kcmp = compress(full[0], pe[0], w1[0], w2[0])
    vcmp = compress(full[1], pe[1], w1[1], w2[1])
    o_cmp, p_cmp = cmp_attend(q, pos, kcmp, vcmp)
    ksb, vsb = to_blocks(full[2]), to_blocks(full[3])
    idx = select_blocks(p_cmp, pos, ksb.shape[2])
    o_sel = sel_attend(q, pos, idx, ksb, vsb)
    wb = win_k.shape[1]
    kwin = jnp.concatenate([win_k.astype(kw.dtype), kw], axis=1)
    vwin = jnp.concatenate([win_v.astype(vw.dtype), vw], axis=1)
    kpos = past_len - wb + jnp.arange(wb + t, dtype=I32)
    o_win = window_attend(q, pos, kwin, vwin, kpos)
    o = (gate[..., 0, None] * o_cmp + gate[..., 1, None] * o_sel + gate[..., 2, None] * o_win).reshape(b, t, TOK_W)
    return o, (kc, vc, ks, vs), (kwin[:, -wb:], vwin[:, -wb:])


def _reorder_nsa_weight(w):
    n_gl = 3 * NSA_HEADS
    parts = [w[:, :NSA_MQ0], w[:, NSA_MQ0 + n_gl:NSA_MQ0 + n_gl + MEM_W], w[:, NSA_MQ0:NSA_MQ0 + n_gl]]
    wr = jnp.concatenate(parts, axis=1)
    return jnp.pad(wr, ((0, 0), (0, NSA_ZW - wr.shape[1]))).astype(BF16)


def kernel(x_prompt, x_sample, state_pool, cache_cmp_k, cache_cmp_v, cache_sel_k, cache_sel_v, state_win_k, state_win_v, cache_mem_k, cache_mem_v, page_table, mem_prompt, norm_mix_g, norm_ffn_g, norm_mem_g, w_mem_kv, mem_q_norm_g, mem_k_norm_g, w_in_pool, w_pool_grp, pool_scale, w_out_pool, w_in_nsa, b_gate, nsa_q_norm_g, nsa_k_norm_g, cmp_pe, cmp_w1, cmp_w2, w_out_nsa, w_router, b_router, w_gate, w_up, w_down):
    bp, t_p, d = x_prompt.shape
    bs, t_s, _ = x_sample.shape
    assert t_s == 1
    n_p, n_s = bp * t_p, bs * t_s
    m_len = mem_prompt.shape[1]
    past_len = page_table.shape[1] * PAGE_SIZE
    pos_p = jnp.arange(t_p, dtype=I32)
    pos_s = past_len + jnp.arange(t_s, dtype=I32)
    xp = x_prompt.reshape(n_p, d)
    xs = x_sample.reshape(n_s, d)
    mem_flat = mem_prompt.reshape(-1, d)
    pool_p, pool_s, rows_p, rows_s, win_p, win_s, mem_k_p, mem_v_p = [], [], [], [], [], [], [], []
    for i in range(DEPTH):
        li = i // 2
        kv = proj(mem_flat, w_mem_kv[i].astype(BF16), gain=norm_mem_g[i])
        mk = head_norm(kv, 0, MEM_W, mem_k_norm_g[i]).reshape(bp, m_len, MEM_W)
        mv = kv[:, MEM_W:].reshape(bp, m_len, MEM_W)
        mem_k_p.append(mk.reshape(bp, m_len, MEM_HEADS, HEAD_DIM))
        mem_v_p.append(mv.reshape(bp, m_len, MEM_HEADS, HEAD_DIM))
        mk_s = cache_mem_k[i].reshape(bs, m_len, MEM_W)
        mv_s = cache_mem_v[i].reshape(bs, m_len, MEM_W)
        if i % 2 == 0:
            w_in = w_in_pool[li].astype(BF16)
            zp = proj(xp, w_in, gain=norm_mix_g[i]).reshape(bp, t_p, -1)
            zs = proj(xs, w_in, gain=norm_mix_g[i]).reshape(bs, t_s, -1)
            op = pool_mix(zp, None, w_pool_grp[li], pool_scale[li], 0)
            zs16 = jnp.pad(zs, ((0, 0), (0, POOL_HALO - t_s), (0, 0)))
            halo = jnp.pad(state_pool[li], ((0, 0), (1, 0), (0, 0)))
            os_ = pool_mix(zs16, halo, w_pool_grp[li], pool_scale[li], past_len)[:, :t_s]
            pool_p.append(zp[:, t_p - POOL_STATE:, :TOK_W])
            pool_s.append(jnp.concatenate([state_pool[li], zs[..., :TOK_W]], axis=1)[:, -POOL_STATE:])
            mq0 = TOK_W
            w_out = w_out_pool[li]
        else:
            w_in = _reorder_nsa_weight(w_in_nsa[li])
            zp = proj(xp, w_in, gain=norm_mix_g[i]).reshape(bp, t_p, -1)
            zs = proj(xs, w_in, gain=norm_mix_g[i]).reshape(bs, t_s, -1)
            q, kc, vc, ks, vs, kw, vw, gate = nsa_prep(zp, pos_p, nsa_q_norm_g[li], nsa_k_norm_g[li], b_gate[li])
            kcmp = compress_rows(kc, cmp_pe[li, 0], cmp_w1[li, 0], cmp_w2[li, 0])
            vcmp = compress_rows(vc, cmp_pe[li, 1], cmp_w1[li, 1], cmp_w2[li, 1])
            o_cmp, sel_t = cmp_select_prompt(q, kcmp, vcmp)
            o_sel = sel_attention_prompt(q, ks, vs, sel_t)
            op = win_combine_prompt(q, kw, vw, o_cmp, o_sel, gate)
            kvr = lambda a: a.reshape(a.shape[0], a.shape[1], NSA_KV_HEADS, HEAD_DIM)
            rows_p.append(tuple(kvr(a) for a in (kc, vc, ks, vs)))
            wb = min(WINDOW, t_p)
            win_p.append((kvr(kw[:, t_p - wb:]), kvr(vw[:, t_p - wb:])))
            prep_s = nsa_prep(zs, pos_s, nsa_q_norm_g[li], nsa_k_norm_g[li], b_gate[li])
            os_ = nsa_sample_step(prep_s, li, (cache_cmp_k, cache_cmp_v, cache_sel_k, cache_sel_v),
                                  state_win_k, state_win_v, page_table, cmp_pe[li], cmp_w1[li], cmp_w2[li])
            rows_s.append(tuple(kvr(a) for a in prep_s[1:5]))
            win_s.append(tuple(jnp.concatenate([st[li][:, t_s:], kvr(new)], axis=1)
                               for st, new in ((state_win_k, prep_s[5]), (state_win_v, prep_s[6]))))
            mq0 = NSA_MQ0
            w_out = w_out_nsa[li]
        ap = mem_attention(zp, mq0, mk, mv, mem_q_norm_g[i])
        as_ = mem_attention(zs, mq0, mk_s, mv_s, mem_q_norm_g[i])
        w_out = w_out.astype(BF16)
        xp = proj([op.reshape(n_p, TOK_W), ap.reshape(n_p, MEM_W)], w_out, residual=xp)
        xs = proj([os_.reshape(n_s, TOK_W), as_.reshape(n_s, MEM_W)], w_out, residual=xs)
        xn_p, e_p, g_p = router(xp, norm_ffn_g[i], w_router, b_router)
        xs_pad = jnp.pad(xs, ((0, LANES - n_s), (0, 0)))
        xn_s, e_s, g_s = router(xs_pad, norm_ffn_g[i], w_router, b_router)
        xa = moe_ffn2(jnp.concatenate([xp, xs], axis=0),
                     jnp.concatenate([xn_p, xn_s[:n_s]], axis=0),
                     jnp.concatenate([e_p, e_s[:, :n_s]], axis=1),
                     jnp.concatenate([g_p, g_s[:, :n_s]], axis=1),
                     w_gate, w_up, w_down, i)
        xp, xs = xa[:n_p], xa[n_p:]
    stk = lambda lst, j: jnp.stack([r[j] for r in lst])
    return (xp.reshape(bp, t_p, d), xs.reshape(bs, t_s, d),
            jnp.stack(pool_p), jnp.stack(pool_s),
            stk(rows_p, 0), stk(rows_p, 1), stk(rows_p, 2), stk(rows_p, 3),
            stk(rows_s, 0), stk(rows_s, 1), stk(rows_s, 2), stk(rows_s, 3),
            stk(win_p, 0), stk(win_p, 1), stk(win_s, 0), stk(win_s, 1),
            jnp.stack(mem_k_p), jnp.stack(mem_v_p))
```

```python
import functools

import jax
import jax.numpy as jnp
import numpy as np
from jax import lax
from jax.experimental import pallas as pl
from jax.experimental.pallas import tpu as pltpu

D_MODEL = 2048
DEPTH = 4
PAGE_SIZE = 128
HEAD_DIM = 128
ROPE_THETA = 500000.0
ROPE_DIM = HEAD_DIM // 4
MEM_HEADS = 4
MEM_W = MEM_HEADS * HEAD_DIM
TOK_W = D_MODEL - MEM_W
POOL_WINDOWS = (2, 4, 8, 16)
POOL_GROUP = TOK_W // len(POOL_WINDOWS)
POOL_STATE = max(POOL_WINDOWS) - 1
POOL_HALO = POOL_STATE + 1
NSA_HEADS = TOK_W // HEAD_DIM
NSA_KV_HEADS = 2
NSA_GROUP = NSA_HEADS // NSA_KV_HEADS
KV_W = NSA_KV_HEADS * HEAD_DIM
GROUP_W = NSA_GROUP * HEAD_DIM
CMP_BLOCK = 32
CMP_STRIDE = 16
SEL_BLOCK = 64
SEL_TOP = 16
WINDOW = 512
NSA_TOK_IN = NSA_HEADS * HEAD_DIM + 6 * KV_W + 3 * NSA_HEADS
N_EXPERTS = 16
N_GROUPS = 4
EXP_PER_GROUP = N_EXPERTS // N_GROUPS
TOP_K = 2
D_FF = 1024
EPS = 1e-6
NEG = -1e30
SCALE = HEAD_DIM ** -0.5
LANES = 128

V7X_VMEM_BYTES = 64 * 1024 * 1024
VMEM_LIMIT = V7X_VMEM_BYTES * 3 // 4

BF16 = jnp.bfloat16
F32 = jnp.float32
I32 = jnp.int32

NSA_Q0 = 0
NSA_KV0 = NSA_HEADS * HEAD_DIM
NSA_MQ0 = NSA_KV0 + 6 * KV_W
NSA_GL0 = NSA_MQ0 + MEM_W
NSA_ZW = 3840


def _cparams(sem):
    return pltpu.CompilerParams(dimension_semantics=sem, vmem_limit_bytes=VMEM_LIMIT)


def _nt(a, b):
    return lax.dot_general(a, b, (((1,), (1,)), ((), ())), preferred_element_type=F32)


def _tn(a, b):
    return lax.dot_general(a, b, (((0,), (0,)), ((), ())), preferred_element_type=F32)


def _rms(x, g):
    return x * lax.rsqrt(jnp.mean(x * x, axis=-1, keepdims=True) + EPS) * g


def _pick_tile(n, cap, unit):
    if n <= cap:
        return n
    best = None
    for t in range(unit, cap + 1, unit):
        if n % t == 0:
            best = t
    assert best is not None, (n, cap, unit)
    return best


def _proj_kernel(*refs, n_x, norm, residual):
    x_refs, refs = refs[:n_x], refs[n_x:]
    if norm:
        g_ref, refs = refs[0], refs[1:]
    w_ref, refs = refs[0], refs[1:]
    if residual:
        r_ref, refs = refs[0], refs[1:]
    o_ref, xn_ref = refs

    @pl.when(pl.program_id(1) == 0)
    def _():
        off = 0
        for x_ref in x_refs:
            x = x_ref[...].astype(F32)
            if norm:
                x = _rms(x, g_ref[...])
            xn_ref[:, off:off + x.shape[1]] = x.astype(BF16)
            off += x.shape[1]

    y = jnp.dot(xn_ref[...], w_ref[...], preferred_element_type=F32)
    if residual:
        y = y + r_ref[...]
    o_ref[...] = y


def proj(xs, w_bf16, gain=None, residual=None):
    if not isinstance(xs, (list, tuple)):
        xs = [xs]
    assert gain is None or len(xs) == 1
    m = xs[0].shape[0]
    k = sum(x.shape[1] for x in xs)
    n = w_bf16.shape[1]
    tm = _pick_tile(m, 1024, 8)
    tn = _pick_tile(n, 1024, LANES)
    in_specs = [pl.BlockSpec((tm, x.shape[1]), lambda i, j: (i, 0)) for x in xs]
    args = list(xs)
    if gain is not None:
        in_specs.append(pl.BlockSpec((1, k), lambda i, j: (0, 0)))
        args.append(gain.reshape(1, k).astype(F32))
    in_specs.append(pl.BlockSpec((k, tn), lambda i, j: (0, j)))
    args.append(w_bf16)
    if residual is not None:
        in_specs.append(pl.BlockSpec((tm, tn), lambda i, j: (i, j)))
        args.append(residual)
    return pl.pallas_call(
        functools.partial(_proj_kernel, n_x=len(xs), norm=gain is not None, residual=residual is not None),
        grid=(m // tm, n // tn),
        in_specs=in_specs,
        out_specs=pl.BlockSpec((tm, tn), lambda i, j: (i, j)),
        out_shape=jax.ShapeDtypeStruct((m, n), F32),
        scratch_shapes=[pltpu.VMEM((tm, k), BF16)],
        compiler_params=_cparams(("parallel", "arbitrary")),
        name="proj",
    )(*args)


def _head_norm_kernel(x_ref, g_ref, o_ref):
    x = x_ref[...]
    g = g_ref[...]
    n_heads = x.shape[1] // HEAD_DIM
    o_ref[...] = jnp.concatenate(
        [_rms(x[:, h * HEAD_DIM:(h + 1) * HEAD_DIM], g) for h in range(n_heads)], axis=1)


def head_norm(x, col0, width, gain):
    m = x.shape[0]
    assert col0 % width == 0
    return pl.pallas_call(
        _head_norm_kernel,
        grid=(1,),
        in_specs=[pl.BlockSpec((m, width), lambda i: (0, col0 // width)),
                  pl.BlockSpec((1, HEAD_DIM), lambda i: (0, 0))],
        out_specs=pl.BlockSpec((m, width), lambda i: (0, 0)),
        out_shape=jax.ShapeDtypeStruct((m, width), F32),
        compiler_params=_cparams(("arbitrary",)),
        name="head_norm",
    )(x, gain.reshape(1, HEAD_DIM))


def _mem_attn_kernel(q_ref, k_ref, v_ref, g_ref, o_ref):
    q = q_ref[...]
    rows = q.shape[0]
    if rows < 8:
        q = jnp.broadcast_to(q[0:1], (8, q.shape[1]))
    g = g_ref[...]
    outs = []
    for h in range(MEM_HEADS):
        sl = slice(h * HEAD_DIM, (h + 1) * HEAD_DIM)
        qh = _rms(q[:, sl], g).astype(BF16)
        s = _nt(qh, k_ref[:, sl].astype(BF16)) * SCALE
        m = jnp.max(s, axis=-1, keepdims=True)
        e = jnp.exp(s - m)
        p = e / jnp.sum(e, axis=-1, keepdims=True)
        outs.append(jnp.dot(p.astype(BF16), v_ref[:, sl].astype(BF16), preferred_element_type=F32))
    o = jnp.concatenate(outs, axis=1)
    o_ref[...] = o[:rows]


def mem_attention(z, col0, k, v, gain):
    b, t, _ = z.shape
    m = k.shape[1]
    tq = _pick_tile(t, 512, 8)
    assert col0 % MEM_W == 0
    return pl.pallas_call(
        _mem_attn_kernel,
        grid=(b, t // tq),
        in_specs=[pl.BlockSpec((None, tq, MEM_W), lambda bi, i: (bi, i, col0 // MEM_W)),
                  pl.BlockSpec((None, m, MEM_W), lambda bi, i: (bi, 0, 0)),
                  pl.BlockSpec((None, m, MEM_W), lambda bi, i: (bi, 0, 0)),
                  pl.BlockSpec((1, HEAD_DIM), lambda bi, i: (0, 0))],
        out_specs=pl.BlockSpec((None, tq, MEM_W), lambda bi, i: (bi, i, 0)),
        out_shape=jax.ShapeDtypeStruct((b, t, MEM_W), F32),
        compiler_params=_cparams(("parallel", "arbitrary")),
        name="mem_attention",
    )(z, k, v, gain.reshape(1, HEAD_DIM))


def _pool_kernel(u_ref, halo_ref, w_ref, sc_ref, o_ref, *, pos0, zero_first_halo):
    tq = u_ref.shape[0]
    qi = pl.program_id(1)
    u = u_ref[...]
    halo = halo_ref[...]
    if zero_first_halo:
        halo = jnp.where(qi > 0, halo, 0.0)
    pos = pos0 + qi * tq + lax.broadcasted_iota(I32, (tq, 1), 0)
    outs = []
    for g, w in enumerate(POOL_WINDOWS):
        cs = slice(g * POOL_GROUP, (g + 1) * POOL_GROUP)
        ug = u[:, cs]
        acc = jnp.concatenate([halo[:, cs], ug], axis=0)
        span = 1
        while span < w:
            acc = acc[span:] + acc[:-span]
            span *= 2
        ssum = acc[POOL_HALO - (w - 1):POOL_HALO - (w - 1) + tq]
        cnt = jnp.minimum(w, pos + 1).astype(F32)
        d = (ssum / cnt - ug).astype(BF16)
        y = jnp.dot(d, w_ref[g].astype(BF16), preferred_element_type=F32)
        outs.append(y)
    o_ref[...] = jnp.concatenate(outs, axis=1) * sc_ref[...]


def pool_mix(z, halo, w_grp, scale, pos0):
    b, t, _ = z.shape
    tq = _pick_tile(t, 256, 16)
    if halo is None:
        halo_arr = z
        halo_spec = pl.BlockSpec((None, POOL_HALO, TOK_W),
                                 lambda bi, i: (bi, jnp.maximum(i * (tq // POOL_HALO) - 1, 0), 0))
    else:
        assert t == tq
        halo_arr = halo
        halo_spec = pl.BlockSpec((None, POOL_HALO, TOK_W), lambda bi, i: (bi, 0, 0))
    return pl.pallas_call(
        functools.partial(_pool_kernel, pos0=pos0, zero_first_halo=halo is None),
        grid=(b, t // tq),
        in_specs=[pl.BlockSpec((None, tq, TOK_W), lambda bi, i: (bi, i, 0)),
                  halo_spec,
                  pl.BlockSpec(w_grp.shape, lambda bi, i: (0, 0, 0)),
                  pl.BlockSpec((1, TOK_W), lambda bi, i: (0, 0))],
        out_specs=pl.BlockSpec((None, tq, TOK_W), lambda bi, i: (bi, i, 0)),
        out_shape=jax.ShapeDtypeStruct((b, t, TOK_W), F32),
        compiler_params=_cparams(("parallel", "arbitrary")),
        name="pool_mix",
    )(z, halo_arr, w_grp, scale.reshape(1, TOK_W))


def _rope_tables(pos):
    half = ROPE_DIM // 2
    inv = 1.0 / (ROPE_THETA ** (jnp.arange(half, dtype=F32) * 2.0 / ROPE_DIM))
    ang = pos.astype(F32)[:, None] * inv[None, :]
    cos, sin = jnp.cos(ang), jnp.sin(ang)
    t = pos.shape[0]
    rest = HEAD_DIM - ROPE_DIM
    c = jnp.concatenate([cos, cos, jnp.ones((t, rest), F32)], axis=1)
    s_lo = jnp.concatenate([-sin, jnp.zeros((t, HEAD_DIM - half), F32)], axis=1)
    s_hi = jnp.concatenate([jnp.zeros((t, half), F32), sin, jnp.zeros((t, rest), F32)], axis=1)
    return c, s_lo, s_hi


def _nsa_prep_kernel(z_ref, c_ref, slo_ref, shi_ref, qg_ref, kg_ref, bg_ref,
                     q_ref, kc_ref, vc_ref, ks_ref, vs_ref, kw_ref, vw_ref, gate_ref):
    rows = z_ref.shape[0]
    c, slo, shi = c_ref[...], slo_ref[...], shi_ref[...]
    half = ROPE_DIM // 2

    def rope_norm(x, g):
        if rows < 8:
            x = jnp.broadcast_to(x[0:1], (8, HEAD_DIM))
        x = _rms(x, g)
        y = x * c + pltpu.roll(x, HEAD_DIM - half, 1) * slo + pltpu.roll(x, half, 1) * shi
        return y[:rows]

    qg = qg_ref[...]
    q_ref[...] = jnp.concatenate(
        [rope_norm(z_ref[:, NSA_Q0 + h * HEAD_DIM:NSA_Q0 + (h + 1) * HEAD_DIM], qg) for h in range(NSA_HEADS)], axis=1)
    for j, (o_ref, which) in enumerate(((kc_ref, 0), (vc_ref, None), (ks_ref, 1), (vs_ref, None), (kw_ref, 2), (vw_ref, None))):
        c0 = NSA_KV0 + j * KV_W
        if which is None:
            o_ref[...] = z_ref[:, c0:c0 + KV_W]
        else:
            g = kg_ref[which:which + 1, :]
            o_ref[...] = jnp.concatenate(
                [rope_norm(z_ref[:, c0 + h * HEAD_DIM:c0 + (h + 1) * HEAD_DIM], g) for h in range(NSA_KV_HEADS)], axis=1)
    gate_ref[...] = jax.nn.sigmoid(z_ref[:, NSA_GL0:NSA_GL0 + LANES] + bg_ref[...])


def nsa_prep(z, pos, q_g, k_g, b_gate):
    b, t, _ = z.shape
    tq = _pick_tile(t, 256, 8)
    c, slo, shi = _rope_tables(pos)
    if t < 8:
        c, slo, shi = (jnp.broadcast_to(a, (8, HEAD_DIM)) for a in (c, slo, shi))
    tt = max(tq, 8)
    bg = jnp.pad(b_gate.reshape(1, -1), ((0, 0), (0, LANES - b_gate.shape[-1])))
    tab_spec = pl.BlockSpec((tt, HEAD_DIM), lambda bi, i: (i, 0))
    kv_spec = pl.BlockSpec((None, tq, KV_W), lambda bi, i: (bi, i, 0))
    kv_shape = jax.ShapeDtypeStruct((b, t, KV_W), F32)
    return pl.pallas_call(
        _nsa_prep_kernel,
        grid=(b, t // tq),
        in_specs=[pl.BlockSpec((None, tq, NSA_ZW), lambda bi, i: (bi, i, 0)),
                  tab_spec, tab_spec, tab_spec,
                  pl.BlockSpec((1, HEAD_DIM), lambda bi, i: (0, 0)),
                  pl.BlockSpec((3, HEAD_DIM), lambda bi, i: (0, 0)),
                  pl.BlockSpec((1, LANES), lambda bi, i: (0, 0))],
        out_specs=[pl.BlockSpec((None, tq, TOK_W), lambda bi, i: (bi, i, 0))] + [kv_spec] * 6
                  + [pl.BlockSpec((None, tq, LANES), lambda bi, i: (bi, i, 0))],
        out_shape=[jax.ShapeDtypeStruct((b, t, TOK_W), F32)] + [kv_shape] * 6
                  + [jax.ShapeDtypeStruct((b, t, LANES), F32)],
        compiler_params=_cparams(("parallel", "arbitrary")),
        name="nsa_prep",
    )(z, c, slo, shi, q_g.reshape(1, HEAD_DIM), k_g, bg)


def _compress_body(slab, n_chunks, pe_ref, w1_ref, w2_ref, o_ref):
    pe = pe_ref[...]
    pe_lo = jnp.concatenate([pe[r:r + 1] for r in range(CMP_STRIDE)], axis=1)
    pe_hi = jnp.concatenate([pe[CMP_STRIDE + r:CMP_STRIDE + r + 1] for r in range(CMP_STRIDE)], axis=1)
    half_k = CMP_STRIDE * HEAD_DIM
    w_lo = w1_ref[0:half_k, :].astype(BF16)
    w_hi = w1_ref[half_k:2 * half_k, :].astype(BF16)
    w2 = w2_ref[...].astype(BF16)
    for h in range(NSA_KV_HEADS):
        xh = jnp.concatenate([slab(r, h) for r in range(CMP_STRIDE)], axis=1)
        a = jnp.dot((xh + pe_lo).astype(BF16), w_lo, preferred_element_type=F32)
        bb = jnp.dot((xh + pe_hi).astype(BF16), w_hi, preferred_element_type=F32)
        hid = jax.nn.gelu(a + pltpu.roll(bb, n_chunks - 1, 0))
        o_ref[h] = jnp.dot(hid.astype(BF16), w2, preferred_element_type=F32)


def _compress_kernel(x_ref, pe_ref, w1_ref, w2_ref, o_ref):
    def slab(r, h):
        c0 = (r * NSA_KV_HEADS + h) * HEAD_DIM
        return x_ref[:, c0:c0 + HEAD_DIM]
    _compress_body(slab, x_ref.shape[0], pe_ref, w1_ref, w2_ref, o_ref)


def compress_rows(x, pe, w1, w2):
    b, t, _ = x.shape
    n_chunks = t // CMP_STRIDE
    cw = CMP_STRIDE * KV_W
    xc = x.reshape(b, n_chunks, cw)
    return pl.pallas_call(
        _compress_kernel,
        grid=(b,),
        in_specs=[pl.BlockSpec((None, n_chunks, cw), lambda bi: (bi, 0, 0)),
                  pl.BlockSpec(pe.shape, lambda bi: (0, 0)),
                  pl.BlockSpec(w1.shape, lambda bi: (0, 0)),
                  pl.BlockSpec(w2.shape, lambda bi: (0, 0))],
        out_specs=pl.BlockSpec((None, NSA_KV_HEADS, n_chunks, HEAD_DIM), lambda bi: (bi, 0, 0, 0)),
        out_shape=jax.ShapeDtypeStruct((b, NSA_KV_HEADS, n_chunks, HEAD_DIM), F32),
        compiler_params=_cparams(("parallel",)),
        name="compress_rows",
    )(xc, pe, w1, w2)


def _stack_heads(q):
    return jnp.concatenate([q[:, g * HEAD_DIM:(g + 1) * HEAD_DIM] for g in range(NSA_GROUP)], axis=0)


def _unstack_heads(o, tq):
    return jnp.concatenate([o[g * tq:(g + 1) * tq] for g in range(NSA_GROUP)], axis=1)


def _rank_select(imp_t, n_real):
    n_blk = imp_t.shape[0]
    blk = lax.broadcasted_iota(I32, imp_t.shape, 0)
    cnt = jnp.zeros(imp_t.shape, F32)
    for j in range(n_real):
        row = imp_t[j:j + 1, :]
        beats = jnp.where(row > imp_t, 1.0, jnp.where(row == imp_t, jnp.where(blk > j, 1.0, 0.0), 0.0))
        cnt = cnt + beats
    return jnp.where(cnt < SEL_TOP, 1.0, 0.0)


def _cmp_select_kernel(q_ref, kc_ref, vc_ref, cov_ref, o_ref, sel_ref):
    tq = q_ref.shape[0]
    nc = kc_ref.shape[0]
    n_blk = sel_ref.shape[0]
    qi = pl.program_id(2)
    q6 = _stack_heads(q_ref[...]).astype(BF16)
    s = _nt(q6, kc_ref[...].astype(BF16)) * SCALE
    s = s.reshape(NSA_GROUP, tq, nc)
    q_pos = qi * tq + lax.broadcasted_iota(I32, (tq, nc), 0)
    c_last = lax.broadcasted_iota(I32, (tq, nc), 1) * CMP_STRIDE + (CMP_BLOCK - 1)
    valid = (c_last <= q_pos)[None]
    s = jnp.where(valid, s, NEG)
    m = jnp.max(s, axis=-1, keepdims=True)
    e = jnp.where(valid, jnp.exp(s - m), 0.0)
    p = (e / jnp.maximum(jnp.sum(e, axis=-1, keepdims=True), 1e-30)).astype(BF16)
    o = jnp.dot(p.reshape(NSA_GROUP * tq, nc), vc_ref[...].astype(BF16), preferred_element_type=F32)
    o_ref[...] = _unstack_heads(o, tq)
    p_cat = jnp.concatenate([p[g] for g in range(NSA_GROUP)], axis=1)
    imp_t = _nt(cov_ref[...], p_cat)
    blk = lax.broadcasted_iota(I32, (n_blk, tq), 0)
    pos_t = qi * tq + lax.broadcasted_iota(I32, (n_blk, tq), 1)
    cur = pos_t // SEL_BLOCK
    forced = (blk == 0) | (blk == cur) | (blk == cur - 1)
    future = blk * SEL_BLOCK > pos_t
    imp_t = jnp.where(future, -1e9, jnp.where(forced, 1e9, imp_t))
    sel_ref[...] = _rank_select(imp_t, n_blk).astype(BF16)


def _cover_t(nc, n_blk, nc_valid):
    c_start = np.arange(nc) * CMP_STRIDE
    c_last = c_start + CMP_BLOCK - 1
    s_start = np.arange(n_blk) * SEL_BLOCK
    cov = (c_start[None, :] < s_start[:, None] + SEL_BLOCK) & (c_last[None, :] >= s_start[:, None])
    cov = cov & (np.arange(nc)[None, :] < nc_valid)
    return np.tile(cov.astype(np.float32), (1, NSA_GROUP))


def cmp_select_prompt(q, kcmp, vcmp):
    b, t, _ = q.shape
    nc = kcmp.shape[2]
    n_blk = -(-t // SEL_BLOCK)
    tq = _pick_tile(t, 256, LANES)
    cov = jnp.asarray(_cover_t(nc, n_blk, nc - 1), BF16)
    return pl.pallas_call(
        _cmp_select_kernel,
        grid=(b, NSA_KV_HEADS, t // tq),
        in_specs=[pl.BlockSpec((None, tq, GROUP_W), lambda bi, h, i: (bi, i, h)),
                  pl.BlockSpec((None, None, nc, HEAD_DIM), lambda bi, h, i: (bi, h, 0, 0)),
                  pl.BlockSpec((None, None, nc, HEAD_DIM), lambda bi, h, i: (bi, h, 0, 0)),
                  pl.BlockSpec(cov.shape, lambda bi, h, i: (0, 0))],
        out_specs=[pl.BlockSpec((None, tq, GROUP_W), lambda bi, h, i: (bi, i, h)),
                   pl.BlockSpec((None, None, n_blk, tq), lambda bi, h, i: (bi, h, 0, i))],
        out_shape=[jax.ShapeDtypeStruct((b, t, TOK_W), F32),
                   jax.ShapeDtypeStruct((b, NSA_KV_HEADS, n_blk, t), BF16)],
        compiler_params=_cparams(("parallel", "parallel", "arbitrary")),
        name="cmp_select_prompt",
    )(q, kcmp, vcmp, cov)


SEL_TQ = 128
SEL_TK = 512


def _sel_kernel(q_ref, k_ref, v_ref, sel_ref, exp_ref, o_ref, mask_ref, m_ref, l_ref, acc_ref):
    t = k_ref.shape[0]
    qi = pl.program_id(2)
    q6 = _stack_heads(q_ref[...]).astype(BF16)
    mask_ref[...] = (1.0 - _tn(sel_ref[...], exp_ref[...])) * NEG
    m_ref[...] = jnp.full(m_ref.shape, NEG, F32)
    l_ref[...] = jnp.zeros(l_ref.shape, F32)
    acc_ref[...] = jnp.zeros(acc_ref.shape, F32)
    q_lo = qi * SEL_TQ

    def key_tile(j, causal):
        lo, hi = j * SEL_TK, (j + 1) * SEL_TK
        k = k_ref[lo:hi, :].astype(BF16)
        v = v_ref[lo:hi, :].astype(BF16)
        bias = mask_ref[:, lo:hi]
        if causal:
            q_pos = q_lo + lax.broadcasted_iota(I32, (SEL_TQ, SEL_TK), 0)
            k_pos = lo + lax.broadcasted_iota(I32, (SEL_TQ, SEL_TK), 1)
            bias = jnp.where(k_pos <= q_pos, bias, NEG)
        s = _nt(q6, k).reshape(NSA_GROUP, SEL_TQ, SEL_TK) * SCALE + bias[None]
        m_old = m_ref[...]
        m_new = jnp.maximum(m_old, jnp.max(s, axis=-1, keepdims=True))
        e = jnp.exp(s - m_new)
        alpha = jnp.exp(m_old - m_new)
        l_ref[...] = alpha * l_ref[...] + jnp.sum(e, axis=-1, keepdims=True)
        pv = jnp.dot(e.reshape(NSA_GROUP * SEL_TQ, SEL_TK).astype(BF16), v, preferred_element_type=F32)
        acc_ref[...] = alpha * acc_ref[...] + pv.reshape(NSA_GROUP, SEL_TQ, HEAD_DIM)
        m_ref[...] = m_new

    for j in range(t // SEL_TK):
        lo, hi = j * SEL_TK, (j + 1) * SEL_TK

        @pl.when(hi - 1 <= q_lo)
        def _():
            key_tile(j, False)

        @pl.when((lo <= q_lo + SEL_TQ - 1) & (hi - 1 > q_lo))
        def _():
            key_tile(j, True)

    o = acc_ref[...] / jnp.maximum(l_ref[...], 1e-30)
    o_ref[...] = jnp.concatenate([o[g] for g in range(NSA_GROUP)], axis=-1)


def sel_attention_prompt(q, ks, vs, sel_t):
    b, t, _ = q.shape
    n_blk = sel_t.shape[2]
    expand = (np.arange(t)[None, :] // SEL_BLOCK == np.arange(n_blk)[:, None]).astype(np.float32)
    expand = jnp.asarray(expand, BF16)
    return pl.pallas_call(
        _sel_kernel,
        grid=(b, NSA_KV_HEADS, t // SEL_TQ),
        in_specs=[pl.BlockSpec((None, SEL_TQ, GROUP_W), lambda bi, h, i: (bi, i, h)),
                  pl.BlockSpec((None, t, HEAD_DIM), lambda bi, h, i: (bi, 0, h)),
                  pl.BlockSpec((None, t, HEAD_DIM), lambda bi, h, i: (bi, 0, h)),
                  pl.BlockSpec((None, None, n_blk, SEL_TQ), lambda bi, h, i: (bi, h, 0, i)),
                  pl.BlockSpec((n_blk, t), lambda bi, h, i: (0, 0))],
        out_specs=pl.BlockSpec((None, SEL_TQ, GROUP_W), lambda bi, h, i: (bi, i, h)),
        out_shape=jax.ShapeDtypeStruct((b, t, TOK_W), F32),
        scratch_shapes=[pltpu.VMEM((SEL_TQ, t), F32),
                        pltpu.VMEM((NSA_GROUP, SEL_TQ, 1), F32),
                        pltpu.VMEM((NSA_GROUP, SEL_TQ, 1), F32),
                        pltpu.VMEM((NSA_GROUP, SEL_TQ, HEAD_DIM), F32)],
        compiler_params=_cparams(("parallel", "parallel", "arbitrary")),
        name="sel_attention_prompt",
    )(q, ks, vs, sel_t, expand)


def _heads_to_lanes_t(q):
    tq = q.shape[0]
    return jnp.concatenate([q[:, g * HEAD_DIM:(g + 1) * HEAD_DIM].T for g in range(NSA_GROUP)], axis=1)


def _lanes_to_heads_t(o_t, tq):
    return jnp.concatenate([o_t[:, g * tq:(g + 1) * tq].T for g in range(NSA_GROUP)], axis=1)


def _selt_kernel(q_ref, k_ref, v_ref, sel_ref, exp_ref, o_ref, mask_ref, m_ref, l_ref, acc_ref):
    t = k_ref.shape[0]
    qi = pl.program_id(2)
    q_t = _heads_to_lanes_t(q_ref[...]).astype(BF16)
    mask_ref[...] = (1.0 - jnp.dot(exp_ref[...], sel_ref[...], preferred_element_type=F32)) * NEG
    m_ref[...] = jnp.full(m_ref.shape, NEG, F32)
    l_ref[...] = jnp.zeros(l_ref.shape, F32)
    acc_ref[...] = jnp.zeros(acc_ref.shape, F32)
    q_lo = qi * SEL_TQ

    def key_tile(j, causal):
        lo, hi = j * SEL_TK, (j + 1) * SEL_TK
        k = k_ref[lo:hi, :].astype(BF16)
        v = v_ref[lo:hi, :].astype(BF16)
        bias = mask_ref[lo:hi, :]
        if causal:
            k_pos = lo + lax.broadcasted_iota(I32, (SEL_TK, SEL_TQ), 0)
            q_pos = q_lo + lax.broadcasted_iota(I32, (SEL_TK, SEL_TQ), 1)
            bias = jnp.where(k_pos <= q_pos, bias, NEG)
        s = jnp.dot(k, q_t, preferred_element_type=F32) * SCALE + jnp.concatenate([bias] * NSA_GROUP, axis=1)
        m_old = m_ref[...]
        m_new = jnp.maximum(m_old, jnp.max(s, axis=0, keepdims=True))
        e = jnp.exp(s - m_new)
        alpha = jnp.exp(m_old - m_new)
        l_ref[...] = alpha * l_ref[...] + jnp.sum(e, axis=0, keepdims=True)
        acc_ref[...] = alpha * acc_ref[...] + _tn(v, e.astype(BF16))
        m_ref[...] = m_new

    for j in range(t // SEL_TK):
        lo, hi = j * SEL_TK, (j + 1) * SEL_TK

        @pl.when(hi - 1 <= q_lo)
        def _():
            key_tile(j, False)

        @pl.when((lo <= q_lo + SEL_TQ - 1) & (hi - 1 > q_lo))
        def _():
            key_tile(j, True)

    o_t = acc_ref[...] / jnp.maximum(l_ref[...], 1e-30)
    o_ref[...] = _lanes_to_heads_t(o_t, SEL_TQ)


def sel_attention_prompt_t(q, ks, vs, sel_t):
    b, t, _ = q.shape
    n_blk = sel_t.shape[2]
    expand = (np.arange(t)[:, None] // SEL_BLOCK == np.arange(n_blk)[None, :]).astype(np.float32)
    expand = jnp.asarray(expand, BF16)
    return pl.pallas_call(
        _selt_kernel,
        grid=(b, NSA_KV_HEADS, t // SEL_TQ),
        in_specs=[pl.BlockSpec((None, SEL_TQ, GROUP_W), lambda bi, h, i: (bi, i, h)),
                  pl.BlockSpec((None, t, HEAD_DIM), lambda bi, h, i: (bi, 0, h)),
                  pl.BlockSpec((None, t, HEAD_DIM), lambda bi, h, i: (bi, 0, h)),
                  pl.BlockSpec((None, None, n_blk, SEL_TQ), lambda bi, h, i: (bi, h, 0, i)),
                  pl.BlockSpec((t, n_blk), lambda bi, h, i: (0, 0))],
        out_specs=pl.BlockSpec((None, SEL_TQ, GROUP_W), lambda bi, h, i: (bi, i, h)),
        out_shape=jax.ShapeDtypeStruct((b, t, TOK_W), F32),
        scratch_shapes=[pltpu.VMEM((t, SEL_TQ), F32),
                        pltpu.VMEM((1, NSA_GROUP * SEL_TQ), F32),
                        pltpu.VMEM((1, NSA_GROUP * SEL_TQ), F32),
                        pltpu.VMEM((HEAD_DIM, NSA_GROUP * SEL_TQ), F32)],
        compiler_params=_cparams(("parallel", "parallel", "arbitrary")),
        name="sel_attention_prompt",
    )(q, ks, vs, sel_t, expand)


WIN_TQ = 128


def _gate_cols(gate, h, j, tq):
    cols = []
    for g in range(NSA_GROUP):
        c = (h * NSA_GROUP + g) * 3 + j
        cols.append(jnp.broadcast_to(gate[:, c:c + 1], (tq, HEAD_DIM)))
    return jnp.concatenate(cols, axis=1)


def _win_combine_kernel(q_ref, k_ref, v_ref, oc_ref, os_ref, gate_ref, o_ref):
    tq = q_ref.shape[0]
    span = WINDOW + tq
    h = pl.program_id(1)
    qi = pl.program_id(2)
    start = pl.multiple_of(jnp.maximum(qi * tq - WINDOW, 0), tq)
    q6 = _stack_heads(q_ref[...]).astype(BF16)
    k = k_ref[pl.ds(start, span), :].astype(BF16)
    v = v_ref[pl.ds(start, span), :].astype(BF16)
    s = _nt(q6, k).reshape(NSA_GROUP, tq, span) * SCALE
    diff = (qi * tq + lax.broadcasted_iota(I32, (tq, span), 0)) - (start + lax.broadcasted_iota(I32, (tq, span), 1))
    valid = ((diff >= 0) & (diff < WINDOW))[None]
    s = jnp.where(valid, s, NEG)
    m = jnp.max(s, axis=-1, keepdims=True)
    e = jnp.where(valid, jnp.exp(s - m), 0.0)
    p = e / jnp.maximum(jnp.sum(e, axis=-1, keepdims=True), 1e-30)
    o = jnp.dot(p.reshape(NSA_GROUP * tq, span).astype(BF16), v, preferred_element_type=F32)
    o_win = _unstack_heads(o, tq)
    gate = gate_ref[...]
    for hh in range(NSA_KV_HEADS):
        @pl.when(h == hh)
        def _():
            o_ref[...] = (_gate_cols(gate, hh, 0, tq) * oc_ref[...] + _gate_cols(gate, hh, 1, tq) * os_ref[...]
                          + _gate_cols(gate, hh, 2, tq) * o_win)


def win_combine_prompt(q, kw, vw, o_cmp, o_sel, gate):
    b, t, _ = q.shape
    assert t >= WINDOW + WIN_TQ
    qspec = pl.BlockSpec((None, WIN_TQ, GROUP_W), lambda bi, h, i: (bi, i, h))
    kspec = pl.BlockSpec((None, t, HEAD_DIM), lambda bi, h, i: (bi, 0, h))
    return pl.pallas_call(
        _win_combine_kernel,
        grid=(b, NSA_KV_HEADS, t // WIN_TQ),
        in_specs=[qspec, kspec, kspec, qspec, qspec,
                  pl.BlockSpec((None, WIN_TQ, LANES), lambda bi, h, i: (bi, i, 0))],
        out_specs=qspec,
        out_shape=jax.ShapeDtypeStruct((b, t, TOK_W), F32),
        compiler_params=_cparams(("parallel", "parallel", "arbitrary")),
        name="win_combine_prompt",
    )(q, kw, vw, o_cmp, o_sel, gate)


def _router_kernel(x_ref, g_ref, wt_ref, b_ref, xn_ref, e_ref, gt_ref):
    xn = _rms(x_ref[...], g_ref[...])
    xn_ref[...] = xn
    lt = _nt(wt_ref[...], xn.astype(BF16))
    ex = jnp.exp(lt - jnp.max(lt, axis=0, keepdims=True))
    aff = ex / jnp.sum(ex, axis=0, keepdims=True)
    sel = aff + b_ref[...]
    row = lambda a, r: a[r:r + 1, :]

    best, g_idx = None, None
    for g in range(N_GROUPS):
        a, b, c, d = (row(sel, g * EXP_PER_GROUP + j) for j in range(EXP_PER_GROUP))
        hi1, lo1, hi2, lo2 = jnp.maximum(a, b), jnp.minimum(a, b), jnp.maximum(c, d), jnp.minimum(c, d)
        score = jnp.maximum(hi1, hi2) + jnp.maximum(jnp.minimum(hi1, hi2), jnp.maximum(lo1, lo2))
        if g == 0:
            best, g_idx = score, jnp.zeros(score.shape, I32)
        else:
            g_idx = jnp.where(score > best, g, g_idx)
            best = jnp.maximum(best, score)

    def in_group(a, j):
        out = row(a, j)
        for g in range(1, N_GROUPS):
            out = jnp.where(g_idx == g, row(a, g * EXP_PER_GROUP + j), out)
        return out

    v = [in_group(sel, j) for j in range(EXP_PER_GROUP)]
    af = [in_group(aff, j) for j in range(EXP_PER_GROUP)]

    def first_max(vals):
        m = functools.reduce(jnp.maximum, vals)
        loc = jnp.full(m.shape, EXP_PER_GROUP - 1, I32)
        for j in range(EXP_PER_GROUP - 2, -1, -1):
            loc = jnp.where(vals[j] == m, j, loc)
        return loc

    l1 = first_max(v)
    l2 = first_max([jnp.where(l1 == j, -jnp.inf, v[j]) for j in range(EXP_PER_GROUP)])
    pick = lambda loc: functools.reduce(lambda acc, j: jnp.where(loc == j, af[j], acc), range(1, EXP_PER_GROUP), af[0])
    a1, a2 = pick(l1), pick(l2)
    tot = a1 + a2
    e_ref[...] = jnp.concatenate([g_idx * EXP_PER_GROUP + l1, g_idx * EXP_PER_GROUP + l2], axis=0)
    gt_ref[...] = jnp.concatenate([a1 / tot, a2 / tot], axis=0)


def router(x, gain, w_router, b_router):
    n, d = x.shape
    tm = _pick_tile(n, 1024, LANES)
    return pl.pallas_call(
        _router_kernel,
        grid=(n // tm,),
        in_specs=[pl.BlockSpec((tm, d), lambda i: (i, 0)),
                  pl.BlockSpec((1, d), lambda i: (0, 0)),
                  pl.BlockSpec((N_EXPERTS, d), lambda i: (0, 0)),
                  pl.BlockSpec((N_EXPERTS, 1), lambda i: (0, 0))],
        out_specs=[pl.BlockSpec((tm, d), lambda i: (i, 0)),
                   pl.BlockSpec((TOP_K, tm), lambda i: (0, i)),
                   pl.BlockSpec((TOP_K, tm), lambda i: (0, i))],
        out_shape=[jax.ShapeDtypeStruct((n, d), F32),
                   jax.ShapeDtypeStruct((TOP_K, n), I32),
                   jax.ShapeDtypeStruct((TOP_K, n), F32)],
        compiler_params=_cparams(("parallel",)),
        name="router",
    )(x, gain.reshape(1, d), w_router.T.astype(BF16), b_router.reshape(N_EXPERTS, 1).astype(F32))


MOE_TM = 512
MOE_NF = 4
MOE_TF = D_FF // MOE_NF
MOE_ROWS = MOE_TM // MOE_NF
MOE_SPLIT = 128


def _moe_kernel(te_ref, nu_ref, src_ref, dst_ref, x_hbm, wg_ref, wu_ref, wd_ref, y_hbm,
                xbuf, xb, ybuf, gsem, ssem):
    i = pl.program_id(0)
    f = pl.program_id(1)
    n_used = nu_ref[0]
    d = xbuf.shape[-1]

    def gather_copy(tile, buf, r):
        row = src_ref[tile * MOE_TM + r]
        return pltpu.make_async_copy(x_hbm.at[pl.ds(row, 1), :], xbuf.at[buf, pl.ds(r, 1), :], gsem.at[buf])

    def scatter_copy(tile, buf, r):
        row = dst_ref[tile * MOE_TM + r]
        return pltpu.make_async_copy(ybuf.at[buf, pl.ds(r, 1), :], y_hbm.at[pl.ds(row, 1), :], ssem.at[buf])

    def wait_all(hbm, vmem_buf, sem):
        pltpu.make_async_copy(hbm.at[pl.ds(0, MOE_TM), :], vmem_buf, sem).wait()

    dump0 = y_hbm.shape[0] - MOE_TM
    cur = i % 2
    oth = 1 - cur

    @pl.when((i == 0) & (f == 0))
    def _():
        def body(r, carry):
            gather_copy(0, 0, r).start()
            return carry
        lax.fori_loop(0, MOE_TM, body, 0)
        ybuf[1] = jnp.zeros(ybuf.shape[1:], F32)

    @pl.when(i < n_used)
    def _():
        @pl.when(f == 0)
        def _():
            wait_all(x_hbm, xbuf.at[cur], gsem.at[cur])
            xb[...] = xbuf[cur].astype(BF16)
            ybuf[cur] = jnp.zeros(ybuf.shape[1:], F32)

        nxt = jnp.minimum(i + 1, n_used - 1)
        prv = jnp.maximum(i - 1, 0)
        wg, wu, wd = wg_ref[...].astype(BF16), wu_ref[...].astype(BF16), wd_ref[...].astype(BF16)
        for c, (lo, hi) in enumerate(((0, MOE_SPLIT), (MOE_SPLIT, MOE_TM))):
            x = xb[lo:hi, :]
            a = jnp.dot(x, wg, preferred_element_type=F32)
            u = jnp.dot(x, wu, preferred_element_type=F32)
            h = (a * jax.nn.sigmoid(a) * u).astype(BF16)
            ybuf[cur, lo:hi, :] += jnp.dot(h, wd, preferred_element_type=F32)
            if c == 0:
                r0 = pl.multiple_of(f * MOE_ROWS, MOE_ROWS)
                for r in range(MOE_ROWS):
                    rr = r0 + r
                    gather_copy(nxt, oth, rr).start()
                    row = jnp.where(i > 0, dst_ref[prv * MOE_TM + rr], dump0 + rr)
                    pltpu.make_async_copy(ybuf.at[oth, pl.ds(rr, 1), :], y_hbm.at[pl.ds(row, 1), :], ssem.at[oth]).start()

        @pl.when(f == MOE_NF - 1)
        def _():
            wait_all(y_hbm, ybuf.at[oth], ssem.at[oth])

    @pl.when((i == n_used) & (f == 0))
    def _():
        wait_all(x_hbm, xbuf.at[cur], gsem.at[cur])

        def body(r, carry):
            scatter_copy(i - 1, oth, r).start()
            return carry
        lax.fori_loop(0, MOE_TM, body, 0)
        wait_all(y_hbm, ybuf.at[oth], ssem.at[oth])


def moe_experts(xn, tile_expert, n_used, src_row, dst_row, n_out_rows, w_gate, w_up, w_down, layer):
    n, d = xn.shape
    n_tiles = tile_expert.shape[0]

    def tile(i, nu):
        return jnp.minimum(i, nu[0] - 1)

    def fcol(i, f, nu):
        return jnp.where(i < nu[0], f, MOE_NF - 1)

    wspec = lambda blk, im: pl.BlockSpec(blk, im)
    grid_spec = pltpu.PrefetchScalarGridSpec(
        num_scalar_prefetch=4,
        grid=(n_tiles, MOE_NF),
        in_specs=[
            pl.BlockSpec(memory_space=pl.ANY),
            wspec((None, None, d, MOE_TF), lambda i, f, te, nu, s, t: (layer, te[tile(i, nu)], 0, fcol(i, f, nu))),
            wspec((None, None, d, MOE_TF), lambda i, f, te, nu, s, t: (layer, te[tile(i, nu)], 0, fcol(i, f, nu))),
            wspec((None, None, MOE_TF, d), lambda i, f, te, nu, s, t: (layer, te[tile(i, nu)], fcol(i, f, nu), 0)),
        ],
        out_specs=pl.BlockSpec(memory_space=pl.ANY),
        scratch_shapes=[pltpu.VMEM((2, MOE_TM, d), F32),
                        pltpu.VMEM((MOE_TM, d), BF16),
                        pltpu.VMEM((2, MOE_TM, d), F32),
                        pltpu.SemaphoreType.DMA((2,)),
                        pltpu.SemaphoreType.DMA((2,))],
    )
    return pl.pallas_call(
        _moe_kernel,
        grid_spec=grid_spec,
        out_shape=jax.ShapeDtypeStruct((n_out_rows, d), F32),
        compiler_params=_cparams(("arbitrary", "arbitrary")),
        name="moe_experts",
    )(tile_expert, n_used, src_row, dst_row, xn, w_gate, w_up, w_down)


def moe_ffn(x, xn, experts, gates, w_gate, w_up, w_down, layer):
    n, d = x.shape
    n_items = n * TOP_K
    n_tiles = -(-n_items // MOE_TM) + N_EXPERTS
    n_slots = n_tiles * MOE_TM
    flat_e = experts.reshape(-1)
    sizes = jnp.sum((flat_e[:, None] == jnp.arange(N_EXPERTS, dtype=I32)[None, :]).astype(I32), axis=0)
    padded = -(-sizes // MOE_TM) * MOE_TM
    ends_p = jnp.cumsum(padded)
    starts_p = ends_p - padded
    starts = jnp.cumsum(sizes) - sizes
    order = jnp.argsort(flat_e).astype(I32)
    tile_expert = jnp.searchsorted(ends_p, jnp.arange(n_tiles, dtype=I32) * MOE_TM, side='right')
    tile_expert = jnp.minimum(tile_expert, N_EXPERTS - 1).astype(I32)
    n_used = (ends_p[-1:] // MOE_TM).astype(I32)
    slot = jnp.arange(n_slots, dtype=I32)
    slot_e = jnp.repeat(tile_expert, MOE_TM)
    within = slot - starts_p[slot_e]
    is_item = within < sizes[slot_e]
    item = order[jnp.clip(starts[slot_e] + within, 0, n_items - 1)]
    src_row = jnp.where(is_item, item % n, 0).astype(I32)
    dst_row = jnp.where(is_item, item, n_items + slot).astype(I32)
    y = moe_experts(xn, tile_expert, n_used, src_row, dst_row, n_items + n_slots + MOE_TM, w_gate, w_up, w_down, layer)
    g = gates.astype(F32)
    return x + y[:n] * g[0][:, None] + y[n:n_items] * g[1][:, None]


MOE2_TM = 1024
MOE2_GROUP = 64
MOE2_CHUNK = 256
MOE2_GPS = MOE2_TM // MOE_NF // MOE2_GROUP
MOE2_VMEM_LIMIT = V7X_VMEM_BYTES * 7 // 8


def _moe2_kernel(te_ref, nu_ref, nv_ref, base_ref, order_ref, x_hbm, wg_ref, wu_ref, wd_ref, y_hbm,
                 xbuf, xb, ybuf, gsem, ssem):
    n_tok = x_hbm.shape[0]
    n_items = n_tok * TOP_K

    def slot_item(tile, rr):
        return order_ref[jnp.minimum(base_ref[tile] + rr, n_items - 1)]
    i = pl.program_id(0)
    f = pl.program_id(1)
    n_used = nu_ref[0]
    cur = i % 2
    oth = 1 - cur
    n_groups = MOE2_TM // MOE2_GROUP

    def gather_group(tile, g):
        for r in range(MOE2_GROUP):
            rr = g * MOE2_GROUP + r
            item = slot_item(tile, rr)
            row = jnp.where(item >= n_tok, item - n_tok, item)
            pltpu.make_async_copy(x_hbm.at[pl.ds(row, 1), :], xbuf.at[pl.ds(rr, 1), :], gsem).start()

    def scatter_group(tile, buf, g):
        for r in range(MOE2_GROUP):
            rr = g * MOE2_GROUP + r
            row = jnp.where(rr < nv_ref[tile], slot_item(tile, rr), n_items + tile * MOE2_TM + rr)
            pltpu.make_async_copy(ybuf.at[buf, pl.ds(rr, 1), :], y_hbm.at[pl.ds(row, 1), :], ssem.at[buf]).start()

    def wait_gather_group():
        pltpu.make_async_copy(x_hbm.at[pl.ds(0, MOE2_GROUP), :], xbuf.at[pl.ds(0, MOE2_GROUP), :], gsem).wait()

    def wait_scatter_group(buf):
        pltpu.make_async_copy(ybuf.at[buf, pl.ds(0, MOE2_GROUP), :], y_hbm.at[pl.ds(0, MOE2_GROUP), :], ssem.at[buf]).wait()

    @pl.when((i == 0) & (f == 0))
    def _():
        xbuf[...] = jnp.zeros(xbuf.shape, F32)

        def body(g, carry):
            @pl.when(g * MOE2_GROUP < nv_ref[0])
            def _():
                gather_group(0, g)
            return carry
        lax.fori_loop(0, n_groups, body, 0)

    @pl.when(i < n_used)
    def _():
        n_valid = nv_ref[i]

        @pl.when(f == 0)
        def _():
            for g in range(n_groups):
                @pl.when(g * MOE2_GROUP < n_valid)
                def _():
                    wait_gather_group()
            xb[...] = xbuf[...].astype(BF16)
            ybuf[cur] = jnp.zeros(ybuf.shape[1:], F32)

        nv_next = nv_ref[i + 1]
        nv_prev = nv_ref[jnp.maximum(i - 1, 0)]
        for gi in range(MOE2_GPS):
            g = f * MOE2_GPS + gi

            @pl.when((i + 1 < n_used) & (g * MOE2_GROUP < nv_next))
            def _():
                gather_group(i + 1, g)

            @pl.when((i > 0) & (g * MOE2_GROUP < nv_prev))
            def _():
                scatter_group(i - 1, oth, g)

        for c in range(MOE2_TM // MOE2_CHUNK):
            lo, hi = c * MOE2_CHUNK, (c + 1) * MOE2_CHUNK

            @pl.when(lo < n_valid)
            def _():
                x = xb[lo:hi, :]
                a = jnp.dot(x, wg_ref[...].astype(BF16), preferred_element_type=F32)
                u = jnp.dot(x, wu_ref[...].astype(BF16), preferred_element_type=F32)
                h = (a * jax.nn.sigmoid(a) * u).astype(BF16)
                ybuf[cur, lo:hi, :] += jnp.dot(h, wd_ref[...].astype(BF16), preferred_element_type=F32)

        @pl.when(f == MOE_NF - 1)
        def _():
            for g in range(n_groups):
                @pl.when((i > 0) & (g * MOE2_GROUP < nv_prev))
                def _():
                    wait_scatter_group(oth)

    @pl.when((i == n_used) & (f == 0))
    def _():
        nv_last = nv_ref[i - 1]

        def start(g, carry):
            @pl.when(g * MOE2_GROUP < nv_last)
            def _():
                scatter_group(i - 1, oth, g)
            return carry
        lax.fori_loop(0, n_groups, start, 0)

        def wait(g, carry):
            @pl.when(g * MOE2_GROUP < nv_last)
            def _():
                wait_scatter_group(oth)
            return carry
        lax.fori_loop(0, n_groups, wait, 0)


def moe_experts2(xn, tile_expert, n_used, n_valid, base, order, n_out_rows, w_gate, w_up, w_down, layer):
    n, d = xn.shape
    n_tiles = tile_expert.shape[0]

    def tile(i, nu):
        return jnp.minimum(i, nu[0] - 1)

    def fcol(i, f, nu):
        return jnp.where(i < nu[0], f, MOE_NF - 1)

    grid_spec = pltpu.PrefetchScalarGridSpec(
        num_scalar_prefetch=5,
        grid=(n_tiles, MOE_NF),
        in_specs=[
            pl.BlockSpec(memory_space=pl.ANY),
            pl.BlockSpec((None, None, d, MOE_TF), lambda i, f, te, nu, nv, s, t: (layer, te[tile(i, nu)], 0, fcol(i, f, nu))),
            pl.BlockSpec((None, None, d, MOE_TF), lambda i, f, te, nu, nv, s, t: (layer, te[tile(i, nu)], 0, fcol(i, f, nu))),
            pl.BlockSpec((None, None, MOE_TF, d), lambda i, f, te, nu, nv, s, t: (layer, te[tile(i, nu)], fcol(i, f, nu), 0)),
        ],
        out_specs=pl.BlockSpec(memory_space=pl.ANY),
        scratch_shapes=[pltpu.VMEM((MOE2_TM, d), F32),
                        pltpu.VMEM((MOE2_TM, d), BF16),
                        pltpu.VMEM((2, MOE2_TM, d), F32),
                        pltpu.SemaphoreType.DMA(()),
                        pltpu.SemaphoreType.DMA((2,))],
    )
    return pl.pallas_call(
        _moe2_kernel,
        grid_spec=grid_spec,
        out_shape=jax.ShapeDtypeStruct((n_out_rows, d), F32),
        compiler_params=pltpu.CompilerParams(dimension_semantics=("arbitrary", "arbitrary"),
                                             vmem_limit_bytes=MOE2_VMEM_LIMIT),
        name="moe_experts",
    )(tile_expert, n_used, n_valid, base, order, xn, w_gate, w_up, w_down)


def moe_ffn2(xs, xn, experts, gates, w_gate, w_up, w_down, layer):
    n, d = xn.shape
    tm = MOE2_TM
    n_items = n * TOP_K
    n_tiles = -(-n_items // tm) + N_EXPERTS
    n_slots = n_tiles * tm
    flat_e = experts.reshape(-1)
    sizes = jnp.sum((flat_e[:, None] == jnp.arange(N_EXPERTS, dtype=I32)[None, :]).astype(I32), axis=0)
    padded = -(-sizes // tm) * tm
    ends_p = jnp.cumsum(padded)
    starts_p = ends_p - padded
    starts = jnp.cumsum(sizes) - sizes
    order = jnp.argsort(flat_e).astype(I32)
    tile_start = jnp.arange(n_tiles, dtype=I32) * tm
    tile_expert = jnp.minimum(jnp.searchsorted(ends_p, tile_start, side='right'), N_EXPERTS - 1).astype(I32)
    n_used = (ends_p[-1:] // tm).astype(I32)
    n_valid = jnp.clip(sizes[tile_expert] - (tile_start - starts_p[tile_expert]), 0, tm).astype(I32)
    base = (starts[tile_expert] + tile_start - starts_p[tile_expert]).astype(I32)
    y = moe_experts2(xn, tile_expert, n_used, n_valid, base, order, n_items + n_slots,
                     w_gate, w_up, w_down, layer)
    outs, r0 = [], 0
    for x, g in zip(xs, gates):
        r1 = r0 + x.shape[0]
        outs.append(x + y[r0:r1] * g[0][:, None] + y[n + r0:n + r1] * g[1][:, None])
        r0 = r1
    return outs


GROUP_ROWS = 8


def _pick_head(ref, h):
    out = ref[:, 0, :]
    for hh in range(1, NSA_KV_HEADS):
        out = jnp.where(h == hh, ref[:, hh, :], out)
    return out


def _group_rows(q_ref, h):
    rows = [q_ref[:, (h * NSA_GROUP + g) * HEAD_DIM:(h * NSA_GROUP + g + 1) * HEAD_DIM] for g in range(NSA_GROUP)]
    rows.append(jnp.zeros((GROUP_ROWS - NSA_GROUP, HEAD_DIM), F32))
    return jnp.concatenate(rows, axis=0)


def _compress_paged_kernel(pt_ref, cache_hbm, pe_ref, w1_ref, w2_ref, o_ref, xbuf, sem, *, layer, n_pages):
    b = pl.program_id(0)
    n_chunks = n_pages * (PAGE_SIZE // CMP_STRIDE)

    def page_copy(p):
        page = pt_ref[b * n_pages + p]
        return pltpu.make_async_copy(cache_hbm.at[layer, page], xbuf.at[pl.ds(p * PAGE_SIZE, PAGE_SIZE)], sem)

    def start(p, carry):
        page_copy(p).start()
        return carry

    def wait(p, carry):
        page_copy(p).wait()
        return carry

    lax.fori_loop(0, n_pages, start, 0)
    lax.fori_loop(0, n_pages, wait, 0)

    def slab(r, h):
        return xbuf[pl.ds(r, n_chunks, stride=CMP_STRIDE), h, :]
    _compress_body(slab, n_chunks, pe_ref, w1_ref, w2_ref, o_ref)


def compress_paged(cache, layer, page_table, pe, w1, w2):
    b, n_pages = page_table.shape
    n_chunks = n_pages * (PAGE_SIZE // CMP_STRIDE)
    grid_spec = pltpu.PrefetchScalarGridSpec(
        num_scalar_prefetch=1,
        grid=(b,),
        in_specs=[pl.BlockSpec(memory_space=pl.ANY),
                  pl.BlockSpec(pe.shape, lambda bi, pt: (0, 0)),
                  pl.BlockSpec(w1.shape, lambda bi, pt: (0, 0)),
                  pl.BlockSpec(w2.shape, lambda bi, pt: (0, 0))],
        out_specs=pl.BlockSpec((None, NSA_KV_HEADS, n_chunks, HEAD_DIM), lambda bi, pt: (bi, 0, 0, 0)),
        scratch_shapes=[pltpu.VMEM((n_pages * PAGE_SIZE, NSA_KV_HEADS, HEAD_DIM), F32), pltpu.SemaphoreType.DMA(())],
    )
    return pl.pallas_call(
        functools.partial(_compress_paged_kernel, layer=layer, n_pages=n_pages),
        grid_spec=grid_spec,
        out_shape=jax.ShapeDtypeStruct((b, NSA_KV_HEADS, n_chunks, HEAD_DIM), F32),
        compiler_params=_cparams(("arbitrary",)),
        name="compress_paged",
    )(page_table.reshape(-1), cache, pe, w1, w2)


def _cmp_select_sample_kernel(q_ref, kc_ref, vc_ref, cov_ref, o_ref, idx_ref, q8_ref, *, q_pos, n_blk):
    nc = kc_ref.shape[1]
    nb_pad = cov_ref.shape[1]
    blk = lax.broadcasted_iota(I32, (1, nb_pad), 1)
    cur = q_pos // SEL_BLOCK
    forced = (blk == 0) | (blk == cur) | (blk == cur - 1)
    future = (blk * SEL_BLOCK > q_pos) | (blk >= n_blk)
    c_last = lax.broadcasted_iota(I32, (GROUP_ROWS, nc), 1) * CMP_STRIDE + (CMP_BLOCK - 1)
    valid = (c_last <= q_pos) & (lax.broadcasted_iota(I32, (GROUP_ROWS, nc), 0) < NSA_GROUP)
    for h in range(NSA_KV_HEADS):
        q8f = _group_rows(q_ref, h)
        q8_ref[h] = q8f
        q8 = q8f.astype(BF16)
        s = jnp.where(valid, _nt(q8, kc_ref[h].astype(BF16)) * SCALE, NEG)
        m = jnp.max(s, axis=-1, keepdims=True)
        e = jnp.where(valid, jnp.exp(s - m), 0.0)
        p = (e / jnp.maximum(jnp.sum(e, axis=-1, keepdims=True), 1e-30)).astype(BF16)
        o_ref[h] = jnp.dot(p, vc_ref[h].astype(BF16), preferred_element_type=F32)
        imp = jnp.sum(jnp.dot(p, cov_ref[...], preferred_element_type=F32), axis=0, keepdims=True)
        imp = jnp.where(future, -1e9, jnp.where(forced, 1e9, imp))
        jj = lax.broadcasted_iota(I32, (nb_pad, nb_pad), 0)
        ss = lax.broadcasted_iota(I32, (nb_pad, nb_pad), 1)
        row = jnp.broadcast_to(imp, (nb_pad, nb_pad))
        col = jnp.sum(jnp.where(jj == ss, row, 0.0), axis=1, keepdims=True)
        beats = jnp.where(col > row, 1.0, jnp.where(col == row, jnp.where(jj < ss, 1.0, 0.0), 0.0))
        rank = jnp.sum(beats, axis=0, keepdims=True)
        want = lax.broadcasted_iota(I32, (SEL_TOP, nb_pad), 0).astype(F32)
        lane = lax.broadcasted_iota(I32, (SEL_TOP, nb_pad), 1).astype(F32)
        idx = jnp.sum(jnp.where(jnp.broadcast_to(rank, (SEL_TOP, nb_pad)) == want, lane, 0.0), axis=1, keepdims=True)
        idx_ref[h] = idx.astype(I32)


def cmp_select_sample(q, kcmp, vcmp, q_pos):
    b = q.shape[0]
    nc = kcmp.shape[2]
    n_blk = q_pos // SEL_BLOCK + 1
    nb_pad = -(-n_blk // LANES) * LANES
    cov = _cover_t(nc, nb_pad, nc - 1)[:, :nc].T
    cov = jnp.asarray(cov, BF16)
    return pl.pallas_call(
        functools.partial(_cmp_select_sample_kernel, q_pos=q_pos, n_blk=n_blk),
        grid=(b,),
        in_specs=[pl.BlockSpec((None, 1, TOK_W), lambda bi: (bi, 0, 0)),
                  pl.BlockSpec((None, NSA_KV_HEADS, nc, HEAD_DIM), lambda bi: (bi, 0, 0, 0)),
                  pl.BlockSpec((None, NSA_KV_HEADS, nc, HEAD_DIM), lambda bi: (bi, 0, 0, 0)),
                  pl.BlockSpec(cov.shape, lambda bi: (0, 0))],
        out_specs=[pl.BlockSpec((None, NSA_KV_HEADS, GROUP_ROWS, HEAD_DIM), lambda bi: (bi, 0, 0, 0)),
                   pl.BlockSpec((None, NSA_KV_HEADS, SEL_TOP, 1), lambda bi: (bi, 0, 0, 0)),
                   pl.BlockSpec((None, NSA_KV_HEADS, GROUP_ROWS, HEAD_DIM), lambda bi: (bi, 0, 0, 0))],
        out_shape=[jax.ShapeDtypeStruct((b, NSA_KV_HEADS, GROUP_ROWS, HEAD_DIM), F32),
                   jax.ShapeDtypeStruct((b, NSA_KV_HEADS, SEL_TOP, 1), I32),
                   jax.ShapeDtypeStruct((b, NSA_KV_HEADS, GROUP_ROWS, HEAD_DIM), F32)],
        compiler_params=_cparams(("parallel",)),
        name="cmp_select_sample",
    )(q, kcmp, vcmp, cov)


def _dot_new_row(q8, k_row):
    a = q8.astype(BF16).astype(F32)
    b = k_row.astype(BF16).astype(F32)
    return jnp.sum(a * b, axis=-1, keepdims=True)


def _sel_sample_kernel(pt_ref, ix_ref, q_ref, k_ref, v_ref, kn_ref, vn_ref, o_ref, m_ref, l_ref, acc_ref, *, n_cache_blk):
    b, h, r = pl.program_id(0), pl.program_id(1), pl.program_id(2)

    @pl.when(r == 0)
    def _():
        m_ref[...] = jnp.full(m_ref.shape, NEG, F32)
        l_ref[...] = jnp.zeros(l_ref.shape, F32)
        acc_ref[...] = jnp.zeros(acc_ref.shape, F32)

    q8 = q_ref[...]
    in_cache = ix_ref[(b * NSA_KV_HEADS + h) * SEL_TOP + r] < n_cache_blk
    rows_ok = lax.broadcasted_iota(I32, (GROUP_ROWS, SEL_BLOCK), 0) < NSA_GROUP
    valid = rows_ok & in_cache
    k_blk = _pick_head(k_ref, h).astype(BF16)
    v_blk = _pick_head(v_ref, h).astype(BF16)
    s = jnp.where(valid, _nt(q8.astype(BF16), k_blk) * SCALE, NEG)
    m_old = m_ref[...]
    m_new = jnp.maximum(m_old, jnp.max(s, axis=-1, keepdims=True))
    e = jnp.where(valid, jnp.exp(s - m_new), 0.0)
    alpha = jnp.exp(m_old - m_new)
    l_ref[...] = alpha * l_ref[...] + jnp.sum(e, axis=-1, keepdims=True)
    acc_ref[...] = alpha * acc_ref[...] + jnp.dot(e.astype(BF16), v_blk, preferred_element_type=F32)
    m_ref[...] = m_new

    @pl.when(r == SEL_TOP - 1)
    def _():
        s_new = _dot_new_row(q8, kn_ref[...]) * SCALE
        m_old = m_ref[...]
        m_fin = jnp.maximum(m_old, s_new)
        alpha = jnp.exp(m_old - m_fin)
        e_new = jnp.exp(s_new - m_fin)
        l_fin = alpha * l_ref[...] + e_new
        acc = alpha * acc_ref[...] + e_new.astype(BF16).astype(F32) * vn_ref[...].astype(BF16).astype(F32)
        o_ref[...] = acc / jnp.maximum(l_fin, 1e-30)


def sel_attention_sample(q8, idx, cache_k, cache_v, layer, page_table, k_new, v_new):
    b, n_pages = page_table.shape
    per_page = PAGE_SIZE // SEL_BLOCK
    n_cache_blk = n_pages * per_page

    def blk_index(bi, h, r, pt, ix):
        s = jnp.minimum(ix[(bi * NSA_KV_HEADS + h) * SEL_TOP + r], n_cache_blk - 1)
        return (layer, pt[bi * n_pages + s // per_page], s % per_page, 0, 0)

    grid_spec = pltpu.PrefetchScalarGridSpec(
        num_scalar_prefetch=2,
        grid=(b, NSA_KV_HEADS, SEL_TOP),
        in_specs=[pl.BlockSpec((None, None, GROUP_ROWS, HEAD_DIM), lambda bi, h, r, pt, ix: (bi, h, 0, 0)),
                  pl.BlockSpec((None, None, SEL_BLOCK, NSA_KV_HEADS, HEAD_DIM), blk_index),
                  pl.BlockSpec((None, None, SEL_BLOCK, NSA_KV_HEADS, HEAD_DIM), blk_index),
                  pl.BlockSpec((None, 1, HEAD_DIM), lambda bi, h, r, pt, ix: (bi, 0, h)),
                  pl.BlockSpec((None, 1, HEAD_DIM), lambda bi, h, r, pt, ix: (bi, 0, h))],
        out_specs=pl.BlockSpec((None, None, GROUP_ROWS, HEAD_DIM), lambda bi, h, r, pt, ix: (bi, h, 0, 0)),
        scratch_shapes=[pltpu.VMEM((GROUP_ROWS, 1), F32), pltpu.VMEM((GROUP_ROWS, 1), F32),
                        pltpu.VMEM((GROUP_ROWS, HEAD_DIM), F32)],
    )
    return pl.pallas_call(
        functools.partial(_sel_sample_kernel, n_cache_blk=n_cache_blk),
        grid_spec=grid_spec,
        out_shape=jax.ShapeDtypeStruct((b, NSA_KV_HEADS, GROUP_ROWS, HEAD_DIM), F32),
        compiler_params=_cparams(("arbitrary", "arbitrary", "arbitrary")),
        name="sel_attention_sample",
    )(page_table.reshape(-1), idx.reshape(-1), q8, cache_k, cache_v, k_new, v_new)


def _win_sample_kernel(q_ref, k_ref, v_ref, kn_ref, vn_ref, oc_ref, os_ref, gate_ref, o_ref):
    h = pl.program_id(1)
    wb = k_ref.shape[0]
    k_win = _pick_head(k_ref, h).astype(BF16)
    v_win = _pick_head(v_ref, h).astype(BF16)
    q8 = q_ref[...]
    col = lax.broadcasted_iota(I32, (GROUP_ROWS, wb), 1)
    valid = col >= 1
    s = jnp.where(valid, _nt(q8.astype(BF16), k_win) * SCALE, NEG)
    s_new = _dot_new_row(q8, kn_ref[...]) * SCALE
    m = jnp.maximum(jnp.max(s, axis=-1, keepdims=True), s_new)
    e = jnp.where(valid, jnp.exp(s - m), 0.0)
    e_new = jnp.exp(s_new - m)
    den = jnp.maximum(jnp.sum(e, axis=-1, keepdims=True) + e_new, 1e-30)
    p = (e / den).astype(BF16)
    p_new = (e_new / den).astype(BF16).astype(F32)
    o_win = jnp.dot(p, v_win, preferred_element_type=F32) + p_new * vn_ref[...].astype(BF16).astype(F32)
    gate = jnp.broadcast_to(gate_ref[...], (GROUP_ROWS, LANES))
    lane = lax.broadcasted_iota(I32, (GROUP_ROWS, LANES), 1)
    head = h * NSA_GROUP + lax.broadcasted_iota(I32, (GROUP_ROWS, LANES), 0)
    g = [jnp.sum(jnp.where(lane == head * 3 + j, gate, 0.0), axis=-1, keepdims=True) for j in range(3)]
    o_ref[...] = g[0] * oc_ref[...] + g[1] * os_ref[...] + g[2] * o_win


def win_combine_sample(q8, win_k, win_v, layer, k_new, v_new, o_cmp, o_sel, gate):
    b = q8.shape[0]
    wb = win_k.shape[2]
    assert wb == WINDOW
    gspec = pl.BlockSpec((None, None, GROUP_ROWS, HEAD_DIM), lambda bi, h: (bi, h, 0, 0))
    wspec = pl.BlockSpec((None, None, wb, NSA_KV_HEADS, HEAD_DIM), lambda bi, h: (layer, bi, 0, 0, 0))
    nspec = pl.BlockSpec((None, 1, HEAD_DIM), lambda bi, h: (bi, 0, h))
    return pl.pallas_call(
        _win_sample_kernel,
        grid=(b, NSA_KV_HEADS),
        in_specs=[gspec, wspec, wspec, nspec, nspec, gspec, gspec,
                  pl.BlockSpec((None, 1, LANES), lambda bi, h: (bi, 0, 0))],
        out_specs=gspec,
        out_shape=jax.ShapeDtypeStruct((b, NSA_KV_HEADS, GROUP_ROWS, HEAD_DIM), F32),
        compiler_params=_cparams(("parallel", "arbitrary")),
        name="win_combine_sample",
    )(q8, win_k, win_v, k_new, v_new, o_cmp, o_sel, gate)


def nsa_sample_step(prep, layer, caches, win_k, win_v, page_table, pe, w1, w2):
    q, kc, vc, ks, vs, kw, vw, gate = prep
    b = q.shape[0]
    past_len = page_table.shape[1] * PAGE_SIZE
    cache_ck, cache_cv, cache_sk, cache_sv = caches
    kcmp = compress_paged(cache_ck, layer, page_table, pe[0], w1[0], w2[0])
    vcmp = compress_paged(cache_cv, layer, page_table, pe[1], w1[1], w2[1])
    o_cmp, idx, q8 = cmp_select_sample(q, kcmp, vcmp, past_len)
    o_sel = sel_attention_sample(q8, idx, cache_sk, cache_sv, layer, page_table, ks, vs)
    o = win_combine_sample(q8, win_k, win_v, layer, kw, vw, o_cmp, o_sel, gate)
    return o[:, :, :NSA_GROUP].reshape(b, 1, TOK_W)


def rms_norm(x, g):
    xf = x.astype(F32)
    y = xf * lax.rsqrt(jnp.mean(xf * xf, axis=-1, keepdims=True) + EPS)
    return (y * g.astype(F32)).astype(x.dtype)


def masked_softmax(s, valid):
    s = jnp.where(valid, s, NEG)
    m = jnp.max(s, axis=-1, keepdims=True)
    e = jnp.where(valid, jnp.exp(s - m), 0.0)
    return e / jnp.maximum(jnp.sum(e, axis=-1, keepdims=True), 1e-30)


def compress(k, pe, w1, w2):
    b, l = k.shape[:2]
    r = CMP_BLOCK // CMP_STRIDE
    n_chunks = l // CMP_STRIDE
    nc = n_chunks - r + 1
    c = k[:, :n_chunks * CMP_STRIDE].reshape(b, n_chunks, CMP_STRIDE, NSA_KV_HEADS, HEAD_DIM)
    blk = jnp.concatenate([c[:, i:i + nc] for i in range(r)], axis=2)
    blk = blk + pe[None, None, :, None, :]
    flat = blk.transpose(0, 1, 3, 2, 4).reshape(b, nc, NSA_KV_HEADS, CMP_BLOCK * HEAD_DIM)
    hid = jax.nn.gelu(jnp.einsum('bnhf,fe->bnhe', flat, w1))
    return jnp.einsum('bnhe,ed->bnhd', hid, w2)


def cmp_attend(q, q_pos, kc, vc):
    nc = kc.shape[1]
    c_last = jnp.arange(nc, dtype=I32) * CMP_STRIDE + CMP_BLOCK - 1
    s = jnp.einsum('bqhgd,bnhd->bqhgn', q, kc).astype(F32) * SCALE
    valid = (c_last[None, :] <= q_pos[:, None])[None, :, None, None, :]
    p = masked_softmax(s, valid)
    o = jnp.einsum('bqhgn,bnhd->bqhgd', p.astype(vc.dtype), vc)
    return o, p


def select_blocks(p, q_pos, n_sel):
    nc = p.shape[-1]
    c_start = jnp.arange(nc, dtype=I32) * CMP_STRIDE
    c_last = c_start + CMP_BLOCK - 1
    s_start = jnp.arange(n_sel, dtype=I32) * SEL_BLOCK
    cover = ((c_start[:, None] < s_start[None, :] + SEL_BLOCK) & (c_last[:, None] >= s_start[None, :])).astype(F32)
    imp = jnp.einsum('bqhgn,ns->bqhs', p, cover)
    blk = jnp.arange(n_sel, dtype=I32)[None, :]
    cur = (q_pos // SEL_BLOCK)[:, None]
    forced = (blk == 0) | (blk == cur) | (blk == cur - 1)
    future = blk * SEL_BLOCK > q_pos[:, None]
    imp = jnp.where(future[None, :, None, :], -1e9, jnp.where(forced[None, :, None, :], 1e9, imp))
    _, idx = lax.top_k(imp, min(SEL_TOP, n_sel))
    return idx


def to_blocks(k):
    b, l = k.shape[:2]
    ns = -(-l // SEL_BLOCK)
    k = jnp.pad(k, ((0, 0), (0, ns * SEL_BLOCK - l), (0, 0), (0, 0)))
    return k.reshape(b, ns, SEL_BLOCK, NSA_KV_HEADS, HEAD_DIM).transpose(0, 3, 1, 2, 4)


def sel_attend(q, q_pos, idx, kb, vb):
    idx_t = idx.transpose(0, 2, 1, 3)
    take = jax.vmap(jax.vmap(lambda blocks, i: blocks[i]))
    kg = take(kb, idx_t)
    vg = take(vb, idx_t)
    kpos = idx_t[..., None] * SEL_BLOCK + jnp.arange(SEL_BLOCK, dtype=I32)
    valid = (kpos <= q_pos[None, None, :, None, None]).transpose(0, 2, 1, 3, 4)[:, :, :, None]
    s = jnp.einsum('bqhgd,bhqkld->bqhgkl', q, kg).astype(F32) * SCALE
    b, tq, h, g, kk, lb = s.shape
    p = masked_softmax(s.reshape(b, tq, h, g, kk * lb), valid.reshape(b, tq, h, 1, kk * lb)).reshape(s.shape)
    return jnp.einsum('bqhgkl,bhqkld->bqhgd', p.astype(vg.dtype), vg)


def window_attend(q, q_pos, k, v, k_pos):
    s = jnp.einsum('bqhgd,bkhd->bqhgk', q, k).astype(F32) * SCALE
    diff = q_pos[:, None] - k_pos[None, :]
    valid = ((diff >= 0) & (diff < WINDOW) & (k_pos[None, :] >= 0))[None, :, None, None, :]
    p = masked_softmax(s, valid)
    return jnp.einsum('bqhgk,bkhd->bqhgd', p.astype(v.dtype), v)


def gather_pages(cache, page_table):
    c = cache[page_table]
    return c.reshape(page_table.shape[0], page_table.shape[1] * cache.shape[1], *cache.shape[2:])


def nsa_sample(prep, pos, past, win_k, win_v, page_table, past_len, pe, w1, w2):
    q, kc, vc, ks, vs, kw, vw, gate = prep
    b, t = q.shape[:2]
    kvr = lambda a: a.reshape(b, t, NSA_KV_HEADS, HEAD_DIM)
    q = q.reshape(b, t, NSA_KV_HEADS, NSA_GROUP, HEAD_DIM)
    kc, vc, ks, vs, kw, vw = (kvr(a) for a in (kc, vc, ks, vs, kw, vw))
    gate = gate[..., :3 * NSA_HEADS].reshape(b, t, NSA_KV_HEADS, NSA_GROUP, 3)
    full = [jnp.concatenate([gather_pages(c, page_table).astype(new.dtype), new], axis=1)
            for c, new in zip(past, (kc, vc, ks, vs))]
    kcmp = compress(full[0], pe[0], w1[0], w2[0])
    vcmp = compress(full[1], pe[1], w1[1], w2[1])
    o_cmp, p_cmp = cmp_attend(q, pos, kcmp, vcmp)
    ksb, vsb = to_blocks(full[2]), to_blocks(full[3])
    idx = select_blocks(p_cmp, pos, ksb.shape[2])
    o_sel = sel_attend(q, pos, idx, ksb, vsb)
    wb = win_k.shape[1]
    kwin = jnp.concatenate([win_k.astype(kw.dtype), kw], axis=1)
    vwin = jnp.concatenate([win_v.astype(vw.dtype), vw], axis=1)
    kpos = past_len - wb + jnp.arange(wb + t, dtype=I32)
    o_win = window_attend(q, pos, kwin, vwin, kpos)
    o = (gate[..., 0, None] * o_cmp + gate[..., 1, None] * o_sel + gate[..., 2, None] * o_win).reshape(b, t, TOK_W)
    return o, (kc, vc, ks, vs), (kwin[:, -wb:], vwin[:, -wb:])


def _reorder_nsa_weight(w):
    n_gl = 3 * NSA_HEADS
    parts = [w[:, :NSA_MQ0], w[:, NSA_MQ0 + n_gl:NSA_MQ0 + n_gl + MEM_W], w[:, NSA_MQ0:NSA_MQ0 + n_gl]]
    wr = jnp.concatenate(parts, axis=1)
    return jnp.pad(wr, ((0, 0), (0, NSA_ZW - wr.shape[1]))).astype(BF16)


def kernel(x_prompt, x_sample, state_pool, cache_cmp_k, cache_cmp_v, cache_sel_k, cache_sel_v, state_win_k, state_win_v, cache_mem_k, cache_mem_v, page_table, mem_prompt, norm_mix_g, norm_ffn_g, norm_mem_g, w_mem_kv, mem_q_norm_g, mem_k_norm_g, w_in_pool, w_pool_grp, pool_scale, w_out_pool, w_in_nsa, b_gate, nsa_q_norm_g, nsa_k_norm_g, cmp_pe, cmp_w1, cmp_w2, w_out_nsa, w_router, b_router, w_gate, w_up, w_down):
    bp, t_p, d = x_prompt.shape
    bs, t_s, _ = x_sample.shape
    assert t_s == 1
    n_p, n_s = bp * t_p, bs * t_s
    m_len = mem_prompt.shape[1]
    past_len = page_table.shape[1] * PAGE_SIZE
    pos_p = jnp.arange(t_p, dtype=I32)
    pos_s = past_len + jnp.arange(t_s, dtype=I32)
    xp = x_prompt.reshape(n_p, d)
    xs = x_sample.reshape(n_s, d)
    mem_flat = mem_prompt.reshape(-1, d)
    pool_p, pool_s, rows_p, rows_s, win_p, win_s, mem_k_p, mem_v_p = [], [], [], [], [], [], [], []
    for i in range(DEPTH):
        li = i // 2
        kv = proj(mem_flat, w_mem_kv[i].astype(BF16), gain=norm_mem_g[i])
        mk = head_norm(kv, 0, MEM_W, mem_k_norm_g[i]).reshape(bp, m_len, MEM_W)
        mv = kv[:, MEM_W:].reshape(bp, m_len, MEM_W)
        mem_k_p.append(mk.reshape(bp, m_len, MEM_HEADS, HEAD_DIM))
        mem_v_p.append(mv.reshape(bp, m_len, MEM_HEADS, HEAD_DIM))
        mk_s = cache_mem_k[i].reshape(bs, m_len, MEM_W)
        mv_s = cache_mem_v[i].reshape(bs, m_len, MEM_W)
        if i % 2 == 0:
            w_in = w_in_pool[li].astype(BF16)
            zp = proj(xp, w_in, gain=norm_mix_g[i]).reshape(bp, t_p, -1)
            zs = proj(xs, w_in, gain=norm_mix_g[i]).reshape(bs, t_s, -1)
            op = pool_mix(zp, None, w_pool_grp[li], pool_scale[li], 0)
            zs16 = jnp.pad(zs, ((0, 0), (0, POOL_HALO - t_s), (0, 0)))
            halo = jnp.pad(state_pool[li], ((0, 0), (1, 0), (0, 0)))
            os_ = pool_mix(zs16, halo, w_pool_grp[li], pool_scale[li], past_len)[:, :t_s]
            pool_p.append(zp[:, t_p - POOL_STATE:, :TOK_W])
            pool_s.append(jnp.concatenate([state_pool[li], zs[..., :TOK_W]], axis=1)[:, -POOL_STATE:])
            mq0 = TOK_W
            w_out = w_out_pool[li]
        else:
            w_in = _reorder_nsa_weight(w_in_nsa[li])
            zp = proj(xp, w_in, gain=norm_mix_g[i]).reshape(bp, t_p, -1)
            zs = proj(xs, w_in, gain=norm_mix_g[i]).reshape(bs, t_s, -1)
            q, kc, vc, ks, vs, kw, vw, gate = nsa_prep(zp, pos_p, nsa_q_norm_g[li], nsa_k_norm_g[li], b_gate[li])
            kcmp = compress_rows(kc, cmp_pe[li, 0], cmp_w1[li, 0], cmp_w2[li, 0])
            vcmp = compress_rows(vc, cmp_pe[li, 1], cmp_w1[li, 1], cmp_w2[li, 1])
            o_cmp, sel_t = cmp_select_prompt(q, kcmp, vcmp)
            o_sel = sel_attention_prompt_t(q, ks, vs, sel_t)
            op = win_combine_prompt(q, kw, vw, o_cmp, o_sel, gate)
            kvr = lambda a: a.reshape(a.shape[0], a.shape[1], NSA_KV_HEADS, HEAD_DIM)
            rows_p.append(tuple(kvr(a) for a in (kc, vc, ks, vs)))
            wb = min(WINDOW, t_p)
            win_p.append((kvr(kw[:, t_p - wb:]), kvr(vw[:, t_p - wb:])))
            prep_s = nsa_prep(zs, pos_s, nsa_q_norm_g[li], nsa_k_norm_g[li], b_gate[li])
            os_ = nsa_sample_step(prep_s, li, (cache_cmp_k, cache_cmp_v, cache_sel_k, cache_sel_v),
                                  state_win_k, state_win_v, page_table, cmp_pe[li], cmp_w1[li], cmp_w2[li])
            rows_s.append(tuple(kvr(a) for a in prep_s[1:5]))
            win_s.append(tuple(jnp.concatenate([st[li][:, t_s:], kvr(new)], axis=1)
                               for st, new in ((state_win_k, prep_s[5]), (state_win_v, prep_s[6]))))
            mq0 = NSA_MQ0
            w_out = w_out_nsa[li]
        ap = mem_attention(zp, mq0, mk, mv, mem_q_norm_g[i])
        as_ = mem_attention(zs, mq0, mk_s, mv_s, mem_q_norm_g[i])
        w_out = w_out.astype(BF16)
        xp = proj([op.reshape(n_p, TOK_W), ap.reshape(n_p, MEM_W)], w_out, residual=xp)
        xs = proj([os_.reshape(n_s, TOK_W), as_.reshape(n_s, MEM_W)], w_out, residual=xs)
        xn_p, e_p, g_p = router(xp, norm_ffn_g[i], w_router, b_router)
        xs_pad = jnp.pad(xs, ((0, LANES - n_s), (0, 0)))
        xn_s, e_s, g_s = router(xs_pad, norm_ffn_g[i], w_router, b_router)
        xp, xs = moe_ffn2([xp, xs],
                          jnp.concatenate([xn_p, xn_s[:n_s]], axis=0),
                          jnp.concatenate([e_p, e_s[:, :n_s]], axis=1),
                          [g_p, g_s[:, :n_s]],
                          w_gate, w_up, w_down, i)
    stk = lambda lst, j: jnp.stack([r[j] for r in lst])
    return (xp.reshape(bp, t_p, d), xs.reshape(bs, t_s, d),
            jnp.stack(pool_p), jnp.stack(pool_s),
            stk(rows_p, 0), stk(rows_p, 1), stk(rows_p, 2), stk(rows_p, 3),
            stk(rows_s, 0), stk(rows_s, 1), stk(rows_s, 2), stk(rows_s, 3),
            stk(win_p, 0), stk(win_p, 1), stk(win_s, 0), stk(win_s, 1),
            jnp.stack(mem_k_p), jnp.stack(mem_v_p))
```

```python
import functools

import jax
import jax.numpy as jnp
import numpy as np
from jax import lax
from jax.experimental import pallas as pl
from jax.experimental.pallas import tpu as pltpu

D_MODEL = 2048
DEPTH = 4
PAGE_SIZE = 128
HEAD_DIM = 128
ROPE_THETA = 500000.0
ROPE_DIM = HEAD_DIM // 4
MEM_HEADS = 4
MEM_W = MEM_HEADS * HEAD_DIM
TOK_W = D_MODEL - MEM_W
POOL_WINDOWS = (2, 4, 8, 16)
POOL_GROUP = TOK_W // len(POOL_WINDOWS)
POOL_STATE = max(POOL_WINDOWS) - 1
POOL_HALO = POOL_STATE + 1
NSA_HEADS = TOK_W // HEAD_DIM
NSA_KV_HEADS = 2
NSA_GROUP = NSA_HEADS // NSA_KV_HEADS
KV_W = NSA_KV_HEADS * HEAD_DIM
GROUP_W = NSA_GROUP * HEAD_DIM
CMP_BLOCK = 32
CMP_STRIDE = 16
SEL_BLOCK = 64
SEL_TOP = 16
WINDOW = 512
NSA_TOK_IN = NSA_HEADS * HEAD_DIM + 6 * KV_W + 3 * NSA_HEADS
N_EXPERTS = 16
N_GROUPS = 4
EXP_PER_GROUP = N_EXPERTS // N_GROUPS
TOP_K = 2
D_FF = 1024
EPS = 1e-6
NEG = -1e30
SCALE = HEAD_DIM ** -0.5
LANES = 128

V7X_VMEM_BYTES = 64 * 1024 * 1024
VMEM_LIMIT = V7X_VMEM_BYTES * 3 // 4

BF16 = jnp.bfloat16
F32 = jnp.float32
I32 = jnp.int32

NSA_Q0 = 0
NSA_KV0 = NSA_HEADS * HEAD_DIM
NSA_MQ0 = NSA_KV0 + 6 * KV_W
NSA_GL0 = NSA_MQ0 + MEM_W
NSA_ZW = 3840


def _cparams(sem):
    return pltpu.CompilerParams(dimension_semantics=sem, vmem_limit_bytes=VMEM_LIMIT)


def _nt(a, b):
    return lax.dot_general(a, b, (((1,), (1,)), ((), ())), preferred_element_type=F32)


def _tn(a, b):
    return lax.dot_general(a, b, (((0,), (0,)), ((), ())), preferred_element_type=F32)


def _rms(x, g):
    return x * lax.rsqrt(jnp.mean(x * x, axis=-1, keepdims=True) + EPS) * g


def _pick_tile(n, cap, unit):
    if n <= cap:
        return n
    best = None
    for t in range(unit, cap + 1, unit):
        if n % t == 0:
            best = t
    assert best is not None, (n, cap, unit)
    return best


def _proj_kernel(*refs, n_x, norm, residual):
    x_refs, refs = refs[:n_x], refs[n_x:]
    if norm:
        g_ref, refs = refs[0], refs[1:]
    w_ref, refs = refs[0], refs[1:]
    if residual:
        r_ref, refs = refs[0], refs[1:]
    o_ref, xn_ref = refs

    @pl.when(pl.program_id(1) == 0)
    def _():
        off = 0
        for x_ref in x_refs:
            x = x_ref[...].astype(F32)
            if norm:
                x = _rms(x, g_ref[...])
            xn_ref[:, off:off + x.shape[1]] = x.astype(BF16)
            off += x.shape[1]

    y = jnp.dot(xn_ref[...], w_ref[...], preferred_element_type=F32)
    if residual:
        y = y + r_ref[...]
    o_ref[...] = y


def proj(xs, w_bf16, gain=None, residual=None):
    if not isinstance(xs, (list, tuple)):
        xs = [xs]
    assert gain is None or len(xs) == 1
    m = xs[0].shape[0]
    k = sum(x.shape[1] for x in xs)
    n = w_bf16.shape[1]
    tm = _pick_tile(m, 1024, 8)
    tn = _pick_tile(n, 1024, LANES)
    in_specs = [pl.BlockSpec((tm, x.shape[1]), lambda i, j: (i, 0)) for x in xs]
    args = list(xs)
    if gain is not None:
        in_specs.append(pl.BlockSpec((1, k), lambda i, j: (0, 0)))
        args.append(gain.reshape(1, k).astype(F32))
    in_specs.append(pl.BlockSpec((k, tn), lambda i, j: (0, j)))
    args.append(w_bf16)
    if residual is not None:
        in_specs.append(pl.BlockSpec((tm, tn), lambda i, j: (i, j)))
        args.append(residual)
    return pl.pallas_call(
        functools.partial(_proj_kernel, n_x=len(xs), norm=gain is not None, residual=residual is not None),
        grid=(m // tm, n // tn),
        in_specs=in_specs,
        out_specs=pl.BlockSpec((tm, tn), lambda i, j: (i, j)),
        out_shape=jax.ShapeDtypeStruct((m, n), F32),
        scratch_shapes=[pltpu.VMEM((tm, k), BF16)],
        compiler_params=_cparams(("parallel", "arbitrary")),
        name="proj",
    )(*args)


def _head_norm_kernel(x_ref, g_ref, o_ref):
    x = x_ref[...]
    g = g_ref[...]
    n_heads = x.shape[1] // HEAD_DIM
    o_ref[...] = jnp.concatenate(
        [_rms(x[:, h * HEAD_DIM:(h + 1) * HEAD_DIM], g) for h in range(n_heads)], axis=1)


def head_norm(x, col0, width, gain):
    m = x.shape[0]
    assert col0 % width == 0
    return pl.pallas_call(
        _head_norm_kernel,
        grid=(1,),
        in_specs=[pl.BlockSpec((m, width), lambda i: (0, col0 // width)),
                  pl.BlockSpec((1, HEAD_DIM), lambda i: (0, 0))],
        out_specs=pl.BlockSpec((m, width), lambda i: (0, 0)),
        out_shape=jax.ShapeDtypeStruct((m, width), F32),
        compiler_params=_cparams(("arbitrary",)),
        name="head_norm",
    )(x, gain.reshape(1, HEAD_DIM))


def _mem_attn_kernel(q_ref, k_ref, v_ref, g_ref, o_ref):
    q = q_ref[...]
    rows = q.shape[0]
    if rows < 8:
        q = jnp.broadcast_to(q[0:1], (8, q.shape[1]))
    g = g_ref[...]
    outs = []
    for h in range(MEM_HEADS):
        sl = slice(h * HEAD_DIM, (h + 1) * HEAD_DIM)
        qh = _rms(q[:, sl], g).astype(BF16)
        s = _nt(qh, k_ref[:, sl].astype(BF16)) * SCALE
        m = jnp.max(s, axis=-1, keepdims=True)
        e = jnp.exp(s - m)
        p = e / jnp.sum(e, axis=-1, keepdims=True)
        outs.append(jnp.dot(p.astype(BF16), v_ref[:, sl].astype(BF16), preferred_element_type=F32))
    o = jnp.concatenate(outs, axis=1)
    o_ref[...] = o[:rows]


def mem_attention(z, col0, k, v, gain):
    b, t, _ = z.shape
    m = k.shape[1]
    tq = _pick_tile(t, 512, 8)
    assert col0 % MEM_W == 0
    return pl.pallas_call(
        _mem_attn_kernel,
        grid=(b, t // tq),
        in_specs=[pl.BlockSpec((None, tq, MEM_W), lambda bi, i: (bi, i, col0 // MEM_W)),
                  pl.BlockSpec((None, m, MEM_W), lambda bi, i: (bi, 0, 0)),
                  pl.BlockSpec((None, m, MEM_W), lambda bi, i: (bi, 0, 0)),
                  pl.BlockSpec((1, HEAD_DIM), lambda bi, i: (0, 0))],
        out_specs=pl.BlockSpec((None, tq, MEM_W), lambda bi, i: (bi, i, 0)),
        out_shape=jax.ShapeDtypeStruct((b, t, MEM_W), F32),
        compiler_params=_cparams(("parallel", "arbitrary")),
        name="mem_attention",
    )(z, k, v, gain.reshape(1, HEAD_DIM))


def _pool_kernel(u_ref, halo_ref, w_ref, sc_ref, o_ref, *, pos0, zero_first_halo):
    tq = u_ref.shape[0]
    qi = pl.program_id(1)
    u = u_ref[...]
    halo = halo_ref[...]
    if zero_first_halo:
        halo = jnp.where(qi > 0, halo, 0.0)
    pos = pos0 + qi * tq + lax.broadcasted_iota(I32, (tq, 1), 0)
    outs = []
    for g, w in enumerate(POOL_WINDOWS):
        cs = slice(g * POOL_GROUP, (g + 1) * POOL_GROUP)
        ug = u[:, cs]
        acc = jnp.concatenate([halo[:, cs], ug], axis=0)
        span = 1
        while span < w:
            acc = acc[span:] + acc[:-span]
            span *= 2
        ssum = acc[POOL_HALO - (w - 1):POOL_HALO - (w - 1) + tq]
        cnt = jnp.minimum(w, pos + 1).astype(F32)
        d = (ssum / cnt - ug).astype(BF16)
        y = jnp.dot(d, w_ref[g].astype(BF16), preferred_element_type=F32)
        outs.append(y)
    o_ref[...] = jnp.concatenate(outs, axis=1) * sc_ref[...]


def pool_mix(z, halo, w_grp, scale, pos0):
    b, t, _ = z.shape
    tq = _pick_tile(t, 256, 16)
    if halo is None:
        halo_arr = z
        halo_spec = pl.BlockSpec((None, POOL_HALO, TOK_W),
                                 lambda bi, i: (bi, jnp.maximum(i * (tq // POOL_HALO) - 1, 0), 0))
    else:
        assert t == tq
        halo_arr = halo
        halo_spec = pl.BlockSpec((None, POOL_HALO, TOK_W), lambda bi, i: (bi, 0, 0))
    return pl.pallas_call(
        functools.partial(_pool_kernel, pos0=pos0, zero_first_halo=halo is None),
        grid=(b, t // tq),
        in_specs=[pl.BlockSpec((None, tq, TOK_W), lambda bi, i: (bi, i, 0)),
                  halo_spec,
                  pl.BlockSpec(w_grp.shape, lambda bi, i: (0, 0, 0)),
                  pl.BlockSpec((1, TOK_W), lambda bi, i: (0, 0))],
        out_specs=pl.BlockSpec((None, tq, TOK_W), lambda bi, i: (bi, i, 0)),
        out_shape=jax.ShapeDtypeStruct((b, t, TOK_W), F32),
        compiler_params=_cparams(("parallel", "arbitrary")),
        name="pool_mix",
    )(z, halo_arr, w_grp, scale.reshape(1, TOK_W))


def _rope_tables(pos):
    half = ROPE_DIM // 2
    inv = 1.0 / (ROPE_THETA ** (jnp.arange(half, dtype=F32) * 2.0 / ROPE_DIM))
    ang = pos.astype(F32)[:, None] * inv[None, :]
    cos, sin = jnp.cos(ang), jnp.sin(ang)
    t = pos.shape[0]
    rest = HEAD_DIM - ROPE_DIM
    c = jnp.concatenate([cos, cos, jnp.ones((t, rest), F32)], axis=1)
    s_lo = jnp.concatenate([-sin, jnp.zeros((t, HEAD_DIM - half), F32)], axis=1)
    s_hi = jnp.concatenate([jnp.zeros((t, half), F32), sin, jnp.zeros((t, rest), F32)], axis=1)
    return c, s_lo, s_hi


def _nsa_prep_kernel(z_ref, c_ref, slo_ref, shi_ref, qg_ref, kg_ref, bg_ref,
                     q_ref, kc_ref, vc_ref, ks_ref, vs_ref, kw_ref, vw_ref, gate_ref):
    rows = z_ref.shape[0]
    c, slo, shi = c_ref[...], slo_ref[...], shi_ref[...]
    half = ROPE_DIM // 2

    def rope_norm(x, g):
        if rows < 8:
            x = jnp.broadcast_to(x[0:1], (8, HEAD_DIM))
        x = _rms(x, g)
        y = x * c + pltpu.roll(x, HEAD_DIM - half, 1) * slo + pltpu.roll(x, half, 1) * shi
        return y[:rows]

    qg = qg_ref[...]
    q_ref[...] = jnp.concatenate(
        [rope_norm(z_ref[:, NSA_Q0 + h * HEAD_DIM:NSA_Q0 + (h + 1) * HEAD_DIM], qg) for h in range(NSA_HEADS)], axis=1)
    for j, (o_ref, which) in enumerate(((kc_ref, 0), (vc_ref, None), (ks_ref, 1), (vs_ref, None), (kw_ref, 2), (vw_ref, None))):
        c0 = NSA_KV0 + j * KV_W
        if which is None:
            o_ref[...] = z_ref[:, c0:c0 + KV_W]
        else:
            g = kg_ref[which:which + 1, :]
            o_ref[...] = jnp.concatenate(
                [rope_norm(z_ref[:, c0 + h * HEAD_DIM:c0 + (h + 1) * HEAD_DIM], g) for h in range(NSA_KV_HEADS)], axis=1)
    gate_ref[...] = jax.nn.sigmoid(z_ref[:, NSA_GL0:NSA_GL0 + LANES] + bg_ref[...])


def nsa_prep(z, pos, q_g, k_g, b_gate):
    b, t, _ = z.shape
    tq = _pick_tile(t, 256, 8)
    c, slo, shi = _rope_tables(pos)
    if t < 8:
        c, slo, shi = (jnp.broadcast_to(a, (8, HEAD_DIM)) for a in (c, slo, shi))
    tt = max(tq, 8)
    bg = jnp.pad(b_gate.reshape(1, -1), ((0, 0), (0, LANES - b_gate.shape[-1])))
    tab_spec = pl.BlockSpec((tt, HEAD_DIM), lambda bi, i: (i, 0))
    kv_spec = pl.BlockSpec((None, tq, KV_W), lambda bi, i: (bi, i, 0))
    kv_shape = jax.ShapeDtypeStruct((b, t, KV_W), F32)
    return pl.pallas_call(
        _nsa_prep_kernel,
        grid=(b, t // tq),
        in_specs=[pl.BlockSpec((None, tq, NSA_ZW), lambda bi, i: (bi, i, 0)),
                  tab_spec, tab_spec, tab_spec,
                  pl.BlockSpec((1, HEAD_DIM), lambda bi, i: (0, 0)),
                  pl.BlockSpec((3, HEAD_DIM), lambda bi, i: (0, 0)),
                  pl.BlockSpec((1, LANES), lambda bi, i: (0, 0))],
        out_specs=[pl.BlockSpec((None, tq, TOK_W), lambda bi, i: (bi, i, 0))] + [kv_spec] * 6
                  + [pl.BlockSpec((None, tq, LANES), lambda bi, i: (bi, i, 0))],
        out_shape=[jax.ShapeDtypeStruct((b, t, TOK_W), F32)] + [kv_shape] * 6
                  + [jax.ShapeDtypeStruct((b, t, LANES), F32)],
        compiler_params=_cparams(("parallel", "arbitrary")),
        name="nsa_prep",
    )(z, c, slo, shi, q_g.reshape(1, HEAD_DIM), k_g, bg)


def _compress_body(slab, n_chunks, pe_ref, w1_ref, w2_ref, o_ref):
    pe = pe_ref[...]
    pe_lo = jnp.concatenate([pe[r:r + 1] for r in range(CMP_STRIDE)], axis=1)
    pe_hi = jnp.concatenate([pe[CMP_STRIDE + r:CMP_STRIDE + r + 1] for r in range(CMP_STRIDE)], axis=1)
    half_k = CMP_STRIDE * HEAD_DIM
    w_lo = w1_ref[0:half_k, :].astype(BF16)
    w_hi = w1_ref[half_k:2 * half_k, :].astype(BF16)
    w2 = w2_ref[...].astype(BF16)
    for h in range(NSA_KV_HEADS):
        xh = jnp.concatenate([slab(r, h) for r in range(CMP_STRIDE)], axis=1)
        a = jnp.dot((xh + pe_lo).astype(BF16), w_lo, preferred_element_type=F32)
        bb = jnp.dot((xh + pe_hi).astype(BF16), w_hi, preferred_element_type=F32)
        hid = jax.nn.gelu(a + pltpu.roll(bb, n_chunks - 1, 0))
        o_ref[h] = jnp.dot(hid.astype(BF16), w2, preferred_element_type=F32)


def _compress_kernel(x_ref, pe_ref, w1_ref, w2_ref, o_ref):
    def slab(r, h):
        c0 = (r * NSA_KV_HEADS + h) * HEAD_DIM
        return x_ref[:, c0:c0 + HEAD_DIM]
    _compress_body(slab, x_ref.shape[0], pe_ref, w1_ref, w2_ref, o_ref)


def compress_rows(x, pe, w1, w2):
    b, t, _ = x.shape
    n_chunks = t // CMP_STRIDE
    cw = CMP_STRIDE * KV_W
    xc = x.reshape(b, n_chunks, cw)
    return pl.pallas_call(
        _compress_kernel,
        grid=(b,),
        in_specs=[pl.BlockSpec((None, n_chunks, cw), lambda bi: (bi, 0, 0)),
                  pl.BlockSpec(pe.shape, lambda bi: (0, 0)),
                  pl.BlockSpec(w1.shape, lambda bi: (0, 0)),
                  pl.BlockSpec(w2.shape, lambda bi: (0, 0))],
        out_specs=pl.BlockSpec((None, NSA_KV_HEADS, n_chunks, HEAD_DIM), lambda bi: (bi, 0, 0, 0)),
        out_shape=jax.ShapeDtypeStruct((b, NSA_KV_HEADS, n_chunks, HEAD_DIM), F32),
        compiler_params=_cparams(("parallel",)),
        name="compress_rows",
    )(xc, pe, w1, w2)


def _stack_heads(q):
    return jnp.concatenate([q[:, g * HEAD_DIM:(g + 1) * HEAD_DIM] for g in range(NSA_GROUP)], axis=0)


def _unstack_heads(o, tq):
    return jnp.concatenate([o[g * tq:(g + 1) * tq] for g in range(NSA_GROUP)], axis=1)


def _rank_select(imp_t, n_real):
    n_blk = imp_t.shape[0]
    blk = lax.broadcasted_iota(I32, imp_t.shape, 0)
    cnt = jnp.zeros(imp_t.shape, F32)
    for j in range(n_real):
        row = imp_t[j:j + 1, :]
        beats = jnp.where(row > imp_t, 1.0, jnp.where(row == imp_t, jnp.where(blk > j, 1.0, 0.0), 0.0))
        cnt = cnt + beats
    return jnp.where(cnt < SEL_TOP, 1.0, 0.0)


def _cmp_select_kernel(q_ref, kc_ref, vc_ref, cov_ref, o_ref, sel_ref):
    tq = q_ref.shape[0]
    nc = kc_ref.shape[0]
    n_blk = sel_ref.shape[0]
    qi = pl.program_id(2)
    q6 = _stack_heads(q_ref[...]).astype(BF16)
    s = _nt(q6, kc_ref[...].astype(BF16)) * SCALE
    s = s.reshape(NSA_GROUP, tq, nc)
    q_pos = qi * tq + lax.broadcasted_iota(I32, (tq, nc), 0)
    c_last = lax.broadcasted_iota(I32, (tq, nc), 1) * CMP_STRIDE + (CMP_BLOCK - 1)
    valid = (c_last <= q_pos)[None]
    s = jnp.where(valid, s, NEG)
    m = jnp.max(s, axis=-1, keepdims=True)
    e = jnp.where(valid, jnp.exp(s - m), 0.0)
    p = (e / jnp.maximum(jnp.sum(e, axis=-1, keepdims=True), 1e-30)).astype(BF16)
    o = jnp.dot(p.reshape(NSA_GROUP * tq, nc), vc_ref[...].astype(BF16), preferred_element_type=F32)
    o_ref[...] = _unstack_heads(o, tq)
    p_cat = jnp.concatenate([p[g] for g in range(NSA_GROUP)], axis=1)
    imp_t = _nt(cov_ref[...], p_cat)
    blk = lax.broadcasted_iota(I32, (n_blk, tq), 0)
    pos_t = qi * tq + lax.broadcasted_iota(I32, (n_blk, tq), 1)
    cur = pos_t // SEL_BLOCK
    forced = (blk == 0) | (blk == cur) | (blk == cur - 1)
    future = blk * SEL_BLOCK > pos_t
    imp_t = jnp.where(future, -1e9, jnp.where(forced, 1e9, imp_t))
    sel_ref[...] = _rank_select(imp_t, n_blk).astype(BF16)


def _cover_t(nc, n_blk, nc_valid):
    c_start = np.arange(nc) * CMP_STRIDE
    c_last = c_start + CMP_BLOCK - 1
    s_start = np.arange(n_blk) * SEL_BLOCK
    cov = (c_start[None, :] < s_start[:, None] + SEL_BLOCK) & (c_last[None, :] >= s_start[:, None])
    cov = cov & (np.arange(nc)[None, :] < nc_valid)
    return np.tile(cov.astype(np.float32), (1, NSA_GROUP))


def cmp_select_prompt(q, kcmp, vcmp):
    b, t, _ = q.shape
    nc = kcmp.shape[2]
    n_blk = -(-t // SEL_BLOCK)
    tq = _pick_tile(t, 256, LANES)
    cov = jnp.asarray(_cover_t(nc, n_blk, nc - 1), BF16)
    return pl.pallas_call(
        _cmp_select_kernel,
        grid=(b, NSA_KV_HEADS, t // tq),
        in_specs=[pl.BlockSpec((None, tq, GROUP_W), lambda bi, h, i: (bi, i, h)),
                  pl.BlockSpec((None, None, nc, HEAD_DIM), lambda bi, h, i: (bi, h, 0, 0)),
                  pl.BlockSpec((None, None, nc, HEAD_DIM), lambda bi, h, i: (bi, h, 0, 0)),
                  pl.BlockSpec(cov.shape, lambda bi, h, i: (0, 0))],
        out_specs=[pl.BlockSpec((None, tq, GROUP_W), lambda bi, h, i: (bi, i, h)),
                   pl.BlockSpec((None, None, n_blk, tq), lambda bi, h, i: (bi, h, 0, i))],
        out_shape=[jax.ShapeDtypeStruct((b, t, TOK_W), F32),
                   jax.ShapeDtypeStruct((b, NSA_KV_HEADS, n_blk, t), BF16)],
        compiler_params=_cparams(("parallel", "parallel", "arbitrary")),
        name="cmp_select_prompt",
    )(q, kcmp, vcmp, cov)


SEL_TQ = 128
SEL_TK = 512


def _sel_kernel(q_ref, k_ref, v_ref, sel_ref, exp_ref, o_ref, mask_ref, m_ref, l_ref, acc_ref):
    t = k_ref.shape[0]
    qi = pl.program_id(2)
    q6 = _stack_heads(q_ref[...]).astype(BF16)
    mask_ref[...] = (1.0 - _tn(sel_ref[...], exp_ref[...])) * NEG
    m_ref[...] = jnp.full(m_ref.shape, NEG, F32)
    l_ref[...] = jnp.zeros(l_ref.shape, F32)
    acc_ref[...] = jnp.zeros(acc_ref.shape, F32)
    q_lo = qi * SEL_TQ

    def key_tile(j, causal):
        lo, hi = j * SEL_TK, (j + 1) * SEL_TK
        k = k_ref[lo:hi, :].astype(BF16)
        v = v_ref[lo:hi, :].astype(BF16)
        bias = mask_ref[:, lo:hi]
        if causal:
            q_pos = q_lo + lax.broadcasted_iota(I32, (SEL_TQ, SEL_TK), 0)
            k_pos = lo + lax.broadcasted_iota(I32, (SEL_TQ, SEL_TK), 1)
            bias = jnp.where(k_pos <= q_pos, bias, NEG)
        s = _nt(q6, k).reshape(NSA_GROUP, SEL_TQ, SEL_TK) * SCALE + bias[None]
        m_old = m_ref[...]
        m_new = jnp.maximum(m_old, jnp.max(s, axis=-1, keepdims=True))
        e = jnp.exp(s - m_new)
        alpha = jnp.exp(m_old - m_new)
        l_ref[...] = alpha * l_ref[...] + jnp.sum(e, axis=-1, keepdims=True)
        pv = jnp.dot(e.reshape(NSA_GROUP * SEL_TQ, SEL_TK).astype(BF16), v, preferred_element_type=F32)
        acc_ref[...] = alpha * acc_ref[...] + pv.reshape(NSA_GROUP, SEL_TQ, HEAD_DIM)
        m_ref[...] = m_new

    for j in range(t // SEL_TK):
        lo, hi = j * SEL_TK, (j + 1) * SEL_TK

        @pl.when(hi - 1 <= q_lo)
        def _():
            key_tile(j, False)

        @pl.when((lo <= q_lo + SEL_TQ - 1) & (hi - 1 > q_lo))
        def _():
            key_tile(j, True)

    o = acc_ref[...] / jnp.maximum(l_ref[...], 1e-30)
    o_ref[...] = jnp.concatenate([o[g] for g in range(NSA_GROUP)], axis=-1)


def sel_attention_prompt(q, ks, vs, sel_t):
    b, t, _ = q.shape
    n_blk = sel_t.shape[2]
    expand = (np.arange(t)[None, :] // SEL_BLOCK == np.arange(n_blk)[:, None]).astype(np.float32)
    expand = jnp.asarray(expand, BF16)
    return pl.pallas_call(
        _sel_kernel,
        grid=(b, NSA_KV_HEADS, t // SEL_TQ),
        in_specs=[pl.BlockSpec((None, SEL_TQ, GROUP_W), lambda bi, h, i: (bi, i, h)),
                  pl.BlockSpec((None, t, HEAD_DIM), lambda bi, h, i: (bi, 0, h)),
                  pl.BlockSpec((None, t, HEAD_DIM), lambda bi, h, i: (bi, 0, h)),
                  pl.BlockSpec((None, None, n_blk, SEL_TQ), lambda bi, h, i: (bi, h, 0, i)),
                  pl.BlockSpec((n_blk, t), lambda bi, h, i: (0, 0))],
        out_specs=pl.BlockSpec((None, SEL_TQ, GROUP_W), lambda bi, h, i: (bi, i, h)),
        out_shape=jax.ShapeDtypeStruct((b, t, TOK_W), F32),
        scratch_shapes=[pltpu.VMEM((SEL_TQ, t), F32),
                        pltpu.VMEM((NSA_GROUP, SEL_TQ, 1), F32),
                        pltpu.VMEM((NSA_GROUP, SEL_TQ, 1), F32),
                        pltpu.VMEM((NSA_GROUP, SEL_TQ, HEAD_DIM), F32)],
        compiler_params=_cparams(("parallel", "parallel", "arbitrary")),
        name="sel_attention_prompt",
    )(q, ks, vs, sel_t, expand)


def _heads_to_lanes_t(q):
    tq = q.shape[0]
    return jnp.concatenate([q[:, g * HEAD_DIM:(g + 1) * HEAD_DIM].T for g in range(NSA_GROUP)], axis=1)


def _lanes_to_heads_t(o_t, tq):
    return jnp.concatenate([o_t[:, g * tq:(g + 1) * tq].T for g in range(NSA_GROUP)], axis=1)


def _selt_kernel(q_ref, k_ref, v_ref, sel_ref, exp_ref, o_ref, mask_ref, m_ref, l_ref, acc_ref):
    t = k_ref.shape[0]
    qi = pl.program_id(2)
    q_t = _heads_to_lanes_t(q_ref[...]).astype(BF16)
    mask_ref[...] = (1.0 - jnp.dot(exp_ref[...], sel_ref[...], preferred_element_type=F32)) * NEG
    m_ref[...] = jnp.full(m_ref.shape, NEG, F32)
    l_ref[...] = jnp.zeros(l_ref.shape, F32)
    acc_ref[...] = jnp.zeros(acc_ref.shape, F32)
    q_lo = qi * SEL_TQ

    def key_tile(j, causal):
        lo, hi = j * SEL_TK, (j + 1) * SEL_TK
        k = k_ref[lo:hi, :].astype(BF16)
        v = v_ref[lo:hi, :].astype(BF16)
        bias = mask_ref[lo:hi, :]
        if causal:
            k_pos = lo + lax.broadcasted_iota(I32, (SEL_TK, SEL_TQ), 0)
            q_pos = q_lo + lax.broadcasted_iota(I32, (SEL_TK, SEL_TQ), 1)
            bias = jnp.where(k_pos <= q_pos, bias, NEG)
        s = jnp.dot(k, q_t, preferred_element_type=F32) * SCALE + jnp.concatenate([bias] * NSA_GROUP, axis=1)
        m_old = m_ref[...]
        m_new = jnp.maximum(m_old, jnp.max(s, axis=0, keepdims=True))
        e = jnp.exp(s - m_new)
        alpha = jnp.exp(m_old - m_new)
        l_ref[...] = alpha * l_ref[...] + jnp.sum(e, axis=0, keepdims=True)
        acc_ref[...] = alpha * acc_ref[...] + _tn(v, e.astype(BF16))
        m_ref[...] = m_new

    for j in range(t // SEL_TK):
        lo, hi = j * SEL_TK, (j + 1) * SEL_TK

        @pl.when(hi - 1 <= q_lo)
        def _():
            key_tile(j, False)

        @pl.when((lo <= q_lo + SEL_TQ - 1) & (hi - 1 > q_lo))
        def _():
            key_tile(j, True)

    o_t = acc_ref[...] / jnp.maximum(l_ref[...], 1e-30)
    o_ref[...] = _lanes_to_heads_t(o_t, SEL_TQ)


def sel_attention_prompt_t(q, ks, vs, sel_t):
    b, t, _ = q.shape
    n_blk = sel_t.shape[2]
    expand = (np.arange(t)[:, None] // SEL_BLOCK == np.arange(n_blk)[None, :]).astype(np.float32)
    expand = jnp.asarray(expand, BF16)
    return pl.pallas_call(
        _selt_kernel,
        grid=(b, NSA_KV_HEADS, t // SEL_TQ),
        in_specs=[pl.BlockSpec((None, SEL_TQ, GROUP_W), lambda bi, h, i: (bi, i, h)),
                  pl.BlockSpec((None, t, HEAD_DIM), lambda bi, h, i: (bi, 0, h)),
                  pl.BlockSpec((None, t, HEAD_DIM), lambda bi, h, i: (bi, 0, h)),
                  pl.BlockSpec((None, None, n_blk, SEL_TQ), lambda bi, h, i: (bi, h, 0, i)),
                  pl.BlockSpec((t, n_blk), lambda bi, h, i: (0, 0))],
        out_specs=pl.BlockSpec((None, SEL_TQ, GROUP_W), lambda bi, h, i: (bi, i, h)),
        out_shape=jax.ShapeDtypeStruct((b, t, TOK_W), F32),
        scratch_shapes=[pltpu.VMEM((t, SEL_TQ), F32),
                        pltpu.VMEM((1, NSA_GROUP * SEL_TQ), F32),
                        pltpu.VMEM((1, NSA_GROUP * SEL_TQ), F32),
                        pltpu.VMEM((HEAD_DIM, NSA_GROUP * SEL_TQ), F32)],
        compiler_params=_cparams(("parallel", "parallel", "arbitrary")),
        name="sel_attention_prompt",
    )(q, ks, vs, sel_t, expand)


WIN_TQ = 128


def _gate_cols(gate, h, j, tq):
    cols = []
    for g in range(NSA_GROUP):
        c = (h * NSA_GROUP + g) * 3 + j
        cols.append(jnp.broadcast_to(gate[:, c:c + 1], (tq, HEAD_DIM)))
    return jnp.concatenate(cols, axis=1)


def _win_combine_kernel(q_ref, k_ref, v_ref, oc_ref, os_ref, gate_ref, o_ref):
    tq = q_ref.shape[0]
    span = WINDOW + tq
    h = pl.program_id(1)
    qi = pl.program_id(2)
    start = pl.multiple_of(jnp.maximum(qi * tq - WINDOW, 0), tq)
    q_t = _heads_to_lanes_t(q_ref[...]).astype(BF16)
    k = k_ref[pl.ds(start, span), :].astype(BF16)
    v = v_ref[pl.ds(start, span), :].astype(BF16)
    diff = (qi * tq + lax.broadcasted_iota(I32, (span, tq), 1)) - (start + lax.broadcasted_iota(I32, (span, tq), 0))
    bias = jnp.where((diff >= 0) & (diff < WINDOW), 0.0, NEG)
    s = jnp.dot(k, q_t, preferred_element_type=F32) * SCALE + jnp.concatenate([bias] * NSA_GROUP, axis=1)
    e = jnp.exp(s - jnp.max(s, axis=0, keepdims=True))
    p = e / jnp.maximum(jnp.sum(e, axis=0, keepdims=True), 1e-30)
    o_win = _lanes_to_heads_t(_tn(v, p.astype(BF16)), tq)
    gate = gate_ref[...]
    for hh in range(NSA_KV_HEADS):
        @pl.when(h == hh)
        def _():
            o_ref[...] = (_gate_cols(gate, hh, 0, tq) * oc_ref[...] + _gate_cols(gate, hh, 1, tq) * os_ref[...]
                          + _gate_cols(gate, hh, 2, tq) * o_win)


def win_combine_prompt(q, kw, vw, o_cmp, o_sel, gate):
    b, t, _ = q.shape
    assert t >= WINDOW + WIN_TQ
    qspec = pl.BlockSpec((None, WIN_TQ, GROUP_W), lambda bi, h, i: (bi, i, h))
    kspec = pl.BlockSpec((None, t, HEAD_DIM), lambda bi, h, i: (bi, 0, h))
    return pl.pallas_call(
        _win_combine_kernel,
        grid=(b, NSA_KV_HEADS, t // WIN_TQ),
        in_specs=[qspec, kspec, kspec, qspec, qspec,
                  pl.BlockSpec((None, WIN_TQ, LANES), lambda bi, h, i: (bi, i, 0))],
        out_specs=qspec,
        out_shape=jax.ShapeDtypeStruct((b, t, TOK_W), F32),
        compiler_params=_cparams(("parallel", "parallel", "arbitrary")),
        name="win_combine_prompt",
    )(q, kw, vw, o_cmp, o_sel, gate)


def _router_kernel(x_ref, g_ref, wt_ref, b_ref, xn_ref, e_ref, gt_ref):
    xn = _rms(x_ref[...], g_ref[...])
    xn_ref[...] = xn
    lt = _nt(wt_ref[...], xn.astype(BF16))
    ex = jnp.exp(lt - jnp.max(lt, axis=0, keepdims=True))
    aff = ex / jnp.sum(ex, axis=0, keepdims=True)
    sel = aff + b_ref[...]
    row = lambda a, r: a[r:r + 1, :]

    best, g_idx = None, None
    for g in range(N_GROUPS):
        a, b, c, d = (row(sel, g * EXP_PER_GROUP + j) for j in range(EXP_PER_GROUP))
        hi1, lo1, hi2, lo2 = jnp.maximum(a, b), jnp.minimum(a, b), jnp.maximum(c, d), jnp.minimum(c, d)
        score = jnp.maximum(hi1, hi2) + jnp.maximum(jnp.minimum(hi1, hi2), jnp.maximum(lo1, lo2))
        if g == 0:
            best, g_idx = score, jnp.zeros(score.shape, I32)
        else:
            g_idx = jnp.where(score > best, g, g_idx)
            best = jnp.maximum(best, score)

    def in_group(a, j):
        out = row(a, j)
        for g in range(1, N_GROUPS):
            out = jnp.where(g_idx == g, row(a, g * EXP_PER_GROUP + j), out)
        return out

    v = [in_group(sel, j) for j in range(EXP_PER_GROUP)]
    af = [in_group(aff, j) for j in range(EXP_PER_GROUP)]

    def first_max(vals):
        m = functools.reduce(jnp.maximum, vals)
        loc = jnp.full(m.shape, EXP_PER_GROUP - 1, I32)
        for j in range(EXP_PER_GROUP - 2, -1, -1):
            loc = jnp.where(vals[j] == m, j, loc)
        return loc

    l1 = first_max(v)
    l2 = first_max([jnp.where(l1 == j, -jnp.inf, v[j]) for j in range(EXP_PER_GROUP)])
    pick = lambda loc: functools.reduce(lambda acc, j: jnp.where(loc == j, af[j], acc), range(1, EXP_PER_GROUP), af[0])
    a1, a2 = pick(l1), pick(l2)
    tot = a1 + a2
    e_ref[...] = jnp.concatenate([g_idx * EXP_PER_GROUP + l1, g_idx * EXP_PER_GROUP + l2], axis=0)
    gt_ref[...] = jnp.concatenate([a1 / tot, a2 / tot], axis=0)


def _router_into_kernel(x_ref, g_ref, wt_ref, b_ref, xn_all_ref, xn_ref, e_ref, gt_ref):
    del xn_all_ref
    _router_kernel(x_ref, g_ref, wt_ref, b_ref, xn_ref, e_ref, gt_ref)


def router(x, gain, w_router, b_router, xn_rows=None, into=None, row0=0):
    n, d = x.shape
    tm = _pick_tile(n, 1024, LANES)
    assert row0 % tm == 0
    xn_rows = n if xn_rows is None else xn_rows
    args = [x, gain.reshape(1, d), w_router.T.astype(BF16), b_router.reshape(N_EXPERTS, 1).astype(F32)]
    in_specs = [pl.BlockSpec((tm, d), lambda i: (i, 0)),
                pl.BlockSpec((1, d), lambda i: (0, 0)),
                pl.BlockSpec((N_EXPERTS, d), lambda i: (0, 0)),
                pl.BlockSpec((N_EXPERTS, 1), lambda i: (0, 0))]
    body, aliases = _router_kernel, {}
    if into is not None:
        assert into.shape == (xn_rows, d)
        args.append(into)
        in_specs.append(pl.BlockSpec(memory_space=pl.ANY))
        body, aliases = _router_into_kernel, {len(args) - 1: 0}
    return pl.pallas_call(
        body,
        grid=(n // tm,),
        in_specs=in_specs,
        out_specs=[pl.BlockSpec((tm, d), lambda i: (row0 // tm + i, 0)),
                   pl.BlockSpec((TOP_K, tm), lambda i: (0, i)),
                   pl.BlockSpec((TOP_K, tm), lambda i: (0, i))],
        out_shape=[jax.ShapeDtypeStruct((xn_rows, d), F32),
                   jax.ShapeDtypeStruct((TOP_K, n), I32),
                   jax.ShapeDtypeStruct((TOP_K, n), F32)],
        input_output_aliases=aliases,
        compiler_params=_cparams(("parallel",)),
        name="router",
    )(*args)


MOE_TM = 512
MOE_NF = 4
MOE_TF = D_FF // MOE_NF
MOE_ROWS = MOE_TM // MOE_NF
MOE_SPLIT = 128


def _moe_kernel(te_ref, nu_ref, src_ref, dst_ref, x_hbm, wg_ref, wu_ref, wd_ref, y_hbm,
                xbuf, xb, ybuf, gsem, ssem):
    i = pl.program_id(0)
    f = pl.program_id(1)
    n_used = nu_ref[0]
    d = xbuf.shape[-1]

    def gather_copy(tile, buf, r):
        row = src_ref[tile * MOE_TM + r]
        return pltpu.make_async_copy(x_hbm.at[pl.ds(row, 1), :], xbuf.at[buf, pl.ds(r, 1), :], gsem.at[buf])

    def scatter_copy(tile, buf, r):
        row = dst_ref[tile * MOE_TM + r]
        return pltpu.make_async_copy(ybuf.at[buf, pl.ds(r, 1), :], y_hbm.at[pl.ds(row, 1), :], ssem.at[buf])

    def wait_all(hbm, vmem_buf, sem):
        pltpu.make_async_copy(hbm.at[pl.ds(0, MOE_TM), :], vmem_buf, sem).wait()

    dump0 = y_hbm.shape[0] - MOE_TM
    cur = i % 2
    oth = 1 - cur

    @pl.when((i == 0) & (f == 0))
    def _():
        def body(r, carry):
            gather_copy(0, 0, r).start()
            return carry
        lax.fori_loop(0, MOE_TM, body, 0)
        ybuf[1] = jnp.zeros(ybuf.shape[1:], F32)

    @pl.when(i < n_used)
    def _():
        @pl.when(f == 0)
        def _():
            wait_all(x_hbm, xbuf.at[cur], gsem.at[cur])
            xb[...] = xbuf[cur].astype(BF16)
            ybuf[cur] = jnp.zeros(ybuf.shape[1:], F32)

        nxt = jnp.minimum(i + 1, n_used - 1)
        prv = jnp.maximum(i - 1, 0)
        wg, wu, wd = wg_ref[...].astype(BF16), wu_ref[...].astype(BF16), wd_ref[...].astype(BF16)
        for c, (lo, hi) in enumerate(((0, MOE_SPLIT), (MOE_SPLIT, MOE_TM))):
            x = xb[lo:hi, :]
            a = jnp.dot(x, wg, preferred_element_type=F32)
            u = jnp.dot(x, wu, preferred_element_type=F32)
            h = (a * jax.nn.sigmoid(a) * u).astype(BF16)
            ybuf[cur, lo:hi, :] += jnp.dot(h, wd, preferred_element_type=F32)
            if c == 0:
                r0 = pl.multiple_of(f * MOE_ROWS, MOE_ROWS)
                for r in range(MOE_ROWS):
                    rr = r0 + r
                    gather_copy(nxt, oth, rr).start()
                    row = jnp.where(i > 0, dst_ref[prv * MOE_TM + rr], dump0 + rr)
                    pltpu.make_async_copy(ybuf.at[oth, pl.ds(rr, 1), :], y_hbm.at[pl.ds(row, 1), :], ssem.at[oth]).start()

        @pl.when(f == MOE_NF - 1)
        def _():
            wait_all(y_hbm, ybuf.at[oth], ssem.at[oth])

    @pl.when((i == n_used) & (f == 0))
    def _():
        wait_all(x_hbm, xbuf.at[cur], gsem.at[cur])

        def body(r, carry):
            scatter_copy(i - 1, oth, r).start()
            return carry
        lax.fori_loop(0, MOE_TM, body, 0)
        wait_all(y_hbm, ybuf.at[oth], ssem.at[oth])


def moe_experts(xn, tile_expert, n_used, src_row, dst_row, n_out_rows, w_gate, w_up, w_down, layer):
    n, d = xn.shape
    n_tiles = tile_expert.shape[0]

    def tile(i, nu):
        return jnp.minimum(i, nu[0] - 1)

    def fcol(i, f, nu):
        return jnp.where(i < nu[0], f, MOE_NF - 1)

    wspec = lambda blk, im: pl.BlockSpec(blk, im)
    grid_spec = pltpu.PrefetchScalarGridSpec(
        num_scalar_prefetch=4,
        grid=(n_tiles, MOE_NF),
        in_specs=[
            pl.BlockSpec(memory_space=pl.ANY),
            wspec((None, None, d, MOE_TF), lambda i, f, te, nu, s, t: (layer, te[tile(i, nu)], 0, fcol(i, f, nu))),
            wspec((None, None, d, MOE_TF), lambda i, f, te, nu, s, t: (layer, te[tile(i, nu)], 0, fcol(i, f, nu))),
            wspec((None, None, MOE_TF, d), lambda i, f, te, nu, s, t: (layer, te[tile(i, nu)], fcol(i, f, nu), 0)),
        ],
        out_specs=pl.BlockSpec(memory_space=pl.ANY),
        scratch_shapes=[pltpu.VMEM((2, MOE_TM, d), F32),
                        pltpu.VMEM((MOE_TM, d), BF16),
                        pltpu.VMEM((2, MOE_TM, d), F32),
                        pltpu.SemaphoreType.DMA((2,)),
                        pltpu.SemaphoreType.DMA((2,))],
    )
    return pl.pallas_call(
        _moe_kernel,
        grid_spec=grid_spec,
        out_shape=jax.ShapeDtypeStruct((n_out_rows, d), F32),
        compiler_params=_cparams(("arbitrary", "arbitrary")),
        name="moe_experts",
    )(tile_expert, n_used, src_row, dst_row, xn, w_gate, w_up, w_down)


def moe_ffn(x, xn, experts, gates, w_gate, w_up, w_down, layer):
    n, d = x.shape
    n_items = n * TOP_K
    n_tiles = -(-n_items // MOE_TM) + N_EXPERTS
    n_slots = n_tiles * MOE_TM
    flat_e = experts.reshape(-1)
    sizes = jnp.sum((flat_e[:, None] == jnp.arange(N_EXPERTS, dtype=I32)[None, :]).astype(I32), axis=0)
    padded = -(-sizes // MOE_TM) * MOE_TM
    ends_p = jnp.cumsum(padded)
    starts_p = ends_p - padded
    starts = jnp.cumsum(sizes) - sizes
    order = jnp.argsort(flat_e).astype(I32)
    tile_expert = jnp.searchsorted(ends_p, jnp.arange(n_tiles, dtype=I32) * MOE_TM, side='right')
    tile_expert = jnp.minimum(tile_expert, N_EXPERTS - 1).astype(I32)
    n_used = (ends_p[-1:] // MOE_TM).astype(I32)
    slot = jnp.arange(n_slots, dtype=I32)
    slot_e = jnp.repeat(tile_expert, MOE_TM)
    within = slot - starts_p[slot_e]
    is_item = within < sizes[slot_e]
    item = order[jnp.clip(starts[slot_e] + within, 0, n_items - 1)]
    src_row = jnp.where(is_item, item % n, 0).astype(I32)
    dst_row = jnp.where(is_item, item, n_items + slot).astype(I32)
    y = moe_experts(xn, tile_expert, n_used, src_row, dst_row, n_items + n_slots + MOE_TM, w_gate, w_up, w_down, layer)
    g = gates.astype(F32)
    return x + y[:n] * g[0][:, None] + y[n:n_items] * g[1][:, None]


MOE2_TM = 1024
MOE2_GROUP = 64
MOE2_CHUNK = 256
MOE2_GPS = MOE2_TM // MOE_NF // MOE2_GROUP
MOE2_VMEM_LIMIT = V7X_VMEM_BYTES * 7 // 8


def _moe2_kernel(te_ref, nu_ref, nv_ref, base_ref, order_ref, x_hbm, wg_ref, wu_ref, wd_ref, y_hbm,
                 xbuf, xb, ybuf, gsem, ssem, *, n_tok):
    n_items = n_tok * TOP_K

    def slot_item(tile, rr):
        return order_ref[jnp.minimum(base_ref[tile] + rr, n_items - 1)]
    i = pl.program_id(0)
    f = pl.program_id(1)
    n_used = nu_ref[0]
    cur = i % 2
    oth = 1 - cur
    n_groups = MOE2_TM // MOE2_GROUP

    def gather_group(tile, g):
        for r in range(MOE2_GROUP):
            rr = g * MOE2_GROUP + r
            item = slot_item(tile, rr)
            row = jnp.where(item >= n_tok, item - n_tok, item)
            pltpu.make_async_copy(x_hbm.at[pl.ds(row, 1), :], xbuf.at[pl.ds(rr, 1), :], gsem).start()

    def scatter_group(tile, buf, g):
        for r in range(MOE2_GROUP):
            rr = g * MOE2_GROUP + r
            row = jnp.where(rr < nv_ref[tile], slot_item(tile, rr), n_items + tile * MOE2_TM + rr)
            pltpu.make_async_copy(ybuf.at[buf, pl.ds(rr, 1), :], y_hbm.at[pl.ds(row, 1), :], ssem.at[buf]).start()

    def wait_gather_group():
        pltpu.make_async_copy(x_hbm.at[pl.ds(0, MOE2_GROUP), :], xbuf.at[pl.ds(0, MOE2_GROUP), :], gsem).wait()

    def wait_scatter_group(buf):
        pltpu.make_async_copy(ybuf.at[buf, pl.ds(0, MOE2_GROUP), :], y_hbm.at[pl.ds(0, MOE2_GROUP), :], ssem.at[buf]).wait()

    @pl.when((i == 0) & (f == 0))
    def _():
        xbuf[...] = jnp.zeros(xbuf.shape, F32)

        def body(g, carry):
            @pl.when(g * MOE2_GROUP < nv_ref[0])
            def _():
                gather_group(0, g)
            return carry
        lax.fori_loop(0, n_groups, body, 0)

    @pl.when(i < n_used)
    def _():
        n_valid = nv_ref[i]

        @pl.when(f == 0)
        def _():
            for g in range(n_groups):
                @pl.when(g * MOE2_GROUP < n_valid)
                def _():
                    wait_gather_group()
            xb[...] = xbuf[...].astype(BF16)
            ybuf[cur] = jnp.zeros(ybuf.shape[1:], F32)

        nv_next = nv_ref[i + 1]
        nv_prev = nv_ref[jnp.maximum(i - 1, 0)]
        for gi in range(MOE2_GPS):
            g = f * MOE2_GPS + gi

            @pl.when((i + 1 < n_used) & (g * MOE2_GROUP < nv_next))
            def _():
                gather_group(i + 1, g)

            @pl.when((i > 0) & (g * MOE2_GROUP < nv_prev))
            def _():
                scatter_group(i - 1, oth, g)

        for c in range(MOE2_TM // MOE2_CHUNK):
            lo, hi = c * MOE2_CHUNK, (c + 1) * MOE2_CHUNK

            @pl.when(lo < n_valid)
            def _():
                x = xb[lo:hi, :]
                a = jnp.dot(x, wg_ref[...].astype(BF16), preferred_element_type=F32)
                u = jnp.dot(x, wu_ref[...].astype(BF16), preferred_element_type=F32)
                h = (a * jax.nn.sigmoid(a) * u).astype(BF16)
                ybuf[cur, lo:hi, :] += jnp.dot(h, wd_ref[...].astype(BF16), preferred_element_type=F32)

        @pl.when(f == MOE_NF - 1)
        def _():
            for g in range(n_groups):
                @pl.when((i > 0) & (g * MOE2_GROUP < nv_prev))
                def _():
                    wait_scatter_group(oth)

    @pl.when((i == n_used) & (f == 0))
    def _():
        nv_last = nv_ref[i - 1]

        def start(g, carry):
            @pl.when(g * MOE2_GROUP < nv_last)
            def _():
                scatter_group(i - 1, oth, g)
            return carry
        lax.fori_loop(0, n_groups, start, 0)

        def wait(g, carry):
            @pl.when(g * MOE2_GROUP < nv_last)
            def _():
                wait_scatter_group(oth)
            return carry
        lax.fori_loop(0, n_groups, wait, 0)


def moe_experts2(xn, n_tok, tile_expert, n_used, n_valid, base, order, n_out_rows, w_gate, w_up, w_down, layer):
    n, d = xn.shape
    n_tiles = tile_expert.shape[0]

    def tile(i, nu):
        return jnp.minimum(i, nu[0] - 1)

    def fcol(i, f, nu):
        return jnp.where(i < nu[0], f, MOE_NF - 1)

    grid_spec = pltpu.PrefetchScalarGridSpec(
        num_scalar_prefetch=5,
        grid=(n_tiles, MOE_NF),
        in_specs=[
            pl.BlockSpec(memory_space=pl.ANY),
            pl.BlockSpec((None, None, d, MOE_TF), lambda i, f, te, nu, nv, s, t: (layer, te[tile(i, nu)], 0, fcol(i, f, nu))),
            pl.BlockSpec((None, None, d, MOE_TF), lambda i, f, te, nu, nv, s, t: (layer, te[tile(i, nu)], 0, fcol(i, f, nu))),
            pl.BlockSpec((None, None, MOE_TF, d), lambda i, f, te, nu, nv, s, t: (layer, te[tile(i, nu)], fcol(i, f, nu), 0)),
        ],
        out_specs=pl.BlockSpec(memory_space=pl.ANY),
        scratch_shapes=[pltpu.VMEM((MOE2_TM, d), F32),
                        pltpu.VMEM((MOE2_TM, d), BF16),
                        pltpu.VMEM((2, MOE2_TM, d), F32),
                        pltpu.SemaphoreType.DMA(()),
                        pltpu.SemaphoreType.DMA((2,))],
    )
    return pl.pallas_call(
        functools.partial(_moe2_kernel, n_tok=n_tok),
        grid_spec=grid_spec,
        out_shape=jax.ShapeDtypeStruct((n_out_rows, d), F32),
        compiler_params=pltpu.CompilerParams(dimension_semantics=("arbitrary", "arbitrary"),
                                             vmem_limit_bytes=MOE2_VMEM_LIMIT),
        name="moe_experts",
    )(tile_expert, n_used, n_valid, base, order, xn, w_gate, w_up, w_down)


def moe_ffn2(xs, xn, experts, gates, w_gate, w_up, w_down, layer):
    n, d = experts.shape[1], xn.shape[1]
    tm = MOE2_TM
    n_items = n * TOP_K
    n_tiles = -(-n_items // tm) + N_EXPERTS
    n_slots = n_tiles * tm
    flat_e = experts.reshape(-1)
    sizes = jnp.sum((flat_e[:, None] == jnp.arange(N_EXPERTS, dtype=I32)[None, :]).astype(I32), axis=0)
    padded = -(-sizes // tm) * tm
    ends_p = jnp.cumsum(padded)
    starts_p = ends_p - padded
    starts = jnp.cumsum(sizes) - sizes
    order = jnp.argsort(flat_e).astype(I32)
    tile_start = jnp.arange(n_tiles, dtype=I32) * tm
    tile_expert = jnp.minimum(jnp.searchsorted(ends_p, tile_start, side='right'), N_EXPERTS - 1).astype(I32)
    n_used = (ends_p[-1:] // tm).astype(I32)
    n_valid = jnp.clip(sizes[tile_expert] - (tile_start - starts_p[tile_expert]), 0, tm).astype(I32)
    base = (starts[tile_expert] + tile_start - starts_p[tile_expert]).astype(I32)
    y = moe_experts2(xn, n, tile_expert, n_used, n_valid, base, order, n_items + n_slots,
                     w_gate, w_up, w_down, layer)
    outs, r0 = [], 0
    for x, g in zip(xs, gates):
        r1 = r0 + x.shape[0]
        outs.append(x + y[r0:r1] * g[0][:, None] + y[n + r0:n + r1] * g[1][:, None])
        r0 = r1
    return outs


GROUP_ROWS = 8


def _pick_head(ref, h):
    out = ref[:, 0, :]
    for hh in range(1, NSA_KV_HEADS):
        out = jnp.where(h == hh, ref[:, hh, :], out)
    return out


def _group_rows(q_ref, h):
    rows = [q_ref[:, (h * NSA_GROUP + g) * HEAD_DIM:(h * NSA_GROUP + g + 1) * HEAD_DIM] for g in range(NSA_GROUP)]
    rows.append(jnp.zeros((GROUP_ROWS - NSA_GROUP, HEAD_DIM), F32))
    return jnp.concatenate(rows, axis=0)


def _compress_paged_kernel(pt_ref, cache_hbm, pe_ref, w1_ref, w2_ref, o_ref, xbuf, sem, *, layer, n_pages):
    b = pl.program_id(0)
    n_chunks = n_pages * (PAGE_SIZE // CMP_STRIDE)

    def page_copy(p):
        page = pt_ref[b * n_pages + p]
        return pltpu.make_async_copy(cache_hbm.at[layer, page], xbuf.at[pl.ds(p * PAGE_SIZE, PAGE_SIZE)], sem)

    def start(p, carry):
        page_copy(p).start()
        return carry

    def wait(p, carry):
        page_copy(p).wait()
        return carry

    lax.fori_loop(0, n_pages, start, 0)
    lax.fori_loop(0, n_pages, wait, 0)

    def slab(r, h):
        return xbuf[pl.ds(r, n_chunks, stride=CMP_STRIDE), h, :]
    _compress_body(slab, n_chunks, pe_ref, w1_ref, w2_ref, o_ref)


def compress_paged(cache, layer, page_table, pe, w1, w2):
    b, n_pages = page_table.shape
    n_chunks = n_pages * (PAGE_SIZE // CMP_STRIDE)
    grid_spec = pltpu.PrefetchScalarGridSpec(
        num_scalar_prefetch=1,
        grid=(b,),
        in_specs=[pl.BlockSpec(memory_space=pl.ANY),
                  pl.BlockSpec(pe.shape, lambda bi, pt: (0, 0)),
                  pl.BlockSpec(w1.shape, lambda bi, pt: (0, 0)),
                  pl.BlockSpec(w2.shape, lambda bi, pt: (0, 0))],
        out_specs=pl.BlockSpec((None, NSA_KV_HEADS, n_chunks, HEAD_DIM), lambda bi, pt: (bi, 0, 0, 0)),
        scratch_shapes=[pltpu.VMEM((n_pages * PAGE_SIZE, NSA_KV_HEADS, HEAD_DIM), F32), pltpu.SemaphoreType.DMA(())],
    )
    return pl.pallas_call(
        functools.partial(_compress_paged_kernel, layer=layer, n_pages=n_pages),
        grid_spec=grid_spec,
        out_shape=jax.ShapeDtypeStruct((b, NSA_KV_HEADS, n_chunks, HEAD_DIM), F32),
        compiler_params=_cparams(("arbitrary",)),
        name="compress_paged",
    )(page_table.reshape(-1), cache, pe, w1, w2)


def _cmp_select_sample_kernel(q_ref, kc_ref, vc_ref, cov_ref, o_ref, idx_ref, q8_ref, *, q_pos, n_blk):
    nc = kc_ref.shape[1]
    nb_pad = cov_ref.shape[1]
    blk = lax.broadcasted_iota(I32, (1, nb_pad), 1)
    cur = q_pos // SEL_BLOCK
    forced = (blk == 0) | (blk == cur) | (blk == cur - 1)
    future = (blk * SEL_BLOCK > q_pos) | (blk >= n_blk)
    c_last = lax.broadcasted_iota(I32, (GROUP_ROWS, nc), 1) * CMP_STRIDE + (CMP_BLOCK - 1)
    valid = (c_last <= q_pos) & (lax.broadcasted_iota(I32, (GROUP_ROWS, nc), 0) < NSA_GROUP)
    for h in range(NSA_KV_HEADS):
        q8f = _group_rows(q_ref, h)
        q8_ref[h] = q8f
        q8 = q8f.astype(BF16)
        s = jnp.where(valid, _nt(q8, kc_ref[h].astype(BF16)) * SCALE, NEG)
        m = jnp.max(s, axis=-1, keepdims=True)
        e = jnp.where(valid, jnp.exp(s - m), 0.0)
        p = (e / jnp.maximum(jnp.sum(e, axis=-1, keepdims=True), 1e-30)).astype(BF16)
        o_ref[h] = jnp.dot(p, vc_ref[h].astype(BF16), preferred_element_type=F32)
        imp = jnp.sum(jnp.dot(p, cov_ref[...], preferred_element_type=F32), axis=0, keepdims=True)
        imp = jnp.where(future, -1e9, jnp.where(forced, 1e9, imp))
        jj = lax.broadcasted_iota(I32, (nb_pad, nb_pad), 0)
        ss = lax.broadcasted_iota(I32, (nb_pad, nb_pad), 1)
        row = jnp.broadcast_to(imp, (nb_pad, nb_pad))
        col = jnp.sum(jnp.where(jj == ss, row, 0.0), axis=1, keepdims=True)
        beats = jnp.where(col > row, 1.0, jnp.where(col == row, jnp.where(jj < ss, 1.0, 0.0), 0.0))
        rank = jnp.sum(beats, axis=0, keepdims=True)
        want = lax.broadcasted_iota(I32, (SEL_TOP, nb_pad), 0).astype(F32)
        lane = lax.broadcasted_iota(I32, (SEL_TOP, nb_pad), 1).astype(F32)
        idx = jnp.sum(jnp.where(jnp.broadcast_to(rank, (SEL_TOP, nb_pad)) == want, lane, 0.0), axis=1, keepdims=True)
        idx_ref[h] = idx.astype(I32)


def cmp_select_sample(q, kcmp, vcmp, q_pos):
    b = q.shape[0]
    nc = kcmp.shape[2]
    n_blk = q_pos // SEL_BLOCK + 1
    nb_pad = -(-n_blk // LANES) * LANES
    cov = _cover_t(nc, nb_pad, nc - 1)[:, :nc].T
    cov = jnp.asarray(cov, BF16)
    return pl.pallas_call(
        functools.partial(_cmp_select_sample_kernel, q_pos=q_pos, n_blk=n_blk),
        grid=(b,),
        in_specs=[pl.BlockSpec((None, 1, TOK_W), lambda bi: (bi, 0, 0)),
                  pl.BlockSpec((None, NSA_KV_HEADS, nc, HEAD_DIM), lambda bi: (bi, 0, 0, 0)),
                  pl.BlockSpec((None, NSA_KV_HEADS, nc, HEAD_DIM), lambda bi: (bi, 0, 0, 0)),
                  pl.BlockSpec(cov.shape, lambda bi: (0, 0))],
        out_specs=[pl.BlockSpec((None, NSA_KV_HEADS, GROUP_ROWS, HEAD_DIM), lambda bi: (bi, 0, 0, 0)),
                   pl.BlockSpec((None, NSA_KV_HEADS, SEL_TOP, 1), lambda bi: (bi, 0, 0, 0)),
                   pl.BlockSpec((None, NSA_KV_HEADS, GROUP_ROWS, HEAD_DIM), lambda bi: (bi, 0, 0, 0))],
        out_shape=[jax.ShapeDtypeStruct((b, NSA_KV_HEADS, GROUP_ROWS, HEAD_DIM), F32),
                   jax.ShapeDtypeStruct((b, NSA_KV_HEADS, SEL_TOP, 1), I32),
                   jax.ShapeDtypeStruct((b, NSA_KV_HEADS, GROUP_ROWS, HEAD_DIM), F32)],
        compiler_params=_cparams(("parallel",)),
        name="cmp_select_sample",
    )(q, kcmp, vcmp, cov)


def _dot_new_row(q8, k_row):
    a = q8.astype(BF16).astype(F32)
    b = k_row.astype(BF16).astype(F32)
    return jnp.sum(a * b, axis=-1, keepdims=True)


SEL_S_BLOCKS = 4


def _sel_sample_kernel(pt_ref, ix_ref, q_ref, *refs, n_cache_blk):
    k_refs, v_refs = refs[:SEL_S_BLOCKS], refs[SEL_S_BLOCKS:2 * SEL_S_BLOCKS]
    kn_ref, vn_ref, o_ref, m_ref, l_ref, acc_ref = refs[2 * SEL_S_BLOCKS:]
    b, h, r = pl.program_id(0), pl.program_id(1), pl.program_id(2)

    @pl.when(r == 0)
    def _():
        m_ref[...] = jnp.full(m_ref.shape, NEG, F32)
        l_ref[...] = jnp.zeros(l_ref.shape, F32)
        acc_ref[...] = jnp.zeros(acc_ref.shape, F32)

    q8 = q_ref[...]
    n_keys = SEL_S_BLOCKS * SEL_BLOCK
    key_blk = lax.broadcasted_iota(I32, (GROUP_ROWS, n_keys), 1) // SEL_BLOCK
    in_cache = jnp.zeros((GROUP_ROWS, n_keys), I32)
    for j in range(SEL_S_BLOCKS):
        flag = (ix_ref[(b * NSA_KV_HEADS + h) * SEL_TOP + r * SEL_S_BLOCKS + j] < n_cache_blk).astype(I32)
        in_cache = jnp.where(key_blk == j, flag, in_cache)
    valid = (lax.broadcasted_iota(I32, (GROUP_ROWS, n_keys), 0) < NSA_GROUP) & (in_cache > 0)
    k_blk = jnp.concatenate([_pick_head(k_ref, h) for k_ref in k_refs], axis=0).astype(BF16)
    v_blk = jnp.concatenate([_pick_head(v_ref, h) for v_ref in v_refs], axis=0).astype(BF16)
    s = jnp.where(valid, _nt(q8.astype(BF16), k_blk) * SCALE, NEG)
    m_old = m_ref[...]
    m_new = jnp.maximum(m_old, jnp.max(s, axis=-1, keepdims=True))
    e = jnp.where(valid, jnp.exp(s - m_new), 0.0)
    alpha = jnp.exp(m_old - m_new)
    l_ref[...] = alpha * l_ref[...] + jnp.sum(e, axis=-1, keepdims=True)
    acc_ref[...] = alpha * acc_ref[...] + jnp.dot(e.astype(BF16), v_blk, preferred_element_type=F32)
    m_ref[...] = m_new

    @pl.when(r == SEL_TOP // SEL_S_BLOCKS - 1)
    def _():
        s_new = _dot_new_row(q8, kn_ref[...]) * SCALE
        m_old = m_ref[...]
        m_fin = jnp.maximum(m_old, s_new)
        alpha = jnp.exp(m_old - m_fin)
        e_new = jnp.exp(s_new - m_fin)
        l_fin = alpha * l_ref[...] + e_new
        acc = alpha * acc_ref[...] + e_new.astype(BF16).astype(F32) * vn_ref[...].astype(BF16).astype(F32)
        o_ref[...] = acc / jnp.maximum(l_fin, 1e-30)


def sel_attention_sample(q8, idx, cache_k, cache_v, layer, page_table, k_new, v_new):
    b, n_pages = page_table.shape
    per_page = PAGE_SIZE // SEL_BLOCK
    n_cache_blk = n_pages * per_page

    def blk_index(j):
        def index(bi, h, r, pt, ix):
            s = jnp.minimum(ix[(bi * NSA_KV_HEADS + h) * SEL_TOP + r * SEL_S_BLOCKS + j], n_cache_blk - 1)
            return (layer, pt[bi * n_pages + s // per_page], s % per_page, 0, 0)
        return index

    blk_specs = [pl.BlockSpec((None, None, SEL_BLOCK, NSA_KV_HEADS, HEAD_DIM), blk_index(j)) for j in range(SEL_S_BLOCKS)]
    grid_spec = pltpu.PrefetchScalarGridSpec(
        num_scalar_prefetch=2,
        grid=(b, NSA_KV_HEADS, SEL_TOP // SEL_S_BLOCKS),
        in_specs=[pl.BlockSpec((None, None, GROUP_ROWS, HEAD_DIM), lambda bi, h, r, pt, ix: (bi, h, 0, 0))]
                 + blk_specs + blk_specs
                 + [pl.BlockSpec((None, 1, HEAD_DIM), lambda bi, h, r, pt, ix: (bi, 0, h)),
                    pl.BlockSpec((None, 1, HEAD_DIM), lambda bi, h, r, pt, ix: (bi, 0, h))],
        out_specs=pl.BlockSpec((None, None, GROUP_ROWS, HEAD_DIM), lambda bi, h, r, pt, ix: (bi, h, 0, 0)),
        scratch_shapes=[pltpu.VMEM((GROUP_ROWS, 1), F32), pltpu.VMEM((GROUP_ROWS, 1), F32),
                        pltpu.VMEM((GROUP_ROWS, HEAD_DIM), F32)],
    )
    return pl.pallas_call(
        functools.partial(_sel_sample_kernel, n_cache_blk=n_cache_blk),
        grid_spec=grid_spec,
        out_shape=jax.ShapeDtypeStruct((b, NSA_KV_HEADS, GROUP_ROWS, HEAD_DIM), F32),
        compiler_params=_cparams(("arbitrary", "arbitrary", "arbitrary")),
        name="sel_attention_sample",
    )(page_table.reshape(-1), idx.reshape(-1), q8, *([cache_k] * SEL_S_BLOCKS), *([cache_v] * SEL_S_BLOCKS), k_new, v_new)


def _win_sample_kernel(q_ref, k_ref, v_ref, kn_ref, vn_ref, oc_ref, os_ref, gate_ref, o_ref):
    h = pl.program_id(1)
    wb = k_ref.shape[0]
    k_win = _pick_head(k_ref, h).astype(BF16)
    v_win = _pick_head(v_ref, h).astype(BF16)
    q8 = q_ref[...]
    col = lax.broadcasted_iota(I32, (GROUP_ROWS, wb), 1)
    valid = col >= 1
    s = jnp.where(valid, _nt(q8.astype(BF16), k_win) * SCALE, NEG)
    s_new = _dot_new_row(q8, kn_ref[...]) * SCALE
    m = jnp.maximum(jnp.max(s, axis=-1, keepdims=True), s_new)
    e = jnp.where(valid, jnp.exp(s - m), 0.0)
    e_new = jnp.exp(s_new - m)
    den = jnp.maximum(jnp.sum(e, axis=-1, keepdims=True) + e_new, 1e-30)
    p = (e / den).astype(BF16)
    p_new = (e_new / den).astype(BF16).astype(F32)
    o_win = jnp.dot(p, v_win, preferred_element_type=F32) + p_new * vn_ref[...].astype(BF16).astype(F32)
    gate = jnp.broadcast_to(gate_ref[...], (GROUP_ROWS, LANES))
    lane = lax.broadcasted_iota(I32, (GROUP_ROWS, LANES), 1)
    head = h * NSA_GROUP + lax.broadcasted_iota(I32, (GROUP_ROWS, LANES), 0)
    g = [jnp.sum(jnp.where(lane == head * 3 + j, gate, 0.0), axis=-1, keepdims=True) for j in range(3)]
    o_ref[...] = g[0] * oc_ref[...] + g[1] * os_ref[...] + g[2] * o_win


def win_combine_sample(q8, win_k, win_v, layer, k_new, v_new, o_cmp, o_sel, gate):
    b = q8.shape[0]
    wb = win_k.shape[2]
    assert wb == WINDOW
    gspec = pl.BlockSpec((None, None, GROUP_ROWS, HEAD_DIM), lambda bi, h: (bi, h, 0, 0))
    wspec = pl.BlockSpec((None, None, wb, NSA_KV_HEADS, HEAD_DIM), lambda bi, h: (layer, bi, 0, 0, 0))
    nspec = pl.BlockSpec((None, 1, HEAD_DIM), lambda bi, h: (bi, 0, h))
    return pl.pallas_call(
        _win_sample_kernel,
        grid=(b, NSA_KV_HEADS),
        in_specs=[gspec, wspec, wspec, nspec, nspec, gspec, gspec,
                  pl.BlockSpec((None, 1, LANES), lambda bi, h: (bi, 0, 0))],
        out_specs=gspec,
        out_shape=jax.ShapeDtypeStruct((b, NSA_KV_HEADS, GROUP_ROWS, HEAD_DIM), F32),
        compiler_params=_cparams(("parallel", "arbitrary")),
        name="win_combine_sample",
    )(q8, win_k, win_v, k_new, v_new, o_cmp, o_sel, gate)


def nsa_sample_step(prep, layer, caches, win_k, win_v, page_table, pe, w1, w2):
    q, kc, vc, ks, vs, kw, vw, gate = prep
    b = q.shape[0]
    past_len = page_table.shape[1] * PAGE_SIZE
    cache_ck, cache_cv, cache_sk, cache_sv = caches
    kcmp = compress_paged(cache_ck, layer, page_table, pe[0], w1[0], w2[0])
    vcmp = compress_paged(cache_cv, layer, page_table, pe[1], w1[1], w2[1])
    o_cmp, idx, q8 = cmp_select_sample(q, kcmp, vcmp, past_len)
    o_sel = sel_attention_sample(q8, idx, cache_sk, cache_sv, layer, page_table, ks, vs)
    o = win_combine_sample(q8, win_k, win_v, layer, kw, vw, o_cmp, o_sel, gate)
    return o[:, :, :NSA_GROUP].reshape(b, 1, TOK_W)


def rms_norm(x, g):
    xf = x.astype(F32)
    y = xf * lax.rsqrt(jnp.mean(xf * xf, axis=-1, keepdims=True) + EPS)
    return (y * g.astype(F32)).astype(x.dtype)


def masked_softmax(s, valid):
    s = jnp.where(valid, s, NEG)
    m = jnp.max(s, axis=-1, keepdims=True)
    e = jnp.where(valid, jnp.exp(s - m), 0.0)
    return e / jnp.maximum(jnp.sum(e, axis=-1, keepdims=True), 1e-30)


def compress(k, pe, w1, w2):
    b, l = k.shape[:2]
    r = CMP_BLOCK // CMP_STRIDE
    n_chunks = l // CMP_STRIDE
    nc = n_chunks - r + 1
    c = k[:, :n_chunks * CMP_STRIDE].reshape(b, n_chunks, CMP_STRIDE, NSA_KV_HEADS, HEAD_DIM)
    blk = jnp.concatenate([c[:, i:i + nc] for i in range(r)], axis=2)
    blk = blk + pe[None, None, :, None, :]
    flat = blk.transpose(0, 1, 3, 2, 4).reshape(b, nc, NSA_KV_HEADS, CMP_BLOCK * HEAD_DIM)
    hid = jax.nn.gelu(jnp.einsum('bnhf,fe->bnhe', flat, w1))
    return jnp.einsum('bnhe,ed->bnhd', hid, w2)


def cmp_attend(q, q_pos, kc, vc):
    nc = kc.shape[1]
    c_last = jnp.arange(nc, dtype=I32) * CMP_STRIDE + CMP_BLOCK - 1
    s = jnp.einsum('bqhgd,bnhd->bqhgn', q, kc).astype(F32) * SCALE
    valid = (c_last[None, :] <= q_pos[:, None])[None, :, None, None, :]
    p = masked_softmax(s, valid)
    o = jnp.einsum('bqhgn,bnhd->bqhgd', p.astype(vc.dtype), vc)
    return o, p


def select_blocks(p, q_pos, n_sel):
    nc = p.shape[-1]
    c_start = jnp.arange(nc, dtype=I32) * CMP_STRIDE
    c_last = c_start + CMP_BLOCK - 1
    s_start = jnp.arange(n_sel, dtype=I32) * SEL_BLOCK
    cover = ((c_start[:, None] < s_start[None, :] + SEL_BLOCK) & (c_last[:, None] >= s_start[None, :])).astype(F32)
    imp = jnp.einsum('bqhgn,ns->bqhs', p, cover)
    blk = jnp.arange(n_sel, dtype=I32)[None, :]
    cur = (q_pos // SEL_BLOCK)[:, None]
    forced = (blk == 0) | (blk == cur) | (blk == cur - 1)
    future = blk * SEL_BLOCK > q_pos[:, None]
    imp = jnp.where(future[None, :, None, :], -1e9, jnp.where(forced[None, :, None, :], 1e9, imp))
    _, idx = lax.top_k(imp, min(SEL_TOP, n_sel))
    return idx


def to_blocks(k):
    b, l = k.shape[:2]
    ns = -(-l // SEL_BLOCK)
    k = jnp.pad(k, ((0, 0), (0, ns * SEL_BLOCK - l), (0, 0), (0, 0)))
    return k.reshape(b, ns, SEL_BLOCK, NSA_KV_HEADS, HEAD_DIM).transpose(0, 3, 1, 2, 4)


def sel_attend(q, q_pos, idx, kb, vb):
    idx_t = idx.transpose(0, 2, 1, 3)
    take = jax.vmap(jax.vmap(lambda blocks, i: blocks[i]))
    kg = take(kb, idx_t)
    vg = take(vb, idx_t)
    kpos = idx_t[..., None] * SEL_BLOCK + jnp.arange(SEL_BLOCK, dtype=I32)
    valid = (kpos <= q_pos[None, None, :, None, None]).transpose(0, 2, 1, 3, 4)[:, :, :, None]
    s = jnp.einsum('bqhgd,bhqkld->bqhgkl', q, kg).astype(F32) * SCALE
    b, tq, h, g, kk, lb = s.shape
    p = masked_softmax(s.reshape(b, tq, h, g, kk * lb), valid.reshape(b, tq, h, 1, kk * lb)).reshape(s.shape)
    return jnp.einsum('bqhgkl,bhqkld->bqhgd', p.astype(vg.dtype), vg)


def window_attend(q, q_pos, k, v, k_pos):
    s = jnp.einsum('bqhgd,bkhd->bqhgk', q, k).astype(F32) * SCALE
    diff = q_pos[:, None] - k_pos[None, :]
    valid = ((diff >= 0) & (diff < WINDOW) & (k_pos[None, :] >= 0))[None, :, None, None, :]
    p = masked_softmax(s, valid)
    return jnp.einsum('bqhgk,bkhd->bqhgd', p.astype(v.dtype), v)


def gather_pages(cache, page_table):
    c = cache[page_table]
    return c.reshape(page_table.shape[0], page_table.shape[1] * cache.shape[1], *cache.shape[2:])


def nsa_sample(prep, pos, past, win_k, win_v, page_table, past_len, pe, w1, w2):
    q, kc, vc, ks, vs, kw, vw, gate = prep
    b, t = q.shape[:2]
    kvr = lambda a: a.reshape(b, t, NSA_KV_HEADS, HEAD_DIM)
    q = q.reshape(b, t, NSA_KV_HEADS, NSA_GROUP, HEAD_DIM)
    kc, vc, ks, vs, kw, vw = (kvr(a) for a in (kc, vc, ks, vs, kw, vw))
    gate = gate[..., :3 * NSA_HEADS].reshape(b, t, NSA_KV_HEADS, NSA_GROUP, 3)
    full = [jnp.concatenate([gather_pages(c, page_table).astype(new.dtype), new], axis=1)
            for c, new in zip(past, (kc, vc, ks, vs))]
    kcmp = compress(full[0], pe[0], w1[0], w2[0])
    vcmp = compress(full[1], pe[1], w1[1], w2[1])
    o_cmp, p_cmp = cmp_attend(q, pos, kcmp, vcmp)
    ksb, vsb = to_blocks(full[2]), to_blocks(full[3])
    idx = select_blocks(p_cmp, pos, ksb.shape[2])
    o_sel = sel_attend(q, pos, idx, ksb, vsb)
    wb = win_k.shape[1]
    kwin = jnp.concatenate([win_k.astype(kw.dtype), kw], axis=1)
    vwin = jnp.concatenate([win_v.astype(vw.dtype), vw], axis=1)
    kpos = past_len - wb + jnp.arange(wb + t, dtype=I32)
    o_win = window_attend(q, pos, kwin, vwin, kpos)
    o = (gate[..., 0, None] * o_cmp + gate[..., 1, None] * o_sel + gate[..., 2, None] * o_win).reshape(b, t, TOK_W)
    return o, (kc, vc, ks, vs), (kwin[:, -wb:], vwin[:, -wb:])


def _reorder_nsa_weight(w):
    n_gl = 3 * NSA_HEADS
    parts = [w[:, :NSA_MQ0], w[:, NSA_MQ0 + n_gl:NSA_MQ0 + n_gl + MEM_W], w[:, NSA_MQ0:NSA_MQ0 + n_gl]]
    wr = jnp.concatenate(parts, axis=1)
    return jnp.pad(wr, ((0, 0), (0, NSA_ZW - wr.shape[1]))).astype(BF16)


def kernel(x_prompt, x_sample, state_pool, cache_cmp_k, cache_cmp_v, cache_sel_k, cache_sel_v, state_win_k, state_win_v, cache_mem_k, cache_mem_v, page_table, mem_prompt, norm_mix_g, norm_ffn_g, norm_mem_g, w_mem_kv, mem_q_norm_g, mem_k_norm_g, w_in_pool, w_pool_grp, pool_scale, w_out_pool, w_in_nsa, b_gate, nsa_q_norm_g, nsa_k_norm_g, cmp_pe, cmp_w1, cmp_w2, w_out_nsa, w_router, b_router, w_gate, w_up, w_down):
    bp, t_p, d = x_prompt.shape
    bs, t_s, _ = x_sample.shape
    assert t_s == 1
    n_p, n_s = bp * t_p, bs * t_s
    m_len = mem_prompt.shape[1]
    past_len = page_table.shape[1] * PAGE_SIZE
    pos_p = jnp.arange(t_p, dtype=I32)
    pos_s = past_len + jnp.arange(t_s, dtype=I32)
    xp = x_prompt.reshape(n_p, d)
    xs = x_sample.reshape(n_s, d)
    mem_flat = mem_prompt.reshape(-1, d)
    pool_p, pool_s, rows_p, rows_s, win_p, win_s, mem_k_p, mem_v_p = [], [], [], [], [], [], [], []
    for i in range(DEPTH):
        li = i // 2
        kv = proj(mem_flat, w_mem_kv[i].astype(BF16), gain=norm_mem_g[i])
        mk = head_norm(kv, 0, MEM_W, mem_k_norm_g[i]).reshape(bp, m_len, MEM_W)
        mv = kv[:, MEM_W:].reshape(bp, m_len, MEM_W)
        mem_k_p.append(mk.reshape(bp, m_len, MEM_HEADS, HEAD_DIM))
        mem_v_p.append(mv.reshape(bp, m_len, MEM_HEADS, HEAD_DIM))
        mk_s = cache_mem_k[i].reshape(bs, m_len, MEM_W)
        mv_s = cache_mem_v[i].reshape(bs, m_len, MEM_W)
        if i % 2 == 0:
            w_in = w_in_pool[li].astype(BF16)
            zp = proj(xp, w_in, gain=norm_mix_g[i]).reshape(bp, t_p, -1)
            zs = proj(xs, w_in, gain=norm_mix_g[i]).reshape(bs, t_s, -1)
            op = pool_mix(zp, None, w_pool_grp[li], pool_scale[li], 0)
            zs16 = jnp.pad(zs, ((0, 0), (0, POOL_HALO - t_s), (0, 0)))
            halo = jnp.pad(state_pool[li], ((0, 0), (1, 0), (0, 0)))
            os_ = pool_mix(zs16, halo, w_pool_grp[li], pool_scale[li], past_len)[:, :t_s]
            pool_p.append(zp[:, t_p - POOL_STATE:, :TOK_W])
            pool_s.append(jnp.concatenate([state_pool[li], zs[..., :TOK_W]], axis=1)[:, -POOL_STATE:])
            mq0 = TOK_W
            w_out = w_out_pool[li]
        else:
            w_in = _reorder_nsa_weight(w_in_nsa[li])
            zp = proj(xp, w_in, gain=norm_mix_g[i]).reshape(bp, t_p, -1)
            zs = proj(xs, w_in, gain=norm_mix_g[i]).reshape(bs, t_s, -1)
            q, kc, vc, ks, vs, kw, vw, gate = nsa_prep(zp, pos_p, nsa_q_norm_g[li], nsa_k_norm_g[li], b_gate[li])
            kcmp = compress_rows(kc, cmp_pe[li, 0], cmp_w1[li, 0], cmp_w2[li, 0])
            vcmp = compress_rows(vc, cmp_pe[li, 1], cmp_w1[li, 1], cmp_w2[li, 1])
            o_cmp, sel_t = cmp_select_prompt(q, kcmp, vcmp)
            o_sel = sel_attention_prompt_t(q, ks, vs, sel_t)
            op = win_combine_prompt(q, kw, vw, o_cmp, o_sel, gate)
            kvr = lambda a: a.reshape(a.shape[0], a.shape[1], NSA_KV_HEADS, HEAD_DIM)
            rows_p.append(tuple(kvr(a) for a in (kc, vc, ks, vs)))
            wb = min(WINDOW, t_p)
            win_p.append((kvr(kw[:, t_p - wb:]), kvr(vw[:, t_p - wb:])))
            prep_s = nsa_prep(zs, pos_s, nsa_q_norm_g[li], nsa_k_norm_g[li], b_gate[li])
            os_ = nsa_sample_step(prep_s, li, (cache_cmp_k, cache_cmp_v, cache_sel_k, cache_sel_v),
                                  state_win_k, state_win_v, page_table, cmp_pe[li], cmp_w1[li], cmp_w2[li])
            rows_s.append(tuple(kvr(a) for a in prep_s[1:5]))
            win_s.append(tuple(jnp.concatenate([st[li][:, t_s:], kvr(new)], axis=1)
                               for st, new in ((state_win_k, prep_s[5]), (state_win_v, prep_s[6]))))
            mq0 = NSA_MQ0
            w_out = w_out_nsa[li]
        ap = mem_attention(zp, mq0, mk, mv, mem_q_norm_g[i])
        as_ = mem_attention(zs, mq0, mk_s, mv_s, mem_q_norm_g[i])
        w_out = w_out.astype(BF16)
        xp = proj([op.reshape(n_p, TOK_W), ap.reshape(n_p, MEM_W)], w_out, residual=xp)
        xs = proj([os_.reshape(n_s, TOK_W), as_.reshape(n_s, MEM_W)], w_out, residual=xs)
        xn, e_p, g_p = router(xp, norm_ffn_g[i], w_router, b_router, xn_rows=n_p + LANES)
        xs_pad = jnp.pad(xs, ((0, LANES - n_s), (0, 0)))
        xn, e_s, g_s = router(xs_pad, norm_ffn_g[i], w_router, b_router, xn_rows=n_p + LANES, into=xn, row0=n_p)
        xp, xs = moe_ffn2([xp, xs],
                          xn,
                          jnp.concatenate([e_p, e_s[:, :n_s]], axis=1),
                          [g_p, g_s[:, :n_s]],
                          w_gate, w_up, w_down, i)
    stk = lambda lst, j: jnp.stack([r[j] for r in lst])
    return (xp.reshape(bp, t_p, d), xs.reshape(bs, t_s, d),
            jnp.stack(pool_p), jnp.stack(pool_s),
            stk(rows_p, 0), stk(rows_p, 1), stk(rows_p, 2), stk(rows_p, 3),
            stk(rows_s, 0), stk(rows_s, 1), stk(rows_s, 2), stk(rows_s, 3),
            stk(win_p, 0), stk(win_p, 1), stk(win_s, 0), stk(win_s, 1),
            jnp.stack(mem_k_p), jnp.stack(mem_v_p))
```

```python
import functools

import jax
import jax.numpy as jnp
import numpy as np
from jax import lax
from jax.experimental import pallas as pl
from jax.experimental.pallas import tpu as pltpu

D_MODEL = 2048
DEPTH = 4
PAGE_SIZE = 128
HEAD_DIM = 128
ROPE_THETA = 500000.0
ROPE_DIM = HEAD_DIM // 4
MEM_HEADS = 4
MEM_W = MEM_HEADS * HEAD_DIM
TOK_W = D_MODEL - MEM_W
POOL_WINDOWS = (2, 4, 8, 16)
POOL_GROUP = TOK_W // len(POOL_WINDOWS)
POOL_STATE = max(POOL_WINDOWS) - 1
POOL_HALO = POOL_STATE + 1
NSA_HEADS = TOK_W // HEAD_DIM
NSA_KV_HEADS = 2
NSA_GROUP = NSA_HEADS // NSA_KV_HEADS
KV_W = NSA_KV_HEADS * HEAD_DIM
GROUP_W = NSA_GROUP * HEAD_DIM
CMP_BLOCK = 32
CMP_STRIDE = 16
SEL_BLOCK = 64
SEL_TOP = 16
WINDOW = 512
NSA_TOK_IN = NSA_HEADS * HEAD_DIM + 6 * KV_W + 3 * NSA_HEADS
N_EXPERTS = 16
N_GROUPS = 4
EXP_PER_GROUP = N_EXPERTS // N_GROUPS
TOP_K = 2
D_FF = 1024
EPS = 1e-6
NEG = -1e30
SCALE = HEAD_DIM ** -0.5
LANES = 128

V7X_VMEM_BYTES = 64 * 1024 * 1024
VMEM_LIMIT = V7X_VMEM_BYTES * 3 // 4

BF16 = jnp.bfloat16
F32 = jnp.float32
I32 = jnp.int32

NSA_Q0 = 0
NSA_KV0 = NSA_HEADS * HEAD_DIM
NSA_MQ0 = NSA_KV0 + 6 * KV_W
NSA_GL0 = NSA_MQ0 + MEM_W
NSA_ZW = 3840


def _cparams(sem):
    return pltpu.CompilerParams(dimension_semantics=sem, vmem_limit_bytes=VMEM_LIMIT)


def _nt(a, b):
    return lax.dot_general(a, b, (((1,), (1,)), ((), ())), preferred_element_type=F32)


def _tn(a, b):
    return lax.dot_general(a, b, (((0,), (0,)), ((), ())), preferred_element_type=F32)


def _rms(x, g):
    return x * lax.rsqrt(jnp.mean(x * x, axis=-1, keepdims=True) + EPS) * g


def _pick_tile(n, cap, unit):
    if n <= cap:
        return n
    best = None
    for t in range(unit, cap + 1, unit):
        if n % t == 0:
            best = t
    assert best is not None, (n, cap, unit)
    return best


def _proj_kernel(*refs, n_x, norm, residual):
    x_refs, refs = refs[:n_x], refs[n_x:]
    if norm:
        g_ref, refs = refs[0], refs[1:]
    w_ref, refs = refs[0], refs[1:]
    if residual:
        r_ref, refs = refs[0], refs[1:]
    o_ref, xn_ref = refs

    @pl.when(pl.program_id(1) == 0)
    def _():
        off = 0
        for x_ref in x_refs:
            x = x_ref[...].astype(F32)
            if norm:
                x = _rms(x, g_ref[...])
            xn_ref[:, off:off + x.shape[1]] = x.astype(BF16)
            off += x.shape[1]

    y = jnp.dot(xn_ref[...], w_ref[...], preferred_element_type=F32)
    if residual:
        y = y + r_ref[...]
    o_ref[...] = y


def proj(xs, w_bf16, gain=None, residual=None):
    if not isinstance(xs, (list, tuple)):
        xs = [xs]
    assert gain is None or len(xs) == 1
    m = xs[0].shape[0]
    k = sum(x.shape[1] for x in xs)
    n = w_bf16.shape[1]
    tm = _pick_tile(m, 1024, 8)
    tn = _pick_tile(n, 1024, LANES)
    in_specs = [pl.BlockSpec((tm, x.shape[1]), lambda i, j: (i, 0)) for x in xs]
    args = list(xs)
    if gain is not None:
        in_specs.append(pl.BlockSpec((1, k), lambda i, j: (0, 0)))
        args.append(gain.reshape(1, k).astype(F32))
    in_specs.append(pl.BlockSpec((k, tn), lambda i, j: (0, j)))
    args.append(w_bf16)
    if residual is not None:
        in_specs.append(pl.BlockSpec((tm, tn), lambda i, j: (i, j)))
        args.append(residual)
    return pl.pallas_call(
        functools.partial(_proj_kernel, n_x=len(xs), norm=gain is not None, residual=residual is not None),
        grid=(m // tm, n // tn),
        in_specs=in_specs,
        out_specs=pl.BlockSpec((tm, tn), lambda i, j: (i, j)),
        out_shape=jax.ShapeDtypeStruct((m, n), F32),
        scratch_shapes=[pltpu.VMEM((tm, k), BF16)],
        compiler_params=_cparams(("parallel", "arbitrary")),
        name="proj",
    )(*args)


def _head_norm_kernel(x_ref, g_ref, o_ref):
    x = x_ref[...]
    g = g_ref[...]
    n_heads = x.shape[1] // HEAD_DIM
    o_ref[...] = jnp.concatenate(
        [_rms(x[:, h * HEAD_DIM:(h + 1) * HEAD_DIM], g) for h in range(n_heads)], axis=1)


def head_norm(x, col0, width, gain):
    m = x.shape[0]
    assert col0 % width == 0
    return pl.pallas_call(
        _head_norm_kernel,
        grid=(1,),
        in_specs=[pl.BlockSpec((m, width), lambda i: (0, col0 // width)),
                  pl.BlockSpec((1, HEAD_DIM), lambda i: (0, 0))],
        out_specs=pl.BlockSpec((m, width), lambda i: (0, 0)),
        out_shape=jax.ShapeDtypeStruct((m, width), F32),
        compiler_params=_cparams(("arbitrary",)),
        name="head_norm",
    )(x, gain.reshape(1, HEAD_DIM))


def _mem_attn_kernel(q_ref, k_ref, v_ref, g_ref, o_ref):
    q = q_ref[...]
    rows = q.shape[0]
    if rows < 8:
        q = jnp.broadcast_to(q[0:1], (8, q.shape[1]))
    g = g_ref[...]
    outs = []
    for h in range(MEM_HEADS):
        sl = slice(h * HEAD_DIM, (h + 1) * HEAD_DIM)
        qh = _rms(q[:, sl], g).astype(BF16)
        s = _nt(qh, k_ref[:, sl].astype(BF16)) * SCALE
        m = jnp.max(s, axis=-1, keepdims=True)
        e = jnp.exp(s - m)
        p = e / jnp.sum(e, axis=-1, keepdims=True)
        outs.append(jnp.dot(p.astype(BF16), v_ref[:, sl].astype(BF16), preferred_element_type=F32))
    o = jnp.concatenate(outs, axis=1)
    o_ref[...] = o[:rows]


def mem_attention(z, col0, k, v, gain):
    b, t, _ = z.shape
    m = k.shape[1]
    tq = _pick_tile(t, 512, 8)
    assert col0 % MEM_W == 0
    return pl.pallas_call(
        _mem_attn_kernel,
        grid=(b, t // tq),
        in_specs=[pl.BlockSpec((None, tq, MEM_W), lambda bi, i: (bi, i, col0 // MEM_W)),
                  pl.BlockSpec((None, m, MEM_W), lambda bi, i: (bi, 0, 0)),
                  pl.BlockSpec((None, m, MEM_W), lambda bi, i: (bi, 0, 0)),
                  pl.BlockSpec((1, HEAD_DIM), lambda bi, i: (0, 0))],
        out_specs=pl.BlockSpec((None, tq, MEM_W), lambda bi, i: (bi, i, 0)),
        out_shape=jax.ShapeDtypeStruct((b, t, MEM_W), F32),
        compiler_params=_cparams(("parallel", "arbitrary")),
        name="mem_attention",
    )(z, k, v, gain.reshape(1, HEAD_DIM))


def _pool_kernel(u_ref, halo_ref, w_ref, sc_ref, o_ref, *, pos0, zero_first_halo):
    tq = u_ref.shape[0]
    qi = pl.program_id(1)
    u = u_ref[...]
    halo = halo_ref[...]
    if zero_first_halo:
        halo = jnp.where(qi > 0, halo, 0.0)
    pos = pos0 + qi * tq + lax.broadcasted_iota(I32, (tq, 1), 0)
    outs = []
    for g, w in enumerate(POOL_WINDOWS):
        cs = slice(g * POOL_GROUP, (g + 1) * POOL_GROUP)
        ug = u[:, cs]
        acc = jnp.concatenate([halo[:, cs], ug], axis=0)
        span = 1
        while span < w:
            acc = acc[span:] + acc[:-span]
            span *= 2
        ssum = acc[POOL_HALO - (w - 1):POOL_HALO - (w - 1) + tq]
        cnt = jnp.minimum(w, pos + 1).astype(F32)
        d = (ssum / cnt - ug).astype(BF16)
        y = jnp.dot(d, w_ref[g].astype(BF16), preferred_element_type=F32)
        outs.append(y)
    o_ref[...] = jnp.concatenate(outs, axis=1) * sc_ref[...]


def pool_mix(z, halo, w_grp, scale, pos0):
    b, t, _ = z.shape
    tq = _pick_tile(t, 256, 16)
    if halo is None:
        halo_arr = z
        halo_spec = pl.BlockSpec((None, POOL_HALO, TOK_W),
                                 lambda bi, i: (bi, jnp.maximum(i * (tq // POOL_HALO) - 1, 0), 0))
    else:
        assert t == tq
        halo_arr = halo
        halo_spec = pl.BlockSpec((None, POOL_HALO, TOK_W), lambda bi, i: (bi, 0, 0))
    return pl.pallas_call(
        functools.partial(_pool_kernel, pos0=pos0, zero_first_halo=halo is None),
        grid=(b, t // tq),
        in_specs=[pl.BlockSpec((None, tq, TOK_W), lambda bi, i: (bi, i, 0)),
                  halo_spec,
                  pl.BlockSpec(w_grp.shape, lambda bi, i: (0, 0, 0)),
                  pl.BlockSpec((1, TOK_W), lambda bi, i: (0, 0))],
        out_specs=pl.BlockSpec((None, tq, TOK_W), lambda bi, i: (bi, i, 0)),
        out_shape=jax.ShapeDtypeStruct((b, t, TOK_W), F32),
        compiler_params=_cparams(("parallel", "arbitrary")),
        name="pool_mix",
    )(z, halo_arr, w_grp, scale.reshape(1, TOK_W))


def _rope_tables(pos):
    half = ROPE_DIM // 2
    inv = 1.0 / (ROPE_THETA ** (jnp.arange(half, dtype=F32) * 2.0 / ROPE_DIM))
    ang = pos.astype(F32)[:, None] * inv[None, :]
    cos, sin = jnp.cos(ang), jnp.sin(ang)
    t = pos.shape[0]
    rest = HEAD_DIM - ROPE_DIM
    c = jnp.concatenate([cos, cos, jnp.ones((t, rest), F32)], axis=1)
    s_lo = jnp.concatenate([-sin, jnp.zeros((t, HEAD_DIM - half), F32)], axis=1)
    s_hi = jnp.concatenate([jnp.zeros((t, half), F32), sin, jnp.zeros((t, rest), F32)], axis=1)
    return c, s_lo, s_hi


def _nsa_prep_kernel(z_ref, c_ref, slo_ref, shi_ref, qg_ref, kg_ref, bg_ref,
                     q_ref, kc_ref, vc_ref, ks_ref, vs_ref, kw_ref, vw_ref, gate_ref):
    rows = z_ref.shape[0]
    c, slo, shi = c_ref[...], slo_ref[...], shi_ref[...]
    half = ROPE_DIM // 2

    def rope_norm(x, g):
        if rows < 8:
            x = jnp.broadcast_to(x[0:1], (8, HEAD_DIM))
        x = _rms(x, g)
        y = x * c + pltpu.roll(x, HEAD_DIM - half, 1) * slo + pltpu.roll(x, half, 1) * shi
        return y[:rows]

    qg = qg_ref[...]
    q_ref[...] = jnp.concatenate(
        [rope_norm(z_ref[:, NSA_Q0 + h * HEAD_DIM:NSA_Q0 + (h + 1) * HEAD_DIM], qg) for h in range(NSA_HEADS)], axis=1)
    for j, (o_ref, which) in enumerate(((kc_ref, 0), (vc_ref, None), (ks_ref, 1), (vs_ref, None), (kw_ref, 2), (vw_ref, None))):
        c0 = NSA_KV0 + j * KV_W
        if which is None:
            o_ref[...] = z_ref[:, c0:c0 + KV_W]
        else:
            g = kg_ref[which:which + 1, :]
            o_ref[...] = jnp.concatenate(
                [rope_norm(z_ref[:, c0 + h * HEAD_DIM:c0 + (h + 1) * HEAD_DIM], g) for h in range(NSA_KV_HEADS)], axis=1)
    gate_ref[...] = jax.nn.sigmoid(z_ref[:, NSA_GL0:NSA_GL0 + LANES] + bg_ref[...])


def nsa_prep(z, pos, q_g, k_g, b_gate):
    b, t, _ = z.shape
    tq = _pick_tile(t, 256, 8)
    c, slo, shi = _rope_tables(pos)
    if t < 8:
        c, slo, shi = (jnp.broadcast_to(a, (8, HEAD_DIM)) for a in (c, slo, shi))
    tt = max(tq, 8)
    bg = jnp.pad(b_gate.reshape(1, -1), ((0, 0), (0, LANES - b_gate.shape[-1])))
    tab_spec = pl.BlockSpec((tt, HEAD_DIM), lambda bi, i: (i, 0))
    kv_spec = pl.BlockSpec((None, tq, KV_W), lambda bi, i: (bi, i, 0))
    kv_shape = jax.ShapeDtypeStruct((b, t, KV_W), F32)
    return pl.pallas_call(
        _nsa_prep_kernel,
        grid=(b, t // tq),
        in_specs=[pl.BlockSpec((None, tq, NSA_ZW), lambda bi, i: (bi, i, 0)),
                  tab_spec, tab_spec, tab_spec,
                  pl.BlockSpec((1, HEAD_DIM), lambda bi, i: (0, 0)),
                  pl.BlockSpec((3, HEAD_DIM), lambda bi, i: (0, 0)),
                  pl.BlockSpec((1, LANES), lambda bi, i: (0, 0))],
        out_specs=[pl.BlockSpec((None, tq, TOK_W), lambda bi, i: (bi, i, 0))] + [kv_spec] * 6
                  + [pl.BlockSpec((None, tq, LANES), lambda bi, i: (bi, i, 0))],
        out_shape=[jax.ShapeDtypeStruct((b, t, TOK_W), F32)] + [kv_shape] * 6
                  + [jax.ShapeDtypeStruct((b, t, LANES), F32)],
        compiler_params=_cparams(("parallel", "arbitrary")),
        name="nsa_prep",
    )(z, c, slo, shi, q_g.reshape(1, HEAD_DIM), k_g, bg)


def _compress_body(slab, n_chunks, pe_ref, w1_ref, w2_ref, o_ref):
    pe = pe_ref[...]
    pe_lo = jnp.concatenate([pe[r:r + 1] for r in range(CMP_STRIDE)], axis=1)
    pe_hi = jnp.concatenate([pe[CMP_STRIDE + r:CMP_STRIDE + r + 1] for r in range(CMP_STRIDE)], axis=1)
    half_k = CMP_STRIDE * HEAD_DIM
    w_lo = w1_ref[0:half_k, :].astype(BF16)
    w_hi = w1_ref[half_k:2 * half_k, :].astype(BF16)
    w2 = w2_ref[...].astype(BF16)
    for h in range(NSA_KV_HEADS):
        xh = jnp.concatenate([slab(r, h) for r in range(CMP_STRIDE)], axis=1)
        a = jnp.dot((xh + pe_lo).astype(BF16), w_lo, preferred_element_type=F32)
        bb = jnp.dot((xh + pe_hi).astype(BF16), w_hi, preferred_element_type=F32)
        hid = jax.nn.gelu(a + pltpu.roll(bb, n_chunks - 1, 0))
        o_ref[h] = jnp.dot(hid.astype(BF16), w2, preferred_element_type=F32)


def _compress_kernel(x_ref, pe_ref, w1_ref, w2_ref, o_ref):
    def slab(r, h):
        c0 = (r * NSA_KV_HEADS + h) * HEAD_DIM
        return x_ref[:, c0:c0 + HEAD_DIM]
    _compress_body(slab, x_ref.shape[0], pe_ref, w1_ref, w2_ref, o_ref)


def compress_rows(x, pe, w1, w2):
    b, t, _ = x.shape
    n_chunks = t // CMP_STRIDE
    cw = CMP_STRIDE * KV_W
    xc = x.reshape(b, n_chunks, cw)
    return pl.pallas_call(
        _compress_kernel,
        grid=(b,),
        in_specs=[pl.BlockSpec((None, n_chunks, cw), lambda bi: (bi, 0, 0)),
                  pl.BlockSpec(pe.shape, lambda bi: (0, 0)),
                  pl.BlockSpec(w1.shape, lambda bi: (0, 0)),
                  pl.BlockSpec(w2.shape, lambda bi: (0, 0))],
        out_specs=pl.BlockSpec((None, NSA_KV_HEADS, n_chunks, HEAD_DIM), lambda bi: (bi, 0, 0, 0)),
        out_shape=jax.ShapeDtypeStruct((b, NSA_KV_HEADS, n_chunks, HEAD_DIM), F32),
        compiler_params=_cparams(("parallel",)),
        name="compress_rows",
    )(xc, pe, w1, w2)


def _stack_heads(q):
    return jnp.concatenate([q[:, g * HEAD_DIM:(g + 1) * HEAD_DIM] for g in range(NSA_GROUP)], axis=0)


def _unstack_heads(o, tq):
    return jnp.concatenate([o[g * tq:(g + 1) * tq] for g in range(NSA_GROUP)], axis=1)


def _rank_select(imp_t, n_real):
    n_blk = imp_t.shape[0]
    blk = lax.broadcasted_iota(I32, imp_t.shape, 0)
    cnt = jnp.zeros(imp_t.shape, F32)
    for j in range(n_real):
        row = imp_t[j:j + 1, :]
        beats = jnp.where(row > imp_t, 1.0, jnp.where(row == imp_t, jnp.where(blk > j, 1.0, 0.0), 0.0))
        cnt = cnt + beats
    return jnp.where(cnt < SEL_TOP, 1.0, 0.0)


def _cmp_select_kernel(q_ref, kc_ref, vc_ref, cov_ref, o_ref, sel_ref):
    tq = q_ref.shape[0]
    nc = kc_ref.shape[0]
    n_blk = sel_ref.shape[0]
    qi = pl.program_id(2)
    q6 = _stack_heads(q_ref[...]).astype(BF16)
    s = _nt(q6, kc_ref[...].astype(BF16)) * SCALE
    s = s.reshape(NSA_GROUP, tq, nc)
    q_pos = qi * tq + lax.broadcasted_iota(I32, (tq, nc), 0)
    c_last = lax.broadcasted_iota(I32, (tq, nc), 1) * CMP_STRIDE + (CMP_BLOCK - 1)
    valid = (c_last <= q_pos)[None]
    s = jnp.where(valid, s, NEG)
    m = jnp.max(s, axis=-1, keepdims=True)
    e = jnp.where(valid, jnp.exp(s - m), 0.0)
    p = (e / jnp.maximum(jnp.sum(e, axis=-1, keepdims=True), 1e-30)).astype(BF16)
    o = jnp.dot(p.reshape(NSA_GROUP * tq, nc), vc_ref[...].astype(BF16), preferred_element_type=F32)
    o_ref[...] = _unstack_heads(o, tq)
    p_cat = jnp.concatenate([p[g] for g in range(NSA_GROUP)], axis=1)
    imp_t = _nt(cov_ref[...], p_cat)
    blk = lax.broadcasted_iota(I32, (n_blk, tq), 0)
    pos_t = qi * tq + lax.broadcasted_iota(I32, (n_blk, tq), 1)
    cur = pos_t // SEL_BLOCK
    forced = (blk == 0) | (blk == cur) | (blk == cur - 1)
    future = blk * SEL_BLOCK > pos_t
    imp_t = jnp.where(future, -1e9, jnp.where(forced, 1e9, imp_t))
    sel_ref[...] = _rank_select(imp_t, n_blk).astype(BF16)


def _cover_t(nc, n_blk, nc_valid):
    c_start = np.arange(nc) * CMP_STRIDE
    c_last = c_start + CMP_BLOCK - 1
    s_start = np.arange(n_blk) * SEL_BLOCK
    cov = (c_start[None, :] < s_start[:, None] + SEL_BLOCK) & (c_last[None, :] >= s_start[:, None])
    cov = cov & (np.arange(nc)[None, :] < nc_valid)
    return np.tile(cov.astype(np.float32), (1, NSA_GROUP))


def cmp_select_prompt(q, kcmp, vcmp):
    b, t, _ = q.shape
    nc = kcmp.shape[2]
    n_blk = -(-t // SEL_BLOCK)
    tq = _pick_tile(t, 256, LANES)
    cov = jnp.asarray(_cover_t(nc, n_blk, nc - 1), BF16)
    return pl.pallas_call(
        _cmp_select_kernel,
        grid=(b, NSA_KV_HEADS, t // tq),
        in_specs=[pl.BlockSpec((None, tq, GROUP_W), lambda bi, h, i: (bi, i, h)),
                  pl.BlockSpec((None, None, nc, HEAD_DIM), lambda bi, h, i: (bi, h, 0, 0)),
                  pl.BlockSpec((None, None, nc, HEAD_DIM), lambda bi, h, i: (bi, h, 0, 0)),
                  pl.BlockSpec(cov.shape, lambda bi, h, i: (0, 0))],
        out_specs=[pl.BlockSpec((None, tq, GROUP_W), lambda bi, h, i: (bi, i, h)),
                   pl.BlockSpec((None, None, n_blk, tq), lambda bi, h, i: (bi, h, 0, i))],
        out_shape=[jax.ShapeDtypeStruct((b, t, TOK_W), F32),
                   jax.ShapeDtypeStruct((b, NSA_KV_HEADS, n_blk, t), BF16)],
        compiler_params=_cparams(("parallel", "parallel", "arbitrary")),
        name="cmp_select_prompt",
    )(q, kcmp, vcmp, cov)


SEL_TQ = 128
SEL_TK = 512


def _heads_to_lanes_t(q):
    tq = q.shape[0]
    return jnp.concatenate([q[:, g * HEAD_DIM:(g + 1) * HEAD_DIM].T for g in range(NSA_GROUP)], axis=1)


def _lanes_to_heads_t(o_t, tq):
    return jnp.concatenate([o_t[:, g * tq:(g + 1) * tq].T for g in range(NSA_GROUP)], axis=1)


def _selt_kernel(q_ref, k_ref, v_ref, sel_ref, exp_ref, o_ref, mask_ref, m_ref, l_ref, acc_ref):
    t = k_ref.shape[0]
    qi = pl.program_id(2)
    q_t = _heads_to_lanes_t(q_ref[...]).astype(BF16)
    mask_ref[...] = (1.0 - jnp.dot(exp_ref[...], sel_ref[...], preferred_element_type=F32)) * NEG
    m_ref[...] = jnp.full(m_ref.shape, NEG, F32)
    l_ref[...] = jnp.zeros(l_ref.shape, F32)
    acc_ref[...] = jnp.zeros(acc_ref.shape, F32)
    q_lo = qi * SEL_TQ

    def key_tile(j, causal):
        lo, hi = j * SEL_TK, (j + 1) * SEL_TK
        k = k_ref[lo:hi, :].astype(BF16)
        v = v_ref[lo:hi, :].astype(BF16)
        bias = mask_ref[lo:hi, :]
        if causal:
            k_pos = lo + lax.broadcasted_iota(I32, (SEL_TK, SEL_TQ), 0)
            q_pos = q_lo + lax.broadcasted_iota(I32, (SEL_TK, SEL_TQ), 1)
            bias = jnp.where(k_pos <= q_pos, bias, NEG)
        s = jnp.dot(k, q_t, preferred_element_type=F32) * SCALE + jnp.concatenate([bias] * NSA_GROUP, axis=1)
        m_old = m_ref[...]
        m_new = jnp.maximum(m_old, jnp.max(s, axis=0, keepdims=True))
        e = jnp.exp(s - m_new)
        alpha = jnp.exp(m_old - m_new)
        l_ref[...] = alpha * l_ref[...] + jnp.sum(e, axis=0, keepdims=True)
        acc_ref[...] = alpha * acc_ref[...] + _tn(v, e.astype(BF16))
        m_ref[...] = m_new

    for j in range(t // SEL_TK):
        lo, hi = j * SEL_TK, (j + 1) * SEL_TK

        @pl.when(hi - 1 <= q_lo)
        def _():
            key_tile(j, False)

        @pl.when((lo <= q_lo + SEL_TQ - 1) & (hi - 1 > q_lo))
        def _():
            key_tile(j, True)

    o_t = acc_ref[...] / jnp.maximum(l_ref[...], 1e-30)
    o_ref[...] = _lanes_to_heads_t(o_t, SEL_TQ)


def sel_attention_prompt_t(q, ks, vs, sel_t):
    b, t, _ = q.shape
    n_blk = sel_t.shape[2]
    expand = (np.arange(t)[:, None] // SEL_BLOCK == np.arange(n_blk)[None, :]).astype(np.float32)
    expand = jnp.asarray(expand, BF16)
    return pl.pallas_call(
        _selt_kernel,
        grid=(b, NSA_KV_HEADS, t // SEL_TQ),
        in_specs=[pl.BlockSpec((None, SEL_TQ, GROUP_W), lambda bi, h, i: (bi, i, h)),
                  pl.BlockSpec((None, t, HEAD_DIM), lambda bi, h, i: (bi, 0, h)),
                  pl.BlockSpec((None, t, HEAD_DIM), lambda bi, h, i: (bi, 0, h)),
                  pl.BlockSpec((None, None, n_blk, SEL_TQ), lambda bi, h, i: (bi, h, 0, i)),
                  pl.BlockSpec((t, n_blk), lambda bi, h, i: (0, 0))],
        out_specs=pl.BlockSpec((None, SEL_TQ, GROUP_W), lambda bi, h, i: (bi, i, h)),
        out_shape=jax.ShapeDtypeStruct((b, t, TOK_W), F32),
        scratch_shapes=[pltpu.VMEM((t, SEL_TQ), F32),
                        pltpu.VMEM((1, NSA_GROUP * SEL_TQ), F32),
                        pltpu.VMEM((1, NSA_GROUP * SEL_TQ), F32),
                        pltpu.VMEM((HEAD_DIM, NSA_GROUP * SEL_TQ), F32)],
        compiler_params=_cparams(("parallel", "parallel", "arbitrary")),
        name="sel_attention_prompt",
    )(q, ks, vs, sel_t, expand)


WIN_TQ = 128


def _gate_cols(gate, h, j, tq):
    cols = []
    for g in range(NSA_GROUP):
        c = (h * NSA_GROUP + g) * 3 + j
        cols.append(jnp.broadcast_to(gate[:, c:c + 1], (tq, HEAD_DIM)))
    return jnp.concatenate(cols, axis=1)


def _win_combine_kernel(q_ref, k_ref, v_ref, oc_ref, os_ref, gate_ref, o_ref):
    tq = q_ref.shape[0]
    span = WINDOW + tq
    h = pl.program_id(1)
    qi = pl.program_id(2)
    start = pl.multiple_of(jnp.maximum(qi * tq - WINDOW, 0), tq)
    q_t = _heads_to_lanes_t(q_ref[...]).astype(BF16)
    k = k_ref[pl.ds(start, span), :].astype(BF16)
    v = v_ref[pl.ds(start, span), :].astype(BF16)
    diff = (qi * tq + lax.broadcasted_iota(I32, (span, tq), 1)) - (start + lax.broadcasted_iota(I32, (span, tq), 0))
    bias = jnp.where((diff >= 0) & (diff < WINDOW), 0.0, NEG)
    s = jnp.dot(k, q_t, preferred_element_type=F32) * SCALE + jnp.concatenate([bias] * NSA_GROUP, axis=1)
    e = jnp.exp(s - jnp.max(s, axis=0, keepdims=True))
    p = e / jnp.maximum(jnp.sum(e, axis=0, keepdims=True), 1e-30)
    o_win = _lanes_to_heads_t(_tn(v, p.astype(BF16)), tq)
    gate = gate_ref[...]
    for hh in range(NSA_KV_HEADS):
        @pl.when(h == hh)
        def _():
            o_ref[...] = (_gate_cols(gate, hh, 0, tq) * oc_ref[...] + _gate_cols(gate, hh, 1, tq) * os_ref[...]
                          + _gate_cols(gate, hh, 2, tq) * o_win)


def win_combine_prompt(q, kw, vw, o_cmp, o_sel, gate):
    b, t, _ = q.shape
    assert t >= WINDOW + WIN_TQ
    qspec = pl.BlockSpec((None, WIN_TQ, GROUP_W), lambda bi, h, i: (bi, i, h))
    kspec = pl.BlockSpec((None, t, HEAD_DIM), lambda bi, h, i: (bi, 0, h))
    return pl.pallas_call(
        _win_combine_kernel,
        grid=(b, NSA_KV_HEADS, t // WIN_TQ),
        in_specs=[qspec, kspec, kspec, qspec, qspec,
                  pl.BlockSpec((None, WIN_TQ, LANES), lambda bi, h, i: (bi, i, 0))],
        out_specs=qspec,
        out_shape=jax.ShapeDtypeStruct((b, t, TOK_W), F32),
        compiler_params=_cparams(("parallel", "parallel", "arbitrary")),
        name="win_combine_prompt",
    )(q, kw, vw, o_cmp, o_sel, gate)


def _router_kernel(x_ref, g_ref, wt_ref, b_ref, xn_ref, e_ref, gt_ref):
    xn = _rms(x_ref[...], g_ref[...])
    xn_ref[...] = xn
    lt = _nt(wt_ref[...], xn.astype(BF16))
    ex = jnp.exp(lt - jnp.max(lt, axis=0, keepdims=True))
    aff = ex / jnp.sum(ex, axis=0, keepdims=True)
    sel = aff + b_ref[...]
    row = lambda a, r: a[r:r + 1, :]

    best, g_idx = None, None
    for g in range(N_GROUPS):
        a, b, c, d = (row(sel, g * EXP_PER_GROUP + j) for j in range(EXP_PER_GROUP))
        hi1, lo1, hi2, lo2 = jnp.maximum(a, b), jnp.minimum(a, b), jnp.maximum(c, d), jnp.minimum(c, d)
        score = jnp.maximum(hi1, hi2) + jnp.maximum(jnp.minimum(hi1, hi2), jnp.maximum(lo1, lo2))
        if g == 0:
            best, g_idx = score, jnp.zeros(score.shape, I32)
        else:
            g_idx = jnp.where(score > best, g, g_idx)
            best = jnp.maximum(best, score)

    def in_group(a, j):
        out = row(a, j)
        for g in range(1, N_GROUPS):
            out = jnp.where(g_idx == g, row(a, g * EXP_PER_GROUP + j), out)
        return out

    v = [in_group(sel, j) for j in range(EXP_PER_GROUP)]
    af = [in_group(aff, j) for j in range(EXP_PER_GROUP)]

    def first_max(vals):
        m = functools.reduce(jnp.maximum, vals)
        loc = jnp.full(m.shape, EXP_PER_GROUP - 1, I32)
        for j in range(EXP_PER_GROUP - 2, -1, -1):
            loc = jnp.where(vals[j] == m, j, loc)
        return loc

    l1 = first_max(v)
    l2 = first_max([jnp.where(l1 == j, -jnp.inf, v[j]) for j in range(EXP_PER_GROUP)])
    pick = lambda loc: functools.reduce(lambda acc, j: jnp.where(loc == j, af[j], acc), range(1, EXP_PER_GROUP), af[0])
    a1, a2 = pick(l1), pick(l2)
    tot = a1 + a2
    e_ref[...] = jnp.concatenate([g_idx * EXP_PER_GROUP + l1, g_idx * EXP_PER_GROUP + l2], axis=0)
    gt_ref[...] = jnp.concatenate([a1 / tot, a2 / tot], axis=0)


def _router_into_kernel(x_ref, g_ref, wt_ref, b_ref, xn_all_ref, xn_ref, e_ref, gt_ref):
    del xn_all_ref
    _router_kernel(x_ref, g_ref, wt_ref, b_ref, xn_ref, e_ref, gt_ref)


def router(x, gain, w_router, b_router, xn_rows=None, into=None, row0=0):
    n, d = x.shape
    tm = _pick_tile(n, 1024, LANES)
    assert row0 % tm == 0
    xn_rows = n if xn_rows is None else xn_rows
    args = [x, gain.reshape(1, d), w_router.T.astype(BF16), b_router.reshape(N_EXPERTS, 1).astype(F32)]
    in_specs = [pl.BlockSpec((tm, d), lambda i: (i, 0)),
                pl.BlockSpec((1, d), lambda i: (0, 0)),
                pl.BlockSpec((N_EXPERTS, d), lambda i: (0, 0)),
                pl.BlockSpec((N_EXPERTS, 1), lambda i: (0, 0))]
    body, aliases = _router_kernel, {}
    if into is not None:
        assert into.shape == (xn_rows, d)
        args.append(into)
        in_specs.append(pl.BlockSpec(memory_space=pl.ANY))
        body, aliases = _router_into_kernel, {len(args) - 1: 0}
    return pl.pallas_call(
        body,
        grid=(n // tm,),
        in_specs=in_specs,
        out_specs=[pl.BlockSpec((tm, d), lambda i: (row0 // tm + i, 0)),
                   pl.BlockSpec((TOP_K, tm), lambda i: (0, i)),
                   pl.BlockSpec((TOP_K, tm), lambda i: (0, i))],
        out_shape=[jax.ShapeDtypeStruct((xn_rows, d), F32),
                   jax.ShapeDtypeStruct((TOP_K, n), I32),
                   jax.ShapeDtypeStruct((TOP_K, n), F32)],
        input_output_aliases=aliases,
        compiler_params=_cparams(("parallel",)),
        name="router",
    )(*args)


MOE_NF = 4
MOE_TF = D_FF // MOE_NF
MOE2_TM = 1024
MOE2_GROUP = 64
MOE2_CHUNK = 256
MOE2_GPS = MOE2_TM // MOE_NF // MOE2_GROUP
MOE2_VMEM_LIMIT = V7X_VMEM_BYTES * 7 // 8


def _moe2_kernel(te_ref, nu_ref, nv_ref, base_ref, order_ref, x_hbm, wg_ref, wu_ref, wd_ref, y_hbm,
                 xbuf, xb, ybuf, gsem, ssem, *, n_tok):
    n_items = n_tok * TOP_K

    def slot_item(tile, rr):
        return order_ref[jnp.minimum(base_ref[tile] + rr, n_items - 1)]
    i = pl.program_id(0)
    f = pl.program_id(1)
    n_used = nu_ref[0]
    cur = i % 2
    oth = 1 - cur
    n_groups = MOE2_TM // MOE2_GROUP

    def gather_group(tile, g):
        for r in range(MOE2_GROUP):
            rr = g * MOE2_GROUP + r
            item = slot_item(tile, rr)
            row = jnp.where(item >= n_tok, item - n_tok, item)
            pltpu.make_async_copy(x_hbm.at[pl.ds(row, 1), :], xbuf.at[pl.ds(rr, 1), :], gsem).start()

    def scatter_group(tile, buf, g):
        for r in range(MOE2_GROUP):
            rr = g * MOE2_GROUP + r
            row = jnp.where(rr < nv_ref[tile], slot_item(tile, rr), n_items + tile * MOE2_TM + rr)
            pltpu.make_async_copy(ybuf.at[buf, pl.ds(rr, 1), :], y_hbm.at[pl.ds(row, 1), :], ssem.at[buf]).start()

    def wait_gather_group():
        pltpu.make_async_copy(x_hbm.at[pl.ds(0, MOE2_GROUP), :], xbuf.at[pl.ds(0, MOE2_GROUP), :], gsem).wait()

    def wait_scatter_group(buf):
        pltpu.make_async_copy(ybuf.at[buf, pl.ds(0, MOE2_GROUP), :], y_hbm.at[pl.ds(0, MOE2_GROUP), :], ssem.at[buf]).wait()

    @pl.when((i == 0) & (f == 0))
    def _():
        xbuf[...] = jnp.zeros(xbuf.shape, F32)

        def body(g, carry):
            @pl.when(g * MOE2_GROUP < nv_ref[0])
            def _():
                gather_group(0, g)
            return carry
        lax.fori_loop(0, n_groups, body, 0)

    @pl.when(i < n_used)
    def _():
        n_valid = nv_ref[i]

        @pl.when(f == 0)
        def _():
            for g in range(n_groups):
                @pl.when(g * MOE2_GROUP < n_valid)
                def _():
                    wait_gather_group()
            xb[...] = xbuf[...].astype(BF16)

        nv_next = nv_ref[i + 1]
        nv_prev = nv_ref[jnp.maximum(i - 1, 0)]
        for gi in range(MOE2_GPS):
            g = f * MOE2_GPS + gi

            @pl.when((i + 1 < n_used) & (g * MOE2_GROUP < nv_next))
            def _():
                gather_group(i + 1, g)

            @pl.when((i > 0) & (g * MOE2_GROUP < nv_prev))
            def _():
                scatter_group(i - 1, oth, g)

        for c in range(MOE2_TM // MOE2_CHUNK):
            lo, hi = c * MOE2_CHUNK, (c + 1) * MOE2_CHUNK

            @pl.when(lo < n_valid)
            def _():
                x = xb[lo:hi, :]
                a = jnp.dot(x, wg_ref[...].astype(BF16), preferred_element_type=F32)
                u = jnp.dot(x, wu_ref[...].astype(BF16), preferred_element_type=F32)
                h = (a * jax.nn.sigmoid(a) * u).astype(BF16)
                y = jnp.dot(h, wd_ref[...].astype(BF16), preferred_element_type=F32)

                @pl.when(f == 0)
                def _():
                    ybuf[cur, lo:hi, :] = y

                @pl.when(f > 0)
                def _():
                    ybuf[cur, lo:hi, :] += y

        @pl.when(f == MOE_NF - 1)
        def _():
            for g in range(n_groups):
                @pl.when((i > 0) & (g * MOE2_GROUP < nv_prev))
                def _():
                    wait_scatter_group(oth)

    @pl.when((i == n_used) & (f == 0))
    def _():
        nv_last = nv_ref[i - 1]

        def start(g, carry):
            @pl.when(g * MOE2_GROUP < nv_last)
            def _():
                scatter_group(i - 1, oth, g)
            return carry
        lax.fori_loop(0, n_groups, start, 0)

        def wait(g, carry):
            @pl.when(g * MOE2_GROUP < nv_last)
            def _():
                wait_scatter_group(oth)
            return carry
        lax.fori_loop(0, n_groups, wait, 0)


def moe_experts2(xn, n_tok, tile_expert, n_used, n_valid, base, order, n_out_rows, w_gate, w_up, w_down, layer):
    n, d = xn.shape
    n_tiles = tile_expert.shape[0]

    def tile(i, nu):
        return jnp.minimum(i, nu[0] - 1)

    def fcol(i, f, nu):
        return jnp.where(i < nu[0], f, MOE_NF - 1)

    grid_spec = pltpu.PrefetchScalarGridSpec(
        num_scalar_prefetch=5,
        grid=(n_tiles, MOE_NF),
        in_specs=[
            pl.BlockSpec(memory_space=pl.ANY),
            pl.BlockSpec((None, None, d, MOE_TF), lambda i, f, te, nu, nv, s, t: (layer, te[tile(i, nu)], 0, fcol(i, f, nu))),
            pl.BlockSpec((None, None, d, MOE_TF), lambda i, f, te, nu, nv, s, t: (layer, te[tile(i, nu)], 0, fcol(i, f, nu))),
            pl.BlockSpec((None, None, MOE_TF, d), lambda i, f, te, nu, nv, s, t: (layer, te[tile(i, nu)], fcol(i, f, nu), 0)),
        ],
        out_specs=pl.BlockSpec(memory_space=pl.ANY),
        scratch_shapes=[pltpu.VMEM((MOE2_TM, d), F32),
                        pltpu.VMEM((MOE2_TM, d), BF16),
                        pltpu.VMEM((2, MOE2_TM, d), F32),
                        pltpu.SemaphoreType.DMA(()),
                        pltpu.SemaphoreType.DMA((2,))],
    )
    return pl.pallas_call(
        functools.partial(_moe2_kernel, n_tok=n_tok),
        grid_spec=grid_spec,
        out_shape=jax.ShapeDtypeStruct((n_out_rows, d), F32),
        compiler_params=pltpu.CompilerParams(dimension_semantics=("arbitrary", "arbitrary"),
                                             vmem_limit_bytes=MOE2_VMEM_LIMIT),
        name="moe_experts",
    )(tile_expert, n_used, n_valid, base, order, xn, w_gate, w_up, w_down)


def moe_ffn2(xs, xn, experts, gates, w_gate, w_up, w_down, layer):
    n, d = experts.shape[1], xn.shape[1]
    tm = MOE2_TM
    n_items = n * TOP_K
    n_tiles = -(-n_items // tm) + N_EXPERTS
    n_slots = n_tiles * tm
    flat_e = experts.reshape(-1)
    sizes = jnp.sum((flat_e[:, None] == jnp.arange(N_EXPERTS, dtype=I32)[None, :]).astype(I32), axis=0)
    padded = -(-sizes // tm) * tm
    ends_p = jnp.cumsum(padded)
    starts_p = ends_p - padded
    starts = jnp.cumsum(sizes) - sizes
    order = jnp.argsort(flat_e).astype(I32)
    tile_start = jnp.arange(n_tiles, dtype=I32) * tm
    tile_expert = jnp.minimum(jnp.searchsorted(ends_p, tile_start, side='right'), N_EXPERTS - 1).astype(I32)
    n_used = (ends_p[-1:] // tm).astype(I32)
    n_valid = jnp.clip(sizes[tile_expert] - (tile_start - starts_p[tile_expert]), 0, tm).astype(I32)
    base = (starts[tile_expert] + tile_start - starts_p[tile_expert]).astype(I32)
    y = moe_experts2(xn, n, tile_expert, n_used, n_valid, base, order, n_items + n_slots,
                     w_gate, w_up, w_down, layer)
    outs, r0 = [], 0
    for x, g in zip(xs, gates):
        r1 = r0 + x.shape[0]
        outs.append(x + y[r0:r1] * g[0][:, None] + y[n + r0:n + r1] * g[1][:, None])
        r0 = r1
    return outs


GROUP_ROWS = 8


def _pick_head(ref, h):
    out = ref[:, 0, :]
    for hh in range(1, NSA_KV_HEADS):
        out = jnp.where(h == hh, ref[:, hh, :], out)
    return out


def _group_rows(q_ref, h):
    rows = [q_ref[:, (h * NSA_GROUP + g) * HEAD_DIM:(h * NSA_GROUP + g + 1) * HEAD_DIM] for g in range(NSA_GROUP)]
    rows.append(jnp.zeros((GROUP_ROWS - NSA_GROUP, HEAD_DIM), F32))
    return jnp.concatenate(rows, axis=0)


def _compress_paged_kernel(pt_ref, cache_hbm, pe_ref, w1_ref, w2_ref, o_ref, xbuf, sem, *, layer, n_pages):
    b = pl.program_id(0)
    n_chunks = n_pages * (PAGE_SIZE // CMP_STRIDE)

    def page_copy(p):
        page = pt_ref[b * n_pages + p]
        return pltpu.make_async_copy(cache_hbm.at[layer, page], xbuf.at[pl.ds(p * PAGE_SIZE, PAGE_SIZE)], sem)

    def start(p, carry):
        page_copy(p).start()
        return carry

    def wait(p, carry):
        page_copy(p).wait()
        return carry

    lax.fori_loop(0, n_pages, start, 0)
    lax.fori_loop(0, n_pages, wait, 0)

    def slab(r, h):
        return xbuf[pl.ds(r, n_chunks, stride=CMP_STRIDE), h, :]
    _compress_body(slab, n_chunks, pe_ref, w1_ref, w2_ref, o_ref)


def compress_paged(cache, layer, page_table, pe, w1, w2):
    b, n_pages = page_table.shape
    n_chunks = n_pages * (PAGE_SIZE // CMP_STRIDE)
    grid_spec = pltpu.PrefetchScalarGridSpec(
        num_scalar_prefetch=1,
        grid=(b,),
        in_specs=[pl.BlockSpec(memory_space=pl.ANY),
                  pl.BlockSpec(pe.shape, lambda bi, pt: (0, 0)),
                  pl.BlockSpec(w1.shape, lambda bi, pt: (0, 0)),
                  pl.BlockSpec(w2.shape, lambda bi, pt: (0, 0))],
        out_specs=pl.BlockSpec((None, NSA_KV_HEADS, n_chunks, HEAD_DIM), lambda bi, pt: (bi, 0, 0, 0)),
        scratch_shapes=[pltpu.VMEM((n_pages * PAGE_SIZE, NSA_KV_HEADS, HEAD_DIM), F32), pltpu.SemaphoreType.DMA(())],
    )
    return pl.pallas_call(
        functools.partial(_compress_paged_kernel, layer=layer, n_pages=n_pages),
        grid_spec=grid_spec,
        out_shape=jax.ShapeDtypeStruct((b, NSA_KV_HEADS, n_chunks, HEAD_DIM), F32),
        compiler_params=_cparams(("arbitrary",)),
        name="compress_paged",
    )(page_table.reshape(-1), cache, pe, w1, w2)


def _cmp_select_sample_kernel(q_ref, kc_ref, vc_ref, cov_ref, o_ref, idx_ref, q8_ref, *, q_pos, n_blk):
    nc = kc_ref.shape[1]
    nb_pad = cov_ref.shape[1]
    blk = lax.broadcasted_iota(I32, (1, nb_pad), 1)
    cur = q_pos // SEL_BLOCK
    forced = (blk == 0) | (blk == cur) | (blk == cur - 1)
    future = (blk * SEL_BLOCK > q_pos) | (blk >= n_blk)
    c_last = lax.broadcasted_iota(I32, (GROUP_ROWS, nc), 1) * CMP_STRIDE + (CMP_BLOCK - 1)
    valid = (c_last <= q_pos) & (lax.broadcasted_iota(I32, (GROUP_ROWS, nc), 0) < NSA_GROUP)
    for h in range(NSA_KV_HEADS):
        q8f = _group_rows(q_ref, h)
        q8_ref[h] = q8f
        q8 = q8f.astype(BF16)
        s = jnp.where(valid, _nt(q8, kc_ref[h].astype(BF16)) * SCALE, NEG)
        m = jnp.max(s, axis=-1, keepdims=True)
        e = jnp.where(valid, jnp.exp(s - m), 0.0)
        p = (e / jnp.maximum(jnp.sum(e, axis=-1, keepdims=True), 1e-30)).astype(BF16)
        o_ref[h] = jnp.dot(p, vc_ref[h].astype(BF16), preferred_element_type=F32)
        imp = jnp.sum(jnp.dot(p, cov_ref[...], preferred_element_type=F32), axis=0, keepdims=True)
        imp = jnp.where(future, -1e9, jnp.where(forced, 1e9, imp))
        jj = lax.broadcasted_iota(I32, (nb_pad, nb_pad), 0)
        ss = lax.broadcasted_iota(I32, (nb_pad, nb_pad), 1)
        row = jnp.broadcast_to(imp, (nb_pad, nb_pad))
        col = jnp.sum(jnp.where(jj == ss, row, 0.0), axis=1, keepdims=True)
        beats = jnp.where(col > row, 1.0, jnp.where(col == row, jnp.where(jj < ss, 1.0, 0.0), 0.0))
        rank = jnp.sum(beats, axis=0, keepdims=True)
        want = lax.broadcasted_iota(I32, (SEL_TOP, nb_pad), 0).astype(F32)
        lane = lax.broadcasted_iota(I32, (SEL_TOP, nb_pad), 1).astype(F32)
        idx = jnp.sum(jnp.where(jnp.broadcast_to(rank, (SEL_TOP, nb_pad)) == want, lane, 0.0), axis=1, keepdims=True)
        idx_ref[h] = idx.astype(I32)


def cmp_select_sample(q, kcmp, vcmp, q_pos):
    b = q.shape[0]
    nc = kcmp.shape[2]
    n_blk = q_pos // SEL_BLOCK + 1
    nb_pad = -(-n_blk // LANES) * LANES
    cov = _cover_t(nc, nb_pad, nc - 1)[:, :nc].T
    cov = jnp.asarray(cov, BF16)
    return pl.pallas_call(
        functools.partial(_cmp_select_sample_kernel, q_pos=q_pos, n_blk=n_blk),
        grid=(b,),
        in_specs=[pl.BlockSpec((None, 1, TOK_W), lambda bi: (bi, 0, 0)),
                  pl.BlockSpec((None, NSA_KV_HEADS, nc, HEAD_DIM), lambda bi: (bi, 0, 0, 0)),
                  pl.BlockSpec((None, NSA_KV_HEADS, nc, HEAD_DIM), lambda bi: (bi, 0, 0, 0)),
                  pl.BlockSpec(cov.shape, lambda bi: (0, 0))],
        out_specs=[pl.BlockSpec((None, NSA_KV_HEADS, GROUP_ROWS, HEAD_DIM), lambda bi: (bi, 0, 0, 0)),
                   pl.BlockSpec((None, NSA_KV_HEADS, SEL_TOP, 1), lambda bi: (bi, 0, 0, 0)),
                   pl.BlockSpec((None, NSA_KV_HEADS, GROUP_ROWS, HEAD_DIM), lambda bi: (bi, 0, 0, 0))],
        out_shape=[jax.ShapeDtypeStruct((b, NSA_KV_HEADS, GROUP_ROWS, HEAD_DIM), F32),
                   jax.ShapeDtypeStruct((b, NSA_KV_HEADS, SEL_TOP, 1), I32),
                   jax.ShapeDtypeStruct((b, NSA_KV_HEADS, GROUP_ROWS, HEAD_DIM), F32)],
        compiler_params=_cparams(("parallel",)),
        name="cmp_select_sample",
    )(q, kcmp, vcmp, cov)


def _dot_new_row(q8, k_row):
    a = q8.astype(BF16).astype(F32)
    b = k_row.astype(BF16).astype(F32)
    return jnp.sum(a * b, axis=-1, keepdims=True)


SEL_S_BLOCKS = 4


def _sel_sample_kernel(pt_ref, ix_ref, q_ref, *refs, n_cache_blk):
    k_refs, v_refs = refs[:SEL_S_BLOCKS], refs[SEL_S_BLOCKS:2 * SEL_S_BLOCKS]
    kn_ref, vn_ref, o_ref, m_ref, l_ref, acc_ref = refs[2 * SEL_S_BLOCKS:]
    b, h, r = pl.program_id(0), pl.program_id(1), pl.program_id(2)

    @pl.when(r == 0)
    def _():
        m_ref[...] = jnp.full(m_ref.shape, NEG, F32)
        l_ref[...] = jnp.zeros(l_ref.shape, F32)
        acc_ref[...] = jnp.zeros(acc_ref.shape, F32)

    q8 = q_ref[...]
    n_keys = SEL_S_BLOCKS * SEL_BLOCK
    key_blk = lax.broadcasted_iota(I32, (GROUP_ROWS, n_keys), 1) // SEL_BLOCK
    in_cache = jnp.zeros((GROUP_ROWS, n_keys), I32)
    for j in range(SEL_S_BLOCKS):
        flag = (ix_ref[(b * NSA_KV_HEADS + h) * SEL_TOP + r * SEL_S_BLOCKS + j] < n_cache_blk).astype(I32)
        in_cache = jnp.where(key_blk == j, flag, in_cache)
    valid = (lax.broadcasted_iota(I32, (GROUP_ROWS, n_keys), 0) < NSA_GROUP) & (in_cache > 0)
    k_blk = jnp.concatenate([_pick_head(k_ref, h) for k_ref in k_refs], axis=0).astype(BF16)
    v_blk = jnp.concatenate([_pick_head(v_ref, h) for v_ref in v_refs], axis=0).astype(BF16)
    s = jnp.where(valid, _nt(q8.astype(BF16), k_blk) * SCALE, NEG)
    m_old = m_ref[...]
    m_new = jnp.maximum(m_old, jnp.max(s, axis=-1, keepdims=True))
    e = jnp.where(valid, jnp.exp(s - m_new), 0.0)
    alpha = jnp.exp(m_old - m_new)
    l_ref[...] = alpha * l_ref[...] + jnp.sum(e, axis=-1, keepdims=True)
    acc_ref[...] = alpha * acc_ref[...] + jnp.dot(e.astype(BF16), v_blk, preferred_element_type=F32)
    m_ref[...] = m_new

    @pl.when(r == SEL_TOP // SEL_S_BLOCKS - 1)
    def _():
        s_new = _dot_new_row(q8, kn_ref[...]) * SCALE
        m_old = m_ref[...]
        m_fin = jnp.maximum(m_old, s_new)
        alpha = jnp.exp(m_old - m_fin)
        e_new = jnp.exp(s_new - m_fin)
        l_fin = alpha * l_ref[...] + e_new
        acc = alpha * acc_ref[...] + e_new.astype(BF16).astype(F32) * vn_ref[...].astype(BF16).astype(F32)
        o_ref[...] = acc / jnp.maximum(l_fin, 1e-30)


def sel_attention_sample(q8, idx, cache_k, cache_v, layer, page_table, k_new, v_new):
    b, n_pages = page_table.shape
    per_page = PAGE_SIZE // SEL_BLOCK
    n_cache_blk = n_pages * per_page

    def blk_index(j):
        def index(bi, h, r, pt, ix):
            s = jnp.minimum(ix[(bi * NSA_KV_HEADS + h) * SEL_TOP + r * SEL_S_BLOCKS + j], n_cache_blk - 1)
            return (layer, pt[bi * n_pages + s // per_page], s % per_page, 0, 0)
        return index

    blk_specs = [pl.BlockSpec((None, None, SEL_BLOCK, NSA_KV_HEADS, HEAD_DIM), blk_index(j)) for j in range(SEL_S_BLOCKS)]
    grid_spec = pltpu.PrefetchScalarGridSpec(
        num_scalar_prefetch=2,
        grid=(b, NSA_KV_HEADS, SEL_TOP // SEL_S_BLOCKS),
        in_specs=[pl.BlockSpec((None, None, GROUP_ROWS, HEAD_DIM), lambda bi, h, r, pt, ix: (bi, h, 0, 0))]
                 + blk_specs + blk_specs
                 + [pl.BlockSpec((None, 1, HEAD_DIM), lambda bi, h, r, pt, ix: (bi, 0, h)),
                    pl.BlockSpec((None, 1, HEAD_DIM), lambda bi, h, r, pt, ix: (bi, 0, h))],
        out_specs=pl.BlockSpec((None, None, GROUP_ROWS, HEAD_DIM), lambda bi, h, r, pt, ix: (bi, h, 0, 0)),
        scratch_shapes=[pltpu.VMEM((GROUP_ROWS, 1), F32), pltpu.VMEM((GROUP_ROWS, 1), F32),
                        pltpu.VMEM((GROUP_ROWS, HEAD_DIM), F32)],
    )
    return pl.pallas_call(
        functools.partial(_sel_sample_kernel, n_cache_blk=n_cache_blk),
        grid_spec=grid_spec,
        out_shape=jax.ShapeDtypeStruct((b, NSA_KV_HEADS, GROUP_ROWS, HEAD_DIM), F32),
        compiler_params=_cparams(("arbitrary", "arbitrary", "arbitrary")),
        name="sel_attention_sample",
    )(page_table.reshape(-1), idx.reshape(-1), q8, *([cache_k] * SEL_S_BLOCKS), *([cache_v] * SEL_S_BLOCKS), k_new, v_new)


def _win_sample_kernel(q_ref, k_ref, v_ref, kn_ref, vn_ref, oc_ref, os_ref, gate_ref, o_ref):
    h = pl.program_id(1)
    wb = k_ref.shape[0]
    k_win = _pick_head(k_ref, h).astype(BF16)
    v_win = _pick_head(v_ref, h).astype(BF16)
    q8 = q_ref[...]
    col = lax.broadcasted_iota(I32, (GROUP_ROWS, wb), 1)
    valid = col >= 1
    s = jnp.where(valid, _nt(q8.astype(BF16), k_win) * SCALE, NEG)
    s_new = _dot_new_row(q8, kn_ref[...]) * SCALE
    m = jnp.maximum(jnp.max(s, axis=-1, keepdims=True), s_new)
    e = jnp.where(valid, jnp.exp(s - m), 0.0)
    e_new = jnp.exp(s_new - m)
    den = jnp.maximum(jnp.sum(e, axis=-1, keepdims=True) + e_new, 1e-30)
    p = (e / den).astype(BF16)
    p_new = (e_new / den).astype(BF16).astype(F32)
    o_win = jnp.dot(p, v_win, preferred_element_type=F32) + p_new * vn_ref[...].astype(BF16).astype(F32)
    gate = jnp.broadcast_to(gate_ref[...], (GROUP_ROWS, LANES))
    lane = lax.broadcasted_iota(I32, (GROUP_ROWS, LANES), 1)
    head = h * NSA_GROUP + lax.broadcasted_iota(I32, (GROUP_ROWS, LANES), 0)
    g = [jnp.sum(jnp.where(lane == head * 3 + j, gate, 0.0), axis=-1, keepdims=True) for j in range(3)]
    o_ref[...] = g[0] * oc_ref[...] + g[1] * os_ref[...] + g[2] * o_win


def win_combine_sample(q8, win_k, win_v, layer, k_new, v_new, o_cmp, o_sel, gate):
    b = q8.shape[0]
    wb = win_k.shape[2]
    assert wb == WINDOW
    gspec = pl.BlockSpec((None, None, GROUP_ROWS, HEAD_DIM), lambda bi, h: (bi, h, 0, 0))
    wspec = pl.BlockSpec((None, None, wb, NSA_KV_HEADS, HEAD_DIM), lambda bi, h: (layer, bi, 0, 0, 0))
    nspec = pl.BlockSpec((None, 1, HEAD_DIM), lambda bi, h: (bi, 0, h))
    return pl.pallas_call(
        _win_sample_kernel,
        grid=(b, NSA_KV_HEADS),
        in_specs=[gspec, wspec, wspec, nspec, nspec, gspec, gspec,
                  pl.BlockSpec((None, 1, LANES), lambda bi, h: (bi, 0, 0))],
        out_specs=gspec,
        out_shape=jax.ShapeDtypeStruct((b, NSA_KV_HEADS, GROUP_ROWS, HEAD_DIM), F32),
        compiler_params=_cparams(("parallel", "arbitrary")),
        name="win_combine_sample",
    )(q8, win_k, win_v, k_new, v_new, o_cmp, o_sel, gate)


def nsa_sample_step(prep, layer, caches, win_k, win_v, page_table, pe, w1, w2):
    q, kc, vc, ks, vs, kw, vw, gate = prep
    b = q.shape[0]
    past_len = page_table.shape[1] * PAGE_SIZE
    cache_ck, cache_cv, cache_sk, cache_sv = caches
    kcmp = compress_paged(cache_ck, layer, page_table, pe[0], w1[0], w2[0])
    vcmp = compress_paged(cache_cv, layer, page_table, pe[1], w1[1], w2[1])
    o_cmp, idx, q8 = cmp_select_sample(q, kcmp, vcmp, past_len)
    o_sel = sel_attention_sample(q8, idx, cache_sk, cache_sv, layer, page_table, ks, vs)
    o = win_combine_sample(q8, win_k, win_v, layer, kw, vw, o_cmp, o_sel, gate)
    return o[:, :, :NSA_GROUP].reshape(b, 1, TOK_W)


def _reorder_nsa_weight(w):
    n_gl = 3 * NSA_HEADS
    parts = [w[:, :NSA_MQ0], w[:, NSA_MQ0 + n_gl:NSA_MQ0 + n_gl + MEM_W], w[:, NSA_MQ0:NSA_MQ0 + n_gl]]
    wr = jnp.concatenate(parts, axis=1)
    return jnp.pad(wr, ((0, 0), (0, NSA_ZW - wr.shape[1]))).astype(BF16)


def kernel(x_prompt, x_sample, state_pool, cache_cmp_k, cache_cmp_v, cache_sel_k, cache_sel_v, state_win_k, state_win_v, cache_mem_k, cache_mem_v, page_table, mem_prompt, norm_mix_g, norm_ffn_g, norm_mem_g, w_mem_kv, mem_q_norm_g, mem_k_norm_g, w_in_pool, w_pool_grp, pool_scale, w_out_pool, w_in_nsa, b_gate, nsa_q_norm_g, nsa_k_norm_g, cmp_pe, cmp_w1, cmp_w2, w_out_nsa, w_router, b_router, w_gate, w_up, w_down):
    bp, t_p, d = x_prompt.shape
    bs, t_s, _ = x_sample.shape
    assert t_s == 1
    n_p, n_s = bp * t_p, bs * t_s
    m_len = mem_prompt.shape[1]
    past_len = page_table.shape[1] * PAGE_SIZE
    pos_p = jnp.arange(t_p, dtype=I32)
    pos_s = past_len + jnp.arange(t_s, dtype=I32)
    xp = x_prompt.reshape(n_p, d)
    xs = x_sample.reshape(n_s, d)
    mem_flat = mem_prompt.reshape(-1, d)
    pool_p, pool_s, rows_p, rows_s, win_p, win_s, mem_k_p, mem_v_p = [], [], [], [], [], [], [], []
    for i in range(DEPTH):
        li = i // 2
        kv = proj(mem_flat, w_mem_kv[i].astype(BF16), gain=norm_mem_g[i])
        mk = head_norm(kv, 0, MEM_W, mem_k_norm_g[i]).reshape(bp, m_len, MEM_W)
        mv = kv[:, MEM_W:].reshape(bp, m_len, MEM_W)
        mem_k_p.append(mk.reshape(bp, m_len, MEM_HEADS, HEAD_DIM))
        mem_v_p.append(mv.reshape(bp, m_len, MEM_HEADS, HEAD_DIM))
        mk_s = cache_mem_k[i].reshape(bs, m_len, MEM_W)
        mv_s = cache_mem_v[i].reshape(bs, m_len, MEM_W)
        if i % 2 == 0:
            w_in = w_in_pool[li].astype(BF16)
            zp = proj(xp, w_in, gain=norm_mix_g[i]).reshape(bp, t_p, -1)
            zs = proj(xs, w_in, gain=norm_mix_g[i]).reshape(bs, t_s, -1)
            op = pool_mix(zp, None, w_pool_grp[li], pool_scale[li], 0)
            zs16 = jnp.pad(zs, ((0, 0), (0, POOL_HALO - t_s), (0, 0)))
            halo = jnp.pad(state_pool[li], ((0, 0), (1, 0), (0, 0)))
            os_ = pool_mix(zs16, halo, w_pool_grp[li], pool_scale[li], past_len)[:, :t_s]
            pool_p.append(zp[:, t_p - POOL_STATE:, :TOK_W])
            pool_s.append(jnp.concatenate([state_pool[li], zs[..., :TOK_W]], axis=1)[:, -POOL_STATE:])
            mq0 = TOK_W
            w_out = w_out_pool[li]
        else:
            w_in = _reorder_nsa_weight(w_in_nsa[li])
            zp = proj(xp, w_in, gain=norm_mix_g[i]).reshape(bp, t_p, -1)
            zs = proj(xs, w_in, gain=norm_mix_g[i]).reshape(bs, t_s, -1)
            q, kc, vc, ks, vs, kw, vw, gate = nsa_prep(zp, pos_p, nsa_q_norm_g[li], nsa_k_norm_g[li], b_gate[li])
            kcmp = compress_rows(kc, cmp_pe[li, 0], cmp_w1[li, 0], cmp_w2[li, 0])
            vcmp = compress_rows(vc, cmp_pe[li, 1], cmp_w1[li, 1], cmp_w2[li, 1])
            o_cmp, sel_t = cmp_select_prompt(q, kcmp, vcmp)
            o_sel = sel_attention_prompt_t(q, ks, vs, sel_t)
            op = win_combine_prompt(q, kw, vw, o_cmp, o_sel, gate)
            kvr = lambda a: a.reshape(a.shape[0], a.shape[1], NSA_KV_HEADS, HEAD_DIM)
            rows_p.append(tuple(kvr(a) for a in (kc, vc, ks, vs)))
            wb = min(WINDOW, t_p)
            win_p.append((kvr(kw[:, t_p - wb:]), kvr(vw[:, t_p - wb:])))
            prep_s = nsa_prep(zs, pos_s, nsa_q_norm_g[li], nsa_k_norm_g[li], b_gate[li])
            os_ = nsa_sample_step(prep_s, li, (cache_cmp_k, cache_cmp_v, cache_sel_k, cache_sel_v),
                                  state_win_k, state_win_v, page_table, cmp_pe[li], cmp_w1[li], cmp_w2[li])
            rows_s.append(tuple(kvr(a) for a in prep_s[1:5]))
            win_s.append(tuple(jnp.concatenate([st[li][:, t_s:], kvr(new)], axis=1)
                               for st, new in ((state_win_k, prep_s[5]), (state_win_v, prep_s[6]))))
            mq0 = NSA_MQ0
            w_out = w_out_nsa[li]
        ap = mem_attention(zp, mq0, mk, mv, mem_q_norm_g[i])
        as_ = mem_attention(zs, mq0, mk_s, mv_s, mem_q_norm_g[i])
        w_out = w_out.astype(BF16)
        xp = proj([op.reshape(n_p, TOK_W), ap.reshape(n_p, MEM_W)], w_out, residual=xp)
        xs = proj([os_.reshape(n_s, TOK_W), as_.reshape(n_s, MEM_W)], w_out, residual=xs)
        xn, e_p, g_p = router(xp, norm_ffn_g[i], w_router, b_router, xn_rows=n_p + LANES)
        xs_pad = jnp.pad(xs, ((0, LANES - n_s), (0, 0)))
        xn, e_s, g_s = router(xs_pad, norm_ffn_g[i], w_router, b_router, xn_rows=n_p + LANES, into=xn, row0=n_p)
        xp, xs = moe_ffn2([xp, xs],
                          xn,
                          jnp.concatenate([e_p, e_s[:, :n_s]], axis=1),
                          [g_p, g_s[:, :n_s]],
                          w_gate, w_up, w_down, i)
    stk = lambda lst, j: jnp.stack([r[j] for r in lst])
    return (xp.reshape(bp, t_p, d), xs.reshape(bs, t_s, d),
            jnp.stack(pool_p), jnp.stack(pool_s),
            stk(rows_p, 0), stk(rows_p, 1), stk(rows_p, 2), stk(rows_p, 3),
            stk(rows_s, 0), stk(rows_s, 1), stk(rows_s, 2), stk(rows_s, 3),
            stk(win_p, 0), stk(win_p, 1), stk(win_s, 0), stk(win_s, 1),
            jnp.stack(mem_k_p), jnp.stack(mem_v_p))
```

```python
import functools

import jax
import jax.numpy as jnp
import numpy as np
from jax import lax
from jax.experimental import pallas as pl
from jax.experimental.pallas import tpu as pltpu

D_MODEL = 2048
DEPTH = 4
PAGE_SIZE = 128
HEAD_DIM = 128
ROPE_THETA = 500000.0
ROPE_DIM = HEAD_DIM // 4
MEM_HEADS = 4
MEM_W = MEM_HEADS * HEAD_DIM
TOK_W = D_MODEL - MEM_W
POOL_WINDOWS = (2, 4, 8, 16)
POOL_GROUP = TOK_W // len(POOL_WINDOWS)
POOL_STATE = max(POOL_WINDOWS) - 1
POOL_HALO = POOL_STATE + 1
NSA_HEADS = TOK_W // HEAD_DIM
NSA_KV_HEADS = 2
NSA_GROUP = NSA_HEADS // NSA_KV_HEADS
KV_W = NSA_KV_HEADS * HEAD_DIM
GROUP_W = NSA_GROUP * HEAD_DIM
CMP_BLOCK = 32
CMP_STRIDE = 16
SEL_BLOCK = 64
SEL_TOP = 16
WINDOW = 512
NSA_TOK_IN = NSA_HEADS * HEAD_DIM + 6 * KV_W + 3 * NSA_HEADS
N_EXPERTS = 16
N_GROUPS = 4
EXP_PER_GROUP = N_EXPERTS // N_GROUPS
TOP_K = 2
D_FF = 1024
EPS = 1e-6
NEG = -1e30
SCALE = HEAD_DIM ** -0.5
LANES = 128

V7X_VMEM_BYTES = 64 * 1024 * 1024
VMEM_LIMIT = V7X_VMEM_BYTES * 3 // 4

BF16 = jnp.bfloat16
F32 = jnp.float32
I32 = jnp.int32

NSA_Q0 = 0
NSA_KV0 = NSA_HEADS * HEAD_DIM
NSA_MQ0 = NSA_KV0 + 6 * KV_W
NSA_GL0 = NSA_MQ0 + MEM_W
NSA_ZW = 3840


def _cparams(sem):
    return pltpu.CompilerParams(dimension_semantics=sem, vmem_limit_bytes=VMEM_LIMIT)


def _nt(a, b):
    return lax.dot_general(a, b, (((1,), (1,)), ((), ())), preferred_element_type=F32)


def _tn(a, b):
    return lax.dot_general(a, b, (((0,), (0,)), ((), ())), preferred_element_type=F32)


def _rms(x, g):
    return x * lax.rsqrt(jnp.mean(x * x, axis=-1, keepdims=True) + EPS) * g


def _pick_tile(n, cap, unit):
    if n <= cap:
        return n
    best = None
    for t in range(unit, cap + 1, unit):
        if n % t == 0:
            best = t
    assert best is not None, (n, cap, unit)
    return best


def _proj_kernel(*refs, n_x, norm, residual):
    x_refs, refs = refs[:n_x], refs[n_x:]
    if norm:
        g_ref, refs = refs[0], refs[1:]
    w_ref, refs = refs[0], refs[1:]
    if residual:
        r_ref, refs = refs[0], refs[1:]
    o_ref, xn_ref = refs

    @pl.when(pl.program_id(1) == 0)
    def _():
        off = 0
        for x_ref in x_refs:
            x = x_ref[...].astype(F32)
            if norm:
                x = _rms(x, g_ref[...])
            xn_ref[:, off:off + x.shape[1]] = x.astype(BF16)
            off += x.shape[1]

    y = jnp.dot(xn_ref[...], w_ref[...], preferred_element_type=F32)
    if residual:
        y = y + r_ref[...]
    o_ref[...] = y


def proj(xs, w_bf16, gain=None, residual=None):
    if not isinstance(xs, (list, tuple)):
        xs = [xs]
    assert gain is None or len(xs) == 1
    m = xs[0].shape[0]
    k = sum(x.shape[1] for x in xs)
    n = w_bf16.shape[1]
    tm = _pick_tile(m, 1024, 8)
    tn = _pick_tile(n, 1024, LANES)
    in_specs = [pl.BlockSpec((tm, x.shape[1]), lambda i, j: (i, 0)) for x in xs]
    args = list(xs)
    if gain is not None:
        in_specs.append(pl.BlockSpec((1, k), lambda i, j: (0, 0)))
        args.append(gain.reshape(1, k).astype(F32))
    in_specs.append(pl.BlockSpec((k, tn), lambda i, j: (0, j)))
    args.append(w_bf16)
    if residual is not None:
        in_specs.append(pl.BlockSpec((tm, tn), lambda i, j: (i, j)))
        args.append(residual)
    return pl.pallas_call(
        functools.partial(_proj_kernel, n_x=len(xs), norm=gain is not None, residual=residual is not None),
        grid=(m // tm, n // tn),
        in_specs=in_specs,
        out_specs=pl.BlockSpec((tm, tn), lambda i, j: (i, j)),
        out_shape=jax.ShapeDtypeStruct((m, n), F32),
        scratch_shapes=[pltpu.VMEM((tm, k), BF16)],
        compiler_params=_cparams(("parallel", "arbitrary")),
        name="proj",
    )(*args)


def _head_norm_kernel(x_ref, g_ref, o_ref):
    x = x_ref[...]
    g = g_ref[...]
    n_heads = x.shape[1] // HEAD_DIM
    o_ref[...] = jnp.concatenate(
        [_rms(x[:, h * HEAD_DIM:(h + 1) * HEAD_DIM], g) for h in range(n_heads)], axis=1)


def head_norm(x, col0, width, gain):
    m = x.shape[0]
    assert col0 % width == 0
    return pl.pallas_call(
        _head_norm_kernel,
        grid=(1,),
        in_specs=[pl.BlockSpec((m, width), lambda i: (0, col0 // width)),
                  pl.BlockSpec((1, HEAD_DIM), lambda i: (0, 0))],
        out_specs=pl.BlockSpec((m, width), lambda i: (0, 0)),
        out_shape=jax.ShapeDtypeStruct((m, width), F32),
        compiler_params=_cparams(("arbitrary",)),
        name="head_norm",
    )(x, gain.reshape(1, HEAD_DIM))


def _mem_attn_kernel(q_ref, k_ref, v_ref, g_ref, o_ref):
    q = q_ref[...]
    rows = q.shape[0]
    if rows < 8:
        q = jnp.broadcast_to(q[0:1], (8, q.shape[1]))
    g = g_ref[...]
    outs = []
    for h in range(MEM_HEADS):
        sl = slice(h * HEAD_DIM, (h + 1) * HEAD_DIM)
        qh = _rms(q[:, sl], g).astype(BF16)
        s = _nt(qh, k_ref[:, sl].astype(BF16)) * SCALE
        m = jnp.max(s, axis=-1, keepdims=True)
        e = jnp.exp(s - m)
        p = e / jnp.sum(e, axis=-1, keepdims=True)
        outs.append(jnp.dot(p.astype(BF16), v_ref[:, sl].astype(BF16), preferred_element_type=F32))
    o = jnp.concatenate(outs, axis=1)
    o_ref[...] = o[:rows]


def mem_attention(z, col0, k, v, gain):
    b, t, _ = z.shape
    m = k.shape[1]
    tq = _pick_tile(t, 512, 8)
    assert col0 % MEM_W == 0
    return pl.pallas_call(
        _mem_attn_kernel,
        grid=(b, t // tq),
        in_specs=[pl.BlockSpec((None, tq, MEM_W), lambda bi, i: (bi, i, col0 // MEM_W)),
                  pl.BlockSpec((None, m, MEM_W), lambda bi, i: (bi, 0, 0)),
                  pl.BlockSpec((None, m, MEM_W), lambda bi, i: (bi, 0, 0)),
                  pl.BlockSpec((1, HEAD_DIM), lambda bi, i: (0, 0))],
        out_specs=pl.BlockSpec((None, tq, MEM_W), lambda bi, i: (bi, i, 0)),
        out_shape=jax.ShapeDtypeStruct((b, t, MEM_W), F32),
        compiler_params=_cparams(("parallel", "arbitrary")),
        name="mem_attention",
    )(z, k, v, gain.reshape(1, HEAD_DIM))


def _pool_kernel(u_ref, halo_ref, w_ref, sc_ref, o_ref, *, pos0, zero_first_halo):
    tq = u_ref.shape[0]
    qi = pl.program_id(1)
    u = u_ref[...]
    halo = halo_ref[...]
    if zero_first_halo:
        halo = jnp.where(qi > 0, halo, 0.0)
    pos = pos0 + qi * tq + lax.broadcasted_iota(I32, (tq, 1), 0)
    outs = []
    for g, w in enumerate(POOL_WINDOWS):
        cs = slice(g * POOL_GROUP, (g + 1) * POOL_GROUP)
        ug = u[:, cs]
        acc = jnp.concatenate([halo[:, cs], ug], axis=0)
        span = 1
        while span < w:
            acc = acc[span:] + acc[:-span]
            span *= 2
        ssum = acc[POOL_HALO - (w - 1):POOL_HALO - (w - 1) + tq]
        cnt = jnp.minimum(w, pos + 1).astype(F32)
        d = (ssum / cnt - ug).astype(BF16)
        y = jnp.dot(d, w_ref[g].astype(BF16), preferred_element_type=F32)
        outs.append(y)
    o_ref[...] = jnp.concatenate(outs, axis=1) * sc_ref[...]


def pool_mix(z, halo, w_grp, scale, pos0):
    b, t, _ = z.shape
    tq = _pick_tile(t, 256, 16)
    if halo is None:
        halo_arr = z
        halo_spec = pl.BlockSpec((None, POOL_HALO, TOK_W),
                                 lambda bi, i: (bi, jnp.maximum(i * (tq // POOL_HALO) - 1, 0), 0))
    else:
        assert t == tq
        halo_arr = halo
        halo_spec = pl.BlockSpec((None, POOL_HALO, TOK_W), lambda bi, i: (bi, 0, 0))
    return pl.pallas_call(
        functools.partial(_pool_kernel, pos0=pos0, zero_first_halo=halo is None),
        grid=(b, t // tq),
        in_specs=[pl.BlockSpec((None, tq, TOK_W), lambda bi, i: (bi, i, 0)),
                  halo_spec,
                  pl.BlockSpec(w_grp.shape, lambda bi, i: (0, 0, 0)),
                  pl.BlockSpec((1, TOK_W), lambda bi, i: (0, 0))],
        out_specs=pl.BlockSpec((None, tq, TOK_W), lambda bi, i: (bi, i, 0)),
        out_shape=jax.ShapeDtypeStruct((b, t, TOK_W), F32),
        compiler_params=_cparams(("parallel", "arbitrary")),
        name="pool_mix",
    )(z, halo_arr, w_grp, scale.reshape(1, TOK_W))


def _rope_tables(pos):
    half = ROPE_DIM // 2
    inv = 1.0 / (ROPE_THETA ** (jnp.arange(half, dtype=F32) * 2.0 / ROPE_DIM))
    ang = pos.astype(F32)[:, None] * inv[None, :]
    cos, sin = jnp.cos(ang), jnp.sin(ang)
    t = pos.shape[0]
    rest = HEAD_DIM - ROPE_DIM
    c = jnp.concatenate([cos, cos, jnp.ones((t, rest), F32)], axis=1)
    s_lo = jnp.concatenate([-sin, jnp.zeros((t, HEAD_DIM - half), F32)], axis=1)
    s_hi = jnp.concatenate([jnp.zeros((t, half), F32), sin, jnp.zeros((t, rest), F32)], axis=1)
    return c, s_lo, s_hi


def _nsa_prep_kernel(z_ref, c_ref, slo_ref, shi_ref, qg_ref, kg_ref, bg_ref,
                     q_ref, kc_ref, vc_ref, ks_ref, vs_ref, kw_ref, vw_ref, gate_ref):
    rows = z_ref.shape[0]
    c, slo, shi = c_ref[...], slo_ref[...], shi_ref[...]
    half = ROPE_DIM // 2

    def rope_norm(x, g):
        if rows < 8:
            x = jnp.broadcast_to(x[0:1], (8, HEAD_DIM))
        x = _rms(x, g)
        y = x * c + pltpu.roll(x, HEAD_DIM - half, 1) * slo + pltpu.roll(x, half, 1) * shi
        return y[:rows]

    qg = qg_ref[...]
    q_ref[...] = jnp.concatenate(
        [rope_norm(z_ref[:, NSA_Q0 + h * HEAD_DIM:NSA_Q0 + (h + 1) * HEAD_DIM], qg) for h in range(NSA_HEADS)], axis=1)
    for j, (o_ref, which) in enumerate(((kc_ref, 0), (vc_ref, None), (ks_ref, 1), (vs_ref, None), (kw_ref, 2), (vw_ref, None))):
        c0 = NSA_KV0 + j * KV_W
        if which is None:
            o_ref[...] = z_ref[:, c0:c0 + KV_W]
        else:
            g = kg_ref[which:which + 1, :]
            o_ref[...] = jnp.concatenate(
                [rope_norm(z_ref[:, c0 + h * HEAD_DIM:c0 + (h + 1) * HEAD_DIM], g) for h in range(NSA_KV_HEADS)], axis=1)
    gate_ref[...] = jax.nn.sigmoid(z_ref[:, NSA_GL0:NSA_GL0 + LANES] + bg_ref[...])


def nsa_prep(z, pos, q_g, k_g, b_gate):
    b, t, _ = z.shape
    tq = _pick_tile(t, 256, 8)
    c, slo, shi = _rope_tables(pos)
    if t < 8:
        c, slo, shi = (jnp.broadcast_to(a, (8, HEAD_DIM)) for a in (c, slo, shi))
    tt = max(tq, 8)
    bg = jnp.pad(b_gate.reshape(1, -1), ((0, 0), (0, LANES - b_gate.shape[-1])))
    tab_spec = pl.BlockSpec((tt, HEAD_DIM), lambda bi, i: (i, 0))
    kv_spec = pl.BlockSpec((None, tq, KV_W), lambda bi, i: (bi, i, 0))
    kv_shape = jax.ShapeDtypeStruct((b, t, KV_W), F32)
    return pl.pallas_call(
        _nsa_prep_kernel,
        grid=(b, t // tq),
        in_specs=[pl.BlockSpec((None, tq, NSA_ZW), lambda bi, i: (bi, i, 0)),
                  tab_spec, tab_spec, tab_spec,
                  pl.BlockSpec((1, HEAD_DIM), lambda bi, i: (0, 0)),
                  pl.BlockSpec((3, HEAD_DIM), lambda bi, i: (0, 0)),
                  pl.BlockSpec((1, LANES), lambda bi, i: (0, 0))],
        out_specs=[pl.BlockSpec((None, tq, TOK_W), lambda bi, i: (bi, i, 0))] + [kv_spec] * 6
                  + [pl.BlockSpec((None, tq, LANES), lambda bi, i: (bi, i, 0))],
        out_shape=[jax.ShapeDtypeStruct((b, t, TOK_W), F32)] + [kv_shape] * 6
                  + [jax.ShapeDtypeStruct((b, t, LANES), F32)],
        compiler_params=_cparams(("parallel", "arbitrary")),
        name="nsa_prep",
    )(z, c, slo, shi, q_g.reshape(1, HEAD_DIM), k_g, bg)


def _compress_body(slab, n_chunks, pe_ref, w1_ref, w2_ref, o_ref):
    pe = pe_ref[...]
    pe_lo = jnp.concatenate([pe[r:r + 1] for r in range(CMP_STRIDE)], axis=1)
    pe_hi = jnp.concatenate([pe[CMP_STRIDE + r:CMP_STRIDE + r + 1] for r in range(CMP_STRIDE)], axis=1)
    half_k = CMP_STRIDE * HEAD_DIM
    w_lo = w1_ref[0:half_k, :].astype(BF16)
    w_hi = w1_ref[half_k:2 * half_k, :].astype(BF16)
    w2 = w2_ref[...].astype(BF16)
    for h in range(NSA_KV_HEADS):
        xh = jnp.concatenate([slab(r, h) for r in range(CMP_STRIDE)], axis=1)
        a = jnp.dot((xh + pe_lo).astype(BF16), w_lo, preferred_element_type=F32)
        bb = jnp.dot((xh + pe_hi).astype(BF16), w_hi, preferred_element_type=F32)
        hid = jax.nn.gelu(a + pltpu.roll(bb, n_chunks - 1, 0))
        o_ref[h] = jnp.dot(hid.astype(BF16), w2, preferred_element_type=F32)


def _compress_kernel(x_ref, pe_ref, w1_ref, w2_ref, o_ref):
    def slab(r, h):
        c0 = (r * NSA_KV_HEADS + h) * HEAD_DIM
        return x_ref[:, c0:c0 + HEAD_DIM]
    _compress_body(slab, x_ref.shape[0], pe_ref, w1_ref, w2_ref, o_ref)


def compress_rows(x, pe, w1, w2):
    b, t, _ = x.shape
    n_chunks = t // CMP_STRIDE
    cw = CMP_STRIDE * KV_W
    xc = x.reshape(b, n_chunks, cw)
    return pl.pallas_call(
        _compress_kernel,
        grid=(b,),
        in_specs=[pl.BlockSpec((None, n_chunks, cw), lambda bi: (bi, 0, 0)),
                  pl.BlockSpec(pe.shape, lambda bi: (0, 0)),
                  pl.BlockSpec(w1.shape, lambda bi: (0, 0)),
                  pl.BlockSpec(w2.shape, lambda bi: (0, 0))],
        out_specs=pl.BlockSpec((None, NSA_KV_HEADS, n_chunks, HEAD_DIM), lambda bi: (bi, 0, 0, 0)),
        out_shape=jax.ShapeDtypeStruct((b, NSA_KV_HEADS, n_chunks, HEAD_DIM), F32),
        compiler_params=_cparams(("parallel",)),
        name="compress_rows",
    )(xc, pe, w1, w2)


def _stack_heads(q):
    return jnp.concatenate([q[:, g * HEAD_DIM:(g + 1) * HEAD_DIM] for g in range(NSA_GROUP)], axis=0)


def _unstack_heads(o, tq):
    return jnp.concatenate([o[g * tq:(g + 1) * tq] for g in range(NSA_GROUP)], axis=1)


def _rank_select(imp_t, n_real):
    n_blk = imp_t.shape[0]
    blk = lax.broadcasted_iota(I32, imp_t.shape, 0)
    cnt = jnp.zeros(imp_t.shape, F32)
    for j in range(n_real):
        row = imp_t[j:j + 1, :]
        beats = jnp.where(row > imp_t, 1.0, jnp.where(row == imp_t, jnp.where(blk > j, 1.0, 0.0), 0.0))
        cnt = cnt + beats
    return jnp.where(cnt < SEL_TOP, 1.0, 0.0)


def _cmp_select_kernel(q_ref, kc_ref, vc_ref, cov_ref, o_ref, sel_ref):
    tq = q_ref.shape[0]
    nc = kc_ref.shape[0]
    n_blk = sel_ref.shape[0]
    qi = pl.program_id(2)
    q6 = _stack_heads(q_ref[...]).astype(BF16)
    s = _nt(q6, kc_ref[...].astype(BF16)) * SCALE
    s = s.reshape(NSA_GROUP, tq, nc)
    q_pos = qi * tq + lax.broadcasted_iota(I32, (tq, nc), 0)
    c_last = lax.broadcasted_iota(I32, (tq, nc), 1) * CMP_STRIDE + (CMP_BLOCK - 1)
    valid = (c_last <= q_pos)[None]
    s = jnp.where(valid, s, NEG)
    m = jnp.max(s, axis=-1, keepdims=True)
    e = jnp.where(valid, jnp.exp(s - m), 0.0)
    p = (e / jnp.maximum(jnp.sum(e, axis=-1, keepdims=True), 1e-30)).astype(BF16)
    o = jnp.dot(p.reshape(NSA_GROUP * tq, nc), vc_ref[...].astype(BF16), preferred_element_type=F32)
    o_ref[...] = _unstack_heads(o, tq)
    p_cat = jnp.concatenate([p[g] for g in range(NSA_GROUP)], axis=1)
    imp_t = _nt(cov_ref[...], p_cat)
    blk = lax.broadcasted_iota(I32, (n_blk, tq), 0)
    pos_t = qi * tq + lax.broadcasted_iota(I32, (n_blk, tq), 1)
    cur = pos_t // SEL_BLOCK
    forced = (blk == 0) | (blk == cur) | (blk == cur - 1)
    future = blk * SEL_BLOCK > pos_t
    imp_t = jnp.where(future, -1e9, jnp.where(forced, 1e9, imp_t))
    sel_ref[...] = _rank_select(imp_t, n_blk).astype(BF16)


def _cover_t(nc, n_blk, nc_valid):
    c_start = np.arange(nc) * CMP_STRIDE
    c_last = c_start + CMP_BLOCK - 1
    s_start = np.arange(n_blk) * SEL_BLOCK
    cov = (c_start[None, :] < s_start[:, None] + SEL_BLOCK) & (c_last[None, :] >= s_start[:, None])
    cov = cov & (np.arange(nc)[None, :] < nc_valid)
    return np.tile(cov.astype(np.float32), (1, NSA_GROUP))


def cmp_select_prompt(q, kcmp, vcmp):
    b, t, _ = q.shape
    nc = kcmp.shape[2]
    n_blk = -(-t // SEL_BLOCK)
    tq = _pick_tile(t, 256, LANES)
    cov = jnp.asarray(_cover_t(nc, n_blk, nc - 1), BF16)
    return pl.pallas_call(
        _cmp_select_kernel,
        grid=(b, NSA_KV_HEADS, t // tq),
        in_specs=[pl.BlockSpec((None, tq, GROUP_W), lambda bi, h, i: (bi, i, h)),
                  pl.BlockSpec((None, None, nc, HEAD_DIM), lambda bi, h, i: (bi, h, 0, 0)),
                  pl.BlockSpec((None, None, nc, HEAD_DIM), lambda bi, h, i: (bi, h, 0, 0)),
                  pl.BlockSpec(cov.shape, lambda bi, h, i: (0, 0))],
        out_specs=[pl.BlockSpec((None, tq, GROUP_W), lambda bi, h, i: (bi, i, h)),
                   pl.BlockSpec((None, None, n_blk, tq), lambda bi, h, i: (bi, h, 0, i))],
        out_shape=[jax.ShapeDtypeStruct((b, t, TOK_W), F32),
                   jax.ShapeDtypeStruct((b, NSA_KV_HEADS, n_blk, t), BF16)],
        compiler_params=_cparams(("parallel", "parallel", "arbitrary")),
        name="cmp_select_prompt",
    )(q, kcmp, vcmp, cov)


SEL_TQ = 128
SEL_TK = 512


def _heads_to_lanes_t(q):
    tq = q.shape[0]
    return jnp.concatenate([q[:, g * HEAD_DIM:(g + 1) * HEAD_DIM].T for g in range(NSA_GROUP)], axis=1)


def _lanes_to_heads_t(o_t, tq):
    return jnp.concatenate([o_t[:, g * tq:(g + 1) * tq].T for g in range(NSA_GROUP)], axis=1)


def _selt_kernel(q_ref, k_ref, v_ref, sel_ref, exp_ref, o_ref, mask_ref, m_ref, l_ref, acc_ref):
    t = k_ref.shape[0]
    qi = pl.program_id(2)
    q_t = _heads_to_lanes_t(q_ref[...]).astype(BF16)
    mask_ref[...] = (1.0 - jnp.dot(exp_ref[...], sel_ref[...], preferred_element_type=F32)) * NEG
    m_ref[...] = jnp.full(m_ref.shape, NEG, F32)
    l_ref[...] = jnp.zeros(l_ref.shape, F32)
    acc_ref[...] = jnp.zeros(acc_ref.shape, F32)
    q_lo = qi * SEL_TQ

    def key_tile(j, causal):
        lo, hi = j * SEL_TK, (j + 1) * SEL_TK
        k = k_ref[lo:hi, :].astype(BF16)
        v = v_ref[lo:hi, :].astype(BF16)
        bias = mask_ref[lo:hi, :]
        if causal:
            k_pos = lo + lax.broadcasted_iota(I32, (SEL_TK, SEL_TQ), 0)
            q_pos = q_lo + lax.broadcasted_iota(I32, (SEL_TK, SEL_TQ), 1)
            bias = jnp.where(k_pos <= q_pos, bias, NEG)
        s = jnp.dot(k, q_t, preferred_element_type=F32) * SCALE + jnp.concatenate([bias] * NSA_GROUP, axis=1)
        m_old = m_ref[...]
        m_new = jnp.maximum(m_old, jnp.max(s, axis=0, keepdims=True))
        e = jnp.exp(s - m_new)
        alpha = jnp.exp(m_old - m_new)
        l_ref[...] = alpha * l_ref[...] + jnp.sum(e, axis=0, keepdims=True)
        acc_ref[...] = alpha * acc_ref[...] + _tn(v, e.astype(BF16))
        m_ref[...] = m_new

    for j in range(t // SEL_TK):
        lo, hi = j * SEL_TK, (j + 1) * SEL_TK

        @pl.when(hi - 1 <= q_lo)
        def _():
            key_tile(j, False)

        @pl.when((lo <= q_lo + SEL_TQ - 1) & (hi - 1 > q_lo))
        def _():
            key_tile(j, True)

    o_t = acc_ref[...] / jnp.maximum(l_ref[...], 1e-30)
    o_ref[...] = _lanes_to_heads_t(o_t, SEL_TQ)


def sel_attention_prompt_t(q, ks, vs, sel_t):
    b, t, _ = q.shape
    n_blk = sel_t.shape[2]
    expand = (np.arange(t)[:, None] // SEL_BLOCK == np.arange(n_blk)[None, :]).astype(np.float32)
    expand = jnp.asarray(expand, BF16)
    return pl.pallas_call(
        _selt_kernel,
        grid=(b, NSA_KV_HEADS, t // SEL_TQ),
        in_specs=[pl.BlockSpec((None, SEL_TQ, GROUP_W), lambda bi, h, i: (bi, i, h)),
                  pl.BlockSpec((None, t, HEAD_DIM), lambda bi, h, i: (bi, 0, h)),
                  pl.BlockSpec((None, t, HEAD_DIM), lambda bi, h, i: (bi, 0, h)),
                  pl.BlockSpec((None, None, n_blk, SEL_TQ), lambda bi, h, i: (bi, h, 0, i)),
                  pl.BlockSpec((t, n_blk), lambda bi, h, i: (0, 0))],
        out_specs=pl.BlockSpec((None, SEL_TQ, GROUP_W), lambda bi, h, i: (bi, i, h)),
        out_shape=jax.ShapeDtypeStruct((b, t, TOK_W), F32),
        scratch_shapes=[pltpu.VMEM((t, SEL_TQ), F32),
                        pltpu.VMEM((1, NSA_GROUP * SEL_TQ), F32),
                        pltpu.VMEM((1, NSA_GROUP * SEL_TQ), F32),
                        pltpu.VMEM((HEAD_DIM, NSA_GROUP * SEL_TQ), F32)],
        compiler_params=_cparams(("parallel", "parallel", "arbitrary")),
        name="sel_attention_prompt",
    )(q, ks, vs, sel_t, expand)


WIN_TQ = 128


def _gate_cols(gate, h, j, tq):
    cols = []
    for g in range(NSA_GROUP):
        c = (h * NSA_GROUP + g) * 3 + j
        cols.append(jnp.broadcast_to(gate[:, c:c + 1], (tq, HEAD_DIM)))
    return jnp.concatenate(cols, axis=1)


def _win_combine_kernel(q_ref, k_ref, v_ref, oc_ref, os_ref, gate_ref, o_ref):
    tq = q_ref.shape[0]
    span = WINDOW + tq
    h = pl.program_id(1)
    qi = pl.program_id(2)
    start = pl.multiple_of(jnp.maximum(qi * tq - WINDOW, 0), tq)
    q_t = _heads_to_lanes_t(q_ref[...]).astype(BF16)
    k = k_ref[pl.ds(start, span), :].astype(BF16)
    v = v_ref[pl.ds(start, span), :].astype(BF16)
    diff = (qi * tq + lax.broadcasted_iota(I32, (span, tq), 1)) - (start + lax.broadcasted_iota(I32, (span, tq), 0))
    bias = jnp.where((diff >= 0) & (diff < WINDOW), 0.0, NEG)
    s = jnp.dot(k, q_t, preferred_element_type=F32) * SCALE + jnp.concatenate([bias] * NSA_GROUP, axis=1)
    e = jnp.exp(s - jnp.max(s, axis=0, keepdims=True))
    p = e / jnp.maximum(jnp.sum(e, axis=0, keepdims=True), 1e-30)
    o_win = _lanes_to_heads_t(_tn(v, p.astype(BF16)), tq)
    gate = gate_ref[...]
    for hh in range(NSA_KV_HEADS):
        @pl.when(h == hh)
        def _():
            o_ref[...] = (_gate_cols(gate, hh, 0, tq) * oc_ref[...] + _gate_cols(gate, hh, 1, tq) * os_ref[...]
                          + _gate_cols(gate, hh, 2, tq) * o_win)


def win_combine_prompt(q, kw, vw, o_cmp, o_sel, gate):
    b, t, _ = q.shape
    assert t >= WINDOW + WIN_TQ
    qspec = pl.BlockSpec((None, WIN_TQ, GROUP_W), lambda bi, h, i: (bi, i, h))
    kspec = pl.BlockSpec((None, t, HEAD_DIM), lambda bi, h, i: (bi, 0, h))
    return pl.pallas_call(
        _win_combine_kernel,
        grid=(b, NSA_KV_HEADS, t // WIN_TQ),
        in_specs=[qspec, kspec, kspec, qspec, qspec,
                  pl.BlockSpec((None, WIN_TQ, LANES), lambda bi, h, i: (bi, i, 0))],
        out_specs=qspec,
        out_shape=jax.ShapeDtypeStruct((b, t, TOK_W), F32),
        compiler_params=_cparams(("parallel", "parallel", "arbitrary")),
        name="win_combine_prompt",
    )(q, kw, vw, o_cmp, o_sel, gate)


def _router_kernel(x_ref, g_ref, wt_ref, b_ref, xn_ref, e_ref, gt_ref):
    xn = _rms(x_ref[...], g_ref[...])
    xn_ref[...] = xn
    lt = _nt(wt_ref[...], xn.astype(BF16))
    ex = jnp.exp(lt - jnp.max(lt, axis=0, keepdims=True))
    aff = ex / jnp.sum(ex, axis=0, keepdims=True)
    sel = aff + b_ref[...]
    row = lambda a, r: a[r:r + 1, :]

    best, g_idx = None, None
    for g in range(N_GROUPS):
        a, b, c, d = (row(sel, g * EXP_PER_GROUP + j) for j in range(EXP_PER_GROUP))
        hi1, lo1, hi2, lo2 = jnp.maximum(a, b), jnp.minimum(a, b), jnp.maximum(c, d), jnp.minimum(c, d)
        score = jnp.maximum(hi1, hi2) + jnp.maximum(jnp.minimum(hi1, hi2), jnp.maximum(lo1, lo2))
        if g == 0:
            best, g_idx = score, jnp.zeros(score.shape, I32)
        else:
            g_idx = jnp.where(score > best, g, g_idx)
            best = jnp.maximum(best, score)

    def in_group(a, j):
        out = row(a, j)
        for g in range(1, N_GROUPS):
            out = jnp.where(g_idx == g, row(a, g * EXP_PER_GROUP + j), out)
        return out

    v = [in_group(sel, j) for j in range(EXP_PER_GROUP)]
    af = [in_group(aff, j) for j in range(EXP_PER_GROUP)]

    def first_max(vals):
        m = functools.reduce(jnp.maximum, vals)
        loc = jnp.full(m.shape, EXP_PER_GROUP - 1, I32)
        for j in range(EXP_PER_GROUP - 2, -1, -1):
            loc = jnp.where(vals[j] == m, j, loc)
        return loc

    l1 = first_max(v)
    l2 = first_max([jnp.where(l1 == j, -jnp.inf, v[j]) for j in range(EXP_PER_GROUP)])
    pick = lambda loc: functools.reduce(lambda acc, j: jnp.where(loc == j, af[j], acc), range(1, EXP_PER_GROUP), af[0])
    a1, a2 = pick(l1), pick(l2)
    tot = a1 + a2
    e_ref[...] = jnp.concatenate([g_idx * EXP_PER_GROUP + l1, g_idx * EXP_PER_GROUP + l2], axis=0)
    gt_ref[...] = jnp.concatenate([a1 / tot, a2 / tot], axis=0)


def _router_into_kernel(x_ref, g_ref, wt_ref, b_ref, xn_all_ref, xn_ref, e_ref, gt_ref):
    del xn_all_ref
    _router_kernel(x_ref, g_ref, wt_ref, b_ref, xn_ref, e_ref, gt_ref)


def router(x, gain, w_router, b_router, xn_rows=None, into=None, row0=0):
    n, d = x.shape
    tm = _pick_tile(n, 1024, LANES)
    assert row0 % tm == 0
    xn_rows = n if xn_rows is None else xn_rows
    args = [x, gain.reshape(1, d), w_router.T.astype(BF16), b_router.reshape(N_EXPERTS, 1).astype(F32)]
    in_specs = [pl.BlockSpec((tm, d), lambda i: (i, 0)),
                pl.BlockSpec((1, d), lambda i: (0, 0)),
                pl.BlockSpec((N_EXPERTS, d), lambda i: (0, 0)),
                pl.BlockSpec((N_EXPERTS, 1), lambda i: (0, 0))]
    body, aliases = _router_kernel, {}
    if into is not None:
        assert into.shape == (xn_rows, d)
        args.append(into)
        in_specs.append(pl.BlockSpec(memory_space=pl.ANY))
        body, aliases = _router_into_kernel, {len(args) - 1: 0}
    return pl.pallas_call(
        body,
        grid=(n // tm,),
        in_specs=in_specs,
        out_specs=[pl.BlockSpec((tm, d), lambda i: (row0 // tm + i, 0)),
                   pl.BlockSpec((TOP_K, tm), lambda i: (0, i)),
                   pl.BlockSpec((TOP_K, tm), lambda i: (0, i))],
        out_shape=[jax.ShapeDtypeStruct((xn_rows, d), F32),
                   jax.ShapeDtypeStruct((TOP_K, n), I32),
                   jax.ShapeDtypeStruct((TOP_K, n), F32)],
        input_output_aliases=aliases,
        compiler_params=_cparams(("parallel",)),
        name="router",
    )(*args)


MOE_NF = 4
MOE_TF = D_FF // MOE_NF
MOE2_TM = 1024
MOE2_GROUP = 64
MOE2_CHUNK = 256
MOE2_GPS = MOE2_TM // MOE_NF // MOE2_GROUP
MOE2_VMEM_LIMIT = V7X_VMEM_BYTES * 7 // 8


def _moe2_kernel(te_ref, nu_ref, nv_ref, base_ref, order_ref, x_hbm, wg_ref, wu_ref, wd_ref, y_hbm,
                 xbuf, xb, ybuf, gsem, ssem, *, n_tok):
    n_items = n_tok * TOP_K

    def slot_item(tile, rr):
        return order_ref[jnp.minimum(base_ref[tile] + rr, n_items - 1)]
    i = pl.program_id(0)
    f = pl.program_id(1)
    n_used = nu_ref[0]
    cur = i % 2
    oth = 1 - cur
    n_groups = MOE2_TM // MOE2_GROUP

    def gather_group(tile, g):
        for r in range(MOE2_GROUP):
            rr = g * MOE2_GROUP + r
            item = slot_item(tile, rr)
            row = jnp.where(item >= n_tok, item - n_tok, item)
            pltpu.make_async_copy(x_hbm.at[pl.ds(row, 1), :], xbuf.at[pl.ds(rr, 1), :], gsem).start()

    def scatter_group(tile, buf, g):
        for r in range(MOE2_GROUP):
            rr = g * MOE2_GROUP + r
            row = jnp.where(rr < nv_ref[tile], slot_item(tile, rr), n_items + tile * MOE2_TM + rr)
            pltpu.make_async_copy(ybuf.at[buf, pl.ds(rr, 1), :], y_hbm.at[pl.ds(row, 1), :], ssem.at[buf]).start()

    def wait_gather_group():
        pltpu.make_async_copy(x_hbm.at[pl.ds(0, MOE2_GROUP), :], xbuf.at[pl.ds(0, MOE2_GROUP), :], gsem).wait()

    def wait_scatter_group(buf):
        pltpu.make_async_copy(ybuf.at[buf, pl.ds(0, MOE2_GROUP), :], y_hbm.at[pl.ds(0, MOE2_GROUP), :], ssem.at[buf]).wait()

    @pl.when((i == 0) & (f == 0))
    def _():
        xbuf[...] = jnp.zeros(xbuf.shape, F32)

        def body(g, carry):
            @pl.when(g * MOE2_GROUP < nv_ref[0])
            def _():
                gather_group(0, g)
            return carry
        lax.fori_loop(0, n_groups, body, 0)

    @pl.when(i < n_used)
    def _():
        n_valid = nv_ref[i]

        @pl.when(f == 0)
        def _():
            for g in range(n_groups):
                @pl.when(g * MOE2_GROUP < n_valid)
                def _():
                    wait_gather_group()
            xb[...] = xbuf[...].astype(BF16)
            ybuf[cur] = jnp.zeros(ybuf.shape[1:], F32)

        nv_next = nv_ref[i + 1]
        nv_prev = nv_ref[jnp.maximum(i - 1, 0)]
        for gi in range(MOE2_GPS):
            g = f * MOE2_GPS + gi

            @pl.when((i + 1 < n_used) & (g * MOE2_GROUP < nv_next))
            def _():
                gather_group(i + 1, g)

            @pl.when((i > 0) & (g * MOE2_GROUP < nv_prev))
            def _():
                scatter_group(i - 1, oth, g)

        for c in range(MOE2_TM // MOE2_CHUNK):
            lo, hi = c * MOE2_CHUNK, (c + 1) * MOE2_CHUNK

            @pl.when(lo < n_valid)
            def _():
                x = xb[lo:hi, :]
                a = jnp.dot(x, wg_ref[...].astype(BF16), preferred_element_type=F32)
                u = jnp.dot(x, wu_ref[...].astype(BF16), preferred_element_type=F32)
                h = (a * jax.nn.sigmoid(a) * u).astype(BF16)
                ybuf[cur, lo:hi, :] += jnp.dot(h, wd_ref[...].astype(BF16), preferred_element_type=F32)

        @pl.when(f == MOE_NF - 1)
        def _():
            for g in range(n_groups):
                @pl.when((i > 0) & (g * MOE2_GROUP < nv_prev))
                def _():
                    wait_scatter_group(oth)

    @pl.when((i == n_used) & (f == 0))
    def _():
        nv_last = nv_ref[i - 1]

        def start(g, carry):
            @pl.when(g * MOE2_GROUP < nv_last)
            def _():
                scatter_group(i - 1, oth, g)
            return carry
        lax.fori_loop(0, n_groups, start, 0)

        def wait(g, carry):
            @pl.when(g * MOE2_GROUP < nv_last)
            def _():
                wait_scatter_group(oth)
            return carry
        lax.fori_loop(0, n_groups, wait, 0)


def moe_experts2(xn, n_tok, tile_expert, n_used, n_valid, base, order, n_out_rows, w_gate, w_up, w_down, layer):
    n, d = xn.shape
    n_tiles = tile_expert.shape[0]

    def tile(i, nu):
        return jnp.minimum(i, nu[0] - 1)

    def fcol(i, f, nu):
        return jnp.where(i < nu[0], f, MOE_NF - 1)

    grid_spec = pltpu.PrefetchScalarGridSpec(
        num_scalar_prefetch=5,
        grid=(n_tiles, MOE_NF),
        in_specs=[
            pl.BlockSpec(memory_space=pl.ANY),
            pl.BlockSpec((None, None, d, MOE_TF), lambda i, f, te, nu, nv, s, t: (layer, te[tile(i, nu)], 0, fcol(i, f, nu))),
            pl.BlockSpec((None, None, d, MOE_TF), lambda i, f, te, nu, nv, s, t: (layer, te[tile(i, nu)], 0, fcol(i, f, nu))),
            pl.BlockSpec((None, None, MOE_TF, d), lambda i, f, te, nu, nv, s, t: (layer, te[tile(i, nu)], fcol(i, f, nu), 0)),
        ],
        out_specs=pl.BlockSpec(memory_space=pl.ANY),
        scratch_shapes=[pltpu.VMEM((MOE2_TM, d), F32),
                        pltpu.VMEM((MOE2_TM, d), BF16),
                        pltpu.VMEM((2, MOE2_TM, d), F32),
                        pltpu.SemaphoreType.DMA(()),
                        pltpu.SemaphoreType.DMA((2,))],
    )
    return pl.pallas_call(
        functools.partial(_moe2_kernel, n_tok=n_tok),
        grid_spec=grid_spec,
        out_shape=jax.ShapeDtypeStruct((n_out_rows, d), F32),
        compiler_params=pltpu.CompilerParams(dimension_semantics=("arbitrary", "arbitrary"),
                                             vmem_limit_bytes=MOE2_VMEM_LIMIT),
        name="moe_experts",
    )(tile_expert, n_used, n_valid, base, order, xn, w_gate, w_up, w_down)


def moe_ffn2(xs, xn, experts, gates, w_gate, w_up, w_down, layer):
    n, d = experts.shape[1], xn.shape[1]
    tm = MOE2_TM
    n_items = n * TOP_K
    n_tiles = -(-n_items // tm) + N_EXPERTS
    n_slots = n_tiles * tm
    flat_e = experts.reshape(-1)
    sizes = jnp.sum((flat_e[:, None] == jnp.arange(N_EXPERTS, dtype=I32)[None, :]).astype(I32), axis=0)
    padded = -(-sizes // tm) * tm
    ends_p = jnp.cumsum(padded)
    starts_p = ends_p - padded
    starts = jnp.cumsum(sizes) - sizes
    order = jnp.argsort(flat_e).astype(I32)
    tile_start = jnp.arange(n_tiles, dtype=I32) * tm
    tile_expert = jnp.minimum(jnp.searchsorted(ends_p, tile_start, side='right'), N_EXPERTS - 1).astype(I32)
    n_used = (ends_p[-1:] // tm).astype(I32)
    n_valid = jnp.clip(sizes[tile_expert] - (tile_start - starts_p[tile_expert]), 0, tm).astype(I32)
    base = (starts[tile_expert] + tile_start - starts_p[tile_expert]).astype(I32)
    y = moe_experts2(xn, n, tile_expert, n_used, n_valid, base, order, n_items + n_slots,
                     w_gate, w_up, w_down, layer)
    outs, r0 = [], 0
    for x, g in zip(xs, gates):
        r1 = r0 + x.shape[0]
        outs.append(x + y[r0:r1] * g[0][:, None] + y[n + r0:n + r1] * g[1][:, None])
        r0 = r1
    return outs


GROUP_ROWS = 8


def _pick_head(ref, h):
    out = ref[:, 0, :]
    for hh in range(1, NSA_KV_HEADS):
        out = jnp.where(h == hh, ref[:, hh, :], out)
    return out


def _group_rows(q_ref, h):
    rows = [q_ref[:, (h * NSA_GROUP + g) * HEAD_DIM:(h * NSA_GROUP + g + 1) * HEAD_DIM] for g in range(NSA_GROUP)]
    rows.append(jnp.zeros((GROUP_ROWS - NSA_GROUP, HEAD_DIM), F32))
    return jnp.concatenate(rows, axis=0)


def _compress_paged_kernel(pt_ref, cache_hbm, pe_ref, w1_ref, w2_ref, o_ref, xbuf, sem, *, layer, n_pages):
    b = pl.program_id(0)
    n_chunks = n_pages * (PAGE_SIZE // CMP_STRIDE)

    def page_copy(p):
        page = pt_ref[b * n_pages + p]
        return pltpu.make_async_copy(cache_hbm.at[layer, page], xbuf.at[pl.ds(p * PAGE_SIZE, PAGE_SIZE)], sem)

    def start(p, carry):
        page_copy(p).start()
        return carry

    def wait(p, carry):
        page_copy(p).wait()
        return carry

    lax.fori_loop(0, n_pages, start, 0)
    lax.fori_loop(0, n_pages, wait, 0)

    def slab(r, h):
        return xbuf[pl.ds(r, n_chunks, stride=CMP_STRIDE), h, :]
    _compress_body(slab, n_chunks, pe_ref, w1_ref, w2_ref, o_ref)


def compress_paged(cache, layer, page_table, pe, w1, w2):
    b, n_pages = page_table.shape
    n_chunks = n_pages * (PAGE_SIZE // CMP_STRIDE)
    grid_spec = pltpu.PrefetchScalarGridSpec(
        num_scalar_prefetch=1,
        grid=(b,),
        in_specs=[pl.BlockSpec(memory_space=pl.ANY),
                  pl.BlockSpec(pe.shape, lambda bi, pt: (0, 0)),
                  pl.BlockSpec(w1.shape, lambda bi, pt: (0, 0)),
                  pl.BlockSpec(w2.shape, lambda bi, pt: (0, 0))],
        out_specs=pl.BlockSpec((None, NSA_KV_HEADS, n_chunks, HEAD_DIM), lambda bi, pt: (bi, 0, 0, 0)),
        scratch_shapes=[pltpu.VMEM((n_pages * PAGE_SIZE, NSA_KV_HEADS, HEAD_DIM), F32), pltpu.SemaphoreType.DMA(())],
    )
    return pl.pallas_call(
        functools.partial(_compress_paged_kernel, layer=layer, n_pages=n_pages),
        grid_spec=grid_spec,
        out_shape=jax.ShapeDtypeStruct((b, NSA_KV_HEADS, n_chunks, HEAD_DIM), F32),
        compiler_params=_cparams(("arbitrary",)),
        name="compress_paged",
    )(page_table.reshape(-1), cache, pe, w1, w2)


def _cmp_select_sample_kernel(q_ref, kc_ref, vc_ref, cov_ref, o_ref, idx_ref, q8_ref, *, q_pos, n_blk):
    nc = kc_ref.shape[1]
    nb_pad = cov_ref.shape[1]
    blk = lax.broadcasted_iota(I32, (1, nb_pad), 1)
    cur = q_pos // SEL_BLOCK
    forced = (blk == 0) | (blk == cur) | (blk == cur - 1)
    future = (blk * SEL_BLOCK > q_pos) | (blk >= n_blk)
    c_last = lax.broadcasted_iota(I32, (GROUP_ROWS, nc), 1) * CMP_STRIDE + (CMP_BLOCK - 1)
    valid = (c_last <= q_pos) & (lax.broadcasted_iota(I32, (GROUP_ROWS, nc), 0) < NSA_GROUP)
    for h in range(NSA_KV_HEADS):
        q8f = _group_rows(q_ref, h)
        q8_ref[h] = q8f
        q8 = q8f.astype(BF16)
        s = jnp.where(valid, _nt(q8, kc_ref[h].astype(BF16)) * SCALE, NEG)
        m = jnp.max(s, axis=-1, keepdims=True)
        e = jnp.where(valid, jnp.exp(s - m), 0.0)
        p = (e / jnp.maximum(jnp.sum(e, axis=-1, keepdims=True), 1e-30)).astype(BF16)
        o_ref[h] = jnp.dot(p, vc_ref[h].astype(BF16), preferred_element_type=F32)
        imp = jnp.sum(jnp.dot(p, cov_ref[...], preferred_element_type=F32), axis=0, keepdims=True)
        imp = jnp.where(future, -1e9, jnp.where(forced, 1e9, imp))
        jj = lax.broadcasted_iota(I32, (nb_pad, nb_pad), 0)
        ss = lax.broadcasted_iota(I32, (nb_pad, nb_pad), 1)
        row = jnp.broadcast_to(imp, (nb_pad, nb_pad))
        col = jnp.sum(jnp.where(jj == ss, row, 0.0), axis=1, keepdims=True)
        beats = jnp.where(col > row, 1.0, jnp.where(col == row, jnp.where(jj < ss, 1.0, 0.0), 0.0))
        rank = jnp.sum(beats, axis=0, keepdims=True)
        want = lax.broadcasted_iota(I32, (SEL_TOP, nb_pad), 0).astype(F32)
        lane = lax.broadcasted_iota(I32, (SEL_TOP, nb_pad), 1).astype(F32)
        idx = jnp.sum(jnp.where(jnp.broadcast_to(rank, (SEL_TOP, nb_pad)) == want, lane, 0.0), axis=1, keepdims=True)
        idx_ref[h] = idx.astype(I32)


def cmp_select_sample(q, kcmp, vcmp, q_pos):
    b = q.shape[0]
    nc = kcmp.shape[2]
    n_blk = q_pos // SEL_BLOCK + 1
    nb_pad = -(-n_blk // LANES) * LANES
    cov = _cover_t(nc, nb_pad, nc - 1)[:, :nc].T
    cov = jnp.asarray(cov, BF16)
    return pl.pallas_call(
        functools.partial(_cmp_select_sample_kernel, q_pos=q_pos, n_blk=n_blk),
        grid=(b,),
        in_specs=[pl.BlockSpec((None, 1, TOK_W), lambda bi: (bi, 0, 0)),
                  pl.BlockSpec((None, NSA_KV_HEADS, nc, HEAD_DIM), lambda bi: (bi, 0, 0, 0)),
                  pl.BlockSpec((None, NSA_KV_HEADS, nc, HEAD_DIM), lambda bi: (bi, 0, 0, 0)),
                  pl.BlockSpec(cov.shape, lambda bi: (0, 0))],
        out_specs=[pl.BlockSpec((None, NSA_KV_HEADS, GROUP_ROWS, HEAD_DIM), lambda bi: (bi, 0, 0, 0)),
                   pl.BlockSpec((None, NSA_KV_HEADS, SEL_TOP, 1), lambda bi: (bi, 0, 0, 0)),
                   pl.BlockSpec((None, NSA_KV_HEADS, GROUP_ROWS, HEAD_DIM), lambda bi: (bi, 0, 0, 0))],
        out_shape=[jax.ShapeDtypeStruct((b, NSA_KV_HEADS, GROUP_ROWS, HEAD_DIM), F32),
                   jax.ShapeDtypeStruct((b, NSA_KV_HEADS, SEL_TOP, 1), I32),
                   jax.ShapeDtypeStruct((b, NSA_KV_HEADS, GROUP_ROWS, HEAD_DIM), F32)],
        compiler_params=_cparams(("parallel",)),
        name="cmp_select_sample",
    )(q, kcmp, vcmp, cov)


def _dot_new_row(q8, k_row):
    a = q8.astype(BF16).astype(F32)
    b = k_row.astype(BF16).astype(F32)
    return jnp.sum(a * b, axis=-1, keepdims=True)


SEL_S_BLOCKS = 4


def _sel_sample_kernel(pt_ref, ix_ref, q_ref, *refs, n_cache_blk):
    k_refs, v_refs = refs[:SEL_S_BLOCKS], refs[SEL_S_BLOCKS:2 * SEL_S_BLOCKS]
    kn_ref, vn_ref, o_ref, m_ref, l_ref, acc_ref = refs[2 * SEL_S_BLOCKS:]
    b, h, r = pl.program_id(0), pl.program_id(1), pl.program_id(2)

    @pl.when(r == 0)
    def _():
        m_ref[...] = jnp.full(m_ref.shape, NEG, F32)
        l_ref[...] = jnp.zeros(l_ref.shape, F32)
        acc_ref[...] = jnp.zeros(acc_ref.shape, F32)

    q8 = q_ref[...]
    n_keys = SEL_S_BLOCKS * SEL_BLOCK
    key_blk = lax.broadcasted_iota(I32, (GROUP_ROWS, n_keys), 1) // SEL_BLOCK
    in_cache = jnp.zeros((GROUP_ROWS, n_keys), I32)
    for j in range(SEL_S_BLOCKS):
        flag = (ix_ref[(b * NSA_KV_HEADS + h) * SEL_TOP + r * SEL_S_BLOCKS + j] < n_cache_blk).astype(I32)
        in_cache = jnp.where(key_blk == j, flag, in_cache)
    valid = (lax.broadcasted_iota(I32, (GROUP_ROWS, n_keys), 0) < NSA_GROUP) & (in_cache > 0)
    k_blk = jnp.concatenate([_pick_head(k_ref, h) for k_ref in k_refs], axis=0).astype(BF16)
    v_blk = jnp.concatenate([_pick_head(v_ref, h) for v_ref in v_refs], axis=0).astype(BF16)
    s = jnp.where(valid, _nt(q8.astype(BF16), k_blk) * SCALE, NEG)
    m_old = m_ref[...]
    m_new = jnp.maximum(m_old, jnp.max(s, axis=-1, keepdims=True))
    e = jnp.where(valid, jnp.exp(s - m_new), 0.0)
    alpha = jnp.exp(m_old - m_new)
    l_ref[...] = alpha * l_ref[...] + jnp.sum(e, axis=-1, keepdims=True)
    acc_ref[...] = alpha * acc_ref[...] + jnp.dot(e.astype(BF16), v_blk, preferred_element_type=F32)
    m_ref[...] = m_new

    @pl.when(r == SEL_TOP // SEL_S_BLOCKS - 1)
    def _():
        s_new = _dot_new_row(q8, kn_ref[...]) * SCALE
        m_old = m_ref[...]
        m_fin = jnp.maximum(m_old, s_new)
        alpha = jnp.exp(m_old - m_fin)
        e_new = jnp.exp(s_new - m_fin)
        l_fin = alpha * l_ref[...] + e_new
        acc = alpha * acc_ref[...] + e_new.astype(BF16).astype(F32) * vn_ref[...].astype(BF16).astype(F32)
        o_ref[...] = acc / jnp.maximum(l_fin, 1e-30)


def sel_attention_sample(q8, idx, cache_k, cache_v, layer, page_table, k_new, v_new):
    b, n_pages = page_table.shape
    per_page = PAGE_SIZE // SEL_BLOCK
    n_cache_blk = n_pages * per_page

    def blk_index(j):
        def index(bi, h, r, pt, ix):
            s = jnp.minimum(ix[(bi * NSA_KV_HEADS + h) * SEL_TOP + r * SEL_S_BLOCKS + j], n_cache_blk - 1)
            return (layer, pt[bi * n_pages + s // per_page], s % per_page, 0, 0)
        return index

    blk_specs = [pl.BlockSpec((None, None, SEL_BLOCK, NSA_KV_HEADS, HEAD_DIM), blk_index(j)) for j in range(SEL_S_BLOCKS)]
    grid_spec = pltpu.PrefetchScalarGridSpec(
        num_scalar_prefetch=2,
        grid=(b, NSA_KV_HEADS, SEL_TOP // SEL_S_BLOCKS),
        in_specs=[pl.BlockSpec((None, None, GROUP_ROWS, HEAD_DIM), lambda bi, h, r, pt, ix: (bi, h, 0, 0))]
                 + blk_specs + blk_specs
                 + [pl.BlockSpec((None, 1, HEAD_DIM), lambda bi, h, r, pt, ix: (bi, 0, h)),
                    pl.BlockSpec((None, 1, HEAD_DIM), lambda bi, h, r, pt, ix: (bi, 0, h))],
        out_specs=pl.BlockSpec((None, None, GROUP_ROWS, HEAD_DIM), lambda bi, h, r, pt, ix: (bi, h, 0, 0)),
        scratch_shapes=[pltpu.VMEM((GROUP_ROWS, 1), F32), pltpu.VMEM((GROUP_ROWS, 1), F32),
                        pltpu.VMEM((GROUP_ROWS, HEAD_DIM), F32)],
    )
    return pl.pallas_call(
        functools.partial(_sel_sample_kernel, n_cache_blk=n_cache_blk),
        grid_spec=grid_spec,
        out_shape=jax.ShapeDtypeStruct((b, NSA_KV_HEADS, GROUP_ROWS, HEAD_DIM), F32),
        compiler_params=_cparams(("arbitrary", "arbitrary", "arbitrary")),
        name="sel_attention_sample",
    )(page_table.reshape(-1), idx.reshape(-1), q8, *([cache_k] * SEL_S_BLOCKS), *([cache_v] * SEL_S_BLOCKS), k_new, v_new)


def _win_sample_kernel(q_ref, k_ref, v_ref, kn_ref, vn_ref, oc_ref, os_ref, gate_ref, o_ref):
    h = pl.program_id(1)
    wb = k_ref.shape[0]
    k_win = _pick_head(k_ref, h).astype(BF16)
    v_win = _pick_head(v_ref, h).astype(BF16)
    q8 = q_ref[...]
    col = lax.broadcasted_iota(I32, (GROUP_ROWS, wb), 1)
    valid = col >= 1
    s = jnp.where(valid, _nt(q8.astype(BF16), k_win) * SCALE, NEG)
    s_new = _dot_new_row(q8, kn_ref[...]) * SCALE
    m = jnp.maximum(jnp.max(s, axis=-1, keepdims=True), s_new)
    e = jnp.where(valid, jnp.exp(s - m), 0.0)
    e_new = jnp.exp(s_new - m)
    den = jnp.maximum(jnp.sum(e, axis=-1, keepdims=True) + e_new, 1e-30)
    p = (e / den).astype(BF16)
    p_new = (e_new / den).astype(BF16).astype(F32)
    o_win = jnp.dot(p, v_win, preferred_element_type=F32) + p_new * vn_ref[...].astype(BF16).astype(F32)
    gate = jnp.broadcast_to(gate_ref[...], (GROUP_ROWS, LANES))
    lane = lax.broadcasted_iota(I32, (GROUP_ROWS, LANES), 1)
    head = h * NSA_GROUP + lax.broadcasted_iota(I32, (GROUP_ROWS, LANES), 0)
    g = [jnp.sum(jnp.where(lane == head * 3 + j, gate, 0.0), axis=-1, keepdims=True) for j in range(3)]
    o_ref[...] = g[0] * oc_ref[...] + g[1] * os_ref[...] + g[2] * o_win


def win_combine_sample(q8, win_k, win_v, layer, k_new, v_new, o_cmp, o_sel, gate):
    b = q8.shape[0]
    wb = win_k.shape[2]
    assert wb == WINDOW
    gspec = pl.BlockSpec((None, None, GROUP_ROWS, HEAD_DIM), lambda bi, h: (bi, h, 0, 0))
    wspec = pl.BlockSpec((None, None, wb, NSA_KV_HEADS, HEAD_DIM), lambda bi, h: (layer, bi, 0, 0, 0))
    nspec = pl.BlockSpec((None, 1, HEAD_DIM), lambda bi, h: (bi, 0, h))
    return pl.pallas_call(
        _win_sample_kernel,
        grid=(b, NSA_KV_HEADS),
        in_specs=[gspec, wspec, wspec, nspec, nspec, gspec, gspec,
                  pl.BlockSpec((None, 1, LANES), lambda bi, h: (bi, 0, 0))],
        out_specs=gspec,
        out_shape=jax.ShapeDtypeStruct((b, NSA_KV_HEADS, GROUP_ROWS, HEAD_DIM), F32),
        compiler_params=_cparams(("parallel", "arbitrary")),
        name="win_combine_sample",
    )(q8, win_k, win_v, k_new, v_new, o_cmp, o_sel, gate)


def nsa_sample_step(prep, layer, caches, win_k, win_v, page_table, pe, w1, w2):
    q, kc, vc, ks, vs, kw, vw, gate = prep
    b = q.shape[0]
    past_len = page_table.shape[1] * PAGE_SIZE
    cache_ck, cache_cv, cache_sk, cache_sv = caches
    kcmp = compress_paged(cache_ck, layer, page_table, pe[0], w1[0], w2[0])
    vcmp = compress_paged(cache_cv, layer, page_table, pe[1], w1[1], w2[1])
    o_cmp, idx, q8 = cmp_select_sample(q, kcmp, vcmp, past_len)
    o_sel = sel_attention_sample(q8, idx, cache_sk, cache_sv, layer, page_table, ks, vs)
    o = win_combine_sample(q8, win_k, win_v, layer, kw, vw, o_cmp, o_sel, gate)
    return o[:, :, :NSA_GROUP].reshape(b, 1, TOK_W)


def _reorder_nsa_weight(w):
    n_gl = 3 * NSA_HEADS
    parts = [w[:, :NSA_MQ0], w[:, NSA_MQ0 + n_gl:NSA_MQ0 + n_gl + MEM_W], w[:, NSA_MQ0:NSA_MQ0 + n_gl]]
    wr = jnp.concatenate(parts, axis=1)
    return jnp.pad(wr, ((0, 0), (0, NSA_ZW - wr.shape[1]))).astype(BF16)


def kernel(x_prompt, x_sample, state_pool, cache_cmp_k, cache_cmp_v, cache_sel_k, cache_sel_v, state_win_k, state_win_v, cache_mem_k, cache_mem_v, page_table, mem_prompt, norm_mix_g, norm_ffn_g, norm_mem_g, w_mem_kv, mem_q_norm_g, mem_k_norm_g, w_in_pool, w_pool_grp, pool_scale, w_out_pool, w_in_nsa, b_gate, nsa_q_norm_g, nsa_k_norm_g, cmp_pe, cmp_w1, cmp_w2, w_out_nsa, w_router, b_router, w_gate, w_up, w_down):
    bp, t_p, d = x_prompt.shape
    bs, t_s, _ = x_sample.shape
    assert t_s == 1
    n_p, n_s = bp * t_p, bs * t_s
    m_len = mem_prompt.shape[1]
    past_len = page_table.shape[1] * PAGE_SIZE
    pos_p = jnp.arange(t_p, dtype=I32)
    pos_s = past_len + jnp.arange(t_s, dtype=I32)
    xp = x_prompt.reshape(n_p, d)
    xs = x_sample.reshape(n_s, d)
    mem_flat = mem_prompt.reshape(-1, d)
    pool_p, pool_s, rows_p, rows_s, win_p, win_s, mem_k_p, mem_v_p = [], [], [], [], [], [], [], []
    for i in range(DEPTH):
        li = i // 2
        kv = proj(mem_flat, w_mem_kv[i].astype(BF16), gain=norm_mem_g[i])
        mk = head_norm(kv, 0, MEM_W, mem_k_norm_g[i]).reshape(bp, m_len, MEM_W)
        mv = kv[:, MEM_W:].reshape(bp, m_len, MEM_W)
        mem_k_p.append(mk.reshape(bp, m_len, MEM_HEADS, HEAD_DIM))
        mem_v_p.append(mv.reshape(bp, m_len, MEM_HEADS, HEAD_DIM))
        mk_s = cache_mem_k[i].reshape(bs, m_len, MEM_W)
        mv_s = cache_mem_v[i].reshape(bs, m_len, MEM_W)
        if i % 2 == 0:
            w_in = w_in_pool[li].astype(BF16)
            zp = proj(xp, w_in, gain=norm_mix_g[i]).reshape(bp, t_p, -1)
            zs = proj(xs, w_in, gain=norm_mix_g[i]).reshape(bs, t_s, -1)
            op = pool_mix(zp, None, w_pool_grp[li], pool_scale[li], 0)
            zs16 = jnp.pad(zs, ((0, 0), (0, POOL_HALO - t_s), (0, 0)))
            halo = jnp.pad(state_pool[li], ((0, 0), (1, 0), (0, 0)))
            os_ = pool_mix(zs16, halo, w_pool_grp[li], pool_scale[li], past_len)[:, :t_s]
            pool_p.append(zp[:, t_p - POOL_STATE:, :TOK_W])
            pool_s.append(jnp.concatenate([state_pool[li], zs[..., :TOK_W]], axis=1)[:, -POOL_STATE:])
            mq0 = TOK_W
            w_out = w_out_pool[li]
        else:
            w_in = _reorder_nsa_weight(w_in_nsa[li])
            zp = proj(xp, w_in, gain=norm_mix_g[i]).reshape(bp, t_p, -1)
            zs = proj(xs, w_in, gain=norm_mix_g[i]).reshape(bs, t_s, -1)
            q, kc, vc, ks, vs, kw, vw, gate = nsa_prep(zp, pos_p, nsa_q_norm_g[li], nsa_k_norm_g[li], b_gate[li])
            kcmp = compress_rows(kc, cmp_pe[li, 0], cmp_w1[li, 0], cmp_w2[li, 0])
            vcmp = compress_rows(vc, cmp_pe[li, 1], cmp_w1[li, 1], cmp_w2[li, 1])
            o_cmp, sel_t = cmp_select_prompt(q, kcmp, vcmp)
            o_sel = sel_attention_prompt_t(q, ks, vs, sel_t)
            op = win_combine_prompt(q, kw, vw, o_cmp, o_sel, gate)
            kvr = lambda a: a.reshape(a.shape[0], a.shape[1], NSA_KV_HEADS, HEAD_DIM)
            rows_p.append(tuple(kvr(a) for a in (kc, vc, ks, vs)))
            wb = min(WINDOW, t_p)
            win_p.append((kvr(kw[:, t_p - wb:]), kvr(vw[:, t_p - wb:])))
            prep_s = nsa_prep(zs, pos_s, nsa_q_norm_g[li], nsa_k_norm_g[li], b_gate[li])
            os_ = nsa_sample_step(prep_s, li, (cache_cmp_k, cache_cmp_v, cache_sel_k, cache_sel_v),
                                  state_win_k, state_win_v, page_table, cmp_pe[li], cmp_w1[li], cmp_w2[li])
            rows_s.append(tuple(kvr(a) for a in prep_s[1:5]))
            win_s.append(tuple(jnp.concatenate([st[li][:, t_s:], kvr(new)], axis=1)
                               for st, new in ((state_win_k, prep_s[5]), (state_win_v, prep_s[6]))))
            mq0 = NSA_MQ0
            w_out = w_out_nsa[li]
        ap = mem_attention(zp, mq0, mk, mv, mem_q_norm_g[i])
        as_ = mem_attention(zs, mq0, mk_s, mv_s, mem_q_norm_g[i])
        w_out = w_out.astype(BF16)
        xp = proj([op.reshape(n_p, TOK_W), ap.reshape(n_p, MEM_W)], w_out, residual=xp)
        xs = proj([os_.reshape(n_s, TOK_W), as_.reshape(n_s, MEM_W)], w_out, residual=xs)
        xn, e_p, g_p = router(xp, norm_ffn_g[i], w_router, b_router, xn_rows=n_p + LANES)
        xs_pad = jnp.pad(xs, ((0, LANES - n_s), (0, 0)))
        xn, e_s, g_s = router(xs_pad, norm_ffn_g[i], w_router, b_router, xn_rows=n_p + LANES, into=xn, row0=n_p)
        xp, xs = moe_ffn2([xp, xs],
                          xn,
                          jnp.concatenate([e_p, e_s[:, :n_s]], axis=1),
                          [g_p, g_s[:, :n_s]],
                          w_gate, w_up, w_down, i)
    stk = lambda lst, j: jnp.stack([r[j] for r in lst])
    return (xp.reshape(bp, t_p, d), xs.reshape(bs, t_s, d),
            jnp.stack(pool_p), jnp.stack(pool_s),
            stk(rows_p, 0), stk(rows_p, 1), stk(rows_p, 2), stk(rows_p, 3),
            stk(rows_s, 0), stk(rows_s, 1), stk(rows_s, 2), stk(rows_s, 3),
            stk(win_p, 0), stk(win_p, 1), stk(win_s, 0), stk(win_s, 1),
            jnp.stack(mem_k_p), jnp.stack(mem_v_p))
```

```python
import functools

import jax
import jax.numpy as jnp
import numpy as np
from jax import lax
from jax.experimental import pallas as pl
from jax.experimental.pallas import tpu as pltpu

D_MODEL = 2048
DEPTH = 4
PAGE_SIZE = 128
HEAD_DIM = 128
ROPE_THETA = 500000.0
ROPE_DIM = HEAD_DIM // 4
MEM_HEADS = 4
MEM_W = MEM_HEADS * HEAD_DIM
TOK_W = D_MODEL - MEM_W
POOL_WINDOWS = (2, 4, 8, 16)
POOL_GROUP = TOK_W // len(POOL_WINDOWS)
POOL_STATE = max(POOL_WINDOWS) - 1
POOL_HALO = POOL_STATE + 1
NSA_HEADS = TOK_W // HEAD_DIM
NSA_KV_HEADS = 2
NSA_GROUP = NSA_HEADS // NSA_KV_HEADS
KV_W = NSA_KV_HEADS * HEAD_DIM
GROUP_W = NSA_GROUP * HEAD_DIM
CMP_BLOCK = 32
CMP_STRIDE = 16
SEL_BLOCK = 64
SEL_TOP = 16
WINDOW = 512
NSA_TOK_IN = NSA_HEADS * HEAD_DIM + 6 * KV_W + 3 * NSA_HEADS
N_EXPERTS = 16
N_GROUPS = 4
EXP_PER_GROUP = N_EXPERTS // N_GROUPS
TOP_K = 2
D_FF = 1024
EPS = 1e-6
NEG = -1e30
SCALE = HEAD_DIM ** -0.5
LANES = 128

V7X_VMEM_BYTES = 64 * 1024 * 1024
VMEM_LIMIT = V7X_VMEM_BYTES * 3 // 4

BF16 = jnp.bfloat16
F32 = jnp.float32
I32 = jnp.int32

NSA_Q0 = 0
NSA_KV0 = NSA_HEADS * HEAD_DIM
NSA_MQ0 = NSA_KV0 + 6 * KV_W
NSA_GL0 = NSA_MQ0 + MEM_W
NSA_ZW = 3840


def _cparams(sem):
    return pltpu.CompilerParams(dimension_semantics=sem, vmem_limit_bytes=VMEM_LIMIT)


def _nt(a, b):
    return lax.dot_general(a, b, (((1,), (1,)), ((), ())), preferred_element_type=F32)


def _tn(a, b):
    return lax.dot_general(a, b, (((0,), (0,)), ((), ())), preferred_element_type=F32)


def _rms(x, g):
    return x * lax.rsqrt(jnp.mean(x * x, axis=-1, keepdims=True) + EPS) * g


def _pick_tile(n, cap, unit):
    if n <= cap:
        return n
    best = None
    for t in range(unit, cap + 1, unit):
        if n % t == 0:
            best = t
    assert best is not None, (n, cap, unit)
    return best


def _proj_kernel(*refs, n_x, norm, residual):
    x_refs, refs = refs[:n_x], refs[n_x:]
    if norm:
        g_ref, refs = refs[0], refs[1:]
    w_ref, refs = refs[0], refs[1:]
    if residual:
        r_ref, refs = refs[0], refs[1:]
    o_ref, xn_ref = refs

    @pl.when(pl.program_id(1) == 0)
    def _():
        off = 0
        for x_ref in x_refs:
            x = x_ref[...].astype(F32)
            if norm:
                x = _rms(x, g_ref[...])
            xn_ref[:, off:off + x.shape[1]] = x.astype(BF16)
            off += x.shape[1]

    y = jnp.dot(xn_ref[...], w_ref[...], preferred_element_type=F32)
    if residual:
        y = y + r_ref[...]
    o_ref[...] = y


PROJ_RESIDENT_N = 2048


def proj(xs, w_bf16, gain=None, residual=None):
    if not isinstance(xs, (list, tuple)):
        xs = [xs]
    assert gain is None or len(xs) == 1
    m = xs[0].shape[0]
    k = sum(x.shape[1] for x in xs)
    n = w_bf16.shape[1]
    if n <= PROJ_RESIDENT_N:
        tm, tn = _pick_tile(m, 512, 8), n
    else:
        tm, tn = _pick_tile(m, 1024, 8), _pick_tile(n, 1024, LANES)
    in_specs = [pl.BlockSpec((tm, x.shape[1]), lambda i, j: (i, 0)) for x in xs]
    args = list(xs)
    if gain is not None:
        in_specs.append(pl.BlockSpec((1, k), lambda i, j: (0, 0)))
        args.append(gain.reshape(1, k).astype(F32))
    in_specs.append(pl.BlockSpec((k, tn), lambda i, j: (0, j)))
    args.append(w_bf16)
    if residual is not None:
        in_specs.append(pl.BlockSpec((tm, tn), lambda i, j: (i, j)))
        args.append(residual)
    return pl.pallas_call(
        functools.partial(_proj_kernel, n_x=len(xs), norm=gain is not None, residual=residual is not None),
        grid=(m // tm, n // tn),
        in_specs=in_specs,
        out_specs=pl.BlockSpec((tm, tn), lambda i, j: (i, j)),
        out_shape=jax.ShapeDtypeStruct((m, n), F32),
        scratch_shapes=[pltpu.VMEM((tm, k), BF16)],
        compiler_params=_cparams(("parallel", "arbitrary")),
        name="proj",
    )(*args)


def _head_norm_kernel(x_ref, g_ref, o_ref):
    x = x_ref[...]
    g = g_ref[...]
    n_heads = x.shape[1] // HEAD_DIM
    o_ref[...] = jnp.concatenate(
        [_rms(x[:, h * HEAD_DIM:(h + 1) * HEAD_DIM], g) for h in range(n_heads)], axis=1)


def head_norm(x, col0, width, gain):
    m = x.shape[0]
    assert col0 % width == 0
    return pl.pallas_call(
        _head_norm_kernel,
        grid=(1,),
        in_specs=[pl.BlockSpec((m, width), lambda i: (0, col0 // width)),
                  pl.BlockSpec((1, HEAD_DIM), lambda i: (0, 0))],
        out_specs=pl.BlockSpec((m, width), lambda i: (0, 0)),
        out_shape=jax.ShapeDtypeStruct((m, width), F32),
        compiler_params=_cparams(("arbitrary",)),
        name="head_norm",
    )(x, gain.reshape(1, HEAD_DIM))


def _mem_attn_kernel(q_ref, k_ref, v_ref, g_ref, o_ref):
    q = q_ref[...]
    rows = q.shape[0]
    if rows < 8:
        q = jnp.broadcast_to(q[0:1], (8, q.shape[1]))
    g = g_ref[...]
    outs = []
    for h in range(MEM_HEADS):
        sl = slice(h * HEAD_DIM, (h + 1) * HEAD_DIM)
        qh = _rms(q[:, sl], g).astype(BF16)
        s = _nt(qh, k_ref[:, sl].astype(BF16)) * SCALE
        m = jnp.max(s, axis=-1, keepdims=True)
        e = jnp.exp(s - m)
        p = e / jnp.sum(e, axis=-1, keepdims=True)
        outs.append(jnp.dot(p.astype(BF16), v_ref[:, sl].astype(BF16), preferred_element_type=F32))
    o = jnp.concatenate(outs, axis=1)
    o_ref[...] = o[:rows]


def mem_attention(z, col0, k, v, gain):
    b, t, _ = z.shape
    m = k.shape[1]
    tq = _pick_tile(t, 512, 8)
    assert col0 % MEM_W == 0
    return pl.pallas_call(
        _mem_attn_kernel,
        grid=(b, t // tq),
        in_specs=[pl.BlockSpec((None, tq, MEM_W), lambda bi, i: (bi, i, col0 // MEM_W)),
                  pl.BlockSpec((None, m, MEM_W), lambda bi, i: (bi, 0, 0)),
                  pl.BlockSpec((None, m, MEM_W), lambda bi, i: (bi, 0, 0)),
                  pl.BlockSpec((1, HEAD_DIM), lambda bi, i: (0, 0))],
        out_specs=pl.BlockSpec((None, tq, MEM_W), lambda bi, i: (bi, i, 0)),
        out_shape=jax.ShapeDtypeStruct((b, t, MEM_W), F32),
        compiler_params=_cparams(("parallel", "arbitrary")),
        name="mem_attention",
    )(z, k, v, gain.reshape(1, HEAD_DIM))


def _pool_kernel(u_ref, halo_ref, w_ref, sc_ref, o_ref, *, pos0, zero_first_halo):
    tq = u_ref.shape[0]
    qi = pl.program_id(1)
    u = u_ref[...]
    halo = halo_ref[...]
    if zero_first_halo:
        halo = jnp.where(qi > 0, halo, 0.0)
    pos = pos0 + qi * tq + lax.broadcasted_iota(I32, (tq, 1), 0)
    outs = []
    for g, w in enumerate(POOL_WINDOWS):
        cs = slice(g * POOL_GROUP, (g + 1) * POOL_GROUP)
        ug = u[:, cs]
        acc = jnp.concatenate([halo[:, cs], ug], axis=0)
        span = 1
        while span < w:
            acc = acc[span:] + acc[:-span]
            span *= 2
        ssum = acc[POOL_HALO - (w - 1):POOL_HALO - (w - 1) + tq]
        cnt = jnp.minimum(w, pos + 1).astype(F32)
        d = (ssum / cnt - ug).astype(BF16)
        y = jnp.dot(d, w_ref[g].astype(BF16), preferred_element_type=F32)
        outs.append(y)
    o_ref[...] = jnp.concatenate(outs, axis=1) * sc_ref[...]


def pool_mix(z, halo, w_grp, scale, pos0):
    b, t, _ = z.shape
    tq = _pick_tile(t, 256, 16)
    if halo is None:
        halo_arr = z
        halo_spec = pl.BlockSpec((None, POOL_HALO, TOK_W),
                                 lambda bi, i: (bi, jnp.maximum(i * (tq // POOL_HALO) - 1, 0), 0))
    else:
        assert t == tq
        halo_arr = halo
        halo_spec = pl.BlockSpec((None, POOL_HALO, TOK_W), lambda bi, i: (bi, 0, 0))
    return pl.pallas_call(
        functools.partial(_pool_kernel, pos0=pos0, zero_first_halo=halo is None),
        grid=(b, t // tq),
        in_specs=[pl.BlockSpec((None, tq, TOK_W), lambda bi, i: (bi, i, 0)),
                  halo_spec,
                  pl.BlockSpec(w_grp.shape, lambda bi, i: (0, 0, 0)),
                  pl.BlockSpec((1, TOK_W), lambda bi, i: (0, 0))],
        out_specs=pl.BlockSpec((None, tq, TOK_W), lambda bi, i: (bi, i, 0)),
        out_shape=jax.ShapeDtypeStruct((b, t, TOK_W), F32),
        compiler_params=_cparams(("parallel", "arbitrary")),
        name="pool_mix",
    )(z, halo_arr, w_grp, scale.reshape(1, TOK_W))


def _rope_tables(pos):
    half = ROPE_DIM // 2
    inv = 1.0 / (ROPE_THETA ** (jnp.arange(half, dtype=F32) * 2.0 / ROPE_DIM))
    ang = pos.astype(F32)[:, None] * inv[None, :]
    cos, sin = jnp.cos(ang), jnp.sin(ang)
    t = pos.shape[0]
    rest = HEAD_DIM - ROPE_DIM
    c = jnp.concatenate([cos, cos, jnp.ones((t, rest), F32)], axis=1)
    s_lo = jnp.concatenate([-sin, jnp.zeros((t, HEAD_DIM - half), F32)], axis=1)
    s_hi = jnp.concatenate([jnp.zeros((t, half), F32), sin, jnp.zeros((t, rest), F32)], axis=1)
    return c, s_lo, s_hi


def _nsa_prep_kernel(z_ref, c_ref, slo_ref, shi_ref, qg_ref, kg_ref, bg_ref,
                     q_ref, kc_ref, vc_ref, ks_ref, vs_ref, kw_ref, vw_ref, gate_ref):
    rows = z_ref.shape[0]
    c, slo, shi = c_ref[...], slo_ref[...], shi_ref[...]
    half = ROPE_DIM // 2

    def rope_norm(x, g):
        if rows < 8:
            x = jnp.broadcast_to(x[0:1], (8, HEAD_DIM))
        x = _rms(x, g)
        y = x * c + pltpu.roll(x, HEAD_DIM - half, 1) * slo + pltpu.roll(x, half, 1) * shi
        return y[:rows]

    qg = qg_ref[...]
    q_ref[...] = jnp.concatenate(
        [rope_norm(z_ref[:, NSA_Q0 + h * HEAD_DIM:NSA_Q0 + (h + 1) * HEAD_DIM], qg) for h in range(NSA_HEADS)], axis=1)
    for j, (o_ref, which) in enumerate(((kc_ref, 0), (vc_ref, None), (ks_ref, 1), (vs_ref, None), (kw_ref, 2), (vw_ref, None))):
        c0 = NSA_KV0 + j * KV_W
        if which is None:
            o_ref[...] = z_ref[:, c0:c0 + KV_W]
        else:
            g = kg_ref[which:which + 1, :]
            o_ref[...] = jnp.concatenate(
                [rope_norm(z_ref[:, c0 + h * HEAD_DIM:c0 + (h + 1) * HEAD_DIM], g) for h in range(NSA_KV_HEADS)], axis=1)
    gate_ref[...] = jax.nn.sigmoid(z_ref[:, NSA_GL0:NSA_GL0 + LANES] + bg_ref[...])


def nsa_prep(z, pos, q_g, k_g, b_gate):
    b, t, _ = z.shape
    tq = _pick_tile(t, 256, 8)
    c, slo, shi = _rope_tables(pos)
    if t < 8:
        c, slo, shi = (jnp.broadcast_to(a, (8, HEAD_DIM)) for a in (c, slo, shi))
    tt = max(tq, 8)
    bg = jnp.pad(b_gate.reshape(1, -1), ((0, 0), (0, LANES - b_gate.shape[-1])))
    tab_spec = pl.BlockSpec((tt, HEAD_DIM), lambda bi, i: (i, 0))
    kv_spec = pl.BlockSpec((None, tq, KV_W), lambda bi, i: (bi, i, 0))
    kv_shape = jax.ShapeDtypeStruct((b, t, KV_W), F32)
    return pl.pallas_call(
        _nsa_prep_kernel,
        grid=(b, t // tq),
        in_specs=[pl.BlockSpec((None, tq, NSA_ZW), lambda bi, i: (bi, i, 0)),
                  tab_spec, tab_spec, tab_spec,
                  pl.BlockSpec((1, HEAD_DIM), lambda bi, i: (0, 0)),
                  pl.BlockSpec((3, HEAD_DIM), lambda bi, i: (0, 0)),
                  pl.BlockSpec((1, LANES), lambda bi, i: (0, 0))],
        out_specs=[pl.BlockSpec((None, tq, TOK_W), lambda bi, i: (bi, i, 0))] + [kv_spec] * 6
                  + [pl.BlockSpec((None, tq, LANES), lambda bi, i: (bi, i, 0))],
        out_shape=[jax.ShapeDtypeStruct((b, t, TOK_W), F32)] + [kv_shape] * 6
                  + [jax.ShapeDtypeStruct((b, t, LANES), F32)],
        compiler_params=_cparams(("parallel", "arbitrary")),
        name="nsa_prep",
    )(z, c, slo, shi, q_g.reshape(1, HEAD_DIM), k_g, bg)


def _compress_body(slab, n_chunks, pe_ref, w1_ref, w2_ref, o_ref):
    pe = pe_ref[...]
    pe_lo = jnp.concatenate([pe[r:r + 1] for r in range(CMP_STRIDE)], axis=1)
    pe_hi = jnp.concatenate([pe[CMP_STRIDE + r:CMP_STRIDE + r + 1] for r in range(CMP_STRIDE)], axis=1)
    half_k = CMP_STRIDE * HEAD_DIM
    w_lo = w1_ref[0:half_k, :].astype(BF16)
    w_hi = w1_ref[half_k:2 * half_k, :].astype(BF16)
    w2 = w2_ref[...].astype(BF16)
    for h in range(NSA_KV_HEADS):
        xh = jnp.concatenate([slab(r, h) for r in range(CMP_STRIDE)], axis=1)
        a = jnp.dot((xh + pe_lo).astype(BF16), w_lo, preferred_element_type=F32)
        bb = jnp.dot((xh + pe_hi).astype(BF16), w_hi, preferred_element_type=F32)
        hid = jax.nn.gelu(a + pltpu.roll(bb, n_chunks - 1, 0))
        o_ref[h] = jnp.dot(hid.astype(BF16), w2, preferred_element_type=F32)


def _compress_kernel(x_ref, pe_ref, w1_ref, w2_ref, o_ref):
    def slab(r, h):
        c0 = (r * NSA_KV_HEADS + h) * HEAD_DIM
        return x_ref[:, c0:c0 + HEAD_DIM]
    _compress_body(slab, x_ref.shape[0], pe_ref, w1_ref, w2_ref, o_ref)


def compress_rows(x, pe, w1, w2):
    b, t, _ = x.shape
    n_chunks = t // CMP_STRIDE
    cw = CMP_STRIDE * KV_W
    xc = x.reshape(b, n_chunks, cw)
    return pl.pallas_call(
        _compress_kernel,
        grid=(b,),
        in_specs=[pl.BlockSpec((None, n_chunks, cw), lambda bi: (bi, 0, 0)),
                  pl.BlockSpec(pe.shape, lambda bi: (0, 0)),
                  pl.BlockSpec(w1.shape, lambda bi: (0, 0)),
                  pl.BlockSpec(w2.shape, lambda bi: (0, 0))],
        out_specs=pl.BlockSpec((None, NSA_KV_HEADS, n_chunks, HEAD_DIM), lambda bi: (bi, 0, 0, 0)),
        out_shape=jax.ShapeDtypeStruct((b, NSA_KV_HEADS, n_chunks, HEAD_DIM), F32),
        compiler_params=_cparams(("parallel",)),
        name="compress_rows",
    )(xc, pe, w1, w2)


def _stack_heads(q):
    return jnp.concatenate([q[:, g * HEAD_DIM:(g + 1) * HEAD_DIM] for g in range(NSA_GROUP)], axis=0)


def _unstack_heads(o, tq):
    return jnp.concatenate([o[g * tq:(g + 1) * tq] for g in range(NSA_GROUP)], axis=1)


def _rank_select(imp_t, n_real):
    n_blk = imp_t.shape[0]
    blk = lax.broadcasted_iota(I32, imp_t.shape, 0)
    cnt = jnp.zeros(imp_t.shape, F32)
    for j in range(n_real):
        row = imp_t[j:j + 1, :]
        beats = jnp.where(row > imp_t, 1.0, jnp.where(row == imp_t, jnp.where(blk > j, 1.0, 0.0), 0.0))
        cnt = cnt + beats
    return jnp.where(cnt < SEL_TOP, 1.0, 0.0)


def _cmp_select_kernel(q_ref, kc_ref, vc_ref, cov_ref, o_ref, sel_ref):
    tq = q_ref.shape[0]
    nc = kc_ref.shape[0]
    n_blk = sel_ref.shape[0]
    qi = pl.program_id(2)
    q6 = _stack_heads(q_ref[...]).astype(BF16)
    s = _nt(q6, kc_ref[...].astype(BF16)) * SCALE
    s = s.reshape(NSA_GROUP, tq, nc)
    q_pos = qi * tq + lax.broadcasted_iota(I32, (tq, nc), 0)
    c_last = lax.broadcasted_iota(I32, (tq, nc), 1) * CMP_STRIDE + (CMP_BLOCK - 1)
    valid = (c_last <= q_pos)[None]
    s = jnp.where(valid, s, NEG)
    m = jnp.max(s, axis=-1, keepdims=True)
    e = jnp.where(valid, jnp.exp(s - m), 0.0)
    p = (e / jnp.maximum(jnp.sum(e, axis=-1, keepdims=True), 1e-30)).astype(BF16)
    o = jnp.dot(p.reshape(NSA_GROUP * tq, nc), vc_ref[...].astype(BF16), preferred_element_type=F32)
    o_ref[...] = _unstack_heads(o, tq)
    p_cat = jnp.concatenate([p[g] for g in range(NSA_GROUP)], axis=1)
    imp_t = _nt(cov_ref[...], p_cat)
    blk = lax.broadcasted_iota(I32, (n_blk, tq), 0)
    pos_t = qi * tq + lax.broadcasted_iota(I32, (n_blk, tq), 1)
    cur = pos_t // SEL_BLOCK
    forced = (blk == 0) | (blk == cur) | (blk == cur - 1)
    future = blk * SEL_BLOCK > pos_t
    imp_t = jnp.where(future, -1e9, jnp.where(forced, 1e9, imp_t))
    sel_ref[...] = _rank_select(imp_t, n_blk).astype(BF16)


def _cover_t(nc, n_blk, nc_valid):
    c_start = np.arange(nc) * CMP_STRIDE
    c_last = c_start + CMP_BLOCK - 1
    s_start = np.arange(n_blk) * SEL_BLOCK
    cov = (c_start[None, :] < s_start[:, None] + SEL_BLOCK) & (c_last[None, :] >= s_start[:, None])
    cov = cov & (np.arange(nc)[None, :] < nc_valid)
    return np.tile(cov.astype(np.float32), (1, NSA_GROUP))


def cmp_select_prompt(q, kcmp, vcmp):
    b, t, _ = q.shape
    nc = kcmp.shape[2]
    n_blk = -(-t // SEL_BLOCK)
    tq = _pick_tile(t, 256, LANES)
    cov = jnp.asarray(_cover_t(nc, n_blk, nc - 1), BF16)
    return pl.pallas_call(
        _cmp_select_kernel,
        grid=(b, NSA_KV_HEADS, t // tq),
        in_specs=[pl.BlockSpec((None, tq, GROUP_W), lambda bi, h, i: (bi, i, h)),
                  pl.BlockSpec((None, None, nc, HEAD_DIM), lambda bi, h, i: (bi, h, 0, 0)),
                  pl.BlockSpec((None, None, nc, HEAD_DIM), lambda bi, h, i: (bi, h, 0, 0)),
                  pl.BlockSpec(cov.shape, lambda bi, h, i: (0, 0))],
        out_specs=[pl.BlockSpec((None, tq, GROUP_W), lambda bi, h, i: (bi, i, h)),
                   pl.BlockSpec((None, None, n_blk, tq), lambda bi, h, i: (bi, h, 0, i))],
        out_shape=[jax.ShapeDtypeStruct((b, t, TOK_W), F32),
                   jax.ShapeDtypeStruct((b, NSA_KV_HEADS, n_blk, t), BF16)],
        compiler_params=_cparams(("parallel", "parallel", "arbitrary")),
        name="cmp_select_prompt",
    )(q, kcmp, vcmp, cov)


SEL_TQ = 128
SEL_TK = 512


def _heads_to_lanes_t(q):
    tq = q.shape[0]
    return jnp.concatenate([q[:, g * HEAD_DIM:(g + 1) * HEAD_DIM].T for g in range(NSA_GROUP)], axis=1)


def _lanes_to_heads_t(o_t, tq):
    return jnp.concatenate([o_t[:, g * tq:(g + 1) * tq].T for g in range(NSA_GROUP)], axis=1)


def _selt_kernel(q_ref, k_ref, v_ref, sel_ref, exp_ref, o_ref, mask_ref, m_ref, l_ref, acc_ref):
    t = k_ref.shape[0]
    qi = pl.program_id(2)
    q_t = _heads_to_lanes_t(q_ref[...]).astype(BF16)
    mask_ref[...] = (1.0 - jnp.dot(exp_ref[...], sel_ref[...], preferred_element_type=F32)) * NEG
    m_ref[...] = jnp.full(m_ref.shape, NEG, F32)
    l_ref[...] = jnp.zeros(l_ref.shape, F32)
    acc_ref[...] = jnp.zeros(acc_ref.shape, F32)
    q_lo = qi * SEL_TQ

    def key_tile(j, causal):
        lo, hi = j * SEL_TK, (j + 1) * SEL_TK
        k = k_ref[lo:hi, :].astype(BF16)
        v = v_ref[lo:hi, :].astype(BF16)
        bias = mask_ref[lo:hi, :]
        if causal:
            k_pos = lo + lax.broadcasted_iota(I32, (SEL_TK, SEL_TQ), 0)
            q_pos = q_lo + lax.broadcasted_iota(I32, (SEL_TK, SEL_TQ), 1)
            bias = jnp.where(k_pos <= q_pos, bias, NEG)
        s = jnp.dot(k, q_t, preferred_element_type=F32) * SCALE + jnp.concatenate([bias] * NSA_GROUP, axis=1)
        m_old = m_ref[...]
        m_new = jnp.maximum(m_old, jnp.max(s, axis=0, keepdims=True))
        e = jnp.exp(s - m_new)
        alpha = jnp.exp(m_old - m_new)
        l_ref[...] = alpha * l_ref[...] + jnp.sum(e, axis=0, keepdims=True)
        acc_ref[...] = alpha * acc_ref[...] + _tn(v, e.astype(BF16))
        m_ref[...] = m_new

    for j in range(t // SEL_TK):
        lo, hi = j * SEL_TK, (j + 1) * SEL_TK

        @pl.when(hi - 1 <= q_lo)
        def _():
            key_tile(j, False)

        @pl.when((lo <= q_lo + SEL_TQ - 1) & (hi - 1 > q_lo))
        def _():
            key_tile(j, True)

    o_t = acc_ref[...] / jnp.maximum(l_ref[...], 1e-30)
    o_ref[...] = _lanes_to_heads_t(o_t, SEL_TQ)


def sel_attention_prompt_t(q, ks, vs, sel_t):
    b, t, _ = q.shape
    n_blk = sel_t.shape[2]
    expand = (np.arange(t)[:, None] // SEL_BLOCK == np.arange(n_blk)[None, :]).astype(np.float32)
    expand = jnp.asarray(expand, BF16)
    return pl.pallas_call(
        _selt_kernel,
        grid=(b, NSA_KV_HEADS, t // SEL_TQ),
        in_specs=[pl.BlockSpec((None, SEL_TQ, GROUP_W), lambda bi, h, i: (bi, i, h)),
                  pl.BlockSpec((None, t, HEAD_DIM), lambda bi, h, i: (bi, 0, h)),
                  pl.BlockSpec((None, t, HEAD_DIM), lambda bi, h, i: (bi, 0, h)),
                  pl.BlockSpec((None, None, n_blk, SEL_TQ), lambda bi, h, i: (bi, h, 0, i)),
                  pl.BlockSpec((t, n_blk), lambda bi, h, i: (0, 0))],
        out_specs=pl.BlockSpec((None, SEL_TQ, GROUP_W), lambda bi, h, i: (bi, i, h)),
        out_shape=jax.ShapeDtypeStruct((b, t, TOK_W), F32),
        scratch_shapes=[pltpu.VMEM((t, SEL_TQ), F32),
                        pltpu.VMEM((1, NSA_GROUP * SEL_TQ), F32),
                        pltpu.VMEM((1, NSA_GROUP * SEL_TQ), F32),
                        pltpu.VMEM((HEAD_DIM, NSA_GROUP * SEL_TQ), F32)],
        compiler_params=_cparams(("parallel", "parallel", "arbitrary")),
        name="sel_attention_prompt",
    )(q, ks, vs, sel_t, expand)


WIN_TQ = 128


def _gate_cols(gate, h, j, tq):
    cols = []
    for g in range(NSA_GROUP):
        c = (h * NSA_GROUP + g) * 3 + j
        cols.append(jnp.broadcast_to(gate[:, c:c + 1], (tq, HEAD_DIM)))
    return jnp.concatenate(cols, axis=1)


def _win_combine_kernel(q_ref, k_ref, v_ref, oc_ref, os_ref, gate_ref, o_ref):
    tq = q_ref.shape[0]
    span = WINDOW + tq
    h = pl.program_id(1)
    qi = pl.program_id(2)
    start = pl.multiple_of(jnp.maximum(qi * tq - WINDOW, 0), tq)
    q_t = _heads_to_lanes_t(q_ref[...]).astype(BF16)
    k = k_ref[pl.ds(start, span), :].astype(BF16)
    v = v_ref[pl.ds(start, span), :].astype(BF16)
    diff = (qi * tq + lax.broadcasted_iota(I32, (span, tq), 1)) - (start + lax.broadcasted_iota(I32, (span, tq), 0))
    bias = jnp.where((diff >= 0) & (diff < WINDOW), 0.0, NEG)
    s = jnp.dot(k, q_t, preferred_element_type=F32) * SCALE + jnp.concatenate([bias] * NSA_GROUP, axis=1)
    e = jnp.exp(s - jnp.max(s, axis=0, keepdims=True))
    p = e / jnp.maximum(jnp.sum(e, axis=0, keepdims=True), 1e-30)
    o_win = _lanes_to_heads_t(_tn(v, p.astype(BF16)), tq)
    gate = gate_ref[...]
    for hh in range(NSA_KV_HEADS):
        @pl.when(h == hh)
        def _():
            o_ref[...] = (_gate_cols(gate, hh, 0, tq) * oc_ref[...] + _gate_cols(gate, hh, 1, tq) * os_ref[...]
                          + _gate_cols(gate, hh, 2, tq) * o_win)


def win_combine_prompt(q, kw, vw, o_cmp, o_sel, gate):
    b, t, _ = q.shape
    assert t >= WINDOW + WIN_TQ
    qspec = pl.BlockSpec((None, WIN_TQ, GROUP_W), lambda bi, h, i: (bi, i, h))
    kspec = pl.BlockSpec((None, t, HEAD_DIM), lambda bi, h, i: (bi, 0, h))
    return pl.pallas_call(
        _win_combine_kernel,
        grid=(b, NSA_KV_HEADS, t // WIN_TQ),
        in_specs=[qspec, kspec, kspec, qspec, qspec,
                  pl.BlockSpec((None, WIN_TQ, LANES), lambda bi, h, i: (bi, i, 0))],
        out_specs=qspec,
        out_shape=jax.ShapeDtypeStruct((b, t, TOK_W), F32),
        compiler_params=_cparams(("parallel", "parallel", "arbitrary")),
        name="win_combine_prompt",
    )(q, kw, vw, o_cmp, o_sel, gate)


def _router_kernel(x_ref, g_ref, wt_ref, b_ref, xn_ref, e_ref, gt_ref):
    xn = _rms(x_ref[...], g_ref[...])
    xn_ref[...] = xn
    lt = _nt(wt_ref[...], xn.astype(BF16))
    ex = jnp.exp(lt - jnp.max(lt, axis=0, keepdims=True))
    aff = ex / jnp.sum(ex, axis=0, keepdims=True)
    sel = aff + b_ref[...]
    row = lambda a, r: a[r:r + 1, :]

    best, g_idx = None, None
    for g in range(N_GROUPS):
        a, b, c, d = (row(sel, g * EXP_PER_GROUP + j) for j in range(EXP_PER_GROUP))
        hi1, lo1, hi2, lo2 = jnp.maximum(a, b), jnp.minimum(a, b), jnp.maximum(c, d), jnp.minimum(c, d)
        score = jnp.maximum(hi1, hi2) + jnp.maximum(jnp.minimum(hi1, hi2), jnp.maximum(lo1, lo2))
        if g == 0:
            best, g_idx = score, jnp.zeros(score.shape, I32)
        else:
            g_idx = jnp.where(score > best, g, g_idx)
            best = jnp.maximum(best, score)

    def in_group(a, j):
        out = row(a, j)
        for g in range(1, N_GROUPS):
            out = jnp.where(g_idx == g, row(a, g * EXP_PER_GROUP + j), out)
        return out

    v = [in_group(sel, j) for j in range(EXP_PER_GROUP)]
    af = [in_group(aff, j) for j in range(EXP_PER_GROUP)]

    def first_max(vals):
        m = functools.reduce(jnp.maximum, vals)
        loc = jnp.full(m.shape, EXP_PER_GROUP - 1, I32)
        for j in range(EXP_PER_GROUP - 2, -1, -1):
            loc = jnp.where(vals[j] == m, j, loc)
        return loc

    l1 = first_max(v)
    l2 = first_max([jnp.where(l1 == j, -jnp.inf, v[j]) for j in range(EXP_PER_GROUP)])
    pick = lambda loc: functools.reduce(lambda acc, j: jnp.where(loc == j, af[j], acc), range(1, EXP_PER_GROUP), af[0])
    a1, a2 = pick(l1), pick(l2)
    tot = a1 + a2
    e_ref[...] = jnp.concatenate([g_idx * EXP_PER_GROUP + l1, g_idx * EXP_PER_GROUP + l2], axis=0)
    gt_ref[...] = jnp.concatenate([a1 / tot, a2 / tot], axis=0)


def _router_into_kernel(x_ref, g_ref, wt_ref, b_ref, xn_all_ref, xn_ref, e_ref, gt_ref):
    del xn_all_ref
    _router_kernel(x_ref, g_ref, wt_ref, b_ref, xn_ref, e_ref, gt_ref)


def router(x, gain, w_router, b_router, xn_rows=None, into=None, row0=0):
    n, d = x.shape
    tm = _pick_tile(n, 1024, LANES)
    assert row0 % tm == 0
    xn_rows = n if xn_rows is None else xn_rows
    args = [x, gain.reshape(1, d), w_router.T.astype(BF16), b_router.reshape(N_EXPERTS, 1).astype(F32)]
    in_specs = [pl.BlockSpec((tm, d), lambda i: (i, 0)),
                pl.BlockSpec((1, d), lambda i: (0, 0)),
                pl.BlockSpec((N_EXPERTS, d), lambda i: (0, 0)),
                pl.BlockSpec((N_EXPERTS, 1), lambda i: (0, 0))]
    body, aliases = _router_kernel, {}
    if into is not None:
        assert into.shape == (xn_rows, d)
        args.append(into)
        in_specs.append(pl.BlockSpec(memory_space=pl.ANY))
        body, aliases = _router_into_kernel, {len(args) - 1: 0}
    return pl.pallas_call(
        body,
        grid=(n // tm,),
        in_specs=in_specs,
        out_specs=[pl.BlockSpec((tm, d), lambda i: (row0 // tm + i, 0)),
                   pl.BlockSpec((TOP_K, tm), lambda i: (0, i)),
                   pl.BlockSpec((TOP_K, tm), lambda i: (0, i))],
        out_shape=[jax.ShapeDtypeStruct((xn_rows, d), F32),
                   jax.ShapeDtypeStruct((TOP_K, n), I32),
                   jax.ShapeDtypeStruct((TOP_K, n), F32)],
        input_output_aliases=aliases,
        compiler_params=_cparams(("parallel",)),
        name="router",
    )(*args)


MOE_NF = 4
MOE_TF = D_FF // MOE_NF
MOE2_TM = 1024
MOE2_GROUP = 64
MOE2_CHUNK = 256
MOE2_GPS = MOE2_TM // MOE_NF // MOE2_GROUP
MOE2_VMEM_LIMIT = V7X_VMEM_BYTES * 7 // 8


def _moe2_kernel(te_ref, nu_ref, nv_ref, base_ref, order_ref, x_hbm, wg_ref, wu_ref, wd_ref, y_hbm,
                 xbuf, xb, ybuf, gsem, ssem, *, n_tok):
    n_items = n_tok * TOP_K

    def slot_item(tile, rr):
        return order_ref[jnp.minimum(base_ref[tile] + rr, n_items - 1)]
    i = pl.program_id(0)
    f = pl.program_id(1)
    n_used = nu_ref[0]
    cur = i % 2
    oth = 1 - cur
    n_groups = MOE2_TM // MOE2_GROUP

    def gather_group(tile, g):
        for r in range(MOE2_GROUP):
            rr = g * MOE2_GROUP + r
            item = slot_item(tile, rr)
            row = jnp.where(item >= n_tok, item - n_tok, item)
            pltpu.make_async_copy(x_hbm.at[pl.ds(row, 1), :], xbuf.at[pl.ds(rr, 1), :], gsem).start()

    def scatter_group(tile, buf, g):
        for r in range(MOE2_GROUP):
            rr = g * MOE2_GROUP + r
            row = jnp.where(rr < nv_ref[tile], slot_item(tile, rr), n_items + tile * MOE2_TM + rr)
            pltpu.make_async_copy(ybuf.at[buf, pl.ds(rr, 1), :], y_hbm.at[pl.ds(row, 1), :], ssem.at[buf]).start()

    def wait_gather_group():
        pltpu.make_async_copy(x_hbm.at[pl.ds(0, MOE2_GROUP), :], xbuf.at[pl.ds(0, MOE2_GROUP), :], gsem).wait()

    def wait_scatter_group(buf):
        pltpu.make_async_copy(ybuf.at[buf, pl.ds(0, MOE2_GROUP), :], y_hbm.at[pl.ds(0, MOE2_GROUP), :], ssem.at[buf]).wait()

    @pl.when((i == 0) & (f == 0))
    def _():
        xbuf[...] = jnp.zeros(xbuf.shape, F32)

        def body(g, carry):
            @pl.when(g * MOE2_GROUP < nv_ref[0])
            def _():
                gather_group(0, g)
            return carry
        lax.fori_loop(0, n_groups, body, 0)

    @pl.when(i < n_used)
    def _():
        n_valid = nv_ref[i]

        @pl.when(f == 0)
        def _():
            for g in range(n_groups):
                @pl.when(g * MOE2_GROUP < n_valid)
                def _():
                    wait_gather_group()
            xb[...] = xbuf[...].astype(BF16)
            ybuf[cur] = jnp.zeros(ybuf.shape[1:], F32)

        nv_next = nv_ref[i + 1]
        nv_prev = nv_ref[jnp.maximum(i - 1, 0)]
        for gi in range(MOE2_GPS):
            g = f * MOE2_GPS + gi

            @pl.when((i + 1 < n_used) & (g * MOE2_GROUP < nv_next))
            def _():
                gather_group(i + 1, g)

            @pl.when((i > 0) & (g * MOE2_GROUP < nv_prev))
            def _():
                scatter_group(i - 1, oth, g)

        for c in range(MOE2_TM // MOE2_CHUNK):
            lo, hi = c * MOE2_CHUNK, (c + 1) * MOE2_CHUNK

            @pl.when(lo < n_valid)
            def _():
                x = xb[lo:hi, :]
                a = jnp.dot(x, wg_ref[...].astype(BF16), preferred_element_type=F32)
                u = jnp.dot(x, wu_ref[...].astype(BF16), preferred_element_type=F32)
                h = (a * jax.nn.sigmoid(a) * u).astype(BF16)
                ybuf[cur, lo:hi, :] += jnp.dot(h, wd_ref[...].astype(BF16), preferred_element_type=F32)

        @pl.when(f == MOE_NF - 1)
        def _():
            for g in range(n_groups):
                @pl.when((i > 0) & (g * MOE2_GROUP < nv_prev))
                def _():
                    wait_scatter_group(oth)

    @pl.when((i == n_used) & (f == 0))
    def _():
        nv_last = nv_ref[i - 1]

        def start(g, carry):
            @pl.when(g * MOE2_GROUP < nv_last)
            def _():
                scatter_group(i - 1, oth, g)
            return carry
        lax.fori_loop(0, n_groups, start, 0)

        def wait(g, carry):
            @pl.when(g * MOE2_GROUP < nv_last)
            def _():
                wait_scatter_group(oth)
            return carry
        lax.fori_loop(0, n_groups, wait, 0)


def moe_experts2(xn, n_tok, tile_expert, n_used, n_valid, base, order, n_out_rows, w_gate, w_up, w_down, layer):
    n, d = xn.shape
    n_tiles = tile_expert.shape[0]

    def tile(i, nu):
        return jnp.minimum(i, nu[0] - 1)

    def fcol(i, f, nu):
        return jnp.where(i < nu[0], f, MOE_NF - 1)

    grid_spec = pltpu.PrefetchScalarGridSpec(
        num_scalar_prefetch=5,
        grid=(n_tiles, MOE_NF),
        in_specs=[
            pl.BlockSpec(memory_space=pl.ANY),
            pl.BlockSpec((None, None, d, MOE_TF), lambda i, f, te, nu, nv, s, t: (layer, te[tile(i, nu)], 0, fcol(i, f, nu))),
            pl.BlockSpec((None, None, d, MOE_TF), lambda i, f, te, nu, nv, s, t: (layer, te[tile(i, nu)], 0, fcol(i, f, nu))),
            pl.BlockSpec((None, None, MOE_TF, d), lambda i, f, te, nu, nv, s, t: (layer, te[tile(i, nu)], fcol(i, f, nu), 0)),
        ],
        out_specs=pl.BlockSpec(memory_space=pl.ANY),
        scratch_shapes=[pltpu.VMEM((MOE2_TM, d), F32),
                        pltpu.VMEM((MOE2_TM, d), BF16),
                        pltpu.VMEM((2, MOE2_TM, d), F32),
                        pltpu.SemaphoreType.DMA(()),
                        pltpu.SemaphoreType.DMA((2,))],
    )
    return pl.pallas_call(
        functools.partial(_moe2_kernel, n_tok=n_tok),
        grid_spec=grid_spec,
        out_shape=jax.ShapeDtypeStruct((n_out_rows, d), F32),
        compiler_params=pltpu.CompilerParams(dimension_semantics=("arbitrary", "arbitrary"),
                                             vmem_limit_bytes=MOE2_VMEM_LIMIT),
        name="moe_experts",
    )(tile_expert, n_used, n_valid, base, order, xn, w_gate, w_up, w_down)


def moe_ffn2(xs, xn, experts, gates, w_gate, w_up, w_down, layer):
    n, d = experts.shape[1], xn.shape[1]
    tm = MOE2_TM
    n_items = n * TOP_K
    n_tiles = -(-n_items // tm) + N_EXPERTS
    n_slots = n_tiles * tm
    flat_e = experts.reshape(-1)
    sizes = jnp.sum((flat_e[:, None] == jnp.arange(N_EXPERTS, dtype=I32)[None, :]).astype(I32), axis=0)
    padded = -(-sizes // tm) * tm
    ends_p = jnp.cumsum(padded)
    starts_p = ends_p - padded
    starts = jnp.cumsum(sizes) - sizes
    order = jnp.argsort(flat_e).astype(I32)
    tile_start = jnp.arange(n_tiles, dtype=I32) * tm
    tile_expert = jnp.minimum(jnp.searchsorted(ends_p, tile_start, side='right'), N_EXPERTS - 1).astype(I32)
    n_used = (ends_p[-1:] // tm).astype(I32)
    n_valid = jnp.clip(sizes[tile_expert] - (tile_start - starts_p[tile_expert]), 0, tm).astype(I32)
    base = (starts[tile_expert] + tile_start - starts_p[tile_expert]).astype(I32)
    y = moe_experts2(xn, n, tile_expert, n_used, n_valid, base, order, n_items + n_slots,
                     w_gate, w_up, w_down, layer)
    outs, r0 = [], 0
    for x, g in zip(xs, gates):
        r1 = r0 + x.shape[0]
        outs.append(x + y[r0:r1] * g[0][:, None] + y[n + r0:n + r1] * g[1][:, None])
        r0 = r1
    return outs


GROUP_ROWS = 8


def _pick_head(ref, h):
    out = ref[:, 0, :]
    for hh in range(1, NSA_KV_HEADS):
        out = jnp.where(h == hh, ref[:, hh, :], out)
    return out


def _group_rows(q_ref, h):
    rows = [q_ref[:, (h * NSA_GROUP + g) * HEAD_DIM:(h * NSA_GROUP + g + 1) * HEAD_DIM] for g in range(NSA_GROUP)]
    rows.append(jnp.zeros((GROUP_ROWS - NSA_GROUP, HEAD_DIM), F32))
    return jnp.concatenate(rows, axis=0)


def _compress_paged_kernel(pt_ref, cache_hbm, pe_ref, w1_ref, w2_ref, o_ref, xbuf, sem, *, layer, n_pages):
    b = pl.program_id(0)
    n_chunks = n_pages * (PAGE_SIZE // CMP_STRIDE)

    def page_copy(p):
        page = pt_ref[b * n_pages + p]
        return pltpu.make_async_copy(cache_hbm.at[layer, page], xbuf.at[pl.ds(p * PAGE_SIZE, PAGE_SIZE)], sem)

    def start(p, carry):
        page_copy(p).start()
        return carry

    def wait(p, carry):
        page_copy(p).wait()
        return carry

    lax.fori_loop(0, n_pages, start, 0)
    lax.fori_loop(0, n_pages, wait, 0)

    def slab(r, h):
        return xbuf[pl.ds(r, n_chunks, stride=CMP_STRIDE), h, :]
    _compress_body(slab, n_chunks, pe_ref, w1_ref, w2_ref, o_ref)


def compress_paged(cache, layer, page_table, pe, w1, w2):
    b, n_pages = page_table.shape
    n_chunks = n_pages * (PAGE_SIZE // CMP_STRIDE)
    grid_spec = pltpu.PrefetchScalarGridSpec(
        num_scalar_prefetch=1,
        grid=(b,),
        in_specs=[pl.BlockSpec(memory_space=pl.ANY),
                  pl.BlockSpec(pe.shape, lambda bi, pt: (0, 0)),
                  pl.BlockSpec(w1.shape, lambda bi, pt: (0, 0)),
                  pl.BlockSpec(w2.shape, lambda bi, pt: (0, 0))],
        out_specs=pl.BlockSpec((None, NSA_KV_HEADS, n_chunks, HEAD_DIM), lambda bi, pt: (bi, 0, 0, 0)),
        scratch_shapes=[pltpu.VMEM((n_pages * PAGE_SIZE, NSA_KV_HEADS, HEAD_DIM), F32), pltpu.SemaphoreType.DMA(())],
    )
    return pl.pallas_call(
        functools.partial(_compress_paged_kernel, layer=layer, n_pages=n_pages),
        grid_spec=grid_spec,
        out_shape=jax.ShapeDtypeStruct((b, NSA_KV_HEADS, n_chunks, HEAD_DIM), F32),
        compiler_params=_cparams(("arbitrary",)),
        name="compress_paged",
    )(page_table.reshape(-1), cache, pe, w1, w2)


def _cmp_select_sample_kernel(q_ref, kc_ref, vc_ref, cov_ref, o_ref, idx_ref, q8_ref, *, q_pos, n_blk):
    nc = kc_ref.shape[1]
    nb_pad = cov_ref.shape[1]
    blk = lax.broadcasted_iota(I32, (1, nb_pad), 1)
    cur = q_pos // SEL_BLOCK
    forced = (blk == 0) | (blk == cur) | (blk == cur - 1)
    future = (blk * SEL_BLOCK > q_pos) | (blk >= n_blk)
    c_last = lax.broadcasted_iota(I32, (GROUP_ROWS, nc), 1) * CMP_STRIDE + (CMP_BLOCK - 1)
    valid = (c_last <= q_pos) & (lax.broadcasted_iota(I32, (GROUP_ROWS, nc), 0) < NSA_GROUP)
    for h in range(NSA_KV_HEADS):
        q8f = _group_rows(q_ref, h)
        q8_ref[h] = q8f
        q8 = q8f.astype(BF16)
        s = jnp.where(valid, _nt(q8, kc_ref[h].astype(BF16)) * SCALE, NEG)
        m = jnp.max(s, axis=-1, keepdims=True)
        e = jnp.where(valid, jnp.exp(s - m), 0.0)
        p = (e / jnp.maximum(jnp.sum(e, axis=-1, keepdims=True), 1e-30)).astype(BF16)
        o_ref[h] = jnp.dot(p, vc_ref[h].astype(BF16), preferred_element_type=F32)
        imp = jnp.sum(jnp.dot(p, cov_ref[...], preferred_element_type=F32), axis=0, keepdims=True)
        imp = jnp.where(future, -1e9, jnp.where(forced, 1e9, imp))
        jj = lax.broadcasted_iota(I32, (nb_pad, nb_pad), 0)
        ss = lax.broadcasted_iota(I32, (nb_pad, nb_pad), 1)
        row = jnp.broadcast_to(imp, (nb_pad, nb_pad))
        col = jnp.sum(jnp.where(jj == ss, row, 0.0), axis=1, keepdims=True)
        beats = jnp.where(col > row, 1.0, jnp.where(col == row, jnp.where(jj < ss, 1.0, 0.0), 0.0))
        rank = jnp.sum(beats, axis=0, keepdims=True)
        want = lax.broadcasted_iota(I32, (SEL_TOP, nb_pad), 0).astype(F32)
        lane = lax.broadcasted_iota(I32, (SEL_TOP, nb_pad), 1).astype(F32)
        idx = jnp.sum(jnp.where(jnp.broadcast_to(rank, (SEL_TOP, nb_pad)) == want, lane, 0.0), axis=1, keepdims=True)
        idx_ref[h] = idx.astype(I32)


def cmp_select_sample(q, kcmp, vcmp, q_pos):
    b = q.shape[0]
    nc = kcmp.shape[2]
    n_blk = q_pos // SEL_BLOCK + 1
    nb_pad = -(-n_blk // LANES) * LANES
    cov = _cover_t(nc, nb_pad, nc - 1)[:, :nc].T
    cov = jnp.asarray(cov, BF16)
    return pl.pallas_call(
        functools.partial(_cmp_select_sample_kernel, q_pos=q_pos, n_blk=n_blk),
        grid=(b,),
        in_specs=[pl.BlockSpec((None, 1, TOK_W), lambda bi: (bi, 0, 0)),
                  pl.BlockSpec((None, NSA_KV_HEADS, nc, HEAD_DIM), lambda bi: (bi, 0, 0, 0)),
                  pl.BlockSpec((None, NSA_KV_HEADS, nc, HEAD_DIM), lambda bi: (bi, 0, 0, 0)),
                  pl.BlockSpec(cov.shape, lambda bi: (0, 0))],
        out_specs=[pl.BlockSpec((None, NSA_KV_HEADS, GROUP_ROWS, HEAD_DIM), lambda bi: (bi, 0, 0, 0)),
                   pl.BlockSpec((None, NSA_KV_HEADS, SEL_TOP, 1), lambda bi: (bi, 0, 0, 0)),
                   pl.BlockSpec((None, NSA_KV_HEADS, GROUP_ROWS, HEAD_DIM), lambda bi: (bi, 0, 0, 0))],
        out_shape=[jax.ShapeDtypeStruct((b, NSA_KV_HEADS, GROUP_ROWS, HEAD_DIM), F32),
                   jax.ShapeDtypeStruct((b, NSA_KV_HEADS, SEL_TOP, 1), I32),
                   jax.ShapeDtypeStruct((b, NSA_KV_HEADS, GROUP_ROWS, HEAD_DIM), F32)],
        compiler_params=_cparams(("parallel",)),
        name="cmp_select_sample",
    )(q, kcmp, vcmp, cov)


def _dot_new_row(q8, k_row):
    a = q8.astype(BF16).astype(F32)
    b = k_row.astype(BF16).astype(F32)
    return jnp.sum(a * b, axis=-1, keepdims=True)


SEL_S_BLOCKS = 4


def _sel_sample_kernel(pt_ref, ix_ref, q_ref, *refs, n_cache_blk):
    k_refs, v_refs = refs[:SEL_S_BLOCKS], refs[SEL_S_BLOCKS:2 * SEL_S_BLOCKS]
    kn_ref, vn_ref, o_ref, m_ref, l_ref, acc_ref = refs[2 * SEL_S_BLOCKS:]
    b, h, r = pl.program_id(0), pl.program_id(1), pl.program_id(2)

    @pl.when(r == 0)
    def _():
        m_ref[...] = jnp.full(m_ref.shape, NEG, F32)
        l_ref[...] = jnp.zeros(l_ref.shape, F32)
        acc_ref[...] = jnp.zeros(acc_ref.shape, F32)

    q8 = q_ref[...]
    n_keys = SEL_S_BLOCKS * SEL_BLOCK
    key_blk = lax.broadcasted_iota(I32, (GROUP_ROWS, n_keys), 1) // SEL_BLOCK
    in_cache = jnp.zeros((GROUP_ROWS, n_keys), I32)
    for j in range(SEL_S_BLOCKS):
        flag = (ix_ref[(b * NSA_KV_HEADS + h) * SEL_TOP + r * SEL_S_BLOCKS + j] < n_cache_blk).astype(I32)
        in_cache = jnp.where(key_blk == j, flag, in_cache)
    valid = (lax.broadcasted_iota(I32, (GROUP_ROWS, n_keys), 0) < NSA_GROUP) & (in_cache > 0)
    k_blk = jnp.concatenate([_pick_head(k_ref, h) for k_ref in k_refs], axis=0).astype(BF16)
    v_blk = jnp.concatenate([_pick_head(v_ref, h) for v_ref in v_refs], axis=0).astype(BF16)
    s = jnp.where(valid, _nt(q8.astype(BF16), k_blk) * SCALE, NEG)
    m_old = m_ref[...]
    m_new = jnp.maximum(m_old, jnp.max(s, axis=-1, keepdims=True))
    e = jnp.where(valid, jnp.exp(s - m_new), 0.0)
    alpha = jnp.exp(m_old - m_new)
    l_ref[...] = alpha * l_ref[...] + jnp.sum(e, axis=-1, keepdims=True)
    acc_ref[...] = alpha * acc_ref[...] + jnp.dot(e.astype(BF16), v_blk, preferred_element_type=F32)
    m_ref[...] = m_new

    @pl.when(r == SEL_TOP // SEL_S_BLOCKS - 1)
    def _():
        s_new = _dot_new_row(q8, kn_ref[...]) * SCALE
        m_old = m_ref[...]
        m_fin = jnp.maximum(m_old, s_new)
        alpha = jnp.exp(m_old - m_fin)
        e_new = jnp.exp(s_new - m_fin)
        l_fin = alpha * l_ref[...] + e_new
        acc = alpha * acc_ref[...] + e_new.astype(BF16).astype(F32) * vn_ref[...].astype(BF16).astype(F32)
        o_ref[...] = acc / jnp.maximum(l_fin, 1e-30)


def sel_attention_sample(q8, idx, cache_k, cache_v, layer, page_table, k_new, v_new):
    b, n_pages = page_table.shape
    per_page = PAGE_SIZE // SEL_BLOCK
    n_cache_blk = n_pages * per_page

    def blk_index(j):
        def index(bi, h, r, pt, ix):
            s = jnp.minimum(ix[(bi * NSA_KV_HEADS + h) * SEL_TOP + r * SEL_S_BLOCKS + j], n_cache_blk - 1)
            return (layer, pt[bi * n_pages + s // per_page], s % per_page, 0, 0)
        return index

    blk_specs = [pl.BlockSpec((None, None, SEL_BLOCK, NSA_KV_HEADS, HEAD_DIM), blk_index(j)) for j in range(SEL_S_BLOCKS)]
    grid_spec = pltpu.PrefetchScalarGridSpec(
        num_scalar_prefetch=2,
        grid=(b, NSA_KV_HEADS, SEL_TOP // SEL_S_BLOCKS),
        in_specs=[pl.BlockSpec((None, None, GROUP_ROWS, HEAD_DIM), lambda bi, h, r, pt, ix: (bi, h, 0, 0))]
                 + blk_specs + blk_specs
                 + [pl.BlockSpec((None, 1, HEAD_DIM), lambda bi, h, r, pt, ix: (bi, 0, h)),
                    pl.BlockSpec((None, 1, HEAD_DIM), lambda bi, h, r, pt, ix: (bi, 0, h))],
        out_specs=pl.BlockSpec((None, None, GROUP_ROWS, HEAD_DIM), lambda bi, h, r, pt, ix: (bi, h, 0, 0)),
        scratch_shapes=[pltpu.VMEM((GROUP_ROWS, 1), F32), pltpu.VMEM((GROUP_ROWS, 1), F32),
                        pltpu.VMEM((GROUP_ROWS, HEAD_DIM), F32)],
    )
    return pl.pallas_call(
        functools.partial(_sel_sample_kernel, n_cache_blk=n_cache_blk),
        grid_spec=grid_spec,
        out_shape=jax.ShapeDtypeStruct((b, NSA_KV_HEADS, GROUP_ROWS, HEAD_DIM), F32),
        compiler_params=_cparams(("arbitrary", "arbitrary", "arbitrary")),
        name="sel_attention_sample",
    )(page_table.reshape(-1), idx.reshape(-1), q8, *([cache_k] * SEL_S_BLOCKS), *([cache_v] * SEL_S_BLOCKS), k_new, v_new)


def _win_sample_kernel(q_ref, k_ref, v_ref, kn_ref, vn_ref, oc_ref, os_ref, gate_ref, o_ref):
    h = pl.program_id(1)
    wb = k_ref.shape[0]
    k_win = _pick_head(k_ref, h).astype(BF16)
    v_win = _pick_head(v_ref, h).astype(BF16)
    q8 = q_ref[...]
    col = lax.broadcasted_iota(I32, (GROUP_ROWS, wb), 1)
    valid = col >= 1
    s = jnp.where(valid, _nt(q8.astype(BF16), k_win) * SCALE, NEG)
    s_new = _dot_new_row(q8, kn_ref[...]) * SCALE
    m = jnp.maximum(jnp.max(s, axis=-1, keepdims=True), s_new)
    e = jnp.where(valid, jnp.exp(s - m), 0.0)
    e_new = jnp.exp(s_new - m)
    den = jnp.maximum(jnp.sum(e, axis=-1, keepdims=True) + e_new, 1e-30)
    p = (e / den).astype(BF16)
    p_new = (e_new / den).astype(BF16).astype(F32)
    o_win = jnp.dot(p, v_win, preferred_element_type=F32) + p_new * vn_ref[...].astype(BF16).astype(F32)
    gate = jnp.broadcast_to(gate_ref[...], (GROUP_ROWS, LANES))
    lane = lax.broadcasted_iota(I32, (GROUP_ROWS, LANES), 1)
    head = h * NSA_GROUP + lax.broadcasted_iota(I32, (GROUP_ROWS, LANES), 0)
    g = [jnp.sum(jnp.where(lane == head * 3 + j, gate, 0.0), axis=-1, keepdims=True) for j in range(3)]
    o_ref[...] = g[0] * oc_ref[...] + g[1] * os_ref[...] + g[2] * o_win


def win_combine_sample(q8, win_k, win_v, layer, k_new, v_new, o_cmp, o_sel, gate):
    b = q8.shape[0]
    wb = win_k.shape[2]
    assert wb == WINDOW
    gspec = pl.BlockSpec((None, None, GROUP_ROWS, HEAD_DIM), lambda bi, h: (bi, h, 0, 0))
    wspec = pl.BlockSpec((None, None, wb, NSA_KV_HEADS, HEAD_DIM), lambda bi, h: (layer, bi, 0, 0, 0))
    nspec = pl.BlockSpec((None, 1, HEAD_DIM), lambda bi, h: (bi, 0, h))
    return pl.pallas_call(
        _win_sample_kernel,
        grid=(b, NSA_KV_HEADS),
        in_specs=[gspec, wspec, wspec, nspec, nspec, gspec, gspec,
                  pl.BlockSpec((None, 1, LANES), lambda bi, h: (bi, 0, 0))],
        out_specs=gspec,
        out_shape=jax.ShapeDtypeStruct((b, NSA_KV_HEADS, GROUP_ROWS, HEAD_DIM), F32),
        compiler_params=_cparams(("parallel", "arbitrary")),
        name="win_combine_sample",
    )(q8, win_k, win_v, k_new, v_new, o_cmp, o_sel, gate)


def nsa_sample_step(prep, layer, caches, win_k, win_v, page_table, pe, w1, w2):
    q, kc, vc, ks, vs, kw, vw, gate = prep
    b = q.shape[0]
    past_len = page_table.shape[1] * PAGE_SIZE
    cache_ck, cache_cv, cache_sk, cache_sv = caches
    kcmp = compress_paged(cache_ck, layer, page_table, pe[0], w1[0], w2[0])
    vcmp = compress_paged(cache_cv, layer, page_table, pe[1], w1[1], w2[1])
    o_cmp, idx, q8 = cmp_select_sample(q, kcmp, vcmp, past_len)
    o_sel = sel_attention_sample(q8, idx, cache_sk, cache_sv, layer, page_table, ks, vs)
    o = win_combine_sample(q8, win_k, win_v, layer, kw, vw, o_cmp, o_sel, gate)
    return o[:, :, :NSA_GROUP].reshape(b, 1, TOK_W)


def _reorder_nsa_weight(w):
    n_gl = 3 * NSA_HEADS
    parts = [w[:, :NSA_MQ0], w[:, NSA_MQ0 + n_gl:NSA_MQ0 + n_gl + MEM_W], w[:, NSA_MQ0:NSA_MQ0 + n_gl]]
    wr = jnp.concatenate(parts, axis=1)
    return jnp.pad(wr, ((0, 0), (0, NSA_ZW - wr.shape[1]))).astype(BF16)


def kernel(x_prompt, x_sample, state_pool, cache_cmp_k, cache_cmp_v, cache_sel_k, cache_sel_v, state_win_k, state_win_v, cache_mem_k, cache_mem_v, page_table, mem_prompt, norm_mix_g, norm_ffn_g, norm_mem_g, w_mem_kv, mem_q_norm_g, mem_k_norm_g, w_in_pool, w_pool_grp, pool_scale, w_out_pool, w_in_nsa, b_gate, nsa_q_norm_g, nsa_k_norm_g, cmp_pe, cmp_w1, cmp_w2, w_out_nsa, w_router, b_router, w_gate, w_up, w_down):
    bp, t_p, d = x_prompt.shape
    bs, t_s, _ = x_sample.shape
    assert t_s == 1
    n_p, n_s = bp * t_p, bs * t_s
    m_len = mem_prompt.shape[1]
    past_len = page_table.shape[1] * PAGE_SIZE
    pos_p = jnp.arange(t_p, dtype=I32)
    pos_s = past_len + jnp.arange(t_s, dtype=I32)
    xp = x_prompt.reshape(n_p, d)
    xs = x_sample.reshape(n_s, d)
    mem_flat = mem_prompt.reshape(-1, d)
    pool_p, pool_s, rows_p, rows_s, win_p, win_s, mem_k_p, mem_v_p = [], [], [], [], [], [], [], []
    for i in range(DEPTH):
        li = i // 2
        kv = proj(mem_flat, w_mem_kv[i].astype(BF16), gain=norm_mem_g[i])
        mk = head_norm(kv, 0, MEM_W, mem_k_norm_g[i]).reshape(bp, m_len, MEM_W)
        mv = kv[:, MEM_W:].reshape(bp, m_len, MEM_W)
        mem_k_p.append(mk.reshape(bp, m_len, MEM_HEADS, HEAD_DIM))
        mem_v_p.append(mv.reshape(bp, m_len, MEM_HEADS, HEAD_DIM))
        mk_s = cache_mem_k[i].reshape(bs, m_len, MEM_W)
        mv_s = cache_mem_v[i].reshape(bs, m_len, MEM_W)
        if i % 2 == 0:
            w_in = w_in_pool[li].astype(BF16)
            zp = proj(xp, w_in, gain=norm_mix_g[i]).reshape(bp, t_p, -1)
            zs = proj(xs, w_in, gain=norm_mix_g[i]).reshape(bs, t_s, -1)
            op = pool_mix(zp, None, w_pool_grp[li], pool_scale[li], 0)
            zs16 = jnp.pad(zs, ((0, 0), (0, POOL_HALO - t_s), (0, 0)))
            halo = jnp.pad(state_pool[li], ((0, 0), (1, 0), (0, 0)))
            os_ = pool_mix(zs16, halo, w_pool_grp[li], pool_scale[li], past_len)[:, :t_s]
            pool_p.append(zp[:, t_p - POOL_STATE:, :TOK_W])
            pool_s.append(jnp.concatenate([state_pool[li], zs[..., :TOK_W]], axis=1)[:, -POOL_STATE:])
            mq0 = TOK_W
            w_out = w_out_pool[li]
        else:
            w_in = _reorder_nsa_weight(w_in_nsa[li])
            zp = proj(xp, w_in, gain=norm_mix_g[i]).reshape(bp, t_p, -1)
            zs = proj(xs, w_in, gain=norm_mix_g[i]).reshape(bs, t_s, -1)
            q, kc, vc, ks, vs, kw, vw, gate = nsa_prep(zp, pos_p, nsa_q_norm_g[li], nsa_k_norm_g[li], b_gate[li])
            kcmp = compress_rows(kc, cmp_pe[li, 0], cmp_w1[li, 0], cmp_w2[li, 0])
            vcmp = compress_rows(vc, cmp_pe[li, 1], cmp_w1[li, 1], cmp_w2[li, 1])
            o_cmp, sel_t = cmp_select_prompt(q, kcmp, vcmp)
            o_sel = sel_attention_prompt_t(q, ks, vs, sel_t)
            op = win_combine_prompt(q, kw, vw, o_cmp, o_sel, gate)
            kvr = lambda a: a.reshape(a.shape[0], a.shape[1], NSA_KV_HEADS, HEAD_DIM)
            rows_p.append(tuple(kvr(a) for a in (kc, vc, ks, vs)))
            wb = min(WINDOW, t_p)
            win_p.append((kvr(kw[:, t_p - wb:]), kvr(vw[:, t_p - wb:])))
            prep_s = nsa_prep(zs, pos_s, nsa_q_norm_g[li], nsa_k_norm_g[li], b_gate[li])
            os_ = nsa_sample_step(prep_s, li, (cache_cmp_k, cache_cmp_v, cache_sel_k, cache_sel_v),
                                  state_win_k, state_win_v, page_table, cmp_pe[li], cmp_w1[li], cmp_w2[li])
            rows_s.append(tuple(kvr(a) for a in prep_s[1:5]))
            win_s.append(tuple(jnp.concatenate([st[li][:, t_s:], kvr(new)], axis=1)
                               for st, new in ((state_win_k, prep_s[5]), (state_win_v, prep_s[6]))))
            mq0 = NSA_MQ0
            w_out = w_out_nsa[li]
        ap = mem_attention(zp, mq0, mk, mv, mem_q_norm_g[i])
        as_ = mem_attention(zs, mq0, mk_s, mv_s, mem_q_norm_g[i])
        w_out = w_out.astype(BF16)
        xp = proj([op.reshape(n_p, TOK_W), ap.reshape(n_p, MEM_W)], w_out, residual=xp)
        xs = proj([os_.reshape(n_s, TOK_W), as_.reshape(n_s, MEM_W)], w_out, residual=xs)
        xn, e_p, g_p = router(xp, norm_ffn_g[i], w_router, b_router, xn_rows=n_p + LANES)
        xs_pad = jnp.pad(xs, ((0, LANES - n_s), (0, 0)))
        xn, e_s, g_s = router(xs_pad, norm_ffn_g[i], w_router, b_router, xn_rows=n_p + LANES, into=xn, row0=n_p)
        xp, xs = moe_ffn2([xp, xs],
                          xn,
                          jnp.concatenate([e_p, e_s[:, :n_s]], axis=1),
                          [g_p, g_s[:, :n_s]],
                          w_gate, w_up, w_down, i)
    stk = lambda lst, j: jnp.stack([r[j] for r in lst])
    return (xp.reshape(bp, t_p, d), xs.reshape(bs, t_s, d),
            jnp.stack(pool_p), jnp.stack(pool_s),
            stk(rows_p, 0), stk(rows_p, 1), stk(rows_p, 2), stk(rows_p, 3),
            stk(rows_s, 0), stk(rows_s, 1), stk(rows_s, 2), stk(rows_s, 3),
            stk(win_p, 0), stk(win_p, 1), stk(win_s, 0), stk(win_s, 1),
            jnp.stack(mem_k_p), jnp.stack(mem_v_p))
```

```python
import functools

import jax
import jax.numpy as jnp
import numpy as np
from jax import lax
from jax.experimental import pallas as pl
from jax.experimental.pallas import tpu as pltpu

D_MODEL = 2048
DEPTH = 4
PAGE_SIZE = 128
HEAD_DIM = 128
ROPE_THETA = 500000.0
ROPE_DIM = HEAD_DIM // 4
MEM_HEADS = 4
MEM_W = MEM_HEADS * HEAD_DIM
TOK_W = D_MODEL - MEM_W
POOL_WINDOWS = (2, 4, 8, 16)
POOL_GROUP = TOK_W // len(POOL_WINDOWS)
POOL_STATE = max(POOL_WINDOWS) - 1
POOL_HALO = POOL_STATE + 1
NSA_HEADS = TOK_W // HEAD_DIM
NSA_KV_HEADS = 2
NSA_GROUP = NSA_HEADS // NSA_KV_HEADS
KV_W = NSA_KV_HEADS * HEAD_DIM
GROUP_W = NSA_GROUP * HEAD_DIM
CMP_BLOCK = 32
CMP_STRIDE = 16
SEL_BLOCK = 64
SEL_TOP = 16
WINDOW = 512
NSA_TOK_IN = NSA_HEADS * HEAD_DIM + 6 * KV_W + 3 * NSA_HEADS
N_EXPERTS = 16
N_GROUPS = 4
EXP_PER_GROUP = N_EXPERTS // N_GROUPS
TOP_K = 2
D_FF = 1024
EPS = 1e-6
NEG = -1e30
SCALE = HEAD_DIM ** -0.5
LANES = 128

V7X_VMEM_BYTES = 64 * 1024 * 1024
VMEM_LIMIT = V7X_VMEM_BYTES * 3 // 4

BF16 = jnp.bfloat16
F32 = jnp.float32
I32 = jnp.int32

NSA_Q0 = 0
NSA_KV0 = NSA_HEADS * HEAD_DIM
NSA_MQ0 = NSA_KV0 + 6 * KV_W
NSA_GL0 = NSA_MQ0 + MEM_W
NSA_ZW = 3840


def _cparams(sem):
    return pltpu.CompilerParams(dimension_semantics=sem, vmem_limit_bytes=VMEM_LIMIT)


def _nt(a, b):
    return lax.dot_general(a, b, (((1,), (1,)), ((), ())), preferred_element_type=F32)


def _tn(a, b):
    return lax.dot_general(a, b, (((0,), (0,)), ((), ())), preferred_element_type=F32)


def _rms(x, g):
    return x * lax.rsqrt(jnp.mean(x * x, axis=-1, keepdims=True) + EPS) * g


def _pick_tile(n, cap, unit):
    if n <= cap:
        return n
    best = None
    for t in range(unit, cap + 1, unit):
        if n % t == 0:
            best = t
    assert best is not None, (n, cap, unit)
    return best


def _proj_kernel(*refs, n_x, norm, residual):
    x_refs, refs = refs[:n_x], refs[n_x:]
    if norm:
        g_ref, refs = refs[0], refs[1:]
    w_ref, refs = refs[0], refs[1:]
    if residual:
        r_ref, refs = refs[0], refs[1:]
    o_ref, xn_ref = refs

    @pl.when(pl.program_id(1) == 0)
    def _():
        off = 0
        for x_ref in x_refs:
            x = x_ref[...].astype(F32)
            if norm:
                x = _rms(x, g_ref[...])
            xn_ref[:, off:off + x.shape[1]] = x.astype(BF16)
            off += x.shape[1]

    y = jnp.dot(xn_ref[...], w_ref[...], preferred_element_type=F32)
    if residual:
        y = y + r_ref[...]
    o_ref[...] = y


PROJ_RESIDENT_N = 3840
PROJ_RESIDENT_ELEMS = 1 << 20


def proj(xs, w_bf16, gain=None, residual=None):
    if not isinstance(xs, (list, tuple)):
        xs = [xs]
    assert gain is None or len(xs) == 1
    m = xs[0].shape[0]
    k = sum(x.shape[1] for x in xs)
    n = w_bf16.shape[1]
    if n <= PROJ_RESIDENT_N:
        tm, tn = _pick_tile(m, PROJ_RESIDENT_ELEMS // n // 8 * 8, 8), n
    else:
        tm, tn = _pick_tile(m, 1024, 8), _pick_tile(n, 1024, LANES)
    in_specs = [pl.BlockSpec((tm, x.shape[1]), lambda i, j: (i, 0)) for x in xs]
    args = list(xs)
    if gain is not None:
        in_specs.append(pl.BlockSpec((1, k), lambda i, j: (0, 0)))
        args.append(gain.reshape(1, k).astype(F32))
    in_specs.append(pl.BlockSpec((k, tn), lambda i, j: (0, j)))
    args.append(w_bf16)
    if residual is not None:
        in_specs.append(pl.BlockSpec((tm, tn), lambda i, j: (i, j)))
        args.append(residual)
    return pl.pallas_call(
        functools.partial(_proj_kernel, n_x=len(xs), norm=gain is not None, residual=residual is not None),
        grid=(m // tm, n // tn),
        in_specs=in_specs,
        out_specs=pl.BlockSpec((tm, tn), lambda i, j: (i, j)),
        out_shape=jax.ShapeDtypeStruct((m, n), F32),
        scratch_shapes=[pltpu.VMEM((tm, k), BF16)],
        compiler_params=_cparams(("parallel", "arbitrary")),
        name="proj",
    )(*args)


def _head_norm_kernel(x_ref, g_ref, o_ref):
    x = x_ref[...]
    g = g_ref[...]
    n_heads = x.shape[1] // HEAD_DIM
    o_ref[...] = jnp.concatenate(
        [_rms(x[:, h * HEAD_DIM:(h + 1) * HEAD_DIM], g) for h in range(n_heads)], axis=1)


def head_norm(x, col0, width, gain):
    m = x.shape[0]
    assert col0 % width == 0
    return pl.pallas_call(
        _head_norm_kernel,
        grid=(1,),
        in_specs=[pl.BlockSpec((m, width), lambda i: (0, col0 // width)),
                  pl.BlockSpec((1, HEAD_DIM), lambda i: (0, 0))],
        out_specs=pl.BlockSpec((m, width), lambda i: (0, 0)),
        out_shape=jax.ShapeDtypeStruct((m, width), F32),
        compiler_params=_cparams(("arbitrary",)),
        name="head_norm",
    )(x, gain.reshape(1, HEAD_DIM))


def _mem_attn_kernel(q_ref, k_ref, v_ref, g_ref, o_ref):
    q = q_ref[...]
    rows = q.shape[0]
    if rows < 8:
        q = jnp.broadcast_to(q[0:1], (8, q.shape[1]))
    g = g_ref[...]
    outs = []
    for h in range(MEM_HEADS):
        sl = slice(h * HEAD_DIM, (h + 1) * HEAD_DIM)
        qh = _rms(q[:, sl], g).astype(BF16)
        s = _nt(qh, k_ref[:, sl].astype(BF16)) * SCALE
        m = jnp.max(s, axis=-1, keepdims=True)
        e = jnp.exp(s - m)
        p = e / jnp.sum(e, axis=-1, keepdims=True)
        outs.append(jnp.dot(p.astype(BF16), v_ref[:, sl].astype(BF16), preferred_element_type=F32))
    o = jnp.concatenate(outs, axis=1)
    o_ref[...] = o[:rows]


def mem_attention(z, col0, k, v, gain):
    b, t, _ = z.shape
    m = k.shape[1]
    tq = _pick_tile(t, 512, 8)
    assert col0 % MEM_W == 0
    return pl.pallas_call(
        _mem_attn_kernel,
        grid=(b, t // tq),
        in_specs=[pl.BlockSpec((None, tq, MEM_W), lambda bi, i: (bi, i, col0 // MEM_W)),
                  pl.BlockSpec((None, m, MEM_W), lambda bi, i: (bi, 0, 0)),
                  pl.BlockSpec((None, m, MEM_W), lambda bi, i: (bi, 0, 0)),
                  pl.BlockSpec((1, HEAD_DIM), lambda bi, i: (0, 0))],
        out_specs=pl.BlockSpec((None, tq, MEM_W), lambda bi, i: (bi, i, 0)),
        out_shape=jax.ShapeDtypeStruct((b, t, MEM_W), F32),
        compiler_params=_cparams(("parallel", "arbitrary")),
        name="mem_attention",
    )(z, k, v, gain.reshape(1, HEAD_DIM))


def _pool_kernel(u_ref, halo_ref, w_ref, sc_ref, o_ref, *, pos0, zero_first_halo):
    tq = u_ref.shape[0]
    qi = pl.program_id(1)
    u = u_ref[...]
    halo = halo_ref[...]
    if zero_first_halo:
        halo = jnp.where(qi > 0, halo, 0.0)
    pos = pos0 + qi * tq + lax.broadcasted_iota(I32, (tq, 1), 0)
    outs = []
    for g, w in enumerate(POOL_WINDOWS):
        cs = slice(g * POOL_GROUP, (g + 1) * POOL_GROUP)
        ug = u[:, cs]
        acc = jnp.concatenate([halo[:, cs], ug], axis=0)
        span = 1
        while span < w:
            acc = acc[span:] + acc[:-span]
            span *= 2
        ssum = acc[POOL_HALO - (w - 1):POOL_HALO - (w - 1) + tq]
        cnt = jnp.minimum(w, pos + 1).astype(F32)
        d = (ssum / cnt - ug).astype(BF16)
        y = jnp.dot(d, w_ref[g].astype(BF16), preferred_element_type=F32)
        outs.append(y)
    o_ref[...] = jnp.concatenate(outs, axis=1) * sc_ref[...]


def pool_mix(z, halo, w_grp, scale, pos0):
    b, t, _ = z.shape
    tq = _pick_tile(t, 256, 16)
    if halo is None:
        halo_arr = z
        halo_spec = pl.BlockSpec((None, POOL_HALO, TOK_W),
                                 lambda bi, i: (bi, jnp.maximum(i * (tq // POOL_HALO) - 1, 0), 0))
    else:
        assert t == tq
        halo_arr = halo
        halo_spec = pl.BlockSpec((None, POOL_HALO, TOK_W), lambda bi, i: (bi, 0, 0))
    return pl.pallas_call(
        functools.partial(_pool_kernel, pos0=pos0, zero_first_halo=halo is None),
        grid=(b, t // tq),
        in_specs=[pl.BlockSpec((None, tq, TOK_W), lambda bi, i: (bi, i, 0)),
                  halo_spec,
                  pl.BlockSpec(w_grp.shape, lambda bi, i: (0, 0, 0)),
                  pl.BlockSpec((1, TOK_W), lambda bi, i: (0, 0))],
        out_specs=pl.BlockSpec((None, tq, TOK_W), lambda bi, i: (bi, i, 0)),
        out_shape=jax.ShapeDtypeStruct((b, t, TOK_W), F32),
        compiler_params=_cparams(("parallel", "arbitrary")),
        name="pool_mix",
    )(z, halo_arr, w_grp, scale.reshape(1, TOK_W))


def _rope_tables(pos):
    half = ROPE_DIM // 2
    inv = 1.0 / (ROPE_THETA ** (jnp.arange(half, dtype=F32) * 2.0 / ROPE_DIM))
    ang = pos.astype(F32)[:, None] * inv[None, :]
    cos, sin = jnp.cos(ang), jnp.sin(ang)
    t = pos.shape[0]
    rest = HEAD_DIM - ROPE_DIM
    c = jnp.concatenate([cos, cos, jnp.ones((t, rest), F32)], axis=1)
    s_lo = jnp.concatenate([-sin, jnp.zeros((t, HEAD_DIM - half), F32)], axis=1)
    s_hi = jnp.concatenate([jnp.zeros((t, half), F32), sin, jnp.zeros((t, rest), F32)], axis=1)
    return c, s_lo, s_hi


def _nsa_prep_kernel(z_ref, c_ref, slo_ref, shi_ref, qg_ref, kg_ref, bg_ref,
                     q_ref, kc_ref, vc_ref, ks_ref, vs_ref, kw_ref, vw_ref, gate_ref):
    rows = z_ref.shape[0]
    c, slo, shi = c_ref[...], slo_ref[...], shi_ref[...]
    half = ROPE_DIM // 2

    def rope_norm(x, g):
        if rows < 8:
            x = jnp.broadcast_to(x[0:1], (8, HEAD_DIM))
        x = _rms(x, g)
        y = x * c + pltpu.roll(x, HEAD_DIM - half, 1) * slo + pltpu.roll(x, half, 1) * shi
        return y[:rows]

    qg = qg_ref[...]
    q_ref[...] = jnp.concatenate(
        [rope_norm(z_ref[:, NSA_Q0 + h * HEAD_DIM:NSA_Q0 + (h + 1) * HEAD_DIM], qg) for h in range(NSA_HEADS)], axis=1)
    for j, (o_ref, which) in enumerate(((kc_ref, 0), (vc_ref, None), (ks_ref, 1), (vs_ref, None), (kw_ref, 2), (vw_ref, None))):
        c0 = NSA_KV0 + j * KV_W
        if which is None:
            o_ref[...] = z_ref[:, c0:c0 + KV_W]
        else:
            g = kg_ref[which:which + 1, :]
            o_ref[...] = jnp.concatenate(
                [rope_norm(z_ref[:, c0 + h * HEAD_DIM:c0 + (h + 1) * HEAD_DIM], g) for h in range(NSA_KV_HEADS)], axis=1)
    gate_ref[...] = jax.nn.sigmoid(z_ref[:, NSA_GL0:NSA_GL0 + LANES] + bg_ref[...])


def nsa_prep(z, pos, q_g, k_g, b_gate):
    b, t, _ = z.shape
    tq = _pick_tile(t, 256, 8)
    c, slo, shi = _rope_tables(pos)
    if t < 8:
        c, slo, shi = (jnp.broadcast_to(a, (8, HEAD_DIM)) for a in (c, slo, shi))
    tt = max(tq, 8)
    bg = jnp.pad(b_gate.reshape(1, -1), ((0, 0), (0, LANES - b_gate.shape[-1])))
    tab_spec = pl.BlockSpec((tt, HEAD_DIM), lambda bi, i: (i, 0))
    kv_spec = pl.BlockSpec((None, tq, KV_W), lambda bi, i: (bi, i, 0))
    kv_shape = jax.ShapeDtypeStruct((b, t, KV_W), F32)
    return pl.pallas_call(
        _nsa_prep_kernel,
        grid=(b, t // tq),
        in_specs=[pl.BlockSpec((None, tq, NSA_ZW), lambda bi, i: (bi, i, 0)),
                  tab_spec, tab_spec, tab_spec,
                  pl.BlockSpec((1, HEAD_DIM), lambda bi, i: (0, 0)),
                  pl.BlockSpec((3, HEAD_DIM), lambda bi, i: (0, 0)),
                  pl.BlockSpec((1, LANES), lambda bi, i: (0, 0))],
        out_specs=[pl.BlockSpec((None, tq, TOK_W), lambda bi, i: (bi, i, 0))] + [kv_spec] * 6
                  + [pl.BlockSpec((None, tq, LANES), lambda bi, i: (bi, i, 0))],
        out_shape=[jax.ShapeDtypeStruct((b, t, TOK_W), F32)] + [kv_shape] * 6
                  + [jax.ShapeDtypeStruct((b, t, LANES), F32)],
        compiler_params=_cparams(("parallel", "arbitrary")),
        name="nsa_prep",
    )(z, c, slo, shi, q_g.reshape(1, HEAD_DIM), k_g, bg)


def _compress_body(slab, n_chunks, pe_ref, w1_ref, w2_ref, o_ref):
    pe = pe_ref[...]
    pe_lo = jnp.concatenate([pe[r:r + 1] for r in range(CMP_STRIDE)], axis=1)
    pe_hi = jnp.concatenate([pe[CMP_STRIDE + r:CMP_STRIDE + r + 1] for r in range(CMP_STRIDE)], axis=1)
    half_k = CMP_STRIDE * HEAD_DIM
    w_lo = w1_ref[0:half_k, :].astype(BF16)
    w_hi = w1_ref[half_k:2 * half_k, :].astype(BF16)
    w2 = w2_ref[...].astype(BF16)
    for h in range(NSA_KV_HEADS):
        xh = jnp.concatenate([slab(r, h) for r in range(CMP_STRIDE)], axis=1)
        a = jnp.dot((xh + pe_lo).astype(BF16), w_lo, preferred_element_type=F32)
        bb = jnp.dot((xh + pe_hi).astype(BF16), w_hi, preferred_element_type=F32)
        hid = jax.nn.gelu(a + pltpu.roll(bb, n_chunks - 1, 0))
        o_ref[h] = jnp.dot(hid.astype(BF16), w2, preferred_element_type=F32)


def _compress_kernel(x_ref, pe_ref, w1_ref, w2_ref, o_ref):
    def slab(r, h):
        c0 = (r * NSA_KV_HEADS + h) * HEAD_DIM
        return x_ref[:, c0:c0 + HEAD_DIM]
    _compress_body(slab, x_ref.shape[0], pe_ref, w1_ref, w2_ref, o_ref)


def compress_rows(x, pe, w1, w2):
    b, t, _ = x.shape
    n_chunks = t // CMP_STRIDE
    cw = CMP_STRIDE * KV_W
    xc = x.reshape(b, n_chunks, cw)
    return pl.pallas_call(
        _compress_kernel,
        grid=(b,),
        in_specs=[pl.BlockSpec((None, n_chunks, cw), lambda bi: (bi, 0, 0)),
                  pl.BlockSpec(pe.shape, lambda bi: (0, 0)),
                  pl.BlockSpec(w1.shape, lambda bi: (0, 0)),
                  pl.BlockSpec(w2.shape, lambda bi: (0, 0))],
        out_specs=pl.BlockSpec((None, NSA_KV_HEADS, n_chunks, HEAD_DIM), lambda bi: (bi, 0, 0, 0)),
        out_shape=jax.ShapeDtypeStruct((b, NSA_KV_HEADS, n_chunks, HEAD_DIM), F32),
        compiler_params=_cparams(("parallel",)),
        name="compress_rows",
    )(xc, pe, w1, w2)


def _stack_heads(q):
    return jnp.concatenate([q[:, g * HEAD_DIM:(g + 1) * HEAD_DIM] for g in range(NSA_GROUP)], axis=0)


def _unstack_heads(o, tq):
    return jnp.concatenate([o[g * tq:(g + 1) * tq] for g in range(NSA_GROUP)], axis=1)


def _rank_select(imp_t, n_real):
    n_blk = imp_t.shape[0]
    blk = lax.broadcasted_iota(I32, imp_t.shape, 0)
    cnt = jnp.zeros(imp_t.shape, F32)
    for j in range(n_real):
        row = imp_t[j:j + 1, :]
        beats = jnp.where(row > imp_t, 1.0, jnp.where(row == imp_t, jnp.where(blk > j, 1.0, 0.0), 0.0))
        cnt = cnt + beats
    return jnp.where(cnt < SEL_TOP, 1.0, 0.0)


def _cmp_select_kernel(q_ref, kc_ref, vc_ref, cov_ref, o_ref, sel_ref):
    tq = q_ref.shape[0]
    nc = kc_ref.shape[0]
    n_blk = sel_ref.shape[0]
    qi = pl.program_id(2)
    q6 = _stack_heads(q_ref[...]).astype(BF16)
    s = _nt(q6, kc_ref[...].astype(BF16)) * SCALE
    s = s.reshape(NSA_GROUP, tq, nc)
    q_pos = qi * tq + lax.broadcasted_iota(I32, (tq, nc), 0)
    c_last = lax.broadcasted_iota(I32, (tq, nc), 1) * CMP_STRIDE + (CMP_BLOCK - 1)
    valid = (c_last <= q_pos)[None]
    s = jnp.where(valid, s, NEG)
    m = jnp.max(s, axis=-1, keepdims=True)
    e = jnp.where(valid, jnp.exp(s - m), 0.0)
    p = (e / jnp.maximum(jnp.sum(e, axis=-1, keepdims=True), 1e-30)).astype(BF16)
    o = jnp.dot(p.reshape(NSA_GROUP * tq, nc), vc_ref[...].astype(BF16), preferred_element_type=F32)
    o_ref[...] = _unstack_heads(o, tq)
    p_cat = jnp.concatenate([p[g] for g in range(NSA_GROUP)], axis=1)
    imp_t = _nt(cov_ref[...], p_cat)
    blk = lax.broadcasted_iota(I32, (n_blk, tq), 0)
    pos_t = qi * tq + lax.broadcasted_iota(I32, (n_blk, tq), 1)
    cur = pos_t // SEL_BLOCK
    forced = (blk == 0) | (blk == cur) | (blk == cur - 1)
    future = blk * SEL_BLOCK > pos_t
    imp_t = jnp.where(future, -1e9, jnp.where(forced, 1e9, imp_t))
    sel_ref[...] = _rank_select(imp_t, n_blk).astype(BF16)


def _cover_t(nc, n_blk, nc_valid):
    c_start = np.arange(nc) * CMP_STRIDE
    c_last = c_start + CMP_BLOCK - 1
    s_start = np.arange(n_blk) * SEL_BLOCK
    cov = (c_start[None, :] < s_start[:, None] + SEL_BLOCK) & (c_last[None, :] >= s_start[:, None])
    cov = cov & (np.arange(nc)[None, :] < nc_valid)
    return np.tile(cov.astype(np.float32), (1, NSA_GROUP))


def cmp_select_prompt(q, kcmp, vcmp):
    b, t, _ = q.shape
    nc = kcmp.shape[2]
    n_blk = -(-t // SEL_BLOCK)
    tq = _pick_tile(t, 256, LANES)
    cov = jnp.asarray(_cover_t(nc, n_blk, nc - 1), BF16)
    return pl.pallas_call(
        _cmp_select_kernel,
        grid=(b, NSA_KV_HEADS, t // tq),
        in_specs=[pl.BlockSpec((None, tq, GROUP_W), lambda bi, h, i: (bi, i, h)),
                  pl.BlockSpec((None, None, nc, HEAD_DIM), lambda bi, h, i: (bi, h, 0, 0)),
                  pl.BlockSpec((None, None, nc, HEAD_DIM), lambda bi, h, i: (bi, h, 0, 0)),
                  pl.BlockSpec(cov.shape, lambda bi, h, i: (0, 0))],
        out_specs=[pl.BlockSpec((None, tq, GROUP_W), lambda bi, h, i: (bi, i, h)),
                   pl.BlockSpec((None, None, n_blk, tq), lambda bi, h, i: (bi, h, 0, i))],
        out_shape=[jax.ShapeDtypeStruct((b, t, TOK_W), F32),
                   jax.ShapeDtypeStruct((b, NSA_KV_HEADS, n_blk, t), BF16)],
        compiler_params=_cparams(("parallel", "parallel", "arbitrary")),
        name="cmp_select_prompt",
    )(q, kcmp, vcmp, cov)


SEL_TQ = 128
SEL_TK = 512


def _heads_to_lanes_t(q):
    tq = q.shape[0]
    return jnp.concatenate([q[:, g * HEAD_DIM:(g + 1) * HEAD_DIM].T for g in range(NSA_GROUP)], axis=1)


def _lanes_to_heads_t(o_t, tq):
    return jnp.concatenate([o_t[:, g * tq:(g + 1) * tq].T for g in range(NSA_GROUP)], axis=1)


def _selt_kernel(q_ref, k_ref, v_ref, sel_ref, exp_ref, o_ref, mask_ref, m_ref, l_ref, acc_ref):
    t = k_ref.shape[0]
    qi = pl.program_id(2)
    q_t = _heads_to_lanes_t(q_ref[...]).astype(BF16)
    mask_ref[...] = (1.0 - jnp.dot(exp_ref[...], sel_ref[...], preferred_element_type=F32)) * NEG
    m_ref[...] = jnp.full(m_ref.shape, NEG, F32)
    l_ref[...] = jnp.zeros(l_ref.shape, F32)
    acc_ref[...] = jnp.zeros(acc_ref.shape, F32)
    q_lo = qi * SEL_TQ

    def key_tile(j, causal):
        lo, hi = j * SEL_TK, (j + 1) * SEL_TK
        k = k_ref[lo:hi, :].astype(BF16)
        v = v_ref[lo:hi, :].astype(BF16)
        bias = mask_ref[lo:hi, :]
        if causal:
            k_pos = lo + lax.broadcasted_iota(I32, (SEL_TK, SEL_TQ), 0)
            q_pos = q_lo + lax.broadcasted_iota(I32, (SEL_TK, SEL_TQ), 1)
            bias = jnp.where(k_pos <= q_pos, bias, NEG)
        s = jnp.dot(k, q_t, preferred_element_type=F32) * SCALE + jnp.concatenate([bias] * NSA_GROUP, axis=1)
        m_old = m_ref[...]
        m_new = jnp.maximum(m_old, jnp.max(s, axis=0, keepdims=True))
        e = jnp.exp(s - m_new)
        alpha = jnp.exp(m_old - m_new)
        l_ref[...] = alpha * l_ref[...] + jnp.sum(e, axis=0, keepdims=True)
        acc_ref[...] = alpha * acc_ref[...] + _tn(v, e.astype(BF16))
        m_ref[...] = m_new

    for j in range(t // SEL_TK):
        lo, hi = j * SEL_TK, (j + 1) * SEL_TK

        @pl.when(hi - 1 <= q_lo)
        def _():
            key_tile(j, False)

        @pl.when((lo <= q_lo + SEL_TQ - 1) & (hi - 1 > q_lo))
        def _():
            key_tile(j, True)

    o_t = acc_ref[...] / jnp.maximum(l_ref[...], 1e-30)
    o_ref[...] = _lanes_to_heads_t(o_t, SEL_TQ)


def sel_attention_prompt_t(q, ks, vs, sel_t):
    b, t, _ = q.shape
    n_blk = sel_t.shape[2]
    expand = (np.arange(t)[:, None] // SEL_BLOCK == np.arange(n_blk)[None, :]).astype(np.float32)
    expand = jnp.asarray(expand, BF16)
    return pl.pallas_call(
        _selt_kernel,
        grid=(b, NSA_KV_HEADS, t // SEL_TQ),
        in_specs=[pl.BlockSpec((None, SEL_TQ, GROUP_W), lambda bi, h, i: (bi, i, h)),
                  pl.BlockSpec((None, t, HEAD_DIM), lambda bi, h, i: (bi, 0, h)),
                  pl.BlockSpec((None, t, HEAD_DIM), lambda bi, h, i: (bi, 0, h)),
                  pl.BlockSpec((None, None, n_blk, SEL_TQ), lambda bi, h, i: (bi, h, 0, i)),
                  pl.BlockSpec((t, n_blk), lambda bi, h, i: (0, 0))],
        out_specs=pl.BlockSpec((None, SEL_TQ, GROUP_W), lambda bi, h, i: (bi, i, h)),
        out_shape=jax.ShapeDtypeStruct((b, t, TOK_W), F32),
        scratch_shapes=[pltpu.VMEM((t, SEL_TQ), F32),
                        pltpu.VMEM((1, NSA_GROUP * SEL_TQ), F32),
                        pltpu.VMEM((1, NSA_GROUP * SEL_TQ), F32),
                        pltpu.VMEM((HEAD_DIM, NSA_GROUP * SEL_TQ), F32)],
        compiler_params=_cparams(("parallel", "parallel", "arbitrary")),
        name="sel_attention_prompt",
    )(q, ks, vs, sel_t, expand)


WIN_TQ = 128


def _gate_cols(gate, h, j, tq):
    cols = []
    for g in range(NSA_GROUP):
        c = (h * NSA_GROUP + g) * 3 + j
        cols.append(jnp.broadcast_to(gate[:, c:c + 1], (tq, HEAD_DIM)))
    return jnp.concatenate(cols, axis=1)


def _win_combine_kernel(q_ref, k_ref, v_ref, oc_ref, os_ref, gate_ref, o_ref):
    tq = q_ref.shape[0]
    span = WINDOW + tq
    h = pl.program_id(1)
    qi = pl.program_id(2)
    start = pl.multiple_of(jnp.maximum(qi * tq - WINDOW, 0), tq)
    q_t = _heads_to_lanes_t(q_ref[...]).astype(BF16)
    k = k_ref[pl.ds(start, span), :].astype(BF16)
    v = v_ref[pl.ds(start, span), :].astype(BF16)
    diff = (qi * tq + lax.broadcasted_iota(I32, (span, tq), 1)) - (start + lax.broadcasted_iota(I32, (span, tq), 0))
    bias = jnp.where((diff >= 0) & (diff < WINDOW), 0.0, NEG)
    s = jnp.dot(k, q_t, preferred_element_type=F32) * SCALE + jnp.concatenate([bias] * NSA_GROUP, axis=1)
    e = jnp.exp(s - jnp.max(s, axis=0, keepdims=True))
    p = e / jnp.maximum(jnp.sum(e, axis=0, keepdims=True), 1e-30)
    o_win = _lanes_to_heads_t(_tn(v, p.astype(BF16)), tq)
    gate = gate_ref[...]
    for hh in range(NSA_KV_HEADS):
        @pl.when(h == hh)
        def _():
            o_ref[...] = (_gate_cols(gate, hh, 0, tq) * oc_ref[...] + _gate_cols(gate, hh, 1, tq) * os_ref[...]
                          + _gate_cols(gate, hh, 2, tq) * o_win)


def win_combine_prompt(q, kw, vw, o_cmp, o_sel, gate):
    b, t, _ = q.shape
    assert t >= WINDOW + WIN_TQ
    qspec = pl.BlockSpec((None, WIN_TQ, GROUP_W), lambda bi, h, i: (bi, i, h))
    kspec = pl.BlockSpec((None, t, HEAD_DIM), lambda bi, h, i: (bi, 0, h))
    return pl.pallas_call(
        _win_combine_kernel,
        grid=(b, NSA_KV_HEADS, t // WIN_TQ),
        in_specs=[qspec, kspec, kspec, qspec, qspec,
                  pl.BlockSpec((None, WIN_TQ, LANES), lambda bi, h, i: (bi, i, 0))],
        out_specs=qspec,
        out_shape=jax.ShapeDtypeStruct((b, t, TOK_W), F32),
        compiler_params=_cparams(("parallel", "parallel", "arbitrary")),
        name="win_combine_prompt",
    )(q, kw, vw, o_cmp, o_sel, gate)


def _router_kernel(x_ref, g_ref, wt_ref, b_ref, xn_ref, e_ref, gt_ref):
    xn = _rms(x_ref[...], g_ref[...])
    xn_ref[...] = xn
    lt = _nt(wt_ref[...], xn.astype(BF16))
    ex = jnp.exp(lt - jnp.max(lt, axis=0, keepdims=True))
    aff = ex / jnp.sum(ex, axis=0, keepdims=True)
    sel = aff + b_ref[...]
    row = lambda a, r: a[r:r + 1, :]

    best, g_idx = None, None
    for g in range(N_GROUPS):
        a, b, c, d = (row(sel, g * EXP_PER_GROUP + j) for j in range(EXP_PER_GROUP))
        hi1, lo1, hi2, lo2 = jnp.maximum(a, b), jnp.minimum(a, b), jnp.maximum(c, d), jnp.minimum(c, d)
        score = jnp.maximum(hi1, hi2) + jnp.maximum(jnp.minimum(hi1, hi2), jnp.maximum(lo1, lo2))
        if g == 0:
            best, g_idx = score, jnp.zeros(score.shape, I32)
        else:
            g_idx = jnp.where(score > best, g, g_idx)
            best = jnp.maximum(best, score)

    def in_group(a, j):
        out = row(a, j)
        for g in range(1, N_GROUPS):
            out = jnp.where(g_idx == g, row(a, g * EXP_PER_GROUP + j), out)
        return out

    v = [in_group(sel, j) for j in range(EXP_PER_GROUP)]
    af = [in_group(aff, j) for j in range(EXP_PER_GROUP)]

    def first_max(vals):
        m = functools.reduce(jnp.maximum, vals)
        loc = jnp.full(m.shape, EXP_PER_GROUP - 1, I32)
        for j in range(EXP_PER_GROUP - 2, -1, -1):
            loc = jnp.where(vals[j] == m, j, loc)
        return loc

    l1 = first_max(v)
    l2 = first_max([jnp.where(l1 == j, -jnp.inf, v[j]) for j in range(EXP_PER_GROUP)])
    pick = lambda loc: functools.reduce(lambda acc, j: jnp.where(loc == j, af[j], acc), range(1, EXP_PER_GROUP), af[0])
    a1, a2 = pick(l1), pick(l2)
    tot = a1 + a2
    e_ref[...] = jnp.concatenate([g_idx * EXP_PER_GROUP + l1, g_idx * EXP_PER_GROUP + l2], axis=0)
    gt_ref[...] = jnp.concatenate([a1 / tot, a2 / tot], axis=0)


def _router_into_kernel(x_ref, g_ref, wt_ref, b_ref, xn_all_ref, xn_ref, e_ref, gt_ref):
    del xn_all_ref
    _router_kernel(x_ref, g_ref, wt_ref, b_ref, xn_ref, e_ref, gt_ref)


def router(x, gain, w_router, b_router, xn_rows=None, into=None, row0=0):
    n, d = x.shape
    tm = _pick_tile(n, 1024, LANES)
    assert row0 % tm == 0
    xn_rows = n if xn_rows is None else xn_rows
    args = [x, gain.reshape(1, d), w_router.T.astype(BF16), b_router.reshape(N_EXPERTS, 1).astype(F32)]
    in_specs = [pl.BlockSpec((tm, d), lambda i: (i, 0)),
                pl.BlockSpec((1, d), lambda i: (0, 0)),
                pl.BlockSpec((N_EXPERTS, d), lambda i: (0, 0)),
                pl.BlockSpec((N_EXPERTS, 1), lambda i: (0, 0))]
    body, aliases = _router_kernel, {}
    if into is not None:
        assert into.shape == (xn_rows, d)
        args.append(into)
        in_specs.append(pl.BlockSpec(memory_space=pl.ANY))
        body, aliases = _router_into_kernel, {len(args) - 1: 0}
    return pl.pallas_call(
        body,
        grid=(n // tm,),
        in_specs=in_specs,
        out_specs=[pl.BlockSpec((tm, d), lambda i: (row0 // tm + i, 0)),
                   pl.BlockSpec((TOP_K, tm), lambda i: (0, i)),
                   pl.BlockSpec((TOP_K, tm), lambda i: (0, i))],
        out_shape=[jax.ShapeDtypeStruct((xn_rows, d), F32),
                   jax.ShapeDtypeStruct((TOP_K, n), I32),
                   jax.ShapeDtypeStruct((TOP_K, n), F32)],
        input_output_aliases=aliases,
        compiler_params=_cparams(("parallel",)),
        name="router",
    )(*args)


MOE_NF = 4
MOE_TF = D_FF // MOE_NF
MOE2_TM = 1024
MOE2_GROUP = 64
MOE2_CHUNK = 256
MOE2_GPS = MOE2_TM // MOE_NF // MOE2_GROUP
MOE2_VMEM_LIMIT = V7X_VMEM_BYTES * 7 // 8


def _moe2_kernel(te_ref, nu_ref, nv_ref, base_ref, order_ref, x_hbm, wg_ref, wu_ref, wd_ref, y_hbm,
                 xbuf, xb, ybuf, gsem, ssem, *, n_tok):
    n_items = n_tok * TOP_K

    def slot_item(tile, rr):
        return order_ref[jnp.minimum(base_ref[tile] + rr, n_items - 1)]
    i = pl.program_id(0)
    f = pl.program_id(1)
    n_used = nu_ref[0]
    cur = i % 2
    oth = 1 - cur
    n_groups = MOE2_TM // MOE2_GROUP

    def gather_group(tile, g):
        for r in range(MOE2_GROUP):
            rr = g * MOE2_GROUP + r
            item = slot_item(tile, rr)
            row = jnp.where(item >= n_tok, item - n_tok, item)
            pltpu.make_async_copy(x_hbm.at[pl.ds(row, 1), :], xbuf.at[pl.ds(rr, 1), :], gsem).start()

    def scatter_group(tile, buf, g):
        for r in range(MOE2_GROUP):
            rr = g * MOE2_GROUP + r
            row = jnp.where(rr < nv_ref[tile], slot_item(tile, rr), n_items + tile * MOE2_TM + rr)
            pltpu.make_async_copy(ybuf.at[buf, pl.ds(rr, 1), :], y_hbm.at[pl.ds(row, 1), :], ssem.at[buf]).start()

    def wait_gather_group():
        pltpu.make_async_copy(x_hbm.at[pl.ds(0, MOE2_GROUP), :], xbuf.at[pl.ds(0, MOE2_GROUP), :], gsem).wait()

    def wait_scatter_group(buf):
        pltpu.make_async_copy(ybuf.at[buf, pl.ds(0, MOE2_GROUP), :], y_hbm.at[pl.ds(0, MOE2_GROUP), :], ssem.at[buf]).wait()

    @pl.when((i == 0) & (f == 0))
    def _():
        xbuf[...] = jnp.zeros(xbuf.shape, F32)

        def body(g, carry):
            @pl.when(g * MOE2_GROUP < nv_ref[0])
            def _():
                gather_group(0, g)
            return carry
        lax.fori_loop(0, n_groups, body, 0)

    @pl.when(i < n_used)
    def _():
        n_valid = nv_ref[i]

        @pl.when(f == 0)
        def _():
            for g in range(n_groups):
                @pl.when(g * MOE2_GROUP < n_valid)
                def _():
                    wait_gather_group()
            xb[...] = xbuf[...].astype(BF16)
            ybuf[cur] = jnp.zeros(ybuf.shape[1:], F32)

        nv_next = nv_ref[i + 1]
        nv_prev = nv_ref[jnp.maximum(i - 1, 0)]
        for gi in range(MOE2_GPS):
            g = f * MOE2_GPS + gi

            @pl.when((i + 1 < n_used) & (g * MOE2_GROUP < nv_next))
            def _():
                gather_group(i + 1, g)

            @pl.when((i > 0) & (g * MOE2_GROUP < nv_prev))
            def _():
                scatter_group(i - 1, oth, g)

        for c in range(MOE2_TM // MOE2_CHUNK):
            lo, hi = c * MOE2_CHUNK, (c + 1) * MOE2_CHUNK

            @pl.when(lo < n_valid)
            def _():
                x = xb[lo:hi, :]
                a = jnp.dot(x, wg_ref[...].astype(BF16), preferred_element_type=F32)
                u = jnp.dot(x, wu_ref[...].astype(BF16), preferred_element_type=F32)
                h = (a * jax.nn.sigmoid(a) * u).astype(BF16)
                ybuf[cur, lo:hi, :] += jnp.dot(h, wd_ref[...].astype(BF16), preferred_element_type=F32)

        @pl.when(f == MOE_NF - 1)
        def _():
            for g in range(n_groups):
                @pl.when((i > 0) & (g * MOE2_GROUP < nv_prev))
                def _():
                    wait_scatter_group(oth)

    @pl.when((i == n_used) & (f == 0))
    def _():
        nv_last = nv_ref[i - 1]

        def start(g, carry):
            @pl.when(g * MOE2_GROUP < nv_last)
            def _():
                scatter_group(i - 1, oth, g)
            return carry
        lax.fori_loop(0, n_groups, start, 0)

        def wait(g, carry):
            @pl.when(g * MOE2_GROUP < nv_last)
            def _():
                wait_scatter_group(oth)
            return carry
        lax.fori_loop(0, n_groups, wait, 0)


def moe_experts2(xn, n_tok, tile_expert, n_used, n_valid, base, order, n_out_rows, w_gate, w_up, w_down, layer):
    n, d = xn.shape
    n_tiles = tile_expert.shape[0]

    def tile(i, nu):
        return jnp.minimum(i, nu[0] - 1)

    def fcol(i, f, nu):
        return jnp.where(i < nu[0], f, MOE_NF - 1)

    grid_spec = pltpu.PrefetchScalarGridSpec(
        num_scalar_prefetch=5,
        grid=(n_tiles, MOE_NF),
        in_specs=[
            pl.BlockSpec(memory_space=pl.ANY),
            pl.BlockSpec((None, None, d, MOE_TF), lambda i, f, te, nu, nv, s, t: (layer, te[tile(i, nu)], 0, fcol(i, f, nu))),
            pl.BlockSpec((None, None, d, MOE_TF), lambda i, f, te, nu, nv, s, t: (layer, te[tile(i, nu)], 0, fcol(i, f, nu))),
            pl.BlockSpec((None, None, MOE_TF, d), lambda i, f, te, nu, nv, s, t: (layer, te[tile(i, nu)], fcol(i, f, nu), 0)),
        ],
        out_specs=pl.BlockSpec(memory_space=pl.ANY),
        scratch_shapes=[pltpu.VMEM((MOE2_TM, d), F32),
                        pltpu.VMEM((MOE2_TM, d), BF16),
                        pltpu.VMEM((2, MOE2_TM, d), F32),
                        pltpu.SemaphoreType.DMA(()),
                        pltpu.SemaphoreType.DMA((2,))],
    )
    return pl.pallas_call(
        functools.partial(_moe2_kernel, n_tok=n_tok),
        grid_spec=grid_spec,
        out_shape=jax.ShapeDtypeStruct((n_out_rows, d), F32),
        compiler_params=pltpu.CompilerParams(dimension_semantics=("arbitrary", "arbitrary"),
                                             vmem_limit_bytes=MOE2_VMEM_LIMIT),
        name="moe_experts",
    )(tile_expert, n_used, n_valid, base, order, xn, w_gate, w_up, w_down)


def moe_ffn2(xs, xn, experts, gates, w_gate, w_up, w_down, layer):
    n, d = experts.shape[1], xn.shape[1]
    tm = MOE2_TM
    n_items = n * TOP_K
    n_tiles = -(-n_items // tm) + N_EXPERTS
    n_slots = n_tiles * tm
    flat_e = experts.reshape(-1)
    sizes = jnp.sum((flat_e[:, None] == jnp.arange(N_EXPERTS, dtype=I32)[None, :]).astype(I32), axis=0)
    padded = -(-sizes // tm) * tm
    ends_p = jnp.cumsum(padded)
    starts_p = ends_p - padded
    starts = jnp.cumsum(sizes) - sizes
    order = jnp.argsort(flat_e).astype(I32)
    tile_start = jnp.arange(n_tiles, dtype=I32) * tm
    tile_expert = jnp.minimum(jnp.searchsorted(ends_p, tile_start, side='right'), N_EXPERTS - 1).astype(I32)
    n_used = (ends_p[-1:] // tm).astype(I32)
    n_valid = jnp.clip(sizes[tile_expert] - (tile_start - starts_p[tile_expert]), 0, tm).astype(I32)
    base = (starts[tile_expert] + tile_start - starts_p[tile_expert]).astype(I32)
    y = moe_experts2(xn, n, tile_expert, n_used, n_valid, base, order, n_items + n_slots,
                     w_gate, w_up, w_down, layer)
    outs, r0 = [], 0
    for x, g in zip(xs, gates):
        r1 = r0 + x.shape[0]
        outs.append(x + y[r0:r1] * g[0][:, None] + y[n + r0:n + r1] * g[1][:, None])
        r0 = r1
    return outs


GROUP_ROWS = 8


def _pick_head(ref, h):
    out = ref[:, 0, :]
    for hh in range(1, NSA_KV_HEADS):
        out = jnp.where(h == hh, ref[:, hh, :], out)
    return out


def _group_rows(q_ref, h):
    rows = [q_ref[:, (h * NSA_GROUP + g) * HEAD_DIM:(h * NSA_GROUP + g + 1) * HEAD_DIM] for g in range(NSA_GROUP)]
    rows.append(jnp.zeros((GROUP_ROWS - NSA_GROUP, HEAD_DIM), F32))
    return jnp.concatenate(rows, axis=0)


def _compress_paged_kernel(pt_ref, cache_hbm, pe_ref, w1_ref, w2_ref, o_ref, xbuf, sem, *, layer, n_pages):
    b = pl.program_id(0)
    n_chunks = n_pages * (PAGE_SIZE // CMP_STRIDE)

    def page_copy(p):
        page = pt_ref[b * n_pages + p]
        return pltpu.make_async_copy(cache_hbm.at[layer, page], xbuf.at[pl.ds(p * PAGE_SIZE, PAGE_SIZE)], sem)

    def start(p, carry):
        page_copy(p).start()
        return carry

    def wait(p, carry):
        page_copy(p).wait()
        return carry

    lax.fori_loop(0, n_pages, start, 0)
    lax.fori_loop(0, n_pages, wait, 0)

    def slab(r, h):
        return xbuf[pl.ds(r, n_chunks, stride=CMP_STRIDE), h, :]
    _compress_body(slab, n_chunks, pe_ref, w1_ref, w2_ref, o_ref)


def compress_paged(cache, layer, page_table, pe, w1, w2):
    b, n_pages = page_table.shape
    n_chunks = n_pages * (PAGE_SIZE // CMP_STRIDE)
    grid_spec = pltpu.PrefetchScalarGridSpec(
        num_scalar_prefetch=1,
        grid=(b,),
        in_specs=[pl.BlockSpec(memory_space=pl.ANY),
                  pl.BlockSpec(pe.shape, lambda bi, pt: (0, 0)),
                  pl.BlockSpec(w1.shape, lambda bi, pt: (0, 0)),
                  pl.BlockSpec(w2.shape, lambda bi, pt: (0, 0))],
        out_specs=pl.BlockSpec((None, NSA_KV_HEADS, n_chunks, HEAD_DIM), lambda bi, pt: (bi, 0, 0, 0)),
        scratch_shapes=[pltpu.VMEM((n_pages * PAGE_SIZE, NSA_KV_HEADS, HEAD_DIM), F32), pltpu.SemaphoreType.DMA(())],
    )
    return pl.pallas_call(
        functools.partial(_compress_paged_kernel, layer=layer, n_pages=n_pages),
        grid_spec=grid_spec,
        out_shape=jax.ShapeDtypeStruct((b, NSA_KV_HEADS, n_chunks, HEAD_DIM), F32),
        compiler_params=_cparams(("arbitrary",)),
        name="compress_paged",
    )(page_table.reshape(-1), cache, pe, w1, w2)


def _cmp_select_sample_kernel(q_ref, kc_ref, vc_ref, cov_ref, o_ref, idx_ref, q8_ref, *, q_pos, n_blk):
    nc = kc_ref.shape[1]
    nb_pad = cov_ref.shape[1]
    blk = lax.broadcasted_iota(I32, (1, nb_pad), 1)
    cur = q_pos // SEL_BLOCK
    forced = (blk == 0) | (blk == cur) | (blk == cur - 1)
    future = (blk * SEL_BLOCK > q_pos) | (blk >= n_blk)
    c_last = lax.broadcasted_iota(I32, (GROUP_ROWS, nc), 1) * CMP_STRIDE + (CMP_BLOCK - 1)
    valid = (c_last <= q_pos) & (lax.broadcasted_iota(I32, (GROUP_ROWS, nc), 0) < NSA_GROUP)
    for h in range(NSA_KV_HEADS):
        q8f = _group_rows(q_ref, h)
        q8_ref[h] = q8f
        q8 = q8f.astype(BF16)
        s = jnp.where(valid, _nt(q8, kc_ref[h].astype(BF16)) * SCALE, NEG)
        m = jnp.max(s, axis=-1, keepdims=True)
        e = jnp.where(valid, jnp.exp(s - m), 0.0)
        p = (e / jnp.maximum(jnp.sum(e, axis=-1, keepdims=True), 1e-30)).astype(BF16)
        o_ref[h] = jnp.dot(p, vc_ref[h].astype(BF16), preferred_element_type=F32)
        imp = jnp.sum(jnp.dot(p, cov_ref[...], preferred_element_type=F32), axis=0, keepdims=True)
        imp = jnp.where(future, -1e9, jnp.where(forced, 1e9, imp))
        jj = lax.broadcasted_iota(I32, (nb_pad, nb_pad), 0)
        ss = lax.broadcasted_iota(I32, (nb_pad, nb_pad), 1)
        row = jnp.broadcast_to(imp, (nb_pad, nb_pad))
        col = jnp.sum(jnp.where(jj == ss, row, 0.0), axis=1, keepdims=True)
        beats = jnp.where(col > row, 1.0, jnp.where(col == row, jnp.where(jj < ss, 1.0, 0.0), 0.0))
        rank = jnp.sum(beats, axis=0, keepdims=True)
        want = lax.broadcasted_iota(I32, (SEL_TOP, nb_pad), 0).astype(F32)
        lane = lax.broadcasted_iota(I32, (SEL_TOP, nb_pad), 1).astype(F32)
        idx = jnp.sum(jnp.where(jnp.broadcast_to(rank, (SEL_TOP, nb_pad)) == want, lane, 0.0), axis=1, keepdims=True)
        idx_ref[h] = idx.astype(I32)


def cmp_select_sample(q, kcmp, vcmp, q_pos):
    b = q.shape[0]
    nc = kcmp.shape[2]
    n_blk = q_pos // SEL_BLOCK + 1
    nb_pad = -(-n_blk // LANES) * LANES
    cov = _cover_t(nc, nb_pad, nc - 1)[:, :nc].T
    cov = jnp.asarray(cov, BF16)
    return pl.pallas_call(
        functools.partial(_cmp_select_sample_kernel, q_pos=q_pos, n_blk=n_blk),
        grid=(b,),
        in_specs=[pl.BlockSpec((None, 1, TOK_W), lambda bi: (bi, 0, 0)),
                  pl.BlockSpec((None, NSA_KV_HEADS, nc, HEAD_DIM), lambda bi: (bi, 0, 0, 0)),
                  pl.BlockSpec((None, NSA_KV_HEADS, nc, HEAD_DIM), lambda bi: (bi, 0, 0, 0)),
                  pl.BlockSpec(cov.shape, lambda bi: (0, 0))],
        out_specs=[pl.BlockSpec((None, NSA_KV_HEADS, GROUP_ROWS, HEAD_DIM), lambda bi: (bi, 0, 0, 0)),
                   pl.BlockSpec((None, NSA_KV_HEADS, SEL_TOP, 1), lambda bi: (bi, 0, 0, 0)),
                   pl.BlockSpec((None, NSA_KV_HEADS, GROUP_ROWS, HEAD_DIM), lambda bi: (bi, 0, 0, 0))],
        out_shape=[jax.ShapeDtypeStruct((b, NSA_KV_HEADS, GROUP_ROWS, HEAD_DIM), F32),
                   jax.ShapeDtypeStruct((b, NSA_KV_HEADS, SEL_TOP, 1), I32),
                   jax.ShapeDtypeStruct((b, NSA_KV_HEADS, GROUP_ROWS, HEAD_DIM), F32)],
        compiler_params=_cparams(("parallel",)),
        name="cmp_select_sample",
    )(q, kcmp, vcmp, cov)


def _dot_new_row(q8, k_row):
    a = q8.astype(BF16).astype(F32)
    b = k_row.astype(BF16).astype(F32)
    return jnp.sum(a * b, axis=-1, keepdims=True)


SEL_S_BLOCKS = 4


def _sel_sample_kernel(pt_ref, ix_ref, q_ref, *refs, n_cache_blk):
    k_refs, v_refs = refs[:SEL_S_BLOCKS], refs[SEL_S_BLOCKS:2 * SEL_S_BLOCKS]
    kn_ref, vn_ref, o_ref, m_ref, l_ref, acc_ref = refs[2 * SEL_S_BLOCKS:]
    b, h, r = pl.program_id(0), pl.program_id(1), pl.program_id(2)

    @pl.when(r == 0)
    def _():
        m_ref[...] = jnp.full(m_ref.shape, NEG, F32)
        l_ref[...] = jnp.zeros(l_ref.shape, F32)
        acc_ref[...] = jnp.zeros(acc_ref.shape, F32)

    q8 = q_ref[...]
    n_keys = SEL_S_BLOCKS * SEL_BLOCK
    key_blk = lax.broadcasted_iota(I32, (GROUP_ROWS, n_keys), 1) // SEL_BLOCK
    in_cache = jnp.zeros((GROUP_ROWS, n_keys), I32)
    for j in range(SEL_S_BLOCKS):
        flag = (ix_ref[(b * NSA_KV_HEADS + h) * SEL_TOP + r * SEL_S_BLOCKS + j] < n_cache_blk).astype(I32)
        in_cache = jnp.where(key_blk == j, flag, in_cache)
    valid = (lax.broadcasted_iota(I32, (GROUP_ROWS, n_keys), 0) < NSA_GROUP) & (in_cache > 0)
    k_blk = jnp.concatenate([_pick_head(k_ref, h) for k_ref in k_refs], axis=0).astype(BF16)
    v_blk = jnp.concatenate([_pick_head(v_ref, h) for v_ref in v_refs], axis=0).astype(BF16)
    s = jnp.where(valid, _nt(q8.astype(BF16), k_blk) * SCALE, NEG)
    m_old = m_ref[...]
    m_new = jnp.maximum(m_old, jnp.max(s, axis=-1, keepdims=True))
    e = jnp.where(valid, jnp.exp(s - m_new), 0.0)
    alpha = jnp.exp(m_old - m_new)
    l_ref[...] = alpha * l_ref[...] + jnp.sum(e, axis=-1, keepdims=True)
    acc_ref[...] = alpha * acc_ref[...] + jnp.dot(e.astype(BF16), v_blk, preferred_element_type=F32)
    m_ref[...] = m_new

    @pl.when(r == SEL_TOP // SEL_S_BLOCKS - 1)
    def _():
        s_new = _dot_new_row(q8, kn_ref[...]) * SCALE
        m_old = m_ref[...]
        m_fin = jnp.maximum(m_old, s_new)
        alpha = jnp.exp(m_old - m_fin)
        e_new = jnp.exp(s_new - m_fin)
        l_fin = alpha * l_ref[...] + e_new
        acc = alpha * acc_ref[...] + e_new.astype(BF16).astype(F32) * vn_ref[...].astype(BF16).astype(F32)
        o_ref[...] = acc / jnp.maximum(l_fin, 1e-30)


def sel_attention_sample(q8, idx, cache_k, cache_v, layer, page_table, k_new, v_new):
    b, n_pages = page_table.shape
    per_page = PAGE_SIZE // SEL_BLOCK
    n_cache_blk = n_pages * per_page

    def blk_index(j):
        def index(bi, h, r, pt, ix):
            s = jnp.minimum(ix[(bi * NSA_KV_HEADS + h) * SEL_TOP + r * SEL_S_BLOCKS + j], n_cache_blk - 1)
            return (layer, pt[bi * n_pages + s // per_page], s % per_page, 0, 0)
        return index

    blk_specs = [pl.BlockSpec((None, None, SEL_BLOCK, NSA_KV_HEADS, HEAD_DIM), blk_index(j)) for j in range(SEL_S_BLOCKS)]
    grid_spec = pltpu.PrefetchScalarGridSpec(
        num_scalar_prefetch=2,
        grid=(b, NSA_KV_HEADS, SEL_TOP // SEL_S_BLOCKS),
        in_specs=[pl.BlockSpec((None, None, GROUP_ROWS, HEAD_DIM), lambda bi, h, r, pt, ix: (bi, h, 0, 0))]
                 + blk_specs + blk_specs
                 + [pl.BlockSpec((None, 1, HEAD_DIM), lambda bi, h, r, pt, ix: (bi, 0, h)),
                    pl.BlockSpec((None, 1, HEAD_DIM), lambda bi, h, r, pt, ix: (bi, 0, h))],
        out_specs=pl.BlockSpec((None, None, GROUP_ROWS, HEAD_DIM), lambda bi, h, r, pt, ix: (bi, h, 0, 0)),
        scratch_shapes=[pltpu.VMEM((GROUP_ROWS, 1), F32), pltpu.VMEM((GROUP_ROWS, 1), F32),
                        pltpu.VMEM((GROUP_ROWS, HEAD_DIM), F32)],
    )
    return pl.pallas_call(
        functools.partial(_sel_sample_kernel, n_cache_blk=n_cache_blk),
        grid_spec=grid_spec,
        out_shape=jax.ShapeDtypeStruct((b, NSA_KV_HEADS, GROUP_ROWS, HEAD_DIM), F32),
        compiler_params=_cparams(("arbitrary", "arbitrary", "arbitrary")),
        name="sel_attention_sample",
    )(page_table.reshape(-1), idx.reshape(-1), q8, *([cache_k] * SEL_S_BLOCKS), *([cache_v] * SEL_S_BLOCKS), k_new, v_new)


def _win_sample_kernel(q_ref, k_ref, v_ref, kn_ref, vn_ref, oc_ref, os_ref, gate_ref, o_ref):
    h = pl.program_id(1)
    wb = k_ref.shape[0]
    k_win = _pick_head(k_ref, h).astype(BF16)
    v_win = _pick_head(v_ref, h).astype(BF16)
    q8 = q_ref[...]
    col = lax.broadcasted_iota(I32, (GROUP_ROWS, wb), 1)
    valid = col >= 1
    s = jnp.where(valid, _nt(q8.astype(BF16), k_win) * SCALE, NEG)
    s_new = _dot_new_row(q8, kn_ref[...]) * SCALE
    m = jnp.maximum(jnp.max(s, axis=-1, keepdims=True), s_new)
    e = jnp.where(valid, jnp.exp(s - m), 0.0)
    e_new = jnp.exp(s_new - m)
    den = jnp.maximum(jnp.sum(e, axis=-1, keepdims=True) + e_new, 1e-30)
    p = (e / den).astype(BF16)
    p_new = (e_new / den).astype(BF16).astype(F32)
    o_win = jnp.dot(p, v_win, preferred_element_type=F32) + p_new * vn_ref[...].astype(BF16).astype(F32)
    gate = jnp.broadcast_to(gate_ref[...], (GROUP_ROWS, LANES))
    lane = lax.broadcasted_iota(I32, (GROUP_ROWS, LANES), 1)
    head = h * NSA_GROUP + lax.broadcasted_iota(I32, (GROUP_ROWS, LANES), 0)
    g = [jnp.sum(jnp.where(lane == head * 3 + j, gate, 0.0), axis=-1, keepdims=True) for j in range(3)]
    o_ref[...] = g[0] * oc_ref[...] + g[1] * os_ref[...] + g[2] * o_win


def win_combine_sample(q8, win_k, win_v, layer, k_new, v_new, o_cmp, o_sel, gate):
    b = q8.shape[0]
    wb = win_k.shape[2]
    assert wb == WINDOW
    gspec = pl.BlockSpec((None, None, GROUP_ROWS, HEAD_DIM), lambda bi, h: (bi, h, 0, 0))
    wspec = pl.BlockSpec((None, None, wb, NSA_KV_HEADS, HEAD_DIM), lambda bi, h: (layer, bi, 0, 0, 0))
    nspec = pl.BlockSpec((None, 1, HEAD_DIM), lambda bi, h: (bi, 0, h))
    return pl.pallas_call(
        _win_sample_kernel,
        grid=(b, NSA_KV_HEADS),
        in_specs=[gspec, wspec, wspec, nspec, nspec, gspec, gspec,
                  pl.BlockSpec((None, 1, LANES), lambda bi, h: (bi, 0, 0))],
        out_specs=gspec,
        out_shape=jax.ShapeDtypeStruct((b, NSA_KV_HEADS, GROUP_ROWS, HEAD_DIM), F32),
        compiler_params=_cparams(("parallel", "arbitrary")),
        name="win_combine_sample",
    )(q8, win_k, win_v, k_new, v_new, o_cmp, o_sel, gate)


def nsa_sample_step(prep, layer, caches, win_k, win_v, page_table, pe, w1, w2):
    q, kc, vc, ks, vs, kw, vw, gate = prep
    b = q.shape[0]
    past_len = page_table.shape[1] * PAGE_SIZE
    cache_ck, cache_cv, cache_sk, cache_sv = caches
    kcmp = compress_paged(cache_ck, layer, page_table, pe[0], w1[0], w2[0])
    vcmp = compress_paged(cache_cv, layer, page_table, pe[1], w1[1], w2[1])
    o_cmp, idx, q8 = cmp_select_sample(q, kcmp, vcmp, past_len)
    o_sel = sel_attention_sample(q8, idx, cache_sk, cache_sv, layer, page_table, ks, vs)
    o = win_combine_sample(q8, win_k, win_v, layer, kw, vw, o_cmp, o_sel, gate)
    return o[:, :, :NSA_GROUP].reshape(b, 1, TOK_W)


def _reorder_nsa_weight(w):
    n_gl = 3 * NSA_HEADS
    parts = [w[:, :NSA_MQ0], w[:, NSA_MQ0 + n_gl:NSA_MQ0 + n_gl + MEM_W], w[:, NSA_MQ0:NSA_MQ0 + n_gl]]
    wr = jnp.concatenate(parts, axis=1)
    return jnp.pad(wr, ((0, 0), (0, NSA_ZW - wr.shape[1]))).astype(BF16)


def kernel(x_prompt, x_sample, state_pool, cache_cmp_k, cache_cmp_v, cache_sel_k, cache_sel_v, state_win_k, state_win_v, cache_mem_k, cache_mem_v, page_table, mem_prompt, norm_mix_g, norm_ffn_g, norm_mem_g, w_mem_kv, mem_q_norm_g, mem_k_norm_g, w_in_pool, w_pool_grp, pool_scale, w_out_pool, w_in_nsa, b_gate, nsa_q_norm_g, nsa_k_norm_g, cmp_pe, cmp_w1, cmp_w2, w_out_nsa, w_router, b_router, w_gate, w_up, w_down):
    bp, t_p, d = x_prompt.shape
    bs, t_s, _ = x_sample.shape
    assert t_s == 1
    n_p, n_s = bp * t_p, bs * t_s
    m_len = mem_prompt.shape[1]
    past_len = page_table.shape[1] * PAGE_SIZE
    pos_p = jnp.arange(t_p, dtype=I32)
    pos_s = past_len + jnp.arange(t_s, dtype=I32)
    xp = x_prompt.reshape(n_p, d)
    xs = x_sample.reshape(n_s, d)
    mem_flat = mem_prompt.reshape(-1, d)
    pool_p, pool_s, rows_p, rows_s, win_p, win_s, mem_k_p, mem_v_p = [], [], [], [], [], [], [], []
    for i in range(DEPTH):
        li = i // 2
        kv = proj(mem_flat, w_mem_kv[i].astype(BF16), gain=norm_mem_g[i])
        mk = head_norm(kv, 0, MEM_W, mem_k_norm_g[i]).reshape(bp, m_len, MEM_W)
        mv = kv[:, MEM_W:].reshape(bp, m_len, MEM_W)
        mem_k_p.append(mk.reshape(bp, m_len, MEM_HEADS, HEAD_DIM))
        mem_v_p.append(mv.reshape(bp, m_len, MEM_HEADS, HEAD_DIM))
        mk_s = cache_mem_k[i].reshape(bs, m_len, MEM_W)
        mv_s = cache_mem_v[i].reshape(bs, m_len, MEM_W)
        if i % 2 == 0:
            w_in = w_in_pool[li].astype(BF16)
            zp = proj(xp, w_in, gain=norm_mix_g[i]).reshape(bp, t_p, -1)
            zs = proj(xs, w_in, gain=norm_mix_g[i]).reshape(bs, t_s, -1)
            op = pool_mix(zp, None, w_pool_grp[li], pool_scale[li], 0)
            zs16 = jnp.pad(zs, ((0, 0), (0, POOL_HALO - t_s), (0, 0)))
            halo = jnp.pad(state_pool[li], ((0, 0), (1, 0), (0, 0)))
            os_ = pool_mix(zs16, halo, w_pool_grp[li], pool_scale[li], past_len)[:, :t_s]
            pool_p.append(zp[:, t_p - POOL_STATE:, :TOK_W])
            pool_s.append(jnp.concatenate([state_pool[li], zs[..., :TOK_W]], axis=1)[:, -POOL_STATE:])
            mq0 = TOK_W
            w_out = w_out_pool[li]
        else:
            w_in = _reorder_nsa_weight(w_in_nsa[li])
            zp = proj(xp, w_in, gain=norm_mix_g[i]).reshape(bp, t_p, -1)
            zs = proj(xs, w_in, gain=norm_mix_g[i]).reshape(bs, t_s, -1)
            q, kc, vc, ks, vs, kw, vw, gate = nsa_prep(zp, pos_p, nsa_q_norm_g[li], nsa_k_norm_g[li], b_gate[li])
            kcmp = compress_rows(kc, cmp_pe[li, 0], cmp_w1[li, 0], cmp_w2[li, 0])
            vcmp = compress_rows(vc, cmp_pe[li, 1], cmp_w1[li, 1], cmp_w2[li, 1])
            o_cmp, sel_t = cmp_select_prompt(q, kcmp, vcmp)
            o_sel = sel_attention_prompt_t(q, ks, vs, sel_t)
            op = win_combine_prompt(q, kw, vw, o_cmp, o_sel, gate)
            kvr = lambda a: a.reshape(a.shape[0], a.shape[1], NSA_KV_HEADS, HEAD_DIM)
            rows_p.append(tuple(kvr(a) for a in (kc, vc, ks, vs)))
            wb = min(WINDOW, t_p)
            win_p.append((kvr(kw[:, t_p - wb:]), kvr(vw[:, t_p - wb:])))
            prep_s = nsa_prep(zs, pos_s, nsa_q_norm_g[li], nsa_k_norm_g[li], b_gate[li])
            os_ = nsa_sample_step(prep_s, li, (cache_cmp_k, cache_cmp_v, cache_sel_k, cache_sel_v),
                                  state_win_k, state_win_v, page_table, cmp_pe[li], cmp_w1[li], cmp_w2[li])
            rows_s.append(tuple(kvr(a) for a in prep_s[1:5]))
            win_s.append(tuple(jnp.concatenate([st[li][:, t_s:], kvr(new)], axis=1)
                               for st, new in ((state_win_k, prep_s[5]), (state_win_v, prep_s[6]))))
            mq0 = NSA_MQ0
            w_out = w_out_nsa[li]
        ap = mem_attention(zp, mq0, mk, mv, mem_q_norm_g[i])
        as_ = mem_attention(zs, mq0, mk_s, mv_s, mem_q_norm_g[i])
        w_out = w_out.astype(BF16)
        xp = proj([op.reshape(n_p, TOK_W), ap.reshape(n_p, MEM_W)], w_out, residual=xp)
        xs = proj([os_.reshape(n_s, TOK_W), as_.reshape(n_s, MEM_W)], w_out, residual=xs)
        xn, e_p, g_p = router(xp, norm_ffn_g[i], w_router, b_router, xn_rows=n_p + LANES)
        xs_pad = jnp.pad(xs, ((0, LANES - n_s), (0, 0)))
        xn, e_s, g_s = router(xs_pad, norm_ffn_g[i], w_router, b_router, xn_rows=n_p + LANES, into=xn, row0=n_p)
        xp, xs = moe_ffn2([xp, xs],
                          xn,
                          jnp.concatenate([e_p, e_s[:, :n_s]], axis=1),
                          [g_p, g_s[:, :n_s]],
                          w_gate, w_up, w_down, i)
    stk = lambda lst, j: jnp.stack([r[j] for r in lst])
    return (xp.reshape(bp, t_p, d), xs.reshape(bs, t_s, d),
            jnp.stack(pool_p), jnp.stack(pool_s),
            stk(rows_p, 0), stk(rows_p, 1), stk(rows_p, 2), stk(rows_p, 3),
            stk(rows_s, 0), stk(rows_s, 1), stk(rows_s, 2), stk(rows_s, 3),
            stk(win_p, 0), stk(win_p, 1), stk(win_s, 0), stk(win_s, 1),
            jnp.stack(mem_k_p), jnp.stack(mem_v_p))
```
